```python
import math
import jax, jax.numpy as jnp
from jax import lax
import numpy as np

D_MODEL = 1024
BATCH = 4
SEQ = 4096
DEPTH = 4

ROPE_THETA = 500000.0
Q_BLOCK = 128
NEG_INF = -1e30
FORCE_SCORE = 1e9
N_BRANCH = 4

DA_HEADS = 4
DA_DIM = 64
DA_ROT = DA_DIM // 4

NSA_HEADS = 4
NSA_DK = 128
NSA_DV = 128
NSA_ROT = NSA_DK // 4
CMP_LEN = 32
CMP_STRIDE = 16
SEL_LEN = 64
SEL_N = 16
WIN = 512

MLA_HEADS = 4
MLA_Q_LORA = 384
MLA_KV_LORA = 256
MLA_NOPE = 128
MLA_ROPE = 64
MLA_V = 128

DSA_HEADS = 4
DSA_DIM = 128
DSA_ROT = DSA_DIM // 4
IDX_HEADS = 8
IDX_DIM = 64
IDX_ROT = IDX_DIM // 4
IDX_TOPK = 256

BR_WIDTH = DA_HEADS * 2 * DA_DIM
D_FF = ((8 * D_MODEL + 3 * 256 - 1) // (3 * 256)) * 256

IN_LAYOUT = (
    ("a_q", DA_HEADS * 2 * DA_DIM), ("a_k", DA_HEADS * 2 * DA_DIM), ("a_v", DA_HEADS * 2 * DA_DIM),
    ("b_q", NSA_HEADS * NSA_DK),
    ("b_kc", NSA_DK), ("b_vc", NSA_DV), ("b_ks", NSA_DK), ("b_vs", NSA_DV),
    ("b_kw", NSA_DK), ("b_vw", NSA_DV), ("b_g", NSA_HEADS * 3),
    ("c_q", MLA_Q_LORA), ("c_kv", MLA_KV_LORA), ("c_kr", MLA_ROPE),
    ("d_q", DSA_HEADS * DSA_DIM), ("d_k", DSA_HEADS * DSA_DIM), ("d_v", DSA_HEADS * DSA_DIM),
    ("d_iq", IDX_HEADS * IDX_DIM), ("d_ik", IDX_DIM), ("d_iw", IDX_HEADS),
    ("gate", N_BRANCH * D_MODEL),
)
D_IN = sum(n for _, n in IN_LAYOUT)

kernel_name = "hybrid_gated_parallel_mixers"


def split_columns(z):
    names = [nm for nm, _ in IN_LAYOUT]
    offs = [int(o) for o in np.cumsum([n for _, n in IN_LAYOUT])[:-1]]
    return dict(zip(names, jnp.split(z, offs, axis=-1)))


def rmsnorm(x, g, eps=1e-6):
    xf = x.astype(jnp.float32)
    y = xf * lax.rsqrt(jnp.mean(xf * xf, axis=-1, keepdims=True) + eps)
    return (y * g.astype(jnp.float32)).astype(x.dtype)


def rope_tables(seq, rot_dim, dtype):
    inv = jnp.power(jnp.float32(ROPE_THETA), -jnp.arange(0, rot_dim, 2, dtype=jnp.float32) / rot_dim)
    ang = jnp.arange(seq, dtype=jnp.float32)[:, None] * inv[None, :]
    return jnp.cos(ang).astype(dtype), jnp.sin(ang).astype(dtype)


def apply_rope(x, cos, sin):
    half = cos.shape[-1]
    shape = (1, cos.shape[0]) + (1,) * (x.ndim - 3) + (half,)
    c = cos.reshape(shape)
    s = sin.reshape(shape)
    x1 = x[..., :half]
    x2 = x[..., half:2 * half]
    return jnp.concatenate([x1 * c - x2 * s, x2 * c + x1 * s, x[..., 2 * half:]], axis=-1)


def masked_softmax(s, mask):
    s = jnp.where(mask, s.astype(jnp.float32), NEG_INF)
    return jnp.where(mask, jax.nn.softmax(s, axis=-1), 0.0)


def sweep_query_blocks(fn, seq):
    out = lax.map(fn, jnp.arange(seq // Q_BLOCK) * Q_BLOCK)
    out = jnp.moveaxis(out, 0, 1)
    return out.reshape((out.shape[0], seq) + out.shape[3:])


def dense_causal_attention(q, k, v, scale):
    seq = q.shape[1]
    kpos = jnp.arange(seq)

    def block(qs):
        qb = lax.dynamic_slice_in_dim(q, qs, Q_BLOCK, axis=1)
        s = jnp.einsum("bqhd,bkhd->bhqk", qb, k).astype(jnp.float32) * scale
        mask = (qs + jnp.arange(Q_BLOCK))[:, None] >= kpos[None, :]
        p = masked_softmax(s, mask)
        return jnp.einsum("bhqk,bkhd->bqhd", p.astype(v.dtype), v)

    return sweep_query_blocks(block, seq)


def diff_attention(q, k, v, lam, lam_init, subln_g, cos, sin):
    B, S, H = q.shape[:3]
    q = apply_rope(q, cos, sin)
    k = apply_rope(k, cos, sin)
    scale = DA_DIM ** -0.5
    kpos = jnp.arange(S)

    def block(qs):
        qb = lax.dynamic_slice_in_dim(q, qs, Q_BLOCK, axis=1)
        s = jnp.einsum("bqhmd,bkhmd->bhmqk", qb, k).astype(jnp.float32) * scale
        mask = (qs + jnp.arange(Q_BLOCK))[:, None] >= kpos[None, :]
        p = masked_softmax(s, mask)
        a = p[:, :, 0] - lam * p[:, :, 1]
        return jnp.einsum("bhqk,bkhd->bqhd", a.astype(v.dtype), v)

    o = sweep_query_blocks(block, S)
    o = rmsnorm(o, subln_g) * (1.0 - lam_init)
    return o.reshape(B, S, H * 2 * DA_DIM)


def nsa_compress(tok, pe, w1, w2):
    S = tok.shape[1]
    nc = (S - CMP_LEN) // CMP_STRIDE + 1
    idx = jnp.arange(nc)[:, None] * CMP_STRIDE + jnp.arange(CMP_LEN)[None, :]
    blocks = tok[:, idx] + pe
    flat = blocks.reshape(blocks.shape[0], nc, CMP_LEN * tok.shape[-1])
    return jax.nn.gelu(flat @ w1) @ w2


def nsa_attention(q, kc_tok, vc_tok, ks, vs, kw, vw, gates,
                  pe_k, w1_k, w2_k, pe_v, w1_v, w2_v, cos, sin):
    B, S, H, _ = q.shape
    q = apply_rope(q, cos, sin)
    kc = nsa_compress(apply_rope(kc_tok, cos, sin), pe_k, w1_k, w2_k)
    vc = nsa_compress(vc_tok, pe_v, w1_v, w2_v)
    ks = apply_rope(ks, cos, sin)
    kw = apply_rope(kw, cos, sin)
    nc = kc.shape[1]
    ns = S // SEL_LEN
    n_sel = min(SEL_N, ns)
    c_start = jnp.arange(nc) * CMP_STRIDE
    c_end = c_start + CMP_LEN - 1
    s_start = jnp.arange(ns) * SEL_LEN
    overlap = ((c_start[:, None] < s_start[None, :] + SEL_LEN)
               & (c_end[:, None] >= s_start[None, :])).astype(jnp.float32)
    ks_blk = ks.reshape(B, ns, SEL_LEN, NSA_DK)
    vs_blk = vs.reshape(B, ns, SEL_LEN, NSA_DV)
    kw_pad = jnp.pad(kw, ((0, 0), (WIN, 0), (0, 0)))
    vw_pad = jnp.pad(vw, ((0, 0), (WIN, 0), (0, 0)))
    scale = NSA_DK ** -0.5
    blk_ids = jnp.arange(ns)

    def block(qs):
        qb = lax.dynamic_slice_in_dim(q, qs, Q_BLOCK, axis=1)
        gb = lax.dynamic_slice_in_dim(gates, qs, Q_BLOCK, axis=1)
        t = qs + jnp.arange(Q_BLOCK)
        sc = jnp.einsum("bqhd,bnd->bhqn", qb, kc) * scale
        pc = masked_softmax(sc, c_end[None, :] <= t[:, None])
        o_cmp = jnp.einsum("bhqn,bnd->bqhd", pc.astype(vc.dtype), vc)
        imp = jnp.einsum("bhqn,nj->bqj", pc, overlap)
        cur = t // SEL_LEN
        forced = ((blk_ids[None, :] == 0) | (blk_ids[None, :] == cur[:, None])
                  | (blk_ids[None, :] == cur[:, None] - 1))
        visible = s_start[None, :] <= t[:, None]
        score = jnp.where(visible[None], jnp.where(forced[None], FORCE_SCORE, imp), NEG_INF)
        _, sel = lax.top_k(score, n_sel)
        kg = jax.vmap(lambda kb, ib: kb[ib])(ks_blk, sel).reshape(B, Q_BLOCK, n_sel * SEL_LEN, NSA_DK)
        vg = jax.vmap(lambda vb, ib: vb[ib])(vs_blk, sel).reshape(B, Q_BLOCK, n_sel * SEL_LEN, NSA_DV)
        pos = (sel[..., None] * SEL_LEN + jnp.arange(SEL_LEN)).reshape(B, Q_BLOCK, n_sel * SEL_LEN)
        ss = jnp.einsum("bqhd,bqkd->bhqk", qb, kg) * scale
        ps = masked_softmax(ss, (pos <= t[None, :, None])[:, None])
        o_slc = jnp.einsum("bhqk,bqkd->bqhd", ps.astype(vg.dtype), vg)
        kwb = lax.dynamic_slice_in_dim(kw_pad, qs, Q_BLOCK + WIN, axis=1)
        vwb = lax.dynamic_slice_in_dim(vw_pad, qs, Q_BLOCK + WIN, axis=1)
        spos = qs - WIN + jnp.arange(Q_BLOCK + WIN)
        dist = t[:, None] - spos[None, :]
        wmask = (spos[None, :] >= 0) & (dist >= 0) & (dist < WIN)
        sw = jnp.einsum("bqhd,bkd->bhqk", qb, kwb) * scale
        pw = masked_softmax(sw, wmask)
        o_win = jnp.einsum("bhqk,bkd->bqhd", pw.astype(vwb.dtype), vwb)
        return gb[..., 0:1] * o_cmp + gb[..., 1:2] * o_slc + gb[..., 2:3] * o_win

    return sweep_query_blocks(block, S).reshape(B, S, H * NSA_DV)


def mla_attention(c_q, c_kv, k_rope, q_norm_g, w_uq, kv_norm_g, w_ukv, cos, sin):
    B, S, _ = c_q.shape
    q = (rmsnorm(c_q, q_norm_g) @ w_uq).reshape(B, S, MLA_HEADS, MLA_NOPE + MLA_ROPE)
    q = jnp.concatenate([q[..., :MLA_NOPE], apply_rope(q[..., MLA_NOPE:], cos, sin)], axis=-1)
    kv = (rmsnorm(c_kv, kv_norm_g) @ w_ukv).reshape(B, S, MLA_HEADS, MLA_NOPE + MLA_V)
    k_pe = apply_rope(k_rope, cos, sin)
    k = jnp.concatenate([kv[..., :MLA_NOPE],
                         jnp.broadcast_to(k_pe[:, :, None, :], (B, S, MLA_HEADS, MLA_ROPE))], axis=-1)
    v = kv[..., MLA_NOPE:]
    o = dense_causal_attention(q, k, v, (MLA_NOPE + MLA_ROPE) ** -0.5)
    return o.reshape(B, S, MLA_HEADS * MLA_V)


def dsa_attention(q, k, v, iq, ik, iw, ik_norm_g, cos, sin, icos, isin):
    B, S, H, _ = q.shape
    q = apply_rope(q, cos, sin)
    k = apply_rope(k, cos, sin)
    iq = apply_rope(iq.reshape(B, S, IDX_HEADS, IDX_DIM), icos, isin)
    ik = apply_rope(rmsnorm(ik, ik_norm_g), icos, isin)
    iw = iw * (IDX_HEADS ** -0.5 * IDX_DIM ** -0.5)
    top = min(IDX_TOPK, S // 4)
    kpos = jnp.arange(S)
    scale = DSA_DIM ** -0.5

    def block(qs):
        t = qs + jnp.arange(Q_BLOCK)
        qb = lax.dynamic_slice_in_dim(q, qs, Q_BLOCK, axis=1)
        iqb = lax.dynamic_slice_in_dim(iq, qs, Q_BLOCK, axis=1)
        iwb = lax.dynamic_slice_in_dim(iw, qs, Q_BLOCK, axis=1)
        rel = jax.nn.relu(jnp.einsum("bqhd,bkd->bqhk", iqb, ik).astype(jnp.float32))
        idx_score = jnp.einsum("bqh,bqhk->bqk", iwb.astype(jnp.float32), rel)
        idx_score = jnp.where((kpos[None, :] <= t[:, None])[None], idx_score, NEG_INF)
        _, sel = lax.top_k(idx_score, top)
        kg = jax.vmap(lambda kk, ii: kk[ii])(k, sel)
        vg = jax.vmap(lambda vv, ii: vv[ii])(v, sel)
        s = jnp.einsum("bqhd,bqkhd->bhqk", qb, kg) * scale
        p = masked_softmax(s, (sel <= t[None, :, None])[:, None])
        return jnp.einsum("bhqk,bqkhd->bqhd", p.astype(vg.dtype), vg)

    return sweep_query_blocks(block, S).reshape(B, S, H * DSA_DIM)


def setup_inputs(seed: int = 0) -> dict:
    key = jax.random.key(seed)
    ks = jax.random.split(key, 25)
    L = DEPTH

    def nrm(k, shape, scale):
        return jax.random.normal(k, shape, jnp.float32) * scale

    def gain(k, shape):
        return 1.0 + 0.02 * jax.random.normal(k, shape, jnp.float32)

    return {
        "x": nrm(ks[0], (BATCH, SEQ, D_MODEL), 1.0),
        "norm1_g": gain(ks[1], (L, D_MODEL)),
        "w_in": nrm(ks[2], (L, D_MODEL, D_IN), D_MODEL ** -0.5),
        "diff_lq1": nrm(ks[3], (L, DA_DIM), 0.1),
        "diff_lk1": nrm(ks[4], (L, DA_DIM), 0.1),
        "diff_lq2": nrm(ks[5], (L, DA_DIM), 0.1),
        "diff_lk2": nrm(ks[6], (L, DA_DIM), 0.1),
        "diff_subln_g": gain(ks[7], (L, 2 * DA_DIM)),
        "nsa_pe_k": nrm(ks[8], (L, CMP_LEN, NSA_DK), 0.1),
        "nsa_w1_k": nrm(ks[9], (L, CMP_LEN * NSA_DK, NSA_DK), (CMP_LEN * NSA_DK) ** -0.5),
        "nsa_w2_k": nrm(ks[10], (L, NSA_DK, NSA_DK), NSA_DK ** -0.5),
        "nsa_pe_v": nrm(ks[11], (L, CMP_LEN, NSA_DV), 0.1),
        "nsa_w1_v": nrm(ks[12], (L, CMP_LEN * NSA_DV, NSA_DV), (CMP_LEN * NSA_DV) ** -0.5),
        "nsa_w2_v": nrm(ks[13], (L, NSA_DV, NSA_DV), NSA_DV ** -0.5),
        "mla_q_norm_g": gain(ks[14], (L, MLA_Q_LORA)),
        "mla_w_uq": nrm(ks[15], (L, MLA_Q_LORA, MLA_HEADS * (MLA_NOPE + MLA_ROPE)), MLA_Q_LORA ** -0.5),
        "mla_kv_norm_g": gain(ks[16], (L, MLA_KV_LORA)),
        "mla_w_ukv": nrm(ks[17], (L, MLA_KV_LORA, MLA_HEADS * (MLA_NOPE + MLA_V)), MLA_KV_LORA ** -0.5),
        "idx_k_norm_g": gain(ks[18], (L, IDX_DIM)),
        "w_branch": nrm(ks[19], (L, N_BRANCH, BR_WIDTH, D_MODEL), BR_WIDTH ** -0.5),
        "w_out": nrm(ks[20], (L, D_MODEL, D_MODEL), D_MODEL ** -0.5),
        "norm2_g": gain(ks[21], (L, D_MODEL)),
        "w_gate_up": nrm(ks[22], (L, D_MODEL, 2 * D_FF), D_MODEL ** -0.5),
        "w_down": nrm(ks[23], (L, D_FF, D_MODEL), D_FF ** -0.5),
        "final_norm_g": gain(ks[24], (D_MODEL,)),
    }


def reference(x, norm1_g, w_in, diff_lq1, diff_lk1, diff_lq2, diff_lk2, diff_subln_g,
              nsa_pe_k, nsa_w1_k, nsa_w2_k, nsa_pe_v, nsa_w1_v, nsa_w2_v,
              mla_q_norm_g, mla_w_uq, mla_kv_norm_g, mla_w_ukv, idx_k_norm_g,
              w_branch, w_out, norm2_g, w_gate_up, w_down, final_norm_g):
    B, S, _ = x.shape
    dt = x.dtype
    rot_da = rope_tables(S, DA_ROT, dt)
    rot_nsa = rope_tables(S, NSA_ROT, dt)
    rot_mla = rope_tables(S, MLA_ROPE, dt)
    rot_dsa = rope_tables(S, DSA_ROT, dt)
    rot_idx = rope_tables(S, IDX_ROT, dt)

    for l in range(DEPTH):
        h = rmsnorm(x, norm1_g[l])
        p = split_columns(h @ w_in[l])

        lam_init = 0.8 - 0.6 * math.exp(-0.3 * l)
        lam = (jnp.exp(jnp.sum(diff_lq1[l] * diff_lk1[l]).astype(jnp.float32))
               - jnp.exp(jnp.sum(diff_lq2[l] * diff_lk2[l]).astype(jnp.float32)) + lam_init)
        o_a = diff_attention(p["a_q"].reshape(B, S, DA_HEADS, 2, DA_DIM),
                             p["a_k"].reshape(B, S, DA_HEADS, 2, DA_DIM),
                             p["a_v"].reshape(B, S, DA_HEADS, 2 * DA_DIM),
                             lam, lam_init, diff_subln_g[l], *rot_da)

        o_b = nsa_attention(p["b_q"].reshape(B, S, NSA_HEADS, NSA_DK),
                            p["b_kc"], p["b_vc"], p["b_ks"], p["b_vs"], p["b_kw"], p["b_vw"],
                            jax.nn.sigmoid(p["b_g"].reshape(B, S, NSA_HEADS, 3)),
                            nsa_pe_k[l], nsa_w1_k[l], nsa_w2_k[l],
                            nsa_pe_v[l], nsa_w1_v[l], nsa_w2_v[l], *rot_nsa)

        o_c = mla_attention(p["c_q"], p["c_kv"], p["c_kr"], mla_q_norm_g[l], mla_w_uq[l],
                            mla_kv_norm_g[l], mla_w_ukv[l], *rot_mla)

        o_d = dsa_attention(p["d_q"].reshape(B, S, DSA_HEADS, DSA_DIM),
                            p["d_k"].reshape(B, S, DSA_HEADS, DSA_DIM),
                            p["d_v"].reshape(B, S, DSA_HEADS, DSA_DIM),
                            p["d_iq"], p["d_ik"], p["d_iw"], idx_k_norm_g[l],
                            *rot_dsa, *rot_idx)

        br = jnp.stack([o_a, o_b, o_c, o_d], axis=2)
        br = jnp.einsum("bsnw,nwd->bsnd", br, w_branch[l])
        g = jax.nn.sigmoid(p["gate"].reshape(B, S, N_BRANCH, D_MODEL))
        x = x + jnp.sum(g * br, axis=2) @ w_out[l]

        h2 = rmsnorm(x, norm2_g[l])
        gate, up = jnp.split(h2 @ w_gate_up[l], 2, axis=-1)
        x = x + (jax.nn.silu(gate) * up) @ w_down[l]

    return rmsnorm(x, final_norm_g)
```

```python
import functools
import math

import numpy as np
import jax
import jax.numpy as jnp
from jax import lax
from jax.experimental import pallas as pl
from jax.experimental.pallas import tpu as pltpu

F32 = jnp.float32
BF16 = jnp.bfloat16
I32 = jnp.int32

LANES = 128
VMEM_LIMIT = 56 * 1024 * 1024

ROPE_THETA = 500000.0
NEG = -1e30
FORCE_SCORE = 1e9
PAD_SCORE = -3e38
EPS = 1e-6
INT_MIN = -2147483648

HEADS = 4
HEAD_W = 128
BR_WIDTH = HEADS * HEAD_W
DA_DIM = 64
NSA_DK = 128
CMP_LEN = 32
CMP_STRIDE = 16
SEL_LEN = 64
SEL_N = 16
WIN = 512
MLA_Q_LORA = 384
MLA_KV_LORA = 256
MLA_NOPE = 128
MLA_ROPE = 64
DSA_DIM = 128
IDX_HEADS = 8
IDX_DIM = 64
IDX_TOPK = 256

Z_AQ, Z_AK, Z_AV, Z_BQ, Z_DQ, Z_DK, Z_DV, Z_IQ = (i * 512 for i in range(8))
Z_CQ = 4096
Z_CKV = 4608
Z_KC, Z_KS, Z_KW, Z_VC, Z_VS, Z_VW, Z_KR, Z_IK, Z_SMALL = (4864 + i * 128 for i in range(9))
P_WIDTH = Z_SMALL
Z_WIDTH = 6144
SMALL_G = 0
SMALL_IW = 12

ROPE_KINDS = ((16, 64), (32, 128), (64, 64))
TAB_W = 3 * LANES


def _cparams(n_axes):
    return pltpu.CompilerParams(dimension_semantics=("arbitrary",) * n_axes,
                                vmem_limit_bytes=VMEM_LIMIT)


def _dot(a, b):
    return jnp.dot(a, b, preferred_element_type=F32)


def _dot_nt(a, b):
    return lax.dot_general(a, b, (((1,), (1,)), ((), ())), preferred_element_type=F32)


def _in_proj_columns():
    names = (("a_q", 512), ("a_k", 512), ("a_v", 512), ("b_q", 512),
             ("b_kc", 128), ("b_vc", 128), ("b_ks", 128), ("b_vs", 128),
             ("b_kw", 128), ("b_vw", 128), ("b_g", 12),
             ("c_q", 384), ("c_kv", 256), ("c_kr", 64),
             ("d_q", 512), ("d_k", 512), ("d_v", 512),
             ("d_iq", 512), ("d_ik", 64), ("d_iw", 8), ("gate", 4096))
    off, o = {}, 0
    for nm, n in names:
        off[nm] = (o, n)
        o += n
    idx = np.full((Z_WIDTH,), -1, np.int64)

    def put(dst, nm):
        s, n = off[nm]
        idx[dst:dst + n] = np.arange(s, s + n)

    put(Z_AQ, "a_q"); put(Z_AK, "a_k"); put(Z_AV, "a_v"); put(Z_BQ, "b_q")
    put(Z_DQ, "d_q"); put(Z_DK, "d_k"); put(Z_DV, "d_v"); put(Z_IQ, "d_iq")
    put(Z_CQ, "c_q"); put(Z_CKV, "c_kv")
    put(Z_KC, "b_kc"); put(Z_KS, "b_ks"); put(Z_KW, "b_kw")
    put(Z_VC, "b_vc"); put(Z_VS, "b_vs"); put(Z_VW, "b_vw")
    put(Z_KR, "c_kr"); put(Z_KR + 64, "c_kr")
    put(Z_IK, "d_ik"); put(Z_IK + 64, "d_ik")
    put(Z_SMALL + SMALL_G, "b_g"); put(Z_SMALL + SMALL_IW, "d_iw")
    return idx, off["gate"][0], o


def _take_cols(w, idx):
    safe = np.where(idx < 0, 0, idx)
    out = jnp.take(w, jnp.asarray(safe, I32), axis=1)
    return jnp.where(jnp.asarray(idx >= 0)[None, :], out, 0.0)


def _rope_table(seq, rot, period):
    half = rot // 2
    inv = jnp.power(jnp.float32(ROPE_THETA), -jnp.arange(0, rot, 2, dtype=F32) / rot)
    ang = jnp.arange(seq, dtype=F32)[:, None] * inv[None, :]
    cos, sin = jnp.cos(ang), jnp.sin(ang)
    lane = np.arange(LANES) % period
    in1 = lane < half
    in2 = (lane >= half) & (lane < 2 * half)
    fidx = np.where(in1, lane, np.where(in2, lane - half, 0))
    cosl, sinl = cos[:, fidx], sin[:, fidx]
    c = jnp.where(jnp.asarray(in1 | in2)[None], cosl, 1.0)
    s1 = jnp.where(jnp.asarray(in1)[None], -sinl, 0.0)
    s2 = jnp.where(jnp.asarray(in2)[None], sinl, 0.0)
    return jnp.concatenate([c, s1, s2], axis=1)


def _rope128(x, tab, half):
    return (x * tab[:, 0:LANES]
            + pltpu.roll(x, LANES - half, 1) * tab[:, LANES:2 * LANES]
            + pltpu.roll(x, half, 1) * tab[:, 2 * LANES:3 * LANES])


def _norm_matmul_kernel(x_ref, g_ref, w_ref, o_ref, h_scr, *, sigmoid):
    @pl.when(pl.program_id(1) == 0)
    def _():
        x = x_ref[...]
        ms = jnp.mean(x * x, axis=-1, keepdims=True)
        h_scr[...] = (x * lax.rsqrt(ms + EPS) * g_ref[...]).astype(BF16)

    z = _dot(h_scr[...], w_ref[...])
    if sigmoid:
        z = jax.nn.sigmoid(z)
    o_ref[...] = z.astype(o_ref.dtype)


def _norm_matmul(x2, g, w, *, out_dtype, sigmoid, tm, tn, name):
    m, d = x2.shape
    n = w.shape[1]
    return pl.pallas_call(
        functools.partial(_norm_matmul_kernel, sigmoid=sigmoid),
        out_shape=jax.ShapeDtypeStruct((m, n), out_dtype),
        grid=(m // tm, n // tn),
        in_specs=[pl.BlockSpec((tm, d), lambda i, j: (i, 0)),
                  pl.BlockSpec((1, d), lambda i, j: (0, 0)),
                  pl.BlockSpec((d, tn), lambda i, j: (0, j))],
        out_specs=pl.BlockSpec((tm, tn), lambda i, j: (i, j)),
        scratch_shapes=[pltpu.VMEM((tm, d), BF16)],
        compiler_params=_cparams(2),
        name=name,
    )(x2, g, w)


def _prep_kernel(z_ref, tab_ref, gq_ref, gkv_ref, gik_ref, p_ref, kc_ref, vc_ref, small_ref):
    def zc(c):
        return z_ref[:, c * LANES:(c + 1) * LANES]

    def tab(kind):
        return tab_ref[:, kind * TAB_W:(kind + 1) * TAB_W]

    def put(c, v):
        p_ref[:, c * LANES:(c + 1) * LANES] = v.astype(BF16)

    def rope(c, kind):
        return _rope128(zc(c), tab(kind), ROPE_KINDS[kind][0] // 2)

    for base, kind in ((Z_AQ, 0), (Z_AK, 0), (Z_BQ, 1), (Z_DQ, 1), (Z_DK, 1), (Z_IQ, 0)):
        for c in range(base // LANES, base // LANES + 4):
            put(c, rope(c, kind))
    for base in (Z_AV, Z_DV):
        for c in range(base // LANES, base // LANES + 4):
            put(c, zc(c))
    for base in (Z_VC, Z_VS, Z_VW):
        put(base // LANES, zc(base // LANES))
    for base in (Z_KC, Z_KS, Z_KW):
        put(base // LANES, rope(base // LANES, 1))
    put(Z_KR // LANES, rope(Z_KR // LANES, 2))
    kc_ref[...] = rope(Z_KC // LANES, 1).astype(BF16)
    vc_ref[...] = zc(Z_VC // LANES).astype(BF16)

    cq = z_ref[:, Z_CQ:Z_CQ + 512]
    ms = jnp.sum(cq * cq, axis=-1, keepdims=True) * (1.0 / MLA_Q_LORA)
    p_ref[:, Z_CQ:Z_CQ + 512] = (cq * lax.rsqrt(ms + EPS) * gq_ref[...]).astype(BF16)
    ckv = z_ref[:, Z_CKV:Z_CKV + MLA_KV_LORA]
    ms = jnp.mean(ckv * ckv, axis=-1, keepdims=True)
    p_ref[:, Z_CKV:Z_CKV + MLA_KV_LORA] = (ckv * lax.rsqrt(ms + EPS) * gkv_ref[...]).astype(BF16)

    ik = zc(Z_IK // LANES)
    ms = jnp.mean(ik * ik, axis=-1, keepdims=True)
    ikn = ik * lax.rsqrt(ms + EPS) * gik_ref[...]
    put(Z_IK // LANES, _rope128(ikn, tab(0), ROPE_KINDS[0][0] // 2))

    sm = zc(Z_SMALL // LANES)
    lane = lax.broadcasted_iota(I32, sm.shape, 1)
    iw_scale = IDX_HEADS ** -0.5 * IDX_DIM ** -0.5
    small_ref[...] = jnp.where(lane < SMALL_IW, jax.nn.sigmoid(sm), sm * iw_scale)


def _prep(z, tab, gq, gkv, gik, *, seq, ts):
    m = z.shape[0]
    spb = seq // ts
    return pl.pallas_call(
        _prep_kernel,
        out_shape=(jax.ShapeDtypeStruct((m, P_WIDTH), BF16),
                   jax.ShapeDtypeStruct((m, LANES), BF16),
                   jax.ShapeDtypeStruct((m, LANES), BF16),
                   jax.ShapeDtypeStruct((m, LANES), F32)),
        grid=(m // ts,),
        in_specs=[pl.BlockSpec((ts, Z_WIDTH), lambda i: (i, 0)),
                  pl.BlockSpec((ts, 3 * TAB_W), lambda i: (i % spb, 0)),
                  pl.BlockSpec((1, 512), lambda i: (0, 0)),
                  pl.BlockSpec((1, MLA_KV_LORA), lambda i: (0, 0)),
                  pl.BlockSpec((1, LANES), lambda i: (0, 0))],
        out_specs=(pl.BlockSpec((ts, P_WIDTH), lambda i: (i, 0)),
                   pl.BlockSpec((ts, LANES), lambda i: (i, 0)),
                   pl.BlockSpec((ts, LANES), lambda i: (i, 0)),
                   pl.BlockSpec((ts, LANES), lambda i: (i, 0))),
        compiler_params=_cparams(1),
        name="prep",
    )(z, tab, gq, gkv, gik)


def _online_step(s, v_tile, m_ref, l_ref, acc_ref):
    m_prev = m_ref[...]
    m_new = jnp.maximum(m_prev, jnp.max(s, axis=-1, keepdims=True))
    alpha = jnp.exp(m_prev - m_new)
    p = jnp.exp(s - m_new)
    l_ref[...] = alpha * l_ref[...] + jnp.sum(p, axis=-1, keepdims=True)
    acc_ref[...] = alpha * acc_ref[...] + _dot(p.astype(BF16), v_tile)
    m_ref[...] = m_new


def _init_softmax_state(m_ref, l_ref, acc_ref):
    m_ref[...] = jnp.full(m_ref.shape, NEG, F32)
    l_ref[...] = jnp.zeros(l_ref.shape, F32)
    acc_ref[...] = jnp.zeros(acc_ref.shape, F32)


def _causal_tiles(step, n_full):
    def body(j, carry):
        step(j, False)
        return carry
    lax.fori_loop(0, n_full, body, 0)
    step(n_full, True)


def _diff_attn_kernel(q_ref, k_ref, v_ref, lv_ref, g_ref, o_ref, m_scr, l_scr, acc_scr,
                      *, tq, tk, lam_init):
    qs = pl.program_id(1) * tq
    n_full = qs // tk
    scale = DA_DIM ** -0.5
    lv = lv_ref[...]
    lam = (jnp.exp(jnp.sum(lv[0:1] * lv[1:2], axis=-1, keepdims=True))
           - jnp.exp(jnp.sum(lv[2:3] * lv[3:4], axis=-1, keepdims=True)) + lam_init)
    lane = lax.broadcasted_iota(I32, (tq, HEAD_W), 1)
    row_t = qs + lax.broadcasted_iota(I32, (2 * tq, 1), 0) % tq
    col0 = lax.broadcasted_iota(I32, (2 * tq, tk), 1)

    for h in range(HEADS):
        hs = slice(h * HEAD_W, (h + 1) * HEAD_W)
        qh = q_ref[0, :, hs]
        zero = jnp.zeros_like(qh)
        q2 = jnp.concatenate([jnp.where(lane < DA_DIM, qh, zero),
                              jnp.where(lane >= DA_DIM, qh, zero)], axis=0)
        _init_softmax_state(m_scr, l_scr, acc_scr)

        def step(j, masked, hs=hs, q2=q2):
            ks = pl.multiple_of(j * tk, tk)
            s = _dot_nt(q2, k_ref[0, pl.ds(ks, tk), hs]) * scale
            if masked:
                s = jnp.where(col0 + ks <= row_t, s, NEG)
            _online_step(s, v_ref[0, pl.ds(ks, tk), hs], m_scr, l_scr, acc_scr)

        _causal_tiles(step, n_full)
        o2 = acc_scr[...] / l_scr[...]
        o = o2[:tq] - lam * o2[tq:]
        ms = jnp.mean(o * o, axis=-1, keepdims=True)
        o = o * lax.rsqrt(ms + EPS) * g_ref[...]
        o_ref[0, :, hs] = (o * (1.0 - lam_init)).astype(BF16)


def _diff_attn(p3, lv, g, *, lam_init, tq, tk):
    b, s, _ = p3.shape
    return pl.pallas_call(
        functools.partial(_diff_attn_kernel, tq=tq, tk=tk, lam_init=lam_init),
        out_shape=jax.ShapeDtypeStruct((b, s, BR_WIDTH), BF16),
        grid=(b, s // tq),
        in_specs=[pl.BlockSpec((1, tq, 512), lambda bi, i: (bi, i, Z_AQ // 512)),
                  pl.BlockSpec((1, s, 512), lambda bi, i: (bi, 0, Z_AK // 512)),
                  pl.BlockSpec((1, s, 512), lambda bi, i: (bi, 0, Z_AV // 512)),
                  pl.BlockSpec((4, DA_DIM), lambda bi, i: (0, 0)),
                  pl.BlockSpec((1, HEAD_W), lambda bi, i: (0, 0))],
        out_specs=pl.BlockSpec((1, tq, BR_WIDTH), lambda bi, i: (bi, i, 0)),
        scratch_shapes=[pltpu.VMEM((2 * tq, 1), F32), pltpu.VMEM((2 * tq, 1), F32),
                        pltpu.VMEM((2 * tq, HEAD_W), F32)],
        compiler_params=_cparams(2),
        name="diff_attn",
    )(p3, p3, p3, lv, g)


def _nsa_compress_kernel(gk_ref, gv_ref, w1k_ref, w1v_ref, pek_ref, pev_ref,
                         w1kf_ref, w1vf_ref, w2k_ref, w2v_ref, kc_ref, vc_ref):
    def one(g_ref, w1cat_ref, pe_ref, w1f_ref, w2_ref, o_ref):
        y = _dot(g_ref[0], w1cat_ref[...])
        n = y.shape[0]
        nxt = pltpu.roll(y[:, HEAD_W:], n - 1, 0)
        c = _dot(pe_ref[...], w1f_ref[...])[0:1]
        hid = jax.nn.gelu(y[:, :HEAD_W] + nxt + c)
        o_ref[0] = _dot(hid.astype(BF16), w2_ref[...]).astype(BF16)

    one(gk_ref, w1k_ref, pek_ref, w1kf_ref, w2k_ref, kc_ref)
    one(gv_ref, w1v_ref, pev_ref, w1vf_ref, w2v_ref, vc_ref)


def _nsa_compress(gk, gv, w1k_cat, w1v_cat, pek, pev, w1k, w1v, w2k, w2v):
    b, ng, gw = gk.shape
    full = lambda shape: pl.BlockSpec(shape, lambda bi: (0,) * len(shape))
    return pl.pallas_call(
        _nsa_compress_kernel,
        out_shape=(jax.ShapeDtypeStruct((b, ng, HEAD_W), BF16),
                   jax.ShapeDtypeStruct((b, ng, HEAD_W), BF16)),
        grid=(b,),
        in_specs=[pl.BlockSpec((1, ng, gw), lambda bi: (bi, 0, 0)),
                  pl.BlockSpec((1, ng, gw), lambda bi: (bi, 0, 0)),
                  full(w1k_cat.shape), full(w1v_cat.shape), full(pek.shape), full(pev.shape),
                  full(w1k.shape), full(w1v.shape), full(w2k.shape), full(w2v.shape)],
        out_specs=(pl.BlockSpec((1, ng, HEAD_W), lambda bi: (bi, 0, 0)),
                   pl.BlockSpec((1, ng, HEAD_W), lambda bi: (bi, 0, 0))),
        compiler_params=_cparams(1),
        name="nsa_compress",
    )(gk, gv, w1k_cat, w1v_cat, pek, pev, w1k, w1v, w2k, w2v)


def _nsa_kernel(q_ref, kc_ref, vc_ref, ks_ref, vs_ref, kw_ref, vw_ref, small_ref, ov_ref, e_ref,
                o_ref, m_scr, l_scr, acc_scr, *, tq, tk, seq):
    qs = pl.program_id(1) * tq
    scale = NSA_DK ** -0.5
    ns = seq // SEL_LEN
    n_sel = min(SEL_N, ns)
    r = HEADS * tq
    q4 = jnp.concatenate([q_ref[0, :, h * HEAD_W:(h + 1) * HEAD_W] for h in range(HEADS)], axis=0)
    t1 = qs + lax.broadcasted_iota(I32, (tq, 1), 0)
    t4 = qs + lax.broadcasted_iota(I32, (r, 1), 0) % tq

    kc = kc_ref[0]
    nc_pad = kc.shape[0]
    sc = _dot_nt(q4, kc) * scale
    c_end = lax.broadcasted_iota(I32, (r, nc_pad), 1) * CMP_STRIDE + (CMP_LEN - 1)
    cmask = c_end <= t4
    mx = jnp.max(jnp.where(cmask, sc, NEG), axis=-1, keepdims=True)
    e = jnp.where(cmask, jnp.exp(sc - mx), 0.0)
    den = jnp.sum(e, axis=-1, keepdims=True)
    pc = e / jnp.where(den > 0.0, den, 1.0)
    o_cmp = _dot(pc.astype(BF16), vc_ref[0])

    psum = pc[0:tq] + pc[tq:2 * tq] + pc[2 * tq:3 * tq] + pc[3 * tq:4 * tq]
    ov = ov_ref[...]
    hi = psum.astype(BF16)
    r1 = psum - hi.astype(F32)
    mid = r1.astype(BF16)
    lo = (r1 - mid.astype(F32)).astype(BF16)
    imp = _dot(hi, ov) + _dot(mid, ov) + _dot(lo, ov)

    blk = lax.broadcasted_iota(I32, (tq, LANES), 1)
    cur = t1 // SEL_LEN
    forced = (blk == 0) | (blk == cur) | (blk == cur - 1)
    visible = blk * SEL_LEN <= t1
    score = jnp.where(visible, jnp.where(forced, FORCE_SCORE, imp), NEG)
    score = jnp.where(blk < ns, score, PAD_SCORE)
    rank = jnp.zeros((tq, LANES), I32)
    for jp in range(ns):
        col = score[:, jp:jp + 1]
        later = (blk > jp).astype(I32)
        rank = rank + jnp.where(col > score, 1, jnp.where(col == score, later, 0))
    selb = jnp.where(rank < n_sel, 1.0, 0.0).astype(BF16)

    _init_softmax_state(m_scr, l_scr, acc_scr)
    col0 = lax.broadcasted_iota(I32, (r, tk), 1)

    def step(j, carry):
        ks0 = pl.multiple_of(j * tk, tk)
        mt = _dot(selb, e_ref[j])
        mt4 = jnp.concatenate([mt] * HEADS, axis=0)
        s = _dot_nt(q4, ks_ref[0, pl.ds(ks0, tk), :]) * scale
        s = jnp.where(mt4 > 0.5, s, NEG)
        s = jnp.where(col0 + ks0 <= t4, s, NEG)
        _online_step(s, vs_ref[0, pl.ds(ks0, tk), :], m_scr, l_scr, acc_scr)
        return carry

    lax.fori_loop(0, qs // tk + 1, step, 0)
    o_slc = acc_scr[...] / l_scr[...]

    wspan = WIN + tq
    start = pl.multiple_of(jnp.maximum(qs - WIN, 0), tq)
    sw = _dot_nt(q4, kw_ref[0, pl.ds(start, wspan), :]) * scale
    dist = t4 - (start + lax.broadcasted_iota(I32, (r, wspan), 1))
    wmask = (dist >= 0) & (dist < WIN)
    mx = jnp.max(jnp.where(wmask, sw, NEG), axis=-1, keepdims=True)
    e = jnp.where(wmask, jnp.exp(sw - mx), 0.0)
    pw = e / jnp.sum(e, axis=-1, keepdims=True)
    o_win = _dot(pw.astype(BF16), vw_ref[0, pl.ds(start, wspan), :])

    gates = small_ref[0]
    for h in range(HEADS):
        rows = slice(h * tq, (h + 1) * tq)
        g0 = gates[:, SMALL_G + 3 * h:SMALL_G + 3 * h + 1]
        g1 = gates[:, SMALL_G + 3 * h + 1:SMALL_G + 3 * h + 2]
        g2 = gates[:, SMALL_G + 3 * h + 2:SMALL_G + 3 * h + 3]
        o = g0 * o_cmp[rows] + g1 * o_slc[rows] + g2 * o_win[rows]
        o_ref[0, :, h * HEAD_W:(h + 1) * HEAD_W] = o.astype(BF16)


def _nsa(p3, kc, vc, small3, ov, emat, *, tq, tk):
    b, s, _ = p3.shape
    ng = kc.shape[1]
    col = lambda off: (lambda bi, i: (bi, 0, off // LANES))
    return pl.pallas_call(
        functools.partial(_nsa_kernel, tq=tq, tk=tk, seq=s),
        out_shape=jax.ShapeDtypeStruct((b, s, BR_WIDTH), BF16),
        grid=(b, s // tq),
        in_specs=[pl.BlockSpec((1, tq, 512), lambda bi, i: (bi, i, Z_BQ // 512)),
                  pl.BlockSpec((1, ng, HEAD_W), lambda bi, i: (bi, 0, 0)),
                  pl.BlockSpec((1, ng, HEAD_W), lambda bi, i: (bi, 0, 0)),
                  pl.BlockSpec((1, s, LANES), col(Z_KS)),
                  pl.BlockSpec((1, s, LANES), col(Z_VS)),
                  pl.BlockSpec((1, s, LANES), col(Z_KW)),
                  pl.BlockSpec((1, s, LANES), col(Z_VW)),
                  pl.BlockSpec((1, tq, LANES), lambda bi, i: (bi, i, 0)),
                  pl.BlockSpec(ov.shape, lambda bi, i: (0, 0)),
                  pl.BlockSpec(emat.shape, lambda bi, i: (0, 0, 0))],
        out_specs=pl.BlockSpec((1, tq, BR_WIDTH), lambda bi, i: (bi, i, 0)),
        scratch_shapes=[pltpu.VMEM((HEADS * tq, 1), F32), pltpu.VMEM((HEADS * tq, 1), F32),
                        pltpu.VMEM((HEADS * tq, HEAD_W), F32)],
        compiler_params=_cparams(2),
        name="nsa_attn",
    )(p3, kc, vc, p3, p3, p3, p3, small3, ov, emat)


def _mla_up_kernel(p_ref, ckv_ref, tab_ref, wq_ref, wkv_ref, q_ref, kv_ref):
    q = _dot(p_ref[...], wq_ref[...])
    nn = HEADS * MLA_NOPE
    q_ref[:, :nn] = q[:, :nn].astype(BF16)
    for c in range(nn // LANES, (nn + HEADS * MLA_ROPE) // LANES):
        tile = _rope128(q[:, c * LANES:(c + 1) * LANES], tab_ref[...], MLA_ROPE // 2)
        q_ref[:, c * LANES:(c + 1) * LANES] = tile.astype(BF16)
    kv_ref[...] = _dot(ckv_ref[...], wkv_ref[...]).astype(BF16)


def _mla_up(p2, tab, wq, wkv, *, seq, ts):
    m = p2.shape[0]
    spb = seq // ts
    nq = wq.shape[1]
    nkv = wkv.shape[1]
    return pl.pallas_call(
        _mla_up_kernel,
        out_shape=(jax.ShapeDtypeStruct((m, nq), BF16), jax.ShapeDtypeStruct((m, nkv), BF16)),
        grid=(m // ts,),
        in_specs=[pl.BlockSpec((ts, 512), lambda i: (i, Z_CQ // 512)),
                  pl.BlockSpec((ts, MLA_KV_LORA), lambda i: (i, Z_CKV // MLA_KV_LORA)),
                  pl.BlockSpec((ts, TAB_W), lambda i: (i % spb, 2)),
                  pl.BlockSpec(wq.shape, lambda i: (0, 0)),
                  pl.BlockSpec(wkv.shape, lambda i: (0, 0))],
        out_specs=(pl.BlockSpec((ts, nq), lambda i: (i, 0)),
                   pl.BlockSpec((ts, nkv), lambda i: (i, 0))),
        compiler_params=_cparams(1),
        name="mla_up",
    )(p2, p2, tab, wq, wkv)


def _mla_attn_kernel(qn_ref, qr_ref, kn_ref, kr_ref, v_ref, o_ref, m_scr, l_scr, acc_scr, *, tq, tk):
    qs = pl.program_id(1) * tq
    n_full = qs // tk
    scale = (MLA_NOPE + MLA_ROPE) ** -0.5
    lane = lax.broadcasted_iota(I32, (tq, LANES), 1)
    row_t = qs + lax.broadcasted_iota(I32, (tq, 1), 0)
    col0 = lax.broadcasted_iota(I32, (tq, tk), 1)

    for h in range(HEADS):
        hs = slice(h * HEAD_W, (h + 1) * HEAD_W)
        qn = qn_ref[0, :, hs]
        pair = qr_ref[0, :, (h // 2) * LANES:(h // 2 + 1) * LANES]
        keep = (lane < MLA_ROPE) if h % 2 == 0 else (lane >= MLA_ROPE)
        qr = jnp.where(keep, pair, jnp.zeros_like(pair))
        _init_softmax_state(m_scr, l_scr, acc_scr)

        def step(j, masked, hs=hs, qn=qn, qr=qr):
            ks = pl.multiple_of(j * tk, tk)
            s = (_dot_nt(qn, kn_ref[0, pl.ds(ks, tk), hs])
                 + _dot_nt(qr, kr_ref[0, pl.ds(ks, tk), :])) * scale
            if masked:
                s = jnp.where(col0 + ks <= row_t, s, NEG)
            _online_step(s, v_ref[0, pl.ds(ks, tk), hs], m_scr, l_scr, acc_scr)

        _causal_tiles(step, n_full)
        o_ref[0, :, hs] = (acc_scr[...] / l_scr[...]).astype(BF16)


def _mla_attn(q3, kv3, p3, *, tq, tk):
    b, s, _ = q3.shape
    return pl.pallas_call(
        functools.partial(_mla_attn_kernel, tq=tq, tk=tk),
        out_shape=jax.ShapeDtypeStruct((b, s, BR_WIDTH), BF16),
        grid=(b, s // tq),
        in_specs=[pl.BlockSpec((1, tq, 512), lambda bi, i: (bi, i, 0)),
                  pl.BlockSpec((1, tq, 256), lambda bi, i: (bi, i, 2)),
                  pl.BlockSpec((1, s, 512), lambda bi, i: (bi, 0, 0)),
                  pl.BlockSpec((1, s, LANES), lambda bi, i: (bi, 0, Z_KR // LANES)),
                  pl.BlockSpec((1, s, 512), lambda bi, i: (bi, 0, 1))],
        out_specs=pl.BlockSpec((1, tq, BR_WIDTH), lambda bi, i: (bi, i, 0)),
        scratch_shapes=[pltpu.VMEM((tq, 1), F32), pltpu.VMEM((tq, 1), F32),
                        pltpu.VMEM((tq, HEAD_W), F32)],
        compiler_params=_cparams(2),
        name="mla_attn",
    )(q3, q3, kv3, p3, kv3)


def _sortable_key(x):
    bits = pltpu.bitcast(x + 0.0, I32)
    return bits ^ (lax.shift_right_arithmetic(bits, 31) & 0x7FFFFFFF)


def _dsa_kernel(q_ref, k_ref, v_ref, iq_ref, ik_ref, small_ref, o_ref,
                key_scr, m_scr, l_scr, acc_scr, *, tq, tk, top):
    qs = pl.program_id(1) * tq
    n_tiles = qs // tk + 1
    scale = DSA_DIM ** -0.5
    lane = lax.broadcasted_iota(I32, (tq, LANES), 1)
    row_t = qs + lax.broadcasted_iota(I32, (tq, 1), 0)
    col0 = lax.broadcasted_iota(I32, (tq, tk), 1)
    small = small_ref[0]

    iq_m, iw = [], []
    for h in range(IDX_HEADS):
        pair = iq_ref[0, :, (h // 2) * LANES:(h // 2 + 1) * LANES]
        keep = (lane < IDX_DIM) if h % 2 == 0 else (lane >= IDX_DIM)
        iq_m.append(jnp.where(keep, pair, jnp.zeros_like(pair)))
        iw.append(small[:, SMALL_IW + h:SMALL_IW + h + 1])

    def score_tile(j, carry):
        ks = pl.multiple_of(j * tk, tk)
        ikt = ik_ref[0, pl.ds(ks, tk), :]
        acc = jnp.zeros((tq, tk), F32)
        for h in range(IDX_HEADS):
            acc = acc + iw[h] * jnp.maximum(_dot_nt(iq_m[h], ikt), 0.0)
        key = jnp.where(col0 + ks <= row_t, _sortable_key(acc), INT_MIN)
        key_scr[j] = key
        return carry

    lax.fori_loop(0, n_tiles, score_tile, 0)

    def bit_body(i, theta):
        cand = theta + lax.shift_left(jnp.int32(1), 31 - i)

        def count_tile(j, cnt):
            ks = pl.multiple_of(j * tk, tk)
            ge = (key_scr[j] >= cand).astype(I32)
            for c in range(tk // LANES):
                cnt = cnt + ge[:, c * LANES:(c + 1) * LANES]
            return cnt

        cnt = lax.fori_loop(0, n_tiles, count_tile, jnp.zeros((tq, LANES), I32))
        total = jnp.sum(cnt, axis=-1, keepdims=True)
        return jnp.where(total >= top, cand, theta)

    theta = lax.fori_loop(0, 32, bit_body, jnp.full((tq, 1), INT_MIN, I32))
    theta = jnp.maximum(theta, INT_MIN + 1)

    for h in range(HEADS):
        m_scr[h] = jnp.full((tq, 1), NEG, F32)
        l_scr[h] = jnp.zeros((tq, 1), F32)
        acc_scr[h] = jnp.zeros((tq, HEAD_W), F32)

    def attn_tile(j, carry):
        ks = pl.multiple_of(j * tk, tk)
        sel = key_scr[j] >= theta
        for h in range(HEADS):
            hs = slice(h * HEAD_W, (h + 1) * HEAD_W)
            s = _dot_nt(q_ref[0, :, hs], k_ref[0, pl.ds(ks, tk), hs]) * scale
            s = jnp.where(sel, s, NEG)
            _online_step(s, v_ref[0, pl.ds(ks, tk), hs], m_scr.at[h], l_scr.at[h], acc_scr.at[h])
        return carry

    lax.fori_loop(0, n_tiles, attn_tile, 0)
    for h in range(HEADS):
        o_ref[0, :, h * HEAD_W:(h + 1) * HEAD_W] = (acc_scr[h] / l_scr[h]).astype(BF16)


def _dsa(p3, small3, *, tq, tk, top):
    b, s, _ = p3.shape
    return pl.pallas_call(
        functools.partial(_dsa_kernel, tq=tq, tk=tk, top=top),
        out_shape=jax.ShapeDtypeStruct((b, s, BR_WIDTH), BF16),
        grid=(b, s // tq),
        in_specs=[pl.BlockSpec((1, tq, 512), lambda bi, i: (bi, i, Z_DQ // 512)),
                  pl.BlockSpec((1, s, 512), lambda bi, i: (bi, 0, Z_DK // 512)),
                  pl.BlockSpec((1, s, 512), lambda bi, i: (bi, 0, Z_DV // 512)),
                  pl.BlockSpec((1, tq, 512), lambda bi, i: (bi, i, Z_IQ // 512)),
                  pl.BlockSpec((1, s, LANES), lambda bi, i: (bi, 0, Z_IK // LANES)),
                  pl.BlockSpec((1, tq, LANES), lambda bi, i: (bi, i, 0))],
        out_specs=pl.BlockSpec((1, tq, BR_WIDTH), lambda bi, i: (bi, i, 0)),
        scratch_shapes=[pltpu.VMEM((s // tk, tq, tk), I32),
                        pltpu.VMEM((HEADS, tq, 1), F32), pltpu.VMEM((HEADS, tq, 1), F32),
                        pltpu.VMEM((HEADS, tq, HEAD_W), F32)],
        compiler_params=_cparams(2),
        name="dsa_attn",
    )(p3, p3, p3, p3, p3, small3)


def _merge_kernel(x_ref, oa_ref, ob_ref, oc_ref, od_ref, g_ref, wb_ref, wo_ref, o_ref):
    d = x_ref.shape[1]
    acc = jnp.zeros(x_ref.shape, F32)
    for n, br_ref in enumerate((oa_ref, ob_ref, oc_ref, od_ref)):
        br = _dot(br_ref[...], wb_ref[n])
        acc = acc + g_ref[:, n * d:(n + 1) * d].astype(F32) * br
    o_ref[...] = x_ref[...] + _dot(acc.astype(BF16), wo_ref[...])


def _merge(x2, oa, ob, oc, od, gates, wb, wo, *, tm):
    m, d = x2.shape
    row = lambda w: pl.BlockSpec((tm, w), lambda i: (i, 0))
    return pl.pallas_call(
        _merge_kernel,
        out_shape=jax.ShapeDtypeStruct((m, d), F32),
        grid=(m // tm,),
        in_specs=[row(d), row(BR_WIDTH), row(BR_WIDTH), row(BR_WIDTH), row(BR_WIDTH),
                  row(gates.shape[1]),
                  pl.BlockSpec(wb.shape, lambda i: (0, 0, 0)),
                  pl.BlockSpec(wo.shape, lambda i: (0, 0))],
        out_specs=row(d),
        compiler_params=_cparams(1),
        name="merge",
    )(x2, oa, ob, oc, od, gates, wb, wo)


def _ffn_kernel(x_ref, g_ref, wg_ref, wu_ref, wd_ref, gf_ref, o_ref, h_scr, acc_scr, *, final):
    j = pl.program_id(1)

    @pl.when(j == 0)
    def _():
        x = x_ref[...]
        ms = jnp.mean(x * x, axis=-1, keepdims=True)
        h_scr[...] = (x * lax.rsqrt(ms + EPS) * g_ref[...]).astype(BF16)
        acc_scr[...] = jnp.zeros(acc_scr.shape, F32)

    h = h_scr[...]
    a = jax.nn.silu(_dot(h, wg_ref[...])) * _dot(h, wu_ref[...])
    acc_scr[...] += _dot(a.astype(BF16), wd_ref[...])

    @pl.when(j == pl.num_programs(1) - 1)
    def _():
        y = x_ref[...] + acc_scr[...]
        if final:
            ms = jnp.mean(y * y, axis=-1, keepdims=True)
            y = y * lax.rsqrt(ms + EPS) * gf_ref[...]
        o_ref[...] = y


def _ffn(x2, g, wg, wu, wd, gf, *, final, tm, tf):
    m, d = x2.shape
    dff = wg.shape[1]
    return pl.pallas_call(
        functools.partial(_ffn_kernel, final=final),
        out_shape=jax.ShapeDtypeStruct((m, d), F32),
        grid=(m // tm, dff // tf),
        in_specs=[pl.BlockSpec((tm, d), lambda i, j: (i, 0)),
                  pl.BlockSpec((1, d), lambda i, j: (0, 0)),
                  pl.BlockSpec((d, tf), lambda i, j: (0, j)),
                  pl.BlockSpec((d, tf), lambda i, j: (0, j)),
                  pl.BlockSpec((tf, d), lambda i, j: (j, 0)),
                  pl.BlockSpec((1, d), lambda i, j: (0, 0))],
        out_specs=pl.BlockSpec((tm, d), lambda i, j: (i, 0)),
        scratch_shapes=[pltpu.VMEM((tm, d), BF16), pltpu.VMEM((tm, d), F32)],
        compiler_params=_cparams(2),
        name="ffn",
    )(x2, g, wg, wu, wd, gf)


def _tiles(seq, m, dff):
    pick = lambda n, cands: next(c for c in cands if n % c == 0)
    return dict(
        proj_tm=pick(m, (512, 256, 128)), proj_tn=1024,
        prep_ts=pick(seq, (256, 128)),
        attn_tq=pick(seq, (256, 128)), attn_tk=pick(seq, (512, 256, 128)),
        sparse_tq=128, sparse_tk=pick(seq, (512, 256, 128)),
        row_tm=pick(m, (512, 256, 128)),
        ffn_tf=pick(dff, (1408, 704, 256, 128)),
    )


def kernel(x, norm1_g, w_in, diff_lq1, diff_lk1, diff_lq2, diff_lk2, diff_subln_g, nsa_pe_k, nsa_w1_k, nsa_w2_k, nsa_pe_v, nsa_w1_v, nsa_w2_v, mla_q_norm_g, mla_w_uq, mla_kv_norm_g, mla_w_ukv, idx_k_norm_g, w_branch, w_out, norm2_g, w_gate_up, w_down, final_norm_g):
    b, seq, d = x.shape
    depth = w_in.shape[0]
    m = b * seq
    dff = w_down.shape[1]
    t = _tiles(seq, m, dff)
    assert seq % SEL_LEN == 0 and seq >= WIN + t["sparse_tq"] and seq // SEL_LEN <= LANES

    col_idx, gate_off, d_in = _in_proj_columns()
    assert w_in.shape[2] == d_in
    tab = jnp.concatenate([_rope_table(seq, rot, per) for rot, per in ROPE_KINDS], axis=1)

    ng = seq // CMP_STRIDE
    ns = seq // SEL_LEN
    c_start = np.arange(ng)[:, None] * CMP_STRIDE
    s_start = np.arange(LANES)[None, :] * SEL_LEN
    ov = ((c_start < s_start + SEL_LEN) & (c_start + CMP_LEN - 1 >= s_start)
          & (np.arange(LANES)[None, :] < ns))
    ov = jnp.asarray(ov, BF16)
    stk = t["sparse_tk"]
    emat = np.arange(LANES)[:, None] == (np.arange(seq)[None, :] // SEL_LEN)
    emat = jnp.asarray(emat.reshape(LANES, seq // stk, stk).transpose(1, 0, 2), BF16)

    qd = MLA_NOPE + MLA_ROPE
    uq_idx = np.concatenate([np.concatenate([np.arange(h * qd, h * qd + MLA_NOPE) for h in range(HEADS)]),
                             np.concatenate([np.arange(h * qd + MLA_NOPE, (h + 1) * qd) for h in range(HEADS)])])
    kvd = MLA_NOPE + HEAD_W
    ukv_idx = np.concatenate([np.concatenate([np.arange(h * kvd, h * kvd + MLA_NOPE) for h in range(HEADS)]),
                              np.concatenate([np.arange(h * kvd + MLA_NOPE, (h + 1) * kvd) for h in range(HEADS)])])

    x2 = x.reshape(m, d)
    half_w1 = CMP_STRIDE * NSA_DK
    for l in range(depth):
        lam_init = 0.8 - 0.6 * math.exp(-0.3 * l)
        w_mix = _take_cols(w_in[l], col_idx).astype(BF16)
        w_gate = w_in[l][:, gate_off:].astype(BF16)
        z = _norm_matmul(x2, norm1_g[l][None], w_mix, out_dtype=F32, sigmoid=False,
                         tm=t["proj_tm"], tn=t["proj_tn"], name="in_proj")
        gates = _norm_matmul(x2, norm1_g[l][None], w_gate, out_dtype=BF16, sigmoid=True,
                             tm=t["proj_tm"], tn=t["proj_tn"], name="gate_proj")

        gq = jnp.pad(mla_q_norm_g[l], (0, 512 - MLA_Q_LORA))[None]
        gkv = mla_kv_norm_g[l][None]
        gik = jnp.concatenate([idx_k_norm_g[l], idx_k_norm_g[l]])[None]
        p2, kc_tok, vc_tok, small = _prep(z, tab, gq, gkv, gik, seq=seq, ts=t["prep_ts"])
        p3 = p2.reshape(b, seq, P_WIDTH)
        small3 = small.reshape(b, seq, LANES)

        lv = jnp.stack([diff_lq1[l], diff_lk1[l], diff_lq2[l], diff_lk2[l]])
        o_a = _diff_attn(p3, lv, diff_subln_g[l][None], lam_init=lam_init,
                         tq=t["attn_tq"], tk=t["attn_tk"])

        w1k, w1v = nsa_w1_k[l].astype(BF16), nsa_w1_v[l].astype(BF16)
        w1k_cat = jnp.concatenate([w1k[:half_w1], w1k[half_w1:]], axis=1)
        w1v_cat = jnp.concatenate([w1v[:half_w1], w1v[half_w1:]], axis=1)
        pek = jnp.broadcast_to(nsa_pe_k[l].reshape(1, -1), (8, CMP_LEN * NSA_DK)).astype(BF16)
        pev = jnp.broadcast_to(nsa_pe_v[l].reshape(1, -1), (8, CMP_LEN * NSA_DK)).astype(BF16)
        kc, vc = _nsa_compress(kc_tok.reshape(b, ng, half_w1), vc_tok.reshape(b, ng, half_w1),
                               w1k_cat, w1v_cat, pek, pev, w1k, w1v,
                               nsa_w2_k[l].astype(BF16), nsa_w2_v[l].astype(BF16))
        o_b = _nsa(p3, kc, vc, small3, ov, emat, tq=t["sparse_tq"], tk=t["sparse_tk"])

        wq = jnp.pad(jnp.take(mla_w_uq[l], jnp.asarray(uq_idx, I32), axis=1),
                     ((0, 512 - MLA_Q_LORA), (0, 0))).astype(BF16)
        wkv = jnp.take(mla_w_ukv[l], jnp.asarray(ukv_idx, I32), axis=1).astype(BF16)
        q_c, kv_c = _mla_up(p2, tab, wq, wkv, seq=seq, ts=t["prep_ts"])
        o_c = _mla_attn(q_c.reshape(b, seq, -1), kv_c.reshape(b, seq, -1), p3,
                        tq=t["attn_tq"], tk=t["attn_tk"])

        o_d = _dsa(p3, small3, tq=t["sparse_tq"], tk=t["sparse_tk"], top=min(IDX_TOPK, seq // 4))

        x2 = _merge(x2, o_a.reshape(m, -1), o_b.reshape(m, -1), o_c.reshape(m, -1), o_d.reshape(m, -1),
                    gates, w_branch[l].astype(BF16), w_out[l].astype(BF16), tm=t["row_tm"])
        wgu = w_gate_up[l].astype(BF16)
        x2 = _ffn(x2, norm2_g[l][None], wgu[:, :dff], wgu[:, dff:], w_down[l].astype(BF16),
                  final_norm_g[None], final=(l == depth - 1), tm=t["row_tm"], tf=t["ffn_tf"])
    return x2.reshape(b, seq, d)
```

```python
import functools
import math

import numpy as np
import jax
import jax.numpy as jnp
from jax import lax
from jax.experimental import pallas as pl
from jax.experimental.pallas import tpu as pltpu

F32 = jnp.float32
BF16 = jnp.bfloat16
I32 = jnp.int32

LANES = 128
SUBLANES = 8
VMEM_LIMIT = 56 * 1024 * 1024

ROPE_THETA = 500000.0
NEG = -1e30
FORCE_SCORE = 1e9
PAD_SCORE = -3e38
EPS = 1e-6
INT_MIN = -2147483648

HEADS = 4
HEAD_W = 128
BR_WIDTH = HEADS * HEAD_W
DA_DIM = 64
NSA_DK = 128
CMP_LEN = 32
CMP_STRIDE = 16
SEL_LEN = 64
SEL_N = 16
WIN = 512
MLA_Q_LORA = 384
MLA_KV_LORA = 256
MLA_NOPE = 128
MLA_ROPE = 64
DSA_DIM = 128
IDX_HEADS = 8
IDX_DIM = 64
IDX_TOPK = 256

Z_AQ, Z_AK, Z_AV, Z_BQ, Z_DQ, Z_DK, Z_DV, Z_IQ = (i * 512 for i in range(8))
Z_CQ = 4096
Z_CKV = 4608
Z_KC, Z_KS, Z_KW, Z_VC, Z_VS, Z_VW, Z_KR, Z_IK, Z_SMALL = (4864 + i * 128 for i in range(9))
Z_WIDTH = 6144
P_AQ, P_AK, P_AV, P_BQ, P_DK, P_CQ = (i * 512 for i in range(6))
P_CKV = 3072
P_KS, P_KW, P_VS, P_VW, P_KR, P_IK = (3328 + i * 128 for i in range(6))
P_WIDTH = 4096
T_DQ, T_IQ = 0, 512
T_ROWS = 1024
VT_TILE = 256
SMALL_G = 0
SMALL_IW = 12

ROPE_KINDS = ((16, 64), (32, 128), (64, 64))
TAB_W = 3 * LANES


def _cparams(n_axes):
    return pltpu.CompilerParams(dimension_semantics=("arbitrary",) * n_axes,
                                vmem_limit_bytes=VMEM_LIMIT)


def _dot(a, b):
    return jnp.dot(a, b, preferred_element_type=F32)


def _dot_nt(a, b):
    return lax.dot_general(a, b, (((1,), (1,)), ((), ())), preferred_element_type=F32)


def _in_proj_columns():
    names = (("a_q", 512), ("a_k", 512), ("a_v", 512), ("b_q", 512),
             ("b_kc", 128), ("b_vc", 128), ("b_ks", 128), ("b_vs", 128),
             ("b_kw", 128), ("b_vw", 128), ("b_g", 12),
             ("c_q", 384), ("c_kv", 256), ("c_kr", 64),
             ("d_q", 512), ("d_k", 512), ("d_v", 512),
             ("d_iq", 512), ("d_ik", 64), ("d_iw", 8), ("gate", 4096))
    off, o = {}, 0
    for nm, n in names:
        off[nm] = (o, n)
        o += n
    idx = np.full((Z_WIDTH,), -1, np.int64)

    def put(dst, nm):
        s, n = off[nm]
        idx[dst:dst + n] = np.arange(s, s + n)

    put(Z_AQ, "a_q"); put(Z_AK, "a_k"); put(Z_AV, "a_v"); put(Z_BQ, "b_q")
    put(Z_DQ, "d_q"); put(Z_DK, "d_k"); put(Z_DV, "d_v"); put(Z_IQ, "d_iq")
    put(Z_CQ, "c_q"); put(Z_CKV, "c_kv")
    put(Z_KC, "b_kc"); put(Z_KS, "b_ks"); put(Z_KW, "b_kw")
    put(Z_VC, "b_vc"); put(Z_VS, "b_vs"); put(Z_VW, "b_vw")
    put(Z_KR, "c_kr"); put(Z_KR + 64, "c_kr")
    put(Z_IK, "d_ik"); put(Z_IK + 64, "d_ik")
    put(Z_SMALL + SMALL_G, "b_g"); put(Z_SMALL + SMALL_IW, "d_iw")
    return idx, off["gate"][0], o


def _take_cols(w, idx):
    safe = np.where(idx < 0, 0, idx)
    out = jnp.take(w, jnp.asarray(safe, I32), axis=1)
    return jnp.where(jnp.asarray(idx >= 0)[None, :], out, 0.0)


def _rope_table(seq, rot, period):
    half = rot // 2
    inv = jnp.power(jnp.float32(ROPE_THETA), -jnp.arange(0, rot, 2, dtype=F32) / rot)
    ang = jnp.arange(seq, dtype=F32)[:, None] * inv[None, :]
    cos, sin = jnp.cos(ang), jnp.sin(ang)
    lane = np.arange(LANES) % period
    in1 = lane < half
    in2 = (lane >= half) & (lane < 2 * half)
    fidx = np.where(in1, lane, np.where(in2, lane - half, 0))
    cosl, sinl = cos[:, fidx], sin[:, fidx]
    c = jnp.where(jnp.asarray(in1 | in2)[None], cosl, 1.0)
    s1 = jnp.where(jnp.asarray(in1)[None], -sinl, 0.0)
    s2 = jnp.where(jnp.asarray(in2)[None], sinl, 0.0)
    return jnp.concatenate([c, s1, s2], axis=1)


def _rope128(x, tab, half):
    return (x * tab[:, 0:LANES]
            + pltpu.roll(x, LANES - half, 1) * tab[:, LANES:2 * LANES]
            + pltpu.roll(x, half, 1) * tab[:, 2 * LANES:3 * LANES])


def _norm_matmul_kernel(x_ref, g_ref, w_ref, o_ref, h_scr, *, sigmoid):
    @pl.when(pl.program_id(1) == 0)
    def _():
        x = x_ref[...]
        ms = jnp.mean(x * x, axis=-1, keepdims=True)
        h_scr[...] = (x * lax.rsqrt(ms + EPS) * g_ref[...]).astype(BF16)

    z = _dot(h_scr[...], w_ref[...])
    if sigmoid:
        z = jax.nn.sigmoid(z)
    o_ref[...] = z.astype(o_ref.dtype)


def _norm_matmul(x2, g, w, *, out_dtype, sigmoid, tm, tn, name):
    m, d = x2.shape
    n = w.shape[1]
    return pl.pallas_call(
        functools.partial(_norm_matmul_kernel, sigmoid=sigmoid),
        out_shape=jax.ShapeDtypeStruct((m, n), out_dtype),
        grid=(m // tm, n // tn),
        in_specs=[pl.BlockSpec((tm, d), lambda i, j: (i, 0)),
                  pl.BlockSpec((1, d), lambda i, j: (0, 0)),
                  pl.BlockSpec((d, tn), lambda i, j: (0, j))],
        out_specs=pl.BlockSpec((tm, tn), lambda i, j: (i, j)),
        scratch_shapes=[pltpu.VMEM((tm, d), BF16)],
        compiler_params=_cparams(2),
        name=name,
    )(x2, g, w)


def _prep_kernel(z_ref, tab_ref, gq_ref, gkv_ref, gik_ref,
                 p_ref, t_ref, vt_ref, kc_ref, vc_ref, small_ref, smallt_ref):
    def zc(off, c=0):
        return z_ref[:, off + c * LANES:off + (c + 1) * LANES]

    def tab(kind):
        return tab_ref[:, kind * TAB_W:(kind + 1) * TAB_W]

    def put(off, c, v):
        p_ref[:, off + c * LANES:off + (c + 1) * LANES] = v.astype(BF16)

    def rope(off, c, kind):
        return _rope128(zc(off, c), tab(kind), ROPE_KINDS[kind][0] // 2)

    for zoff, poff, kind in ((Z_AQ, P_AQ, 0), (Z_AK, P_AK, 0), (Z_BQ, P_BQ, 1), (Z_DK, P_DK, 1)):
        for c in range(4):
            put(poff, c, rope(zoff, c, kind))
    for c in range(4):
        put(P_AV, c, zc(Z_AV, c))
    put(P_VS, 0, zc(Z_VS)); put(P_VW, 0, zc(Z_VW))
    put(P_KS, 0, rope(Z_KS, 0, 1)); put(P_KW, 0, rope(Z_KW, 0, 1))
    put(P_KR, 0, rope(Z_KR, 0, 2))
    kc_ref[...] = rope(Z_KC, 0, 1).astype(BF16)
    vc_ref[...] = zc(Z_VC).astype(BF16)

    for zoff, toff, kind in ((Z_DQ, T_DQ, 1), (Z_IQ, T_IQ, 0)):
        for c in range(4):
            t_ref[0, toff + c * LANES:toff + (c + 1) * LANES, :] = rope(zoff, c, kind).T.astype(BF16)
    for c in range(4):
        vt_ref[0, 0, c * LANES:(c + 1) * LANES, :] = zc(Z_DV, c).T.astype(BF16)

    cq = z_ref[:, Z_CQ:Z_CQ + 512]
    ms = jnp.sum(cq * cq, axis=-1, keepdims=True) * (1.0 / MLA_Q_LORA)
    p_ref[:, P_CQ:P_CQ + 512] = (cq * lax.rsqrt(ms + EPS) * gq_ref[...]).astype(BF16)
    ckv = z_ref[:, Z_CKV:Z_CKV + MLA_KV_LORA]
    ms = jnp.mean(ckv * ckv, axis=-1, keepdims=True)
    p_ref[:, P_CKV:P_CKV + MLA_KV_LORA] = (ckv * lax.rsqrt(ms + EPS) * gkv_ref[...]).astype(BF16)

    ik = zc(Z_IK)
    ms = jnp.mean(ik * ik, axis=-1, keepdims=True)
    ikn = ik * lax.rsqrt(ms + EPS) * gik_ref[...]
    put(P_IK, 0, _rope128(ikn, tab(0), ROPE_KINDS[0][0] // 2))

    sm = zc(Z_SMALL)
    lane = lax.broadcasted_iota(I32, sm.shape, 1)
    iw_scale = IDX_HEADS ** -0.5 * IDX_DIM ** -0.5
    small = jnp.where(lane < SMALL_IW, jax.nn.sigmoid(sm), sm * iw_scale)
    small_ref[...] = small
    smallt_ref[0] = small.T


def _prep(z, tab, gq, gkv, gik, *, batch, seq, ts):
    m = z.shape[0]
    spb = seq // ts
    assert ts == VT_TILE
    row = lambda w: pl.BlockSpec((ts, w), lambda i: (i, 0))
    return pl.pallas_call(
        _prep_kernel,
        out_shape=(jax.ShapeDtypeStruct((m, P_WIDTH), BF16),
                   jax.ShapeDtypeStruct((batch, T_ROWS, seq), BF16),
                   jax.ShapeDtypeStruct((batch, spb, BR_WIDTH, VT_TILE), BF16),
                   jax.ShapeDtypeStruct((m, LANES), BF16),
                   jax.ShapeDtypeStruct((m, LANES), BF16),
                   jax.ShapeDtypeStruct((m, LANES), F32),
                   jax.ShapeDtypeStruct((batch, LANES, seq), F32)),
        grid=(m // ts,),
        in_specs=[row(Z_WIDTH),
                  pl.BlockSpec((ts, 3 * TAB_W), lambda i: (i % spb, 0)),
                  pl.BlockSpec((1, 512), lambda i: (0, 0)),
                  pl.BlockSpec((1, MLA_KV_LORA), lambda i: (0, 0)),
                  pl.BlockSpec((1, LANES), lambda i: (0, 0))],
        out_specs=(row(P_WIDTH),
                   pl.BlockSpec((1, T_ROWS, ts), lambda i: (i // spb, 0, i % spb)),
                   pl.BlockSpec((1, 1, BR_WIDTH, VT_TILE), lambda i: (i // spb, i % spb, 0, 0)),
                   row(LANES), row(LANES), row(LANES),
                   pl.BlockSpec((1, LANES, ts), lambda i: (i // spb, 0, i % spb))),
        compiler_params=_cparams(1),
        name="prep",
    )(z, tab, gq, gkv, gik)


def _softmax_init(mx_scr, l_scr, acc_scr):
    mx_scr[...] = jnp.full(mx_scr.shape, NEG, F32)
    l_scr[...] = jnp.zeros(l_scr.shape, F32)
    acc_scr[...] = jnp.zeros(acc_scr.shape, F32)


def _score_store(g, j, s, s_scr, mx_scr):
    s_scr[g, j] = s
    m = s[:, 0:LANES]
    for c in range(1, s.shape[1] // LANES):
        m = jnp.maximum(m, s[:, c * LANES:(c + 1) * LANES])
    mx_scr[g] = jnp.maximum(mx_scr[g], m)


def _row_max_finish(mx_scr):
    for g in range(mx_scr.shape[0]):
        m = jnp.max(mx_scr[g], axis=-1, keepdims=True)
        mx_scr[g] = jnp.broadcast_to(m, mx_scr.shape[1:])


def _prob_accumulate(g, j, v_tile, s_scr, mx_scr, l_scr, acc_scr):
    mb = mx_scr[g]
    s = s_scr[g, j]
    ps = [jnp.exp(s[:, c * LANES:(c + 1) * LANES] - mb) for c in range(s.shape[1] // LANES)]
    tot = ps[0]
    for p in ps[1:]:
        tot = tot + p
    l_scr[g] += tot
    acc_scr[g] += _dot(jnp.concatenate(ps, axis=1).astype(BF16), v_tile)


def _softmax_out(g, l_scr, acc_scr):
    return acc_scr[g] / jnp.sum(l_scr[g], axis=-1, keepdims=True)


def _causal_tiles(step, n_full):
    def body(j, carry):
        step(j, False)
        return carry
    lax.fori_loop(0, n_full, body, 0)
    step(n_full, True)


def _softmax_scratch(groups, n_tiles, rows, tk):
    return [pltpu.VMEM((groups, n_tiles, rows, tk), F32),
            pltpu.VMEM((groups, rows, LANES), F32),
            pltpu.VMEM((groups, rows, LANES), F32),
            pltpu.VMEM((groups, rows, HEAD_W), F32)]


def _diff_attn_kernel(q_ref, k_ref, v_ref, lv_ref, g_ref, o_ref, s_scr, mx_scr, l_scr, acc_scr,
                      *, tq, tk, lam_init):
    qs = pl.program_id(1) * tq
    n_full = qs // tk
    scale = DA_DIM ** -0.5
    lv = lv_ref[...]
    lam = (jnp.exp(jnp.sum(lv[0:1] * lv[1:2], axis=-1, keepdims=True))
           - jnp.exp(jnp.sum(lv[2:3] * lv[3:4], axis=-1, keepdims=True)) + lam_init)
    lane = lax.broadcasted_iota(I32, (tq, HEAD_W), 1)
    row_t = qs + lax.broadcasted_iota(I32, (2 * tq, 1), 0) % tq
    col0 = lax.broadcasted_iota(I32, (2 * tq, tk), 1)
    _softmax_init(mx_scr, l_scr, acc_scr)

    def scores(j, masked):
        ks = pl.multiple_of(j * tk, tk)
        for h in range(HEADS):
            hs = slice(h * HEAD_W, (h + 1) * HEAD_W)
            qh = q_ref[0, :, hs]
            zero = jnp.zeros_like(qh)
            q2 = jnp.concatenate([jnp.where(lane < DA_DIM, qh, zero),
                                  jnp.where(lane >= DA_DIM, qh, zero)], axis=0)
            s = _dot_nt(q2, k_ref[0, pl.ds(ks, tk), hs]) * scale
            if masked:
                s = jnp.where(col0 + ks <= row_t, s, NEG)
            _score_store(h, j, s, s_scr, mx_scr)

    _causal_tiles(scores, n_full)
    _row_max_finish(mx_scr)

    def probs(j, carry):
        ks = pl.multiple_of(j * tk, tk)
        for h in range(HEADS):
            v_tile = v_ref[0, pl.ds(ks, tk), h * HEAD_W:(h + 1) * HEAD_W]
            _prob_accumulate(h, j, v_tile, s_scr, mx_scr, l_scr, acc_scr)
        return carry

    lax.fori_loop(0, n_full + 1, probs, 0)
    for h in range(HEADS):
        o2 = _softmax_out(h, l_scr, acc_scr)
        o = o2[:tq] - lam * o2[tq:]
        ms = jnp.mean(o * o, axis=-1, keepdims=True)
        o = o * lax.rsqrt(ms + EPS) * g_ref[...]
        o_ref[0, :, h * HEAD_W:(h + 1) * HEAD_W] = (o * (1.0 - lam_init)).astype(BF16)


def _diff_attn(p3, lv, g, *, lam_init, tq, tk):
    b, s, _ = p3.shape
    return pl.pallas_call(
        functools.partial(_diff_attn_kernel, tq=tq, tk=tk, lam_init=lam_init),
        out_shape=jax.ShapeDtypeStruct((b, s, BR_WIDTH), BF16),
        grid=(b, s // tq),
        in_specs=[pl.BlockSpec((1, tq, 512), lambda bi, i: (bi, i, P_AQ // 512)),
                  pl.BlockSpec((1, s, 512), lambda bi, i: (bi, 0, P_AK // 512)),
                  pl.BlockSpec((1, s, 512), lambda bi, i: (bi, 0, P_AV // 512)),
                  pl.BlockSpec((4, DA_DIM), lambda bi, i: (0, 0)),
                  pl.BlockSpec((1, HEAD_W), lambda bi, i: (0, 0))],
        out_specs=pl.BlockSpec((1, tq, BR_WIDTH), lambda bi, i: (bi, i, 0)),
        scratch_shapes=_softmax_scratch(HEADS, s // tk, 2 * tq, tk),
        compiler_params=_cparams(2),
        name="diff_attn",
    )(p3, p3, p3, lv, g)


def _nsa_compress_kernel(gk_ref, gv_ref, w1k_ref, w1v_ref, pek_ref, pev_ref,
                         w1kf_ref, w1vf_ref, w2k_ref, w2v_ref, kc_ref, vc_ref):
    def one(g_ref, w1cat_ref, pe_ref, w1f_ref, w2_ref, o_ref):
        y = _dot(g_ref[0], w1cat_ref[...])
        n = y.shape[0]
        nxt = pltpu.roll(y[:, HEAD_W:], n - 1, 0)
        c = _dot(pe_ref[...], w1f_ref[...])[0:1]
        hid = jax.nn.gelu(y[:, :HEAD_W] + nxt + c)
        o_ref[0] = _dot(hid.astype(BF16), w2_ref[...]).astype(BF16)

    one(gk_ref, w1k_ref, pek_ref, w1kf_ref, w2k_ref, kc_ref)
    one(gv_ref, w1v_ref, pev_ref, w1vf_ref, w2v_ref, vc_ref)


def _nsa_compress(gk, gv, w1k_cat, w1v_cat, pek, pev, w1k, w1v, w2k, w2v):
    b, ng, gw = gk.shape
    full = lambda shape: pl.BlockSpec(shape, lambda bi: (0,) * len(shape))
    return pl.pallas_call(
        _nsa_compress_kernel,
        out_shape=(jax.ShapeDtypeStruct((b, ng, HEAD_W), BF16),
                   jax.ShapeDtypeStruct((b, ng, HEAD_W), BF16)),
        grid=(b,),
        in_specs=[pl.BlockSpec((1, ng, gw), lambda bi: (bi, 0, 0)),
                  pl.BlockSpec((1, ng, gw), lambda bi: (bi, 0, 0)),
                  full(w1k_cat.shape), full(w1v_cat.shape), full(pek.shape), full(pev.shape),
                  full(w1k.shape), full(w1v.shape), full(w2k.shape), full(w2v.shape)],
        out_specs=(pl.BlockSpec((1, ng, HEAD_W), lambda bi: (bi, 0, 0)),
                   pl.BlockSpec((1, ng, HEAD_W), lambda bi: (bi, 0, 0))),
        compiler_params=_cparams(1),
        name="nsa_compress",
    )(gk, gv, w1k_cat, w1v_cat, pek, pev, w1k, w1v, w2k, w2v)


NSA_GROUPS = 2


def _nsa_kernel(q_ref, kc_ref, vc_ref, ks_ref, vs_ref, kw_ref, vw_ref, small_ref, ov_ref, e_ref,
                o_ref, s_scr, mx_scr, l_scr, acc_scr, *, tq, tk, seq):
    qs = pl.program_id(1) * tq
    scale = NSA_DK ** -0.5
    ns = seq // SEL_LEN
    n_sel = min(SEL_N, ns)
    r = HEADS * tq
    rg = r // NSA_GROUPS
    q4 = jnp.concatenate([q_ref[0, :, h * HEAD_W:(h + 1) * HEAD_W] for h in range(HEADS)], axis=0)
    t1 = qs + lax.broadcasted_iota(I32, (tq, 1), 0)
    t4 = qs + lax.broadcasted_iota(I32, (r, 1), 0) % tq

    kc = kc_ref[0]
    nc_pad = kc.shape[0]
    sc = _dot_nt(q4, kc) * scale
    c_end = lax.broadcasted_iota(I32, (r, nc_pad), 1) * CMP_STRIDE + (CMP_LEN - 1)
    cmask = c_end <= t4
    mx = jnp.max(jnp.where(cmask, sc, NEG), axis=-1, keepdims=True)
    e = jnp.where(cmask, jnp.exp(sc - mx), 0.0)
    den = jnp.sum(e, axis=-1, keepdims=True)
    pc = e / jnp.where(den > 0.0, den, 1.0)
    o_cmp = _dot(pc.astype(BF16), vc_ref[0])

    psum = pc[0:tq] + pc[tq:2 * tq] + pc[2 * tq:3 * tq] + pc[3 * tq:4 * tq]
    ov = ov_ref[...]
    hi = psum.astype(BF16)
    r1 = psum - hi.astype(F32)
    mid = r1.astype(BF16)
    lo = (r1 - mid.astype(F32)).astype(BF16)
    imp = _dot(hi, ov) + _dot(mid, ov) + _dot(lo, ov)

    blk = lax.broadcasted_iota(I32, (tq, LANES), 1)
    cur = t1 // SEL_LEN
    forced = (blk == 0) | (blk == cur) | (blk == cur - 1)
    visible = blk * SEL_LEN <= t1
    score = jnp.where(visible, jnp.where(forced, FORCE_SCORE, imp), NEG)
    score = jnp.where(blk < ns, score, PAD_SCORE)
    ns_pad = -(-ns // SUBLANES) * SUBLANES
    score_t = score.T[:ns_pad]
    blk_t = lax.broadcasted_iota(I32, (ns_pad, tq), 0)
    rank = jnp.zeros((ns_pad, tq), I32)
    for jp in range(ns):
        row = score_t[jp:jp + 1, :]
        later = (blk_t > jp).astype(I32)
        rank = rank + jnp.where(row > score_t, 1, jnp.where(row == score_t, later, 0))
    sel_t = jnp.where(rank < n_sel, 1.0, 0.0)
    if ns_pad < LANES:
        sel_t = jnp.concatenate([sel_t, jnp.zeros((LANES - ns_pad, tq), F32)], axis=0)
    selb = sel_t.T.astype(BF16)

    _softmax_init(mx_scr, l_scr, acc_scr)
    col0 = lax.broadcasted_iota(I32, (rg, tk), 1)
    tg = qs + lax.broadcasted_iota(I32, (rg, 1), 0) % tq
    n_tiles = qs // tk + 1

    def scores(j, carry):
        ks0 = pl.multiple_of(j * tk, tk)
        mt = _dot(selb, e_ref[j])
        mg = jnp.concatenate([mt] * (rg // tq), axis=0)
        k_tile = ks_ref[0, pl.ds(ks0, tk), :]
        for g in range(NSA_GROUPS):
            s = _dot_nt(q4[g * rg:(g + 1) * rg], k_tile) * scale
            s = jnp.where(mg > 0.5, s, NEG)
            s = jnp.where(col0 + ks0 <= tg, s, NEG)
            _score_store(g, j, s, s_scr, mx_scr)
        return carry

    lax.fori_loop(0, n_tiles, scores, 0)
    _row_max_finish(mx_scr)

    def probs(j, carry):
        ks0 = pl.multiple_of(j * tk, tk)
        v_tile = vs_ref[0, pl.ds(ks0, tk), :]
        for g in range(NSA_GROUPS):
            _prob_accumulate(g, j, v_tile, s_scr, mx_scr, l_scr, acc_scr)
        return carry

    lax.fori_loop(0, n_tiles, probs, 0)
    o_slc = jnp.concatenate([_softmax_out(g, l_scr, acc_scr) for g in range(NSA_GROUPS)], axis=0)

    wspan = WIN + tq
    start = pl.multiple_of(jnp.maximum(qs - WIN, 0), tq)
    sw = _dot_nt(q4, kw_ref[0, pl.ds(start, wspan), :]) * scale
    dist = t4 - (start + lax.broadcasted_iota(I32, (r, wspan), 1))
    wmask = (dist >= 0) & (dist < WIN)
    mx = jnp.max(jnp.where(wmask, sw, NEG), axis=-1, keepdims=True)
    e = jnp.where(wmask, jnp.exp(sw - mx), 0.0)
    pw = e / jnp.sum(e, axis=-1, keepdims=True)
    o_win = _dot(pw.astype(BF16), vw_ref[0, pl.ds(start, wspan), :])

    gates = small_ref[0]
    for h in range(HEADS):
        rows = slice(h * tq, (h + 1) * tq)
        g0 = gates[:, SMALL_G + 3 * h:SMALL_G + 3 * h + 1]
        g1 = gates[:, SMALL_G + 3 * h + 1:SMALL_G + 3 * h + 2]
        g2 = gates[:, SMALL_G + 3 * h + 2:SMALL_G + 3 * h + 3]
        o = g0 * o_cmp[rows] + g1 * o_slc[rows] + g2 * o_win[rows]
        o_ref[0, :, h * HEAD_W:(h + 1) * HEAD_W] = o.astype(BF16)


def _nsa(p3, kc, vc, small3, ov, emat, *, tq, tk):
    b, s, _ = p3.shape
    ng = kc.shape[1]
    col = lambda off: (lambda bi, i: (bi, 0, off // LANES))
    return pl.pallas_call(
        functools.partial(_nsa_kernel, tq=tq, tk=tk, seq=s),
        out_shape=jax.ShapeDtypeStruct((b, s, BR_WIDTH), BF16),
        grid=(b, s // tq),
        in_specs=[pl.BlockSpec((1, tq, 512), lambda bi, i: (bi, i, P_BQ // 512)),
                  pl.BlockSpec((1, ng, HEAD_W), lambda bi, i: (bi, 0, 0)),
                  pl.BlockSpec((1, ng, HEAD_W), lambda bi, i: (bi, 0, 0)),
                  pl.BlockSpec((1, s, LANES), col(P_KS)),
                  pl.BlockSpec((1, s, LANES), col(P_VS)),
                  pl.BlockSpec((1, s, LANES), col(P_KW)),
                  pl.BlockSpec((1, s, LANES), col(P_VW)),
                  pl.BlockSpec((1, tq, LANES), lambda bi, i: (bi, i, 0)),
                  pl.BlockSpec(ov.shape, lambda bi, i: (0, 0)),
                  pl.BlockSpec(emat.shape, lambda bi, i: (0, 0, 0))],
        out_specs=pl.BlockSpec((1, tq, BR_WIDTH), lambda bi, i: (bi, i, 0)),
        scratch_shapes=_softmax_scratch(NSA_GROUPS, s // tk, HEADS * tq // NSA_GROUPS, tk),
        compiler_params=_cparams(2),
        name="nsa_attn",
    )(p3, kc, vc, p3, p3, p3, p3, small3, ov, emat)


def _mla_up_kernel(p_ref, ckv_ref, tab_ref, wq_ref, wkv_ref, q_ref, kv_ref):
    q = _dot(p_ref[...], wq_ref[...])
    nn = HEADS * MLA_NOPE
    q_ref[:, :nn] = q[:, :nn].astype(BF16)
    for c in range(nn // LANES, (nn + HEADS * MLA_ROPE) // LANES):
        tile = _rope128(q[:, c * LANES:(c + 1) * LANES], tab_ref[...], MLA_ROPE // 2)
        q_ref[:, c * LANES:(c + 1) * LANES] = tile.astype(BF16)
    kv_ref[...] = _dot(ckv_ref[...], wkv_ref[...]).astype(BF16)


def _mla_up(p2, tab, wq, wkv, *, seq, ts):
    m = p2.shape[0]
    spb = seq // ts
    nq = wq.shape[1]
    nkv = wkv.shape[1]
    return pl.pallas_call(
        _mla_up_kernel,
        out_shape=(jax.ShapeDtypeStruct((m, nq), BF16), jax.ShapeDtypeStruct((m, nkv), BF16)),
        grid=(m // ts,),
        in_specs=[pl.BlockSpec((ts, 512), lambda i: (i, P_CQ // 512)),
                  pl.BlockSpec((ts, MLA_KV_LORA), lambda i: (i, P_CKV // MLA_KV_LORA)),
                  pl.BlockSpec((ts, TAB_W), lambda i: (i % spb, 2)),
                  pl.BlockSpec(wq.shape, lambda i: (0, 0)),
                  pl.BlockSpec(wkv.shape, lambda i: (0, 0))],
        out_specs=(pl.BlockSpec((ts, nq), lambda i: (i, 0)),
                   pl.BlockSpec((ts, nkv), lambda i: (i, 0))),
        compiler_params=_cparams(1),
        name="mla_up",
    )(p2, p2, tab, wq, wkv)


def _mla_attn_kernel(qn_ref, qr_ref, kn_ref, kr_ref, v_ref, o_ref, s_scr, mx_scr, l_scr, acc_scr,
                     *, tq, tk):
    qs = pl.program_id(1) * tq
    n_full = qs // tk
    scale = (MLA_NOPE + MLA_ROPE) ** -0.5
    lane = lax.broadcasted_iota(I32, (tq, LANES), 1)
    row_t = qs + lax.broadcasted_iota(I32, (tq, 1), 0)
    col0 = lax.broadcasted_iota(I32, (tq, tk), 1)
    _softmax_init(mx_scr, l_scr, acc_scr)

    def scores(j, masked):
        ks = pl.multiple_of(j * tk, tk)
        kr_tile = kr_ref[0, pl.ds(ks, tk), :]
        for h in range(HEADS):
            hs = slice(h * HEAD_W, (h + 1) * HEAD_W)
            pair = qr_ref[0, :, (h // 2) * LANES:(h // 2 + 1) * LANES]
            keep = (lane < MLA_ROPE) if h % 2 == 0 else (lane >= MLA_ROPE)
            qr = jnp.where(keep, pair, jnp.zeros_like(pair))
            s = (_dot_nt(qn_ref[0, :, hs], kn_ref[0, pl.ds(ks, tk), hs])
                 + _dot_nt(qr, kr_tile)) * scale
            if masked:
                s = jnp.where(col0 + ks <= row_t, s, NEG)
            _score_store(h, j, s, s_scr, mx_scr)

    _causal_tiles(scores, n_full)
    _row_max_finish(mx_scr)

    def probs(j, carry):
        ks = pl.multiple_of(j * tk, tk)
        for h in range(HEADS):
            v_tile = v_ref[0, pl.ds(ks, tk), h * HEAD_W:(h + 1) * HEAD_W]
            _prob_accumulate(h, j, v_tile, s_scr, mx_scr, l_scr, acc_scr)
        return carry

    lax.fori_loop(0, n_full + 1, probs, 0)
    for h in range(HEADS):
        o_ref[0, :, h * HEAD_W:(h + 1) * HEAD_W] = _softmax_out(h, l_scr, acc_scr).astype(BF16)


def _mla_attn(q3, kv3, p3, *, tq, tk):
    b, s, _ = q3.shape
    return pl.pallas_call(
        functools.partial(_mla_attn_kernel, tq=tq, tk=tk),
        out_shape=jax.ShapeDtypeStruct((b, s, BR_WIDTH), BF16),
        grid=(b, s // tq),
        in_specs=[pl.BlockSpec((1, tq, 512), lambda bi, i: (bi, i, 0)),
                  pl.BlockSpec((1, tq, 256), lambda bi, i: (bi, i, 2)),
                  pl.BlockSpec((1, s, 512), lambda bi, i: (bi, 0, 0)),
                  pl.BlockSpec((1, s, LANES), lambda bi, i: (bi, 0, P_KR // LANES)),
                  pl.BlockSpec((1, s, 512), lambda bi, i: (bi, 0, 1))],
        out_specs=pl.BlockSpec((1, tq, BR_WIDTH), lambda bi, i: (bi, i, 0)),
        scratch_shapes=_softmax_scratch(HEADS, s // tk, tq, tk),
        compiler_params=_cparams(2),
        name="mla_attn",
    )(q3, q3, kv3, p3, kv3)


def _sortable_key(x):
    bits = pltpu.bitcast(x + 0.0, I32)
    return bits ^ (lax.shift_right_arithmetic(bits, 31) & 0x7FFFFFFF)


def _fold_rows(x, op):
    n = x.shape[0] // SUBLANES
    return op(x.reshape(n, SUBLANES, x.shape[1]), axis=0)


def _dsa_kernel(qt_ref, iqt_ref, iwt_ref, k_ref, ik_ref, vt_ref, o_ref,
                key_scr, s_scr, mx_scr, l_scr, acc_scr, *, tq, tk, top):
    qs = pl.program_id(1) * tq
    n_tiles = (qs + tq - 1) // tk + 1
    scale = DSA_DIM ** -0.5
    t_lane = qs + lax.broadcasted_iota(I32, (tk, tq), 1)
    krow0 = lax.broadcasted_iota(I32, (tk, tq), 0)
    half_rows = lax.broadcasted_iota(I32, (LANES, tq), 0) < IDX_DIM
    vt_per_tile = tk // VT_TILE

    def score_tile(j, carry):
        ks = pl.multiple_of(j * tk, tk)
        ikt = ik_ref[0, pl.ds(ks, tk), :]
        acc = jnp.zeros((tk, tq), F32)
        for h in range(IDX_HEADS):
            pair = iqt_ref[0, (h // 2) * LANES:(h // 2 + 1) * LANES, :]
            keep = half_rows if h % 2 == 0 else jnp.logical_not(half_rows)
            iq_h = jnp.where(keep, pair, jnp.zeros_like(pair))
            w_h = iwt_ref[0, SMALL_IW + h:SMALL_IW + h + 1, :]
            acc = acc + w_h * jnp.maximum(_dot(ikt, iq_h), 0.0)
        key_scr[j] = jnp.where(krow0 + ks <= t_lane, _sortable_key(acc), INT_MIN)
        return carry

    lax.fori_loop(0, n_tiles, score_tile, 0)

    def bit_body(i, theta):
        cand = theta + lax.shift_left(jnp.int32(1), 31 - i)

        def count_tile(j, cnt):
            ge = (key_scr[j] >= cand).astype(I32)
            return cnt + _fold_rows(ge, jnp.sum)

        cnt = lax.fori_loop(0, n_tiles, count_tile, jnp.zeros((SUBLANES, tq), I32))
        total = jnp.sum(cnt, axis=0, keepdims=True)
        return jnp.where(total >= top, cand, theta)

    theta = lax.fori_loop(0, 32, bit_body, jnp.full((1, tq), INT_MIN, I32))
    theta = jnp.maximum(theta, INT_MIN + 1)

    mx_scr[...] = jnp.full(mx_scr.shape, NEG, F32)
    l_scr[...] = jnp.zeros(l_scr.shape, F32)
    acc_scr[...] = jnp.zeros(acc_scr.shape, F32)

    def scores(j, carry):
        ks = pl.multiple_of(j * tk, tk)
        sel = key_scr[j] >= theta
        for h in range(HEADS):
            hs = slice(h * HEAD_W, (h + 1) * HEAD_W)
            s = _dot(k_ref[0, pl.ds(ks, tk), hs], qt_ref[0, hs, :]) * scale
            s = jnp.where(sel, s, NEG)
            s_scr[h, j] = s
            mx_scr[h] = jnp.maximum(mx_scr[h], _fold_rows(s, jnp.max))
        return carry

    lax.fori_loop(0, n_tiles, scores, 0)
    for h in range(HEADS):
        m = jnp.max(mx_scr[h], axis=0, keepdims=True)
        mx_scr[h] = jnp.broadcast_to(m, (SUBLANES, tq))

    def probs(j, carry):
        for h in range(HEADS):
            p = jnp.exp(s_scr[h, j] - mx_scr[h][0:1])
            l_scr[h] += _fold_rows(p, jnp.sum)
            pb = p.astype(BF16)
            for c in range(vt_per_tile):
                vt = vt_ref[0, j * vt_per_tile + c, h * HEAD_W:(h + 1) * HEAD_W, :]
                acc_scr[h] += _dot(vt, pb[c * VT_TILE:(c + 1) * VT_TILE])
        return carry

    lax.fori_loop(0, n_tiles, probs, 0)
    for h in range(HEADS):
        ot = acc_scr[h] / jnp.sum(l_scr[h], axis=0, keepdims=True)
        o_ref[0, :, h * HEAD_W:(h + 1) * HEAD_W] = ot.T.astype(BF16)


def _dsa(p3, t3, vt4, smallt, *, tq, tk, top):
    b, s, _ = p3.shape
    n_vt = vt4.shape[1]
    return pl.pallas_call(
        functools.partial(_dsa_kernel, tq=tq, tk=tk, top=top),
        out_shape=jax.ShapeDtypeStruct((b, s, BR_WIDTH), BF16),
        grid=(b, s // tq),
        in_specs=[pl.BlockSpec((1, 512, tq), lambda bi, i: (bi, T_DQ // 512, i)),
                  pl.BlockSpec((1, 512, tq), lambda bi, i: (bi, T_IQ // 512, i)),
                  pl.BlockSpec((1, LANES, tq), lambda bi, i: (bi, 0, i)),
                  pl.BlockSpec((1, s, 512), lambda bi, i: (bi, 0, P_DK // 512)),
                  pl.BlockSpec((1, s, LANES), lambda bi, i: (bi, 0, P_IK // LANES)),
                  pl.BlockSpec((1, n_vt, BR_WIDTH, VT_TILE), lambda bi, i: (bi, 0, 0, 0))],
        out_specs=pl.BlockSpec((1, tq, BR_WIDTH), lambda bi, i: (bi, i, 0)),
        scratch_shapes=[pltpu.VMEM((s // tk, tk, tq), I32),
                        pltpu.VMEM((HEADS, s // tk, tk, tq), F32),
                        pltpu.VMEM((HEADS, SUBLANES, tq), F32),
                        pltpu.VMEM((HEADS, SUBLANES, tq), F32),
                        pltpu.VMEM((HEADS, HEAD_W, tq), F32)],
        compiler_params=_cparams(2),
        name="dsa_attn",
    )(t3, t3, smallt, p3, p3, vt4)


def _merge_kernel(x_ref, oa_ref, ob_ref, oc_ref, od_ref, g_ref, wb_ref, wo_ref, o_ref):
    d = x_ref.shape[1]
    acc = jnp.zeros(x_ref.shape, F32)
    for n, br_ref in enumerate((oa_ref, ob_ref, oc_ref, od_ref)):
        br = _dot(br_ref[...], wb_ref[n])
        acc = acc + g_ref[:, n * d:(n + 1) * d].astype(F32) * br
    o_ref[...] = x_ref[...] + _dot(acc.astype(BF16), wo_ref[...])


def _merge(x2, oa, ob, oc, od, gates, wb, wo, *, tm):
    m, d = x2.shape
    row = lambda w: pl.BlockSpec((tm, w), lambda i: (i, 0))
    return pl.pallas_call(
        _merge_kernel,
        out_shape=jax.ShapeDtypeStruct((m, d), F32),
        grid=(m // tm,),
        in_specs=[row(d), row(BR_WIDTH), row(BR_WIDTH), row(BR_WIDTH), row(BR_WIDTH),
                  row(gates.shape[1]),
                  pl.BlockSpec(wb.shape, lambda i: (0, 0, 0)),
                  pl.BlockSpec(wo.shape, lambda i: (0, 0))],
        out_specs=row(d),
        compiler_params=_cparams(1),
        name="merge",
    )(x2, oa, ob, oc, od, gates, wb, wo)


def _ffn_kernel(x_ref, g_ref, wg_ref, wu_ref, wd_ref, gf_ref, o_ref, h_scr, acc_scr, *, final):
    j = pl.program_id(1)

    @pl.when(j == 0)
    def _():
        x = x_ref[...]
        ms = jnp.mean(x * x, axis=-1, keepdims=True)
        h_scr[...] = (x * lax.rsqrt(ms + EPS) * g_ref[...]).astype(BF16)
        acc_scr[...] = jnp.zeros(acc_scr.shape, F32)

    h = h_scr[...]
    a = jax.nn.silu(_dot(h, wg_ref[...])) * _dot(h, wu_ref[...])
    acc_scr[...] += _dot(a.astype(BF16), wd_ref[...])

    @pl.when(j == pl.num_programs(1) - 1)
    def _():
        y = x_ref[...] + acc_scr[...]
        if final:
            ms = jnp.mean(y * y, axis=-1, keepdims=True)
            y = y * lax.rsqrt(ms + EPS) * gf_ref[...]
        o_ref[...] = y


def _ffn(x2, g, wg, wu, wd, gf, *, final, tm, tf):
    m, d = x2.shape
    dff = wg.shape[1]
    return pl.pallas_call(
        functools.partial(_ffn_kernel, final=final),
        out_shape=jax.ShapeDtypeStruct((m, d), F32),
        grid=(m // tm, dff // tf),
        in_specs=[pl.BlockSpec((tm, d), lambda i, j: (i, 0)),
                  pl.BlockSpec((1, d), lambda i, j: (0, 0)),
                  pl.BlockSpec((d, tf), lambda i, j: (0, j)),
                  pl.BlockSpec((d, tf), lambda i, j: (0, j)),
                  pl.BlockSpec((tf, d), lambda i, j: (j, 0)),
                  pl.BlockSpec((1, d), lambda i, j: (0, 0))],
        out_specs=pl.BlockSpec((tm, d), lambda i, j: (i, 0)),
        scratch_shapes=[pltpu.VMEM((tm, d), BF16), pltpu.VMEM((tm, d), F32)],
        compiler_params=_cparams(2),
        name="ffn",
    )(x2, g, wg, wu, wd, gf)


def _tiles(seq, m, dff):
    pick = lambda n, cands: next(c for c in cands if n % c == 0)
    tk = pick(seq, (512, 256))
    return dict(
        proj_tm=pick(m, (512, 256, 128)), proj_tn=1024,
        prep_ts=VT_TILE,
        diff_tq=128, mla_tq=pick(seq, (256, 128)), nsa_tq=128, dsa_tq=256, tk=tk,
        row_tm=pick(m, (512, 256, 128)),
        ffn_tf=pick(dff, (1408, 704, 256, 128)),
    )


def kernel(x, norm1_g, w_in, diff_lq1, diff_lk1, diff_lq2, diff_lk2, diff_subln_g, nsa_pe_k, nsa_w1_k, nsa_w2_k, nsa_pe_v, nsa_w1_v, nsa_w2_v, mla_q_norm_g, mla_w_uq, mla_kv_norm_g, mla_w_ukv, idx_k_norm_g, w_branch, w_out, norm2_g, w_gate_up, w_down, final_norm_g):
    b, seq, d = x.shape
    depth = w_in.shape[0]
    m = b * seq
    dff = w_down.shape[1]
    t = _tiles(seq, m, dff)
    tk = t["tk"]
    assert seq % SEL_LEN == 0 and seq >= WIN + t["nsa_tq"] and seq // SEL_LEN <= LANES
    assert seq % t["dsa_tq"] == 0 and tk % VT_TILE == 0 and tk >= min(IDX_TOPK, seq // 4)

    col_idx, gate_off, d_in = _in_proj_columns()
    assert w_in.shape[2] == d_in
    tab = jnp.concatenate([_rope_table(seq, rot, per) for rot, per in ROPE_KINDS], axis=1)

    ng = seq // CMP_STRIDE
    ns = seq // SEL_LEN
    c_start = np.arange(ng)[:, None] * CMP_STRIDE
    s_start = np.arange(LANES)[None, :] * SEL_LEN
    ov = ((c_start < s_start + SEL_LEN) & (c_start + CMP_LEN - 1 >= s_start)
          & (np.arange(LANES)[None, :] < ns))
    ov = jnp.asarray(ov, BF16)
    emat = np.arange(LANES)[:, None] == (np.arange(seq)[None, :] // SEL_LEN)
    emat = jnp.asarray(emat.reshape(LANES, seq // tk, tk).transpose(1, 0, 2), BF16)

    qd = MLA_NOPE + MLA_ROPE
    uq_idx = np.concatenate([np.concatenate([np.arange(h * qd, h * qd + MLA_NOPE) for h in range(HEADS)]),
                             np.concatenate([np.arange(h * qd + MLA_NOPE, (h + 1) * qd) for h in range(HEADS)])])
    kvd = MLA_NOPE + HEAD_W
    ukv_idx = np.concatenate([np.concatenate([np.arange(h * kvd, h * kvd + MLA_NOPE) for h in range(HEADS)]),
                              np.concatenate([np.arange(h * kvd + MLA_NOPE, (h + 1) * kvd) for h in range(HEADS)])])

    x2 = x.reshape(m, d)
    half_w1 = CMP_STRIDE * NSA_DK
    for l in range(depth):
        lam_init = 0.8 - 0.6 * math.exp(-0.3 * l)
        w_mix = _take_cols(w_in[l], col_idx).astype(BF16)
        w_gate = w_in[l][:, gate_off:].astype(BF16)
        z = _norm_matmul(x2, norm1_g[l][None], w_mix, out_dtype=F32, sigmoid=False,
                         tm=t["proj_tm"], tn=t["proj_tn"], name="in_proj")
        gates = _norm_matmul(x2, norm1_g[l][None], w_gate, out_dtype=BF16, sigmoid=True,
                             tm=t["proj_tm"], tn=t["proj_tn"], name="gate_proj")

        gq = jnp.pad(mla_q_norm_g[l], (0, 512 - MLA_Q_LORA))[None]
        gkv = mla_kv_norm_g[l][None]
        gik = jnp.concatenate([idx_k_norm_g[l], idx_k_norm_g[l]])[None]
        p2, t3, vt4, kc_tok, vc_tok, small, smallt = _prep(z, tab, gq, gkv, gik,
                                                           batch=b, seq=seq, ts=t["prep_ts"])
        p3 = p2.reshape(b, seq, P_WIDTH)
        small3 = small.reshape(b, seq, LANES)

        lv = jnp.stack([diff_lq1[l], diff_lk1[l], diff_lq2[l], diff_lk2[l]])
        o_a = _diff_attn(p3, lv, diff_subln_g[l][None], lam_init=lam_init, tq=t["diff_tq"], tk=tk)

        w1k, w1v = nsa_w1_k[l].astype(BF16), nsa_w1_v[l].astype(BF16)
        w1k_cat = jnp.concatenate([w1k[:half_w1], w1k[half_w1:]], axis=1)
        w1v_cat = jnp.concatenate([w1v[:half_w1], w1v[half_w1:]], axis=1)
        pek = jnp.broadcast_to(nsa_pe_k[l].reshape(1, -1), (8, CMP_LEN * NSA_DK)).astype(BF16)
        pev = jnp.broadcast_to(nsa_pe_v[l].reshape(1, -1), (8, CMP_LEN * NSA_DK)).astype(BF16)
        kc, vc = _nsa_compress(kc_tok.reshape(b, ng, half_w1), vc_tok.reshape(b, ng, half_w1),
                               w1k_cat, w1v_cat, pek, pev, w1k, w1v,
                               nsa_w2_k[l].astype(BF16), nsa_w2_v[l].astype(BF16))
        o_b = _nsa(p3, kc, vc, small3, ov, emat, tq=t["nsa_tq"], tk=tk)

        wq = jnp.pad(jnp.take(mla_w_uq[l], jnp.asarray(uq_idx, I32), axis=1),
                     ((0, 512 - MLA_Q_LORA), (0, 0))).astype(BF16)
        wkv = jnp.take(mla_w_ukv[l], jnp.asarray(ukv_idx, I32), axis=1).astype(BF16)
        q_c, kv_c = _mla_up(p2, tab, wq, wkv, seq=seq, ts=t["prep_ts"])
        o_c = _mla_attn(q_c.reshape(b, seq, -1), kv_c.reshape(b, seq, -1), p3, tq=t["mla_tq"], tk=tk)

        o_d = _dsa(p3, t3, vt4, smallt, tq=t["dsa_tq"], tk=tk, top=min(IDX_TOPK, seq // 4))

        x2 = _merge(x2, o_a.reshape(m, -1), o_b.reshape(m, -1), o_c.reshape(m, -1), o_d.reshape(m, -1),
                    gates, w_branch[l].astype(BF16), w_out[l].astype(BF16), tm=t["row_tm"])
        wgu = w_gate_up[l].astype(BF16)
        x2 = _ffn(x2, norm2_g[l][None], wgu[:, :dff], wgu[:, dff:], w_down[l].astype(BF16),
                  final_norm_g[None], final=(l == depth - 1), tm=t["row_tm"], tf=t["ffn_tf"])
    return x2.reshape(b, seq, d)
```

```python
import functools
import math

import numpy as np
import jax
import jax.numpy as jnp
from jax import lax
from jax.experimental import pallas as pl
from jax.experimental.pallas import tpu as pltpu

F32 = jnp.float32
BF16 = jnp.bfloat16
I32 = jnp.int32
I16 = jnp.int16

LANES = 128
SUBLANES = 8
PACKED_ROWS = 16
HALF_MIN = -32768
VMEM_LIMIT = 56 * 1024 * 1024

ROPE_THETA = 500000.0
NEG = -1e30
LOG2E = math.log2(math.e)
FORCE_SCORE = 1e9
PAD_SCORE = -3e38
EPS = 1e-6
INT_MIN = -2147483648

HEADS = 4
HEAD_W = 128
BR_WIDTH = HEADS * HEAD_W
DA_DIM = 64
NSA_DK = 128
CMP_LEN = 32
CMP_STRIDE = 16
SEL_LEN = 64
SEL_N = 16
WIN = 512
MLA_Q_LORA = 384
MLA_KV_LORA = 256
MLA_NOPE = 128
MLA_ROPE = 64
DSA_DIM = 128
IDX_HEADS = 8
IDX_DIM = 64
IDX_TOPK = 256

Z_AQ, Z_AK, Z_AV, Z_BQ, Z_DQ, Z_DK, Z_DV, Z_IQ = (i * 512 for i in range(8))
Z_CQ = 4096
Z_CKV = 4608
Z_KC, Z_KS, Z_KW, Z_VC, Z_VS, Z_VW, Z_KR, Z_IK, Z_SMALL = (4864 + i * 128 for i in range(9))
Z_WIDTH = 6144
P_AQ, P_AK, P_AV, P_BQ, P_DK, P_CQ = (i * 512 for i in range(6))
P_CKV = 3072
P_KS, P_KW, P_VS, P_VW, P_KR, P_IK = (3328 + i * 128 for i in range(6))
P_WIDTH = 4096
T_DQ, T_IQ = 0, 512
T_ROWS = 1024
VT_TILE = 256
SMALL_G = 0
SMALL_IW = 12

ROPE_KINDS = ((16, 64), (32, 128), (64, 64))
TAB_W = 3 * LANES


def _cparams(n_axes):
    return pltpu.CompilerParams(dimension_semantics=("arbitrary",) * n_axes,
                                vmem_limit_bytes=VMEM_LIMIT)


def _dot(a, b):
    return jnp.dot(a, b, preferred_element_type=F32)


def _dot_nt(a, b):
    return lax.dot_general(a, b, (((1,), (1,)), ((), ())), preferred_element_type=F32)


def _in_proj_columns():
    names = (("a_q", 512), ("a_k", 512), ("a_v", 512), ("b_q", 512),
             ("b_kc", 128), ("b_vc", 128), ("b_ks", 128), ("b_vs", 128),
             ("b_kw", 128), ("b_vw", 128), ("b_g", 12),
             ("c_q", 384), ("c_kv", 256), ("c_kr", 64),
             ("d_q", 512), ("d_k", 512), ("d_v", 512),
             ("d_iq", 512), ("d_ik", 64), ("d_iw", 8), ("gate", 4096))
    off, o = {}, 0
    for nm, n in names:
        off[nm] = (o, n)
        o += n
    idx = np.full((Z_WIDTH,), -1, np.int64)

    def put(dst, nm):
        s, n = off[nm]
        idx[dst:dst + n] = np.arange(s, s + n)

    put(Z_AQ, "a_q"); put(Z_AK, "a_k"); put(Z_AV, "a_v"); put(Z_BQ, "b_q")
    put(Z_DQ, "d_q"); put(Z_DK, "d_k"); put(Z_DV, "d_v"); put(Z_IQ, "d_iq")
    put(Z_CQ, "c_q"); put(Z_CKV, "c_kv")
    put(Z_KC, "b_kc"); put(Z_KS, "b_ks"); put(Z_KW, "b_kw")
    put(Z_VC, "b_vc"); put(Z_VS, "b_vs"); put(Z_VW, "b_vw")
    put(Z_KR, "c_kr"); put(Z_KR + 64, "c_kr")
    put(Z_IK, "d_ik"); put(Z_IK + 64, "d_ik")
    put(Z_SMALL + SMALL_G, "b_g"); put(Z_SMALL + SMALL_IW, "d_iw")
    return idx, off["gate"][0], o


def _take_cols(w, idx):
    safe = np.where(idx < 0, 0, idx)
    out = jnp.take(w, jnp.asarray(safe, I32), axis=1)
    return jnp.where(jnp.asarray(idx >= 0)[None, :], out, 0.0)


def _rope_table(seq, rot, period):
    half = rot // 2
    inv = jnp.power(jnp.float32(ROPE_THETA), -jnp.arange(0, rot, 2, dtype=F32) / rot)
    ang = jnp.arange(seq, dtype=F32)[:, None] * inv[None, :]
    cos, sin = jnp.cos(ang), jnp.sin(ang)
    lane = np.arange(LANES) % period
    in1 = lane < half
    in2 = (lane >= half) & (lane < 2 * half)
    fidx = np.where(in1, lane, np.where(in2, lane - half, 0))
    cosl, sinl = cos[:, fidx], sin[:, fidx]
    c = jnp.where(jnp.asarray(in1 | in2)[None], cosl, 1.0)
    s1 = jnp.where(jnp.asarray(in1)[None], -sinl, 0.0)
    s2 = jnp.where(jnp.asarray(in2)[None], sinl, 0.0)
    return jnp.concatenate([c, s1, s2], axis=1)


def _rope128(x, tab, half):
    return (x * tab[:, 0:LANES]
            + pltpu.roll(x, LANES - half, 1) * tab[:, LANES:2 * LANES]
            + pltpu.roll(x, half, 1) * tab[:, 2 * LANES:3 * LANES])


def _norm_matmul_kernel(x_ref, g_ref, w_ref, o_ref, h_scr, *, sigmoid):
    @pl.when(pl.program_id(1) == 0)
    def _():
        x = x_ref[...]
        ms = jnp.mean(x * x, axis=-1, keepdims=True)
        h_scr[...] = (x * lax.rsqrt(ms + EPS) * g_ref[...]).astype(BF16)

    z = _dot(h_scr[...], w_ref[...])
    if sigmoid:
        z = jax.nn.sigmoid(z)
    o_ref[...] = z.astype(o_ref.dtype)


def _norm_matmul(x2, g, w, *, out_dtype, sigmoid, tm, tn, name):
    m, d = x2.shape
    n = w.shape[1]
    return pl.pallas_call(
        functools.partial(_norm_matmul_kernel, sigmoid=sigmoid),
        out_shape=jax.ShapeDtypeStruct((m, n), out_dtype),
        grid=(m // tm, n // tn),
        in_specs=[pl.BlockSpec((tm, d), lambda i, j: (i, 0)),
                  pl.BlockSpec((1, d), lambda i, j: (0, 0)),
                  pl.BlockSpec((d, tn), lambda i, j: (0, j))],
        out_specs=pl.BlockSpec((tm, tn), lambda i, j: (i, j)),
        scratch_shapes=[pltpu.VMEM((tm, d), BF16)],
        compiler_params=_cparams(2),
        name=name,
    )(x2, g, w)


def _prep_kernel(z_ref, tab_ref, gq_ref, gkv_ref, gik_ref,
                 p_ref, t_ref, vt_ref, kc_ref, vc_ref, small_ref, smallt_ref):
    def zc(off, c=0):
        return z_ref[:, off + c * LANES:off + (c + 1) * LANES]

    def tab(kind):
        return tab_ref[:, kind * TAB_W:(kind + 1) * TAB_W]

    def put(off, c, v):
        p_ref[:, off + c * LANES:off + (c + 1) * LANES] = v.astype(BF16)

    def rope(off, c, kind):
        return _rope128(zc(off, c), tab(kind), ROPE_KINDS[kind][0] // 2)

    for zoff, poff, kind in ((Z_AQ, P_AQ, 0), (Z_AK, P_AK, 0), (Z_BQ, P_BQ, 1), (Z_DK, P_DK, 1)):
        for c in range(4):
            put(poff, c, rope(zoff, c, kind))
    for c in range(4):
        put(P_AV, c, zc(Z_AV, c))
    put(P_VS, 0, zc(Z_VS)); put(P_VW, 0, zc(Z_VW))
    put(P_KS, 0, rope(Z_KS, 0, 1)); put(P_KW, 0, rope(Z_KW, 0, 1))
    put(P_KR, 0, rope(Z_KR, 0, 2))
    kc_ref[...] = rope(Z_KC, 0, 1).astype(BF16)
    vc_ref[...] = zc(Z_VC).astype(BF16)

    for zoff, toff, kind in ((Z_DQ, T_DQ, 1), (Z_IQ, T_IQ, 0)):
        for c in range(4):
            t_ref[0, toff + c * LANES:toff + (c + 1) * LANES, :] = rope(zoff, c, kind).T.astype(BF16)
    for c in range(4):
        vt_ref[0, 0, c * LANES:(c + 1) * LANES, :] = zc(Z_DV, c).T.astype(BF16)

    cq = z_ref[:, Z_CQ:Z_CQ + 512]
    ms = jnp.sum(cq * cq, axis=-1, keepdims=True) * (1.0 / MLA_Q_LORA)
    p_ref[:, P_CQ:P_CQ + 512] = (cq * lax.rsqrt(ms + EPS) * gq_ref[...]).astype(BF16)
    ckv = z_ref[:, Z_CKV:Z_CKV + MLA_KV_LORA]
    ms = jnp.mean(ckv * ckv, axis=-1, keepdims=True)
    p_ref[:, P_CKV:P_CKV + MLA_KV_LORA] = (ckv * lax.rsqrt(ms + EPS) * gkv_ref[...]).astype(BF16)

    ik = zc(Z_IK)
    ms = jnp.mean(ik * ik, axis=-1, keepdims=True)
    ikn = ik * lax.rsqrt(ms + EPS) * gik_ref[...]
    put(P_IK, 0, _rope128(ikn, tab(0), ROPE_KINDS[0][0] // 2))

    sm = zc(Z_SMALL)
    lane = lax.broadcasted_iota(I32, sm.shape, 1)
    iw_scale = IDX_HEADS ** -0.5 * IDX_DIM ** -0.5
    small = jnp.where(lane < SMALL_IW, jax.nn.sigmoid(sm), sm * iw_scale)
    small_ref[...] = small
    smallt_ref[0] = small.T


def _prep(z, tab, gq, gkv, gik, *, batch, seq, ts):
    m = z.shape[0]
    spb = seq // ts
    assert ts == VT_TILE
    row = lambda w: pl.BlockSpec((ts, w), lambda i: (i, 0))
    return pl.pallas_call(
        _prep_kernel,
        out_shape=(jax.ShapeDtypeStruct((m, P_WIDTH), BF16),
                   jax.ShapeDtypeStruct((batch, T_ROWS, seq), BF16),
                   jax.ShapeDtypeStruct((batch, spb, BR_WIDTH, VT_TILE), BF16),
                   jax.ShapeDtypeStruct((m, LANES), BF16),
                   jax.ShapeDtypeStruct((m, LANES), BF16),
                   jax.ShapeDtypeStruct((m, LANES), F32),
                   jax.ShapeDtypeStruct((batch, LANES, seq), F32)),
        grid=(m // ts,),
        in_specs=[row(Z_WIDTH),
                  pl.BlockSpec((ts, 3 * TAB_W), lambda i: (i % spb, 0)),
                  pl.BlockSpec((1, 512), lambda i: (0, 0)),
                  pl.BlockSpec((1, MLA_KV_LORA), lambda i: (0, 0)),
                  pl.BlockSpec((1, LANES), lambda i: (0, 0))],
        out_specs=(row(P_WIDTH),
                   pl.BlockSpec((1, T_ROWS, ts), lambda i: (i // spb, 0, i % spb)),
                   pl.BlockSpec((1, 1, BR_WIDTH, VT_TILE), lambda i: (i // spb, i % spb, 0, 0)),
                   row(LANES), row(LANES), row(LANES),
                   pl.BlockSpec((1, LANES, ts), lambda i: (i // spb, 0, i % spb))),
        compiler_params=_cparams(1),
        name="prep",
    )(z, tab, gq, gkv, gik)


def _softmax_init(mx_scr, l_scr, acc_scr):
    mx_scr[...] = jnp.full(mx_scr.shape, NEG, F32)
    l_scr[...] = jnp.zeros(l_scr.shape, F32)
    acc_scr[...] = jnp.zeros(acc_scr.shape, F32)


def _score_store(g, j, s, s_scr, mx_scr):
    s_scr[g, j] = s
    m = s[:, 0:LANES]
    for c in range(1, s.shape[1] // LANES):
        m = jnp.maximum(m, s[:, c * LANES:(c + 1) * LANES])
    mx_scr[g] = jnp.maximum(mx_scr[g], m)


def _row_max_finish(mx_scr):
    for g in range(mx_scr.shape[0]):
        m = jnp.max(mx_scr[g], axis=-1, keepdims=True)
        mx_scr[g] = jnp.broadcast_to(m, mx_scr.shape[1:])


def _prob_accumulate(g, j, v_tile, s_scr, mx_scr, l_scr, acc_scr):
    mb = mx_scr[g]
    s = s_scr[g, j]
    ps = [jnp.exp2(s[:, c * LANES:(c + 1) * LANES] - mb) for c in range(s.shape[1] // LANES)]
    tot = ps[0]
    for p in ps[1:]:
        tot = tot + p
    l_scr[g] += tot
    acc_scr[g] += _dot(jnp.concatenate(ps, axis=1).astype(BF16), v_tile)


def _softmax_out(g, l_scr, acc_scr):
    return acc_scr[g] / jnp.sum(l_scr[g], axis=-1, keepdims=True)


def _causal_tiles(step, n_full):
    def body(j, carry):
        step(j, False)
        return carry
    lax.fori_loop(0, n_full, body, 0)
    step(n_full, True)


def _softmax_scratch(groups, n_tiles, rows, tk):
    return [pltpu.VMEM((groups, n_tiles, rows, tk), F32),
            pltpu.VMEM((groups, rows, LANES), F32),
            pltpu.VMEM((groups, rows, LANES), F32),
            pltpu.VMEM((groups, rows, HEAD_W), F32)]


def _diff_attn_kernel(q_ref, k_ref, v_ref, lv_ref, g_ref, o_ref, s_scr, mx_scr, l_scr, acc_scr,
                      *, tq, tk, lam_init):
    qs = pl.program_id(1) * tq
    n_full = qs // tk
    scale = DA_DIM ** -0.5 * LOG2E
    lv = lv_ref[...]
    lam = (jnp.exp(jnp.sum(lv[0:1] * lv[1:2], axis=-1, keepdims=True))
           - jnp.exp(jnp.sum(lv[2:3] * lv[3:4], axis=-1, keepdims=True)) + lam_init)
    lane = lax.broadcasted_iota(I32, (tq, HEAD_W), 1)
    row_t = qs + lax.broadcasted_iota(I32, (2 * tq, 1), 0) % tq
    col0 = lax.broadcasted_iota(I32, (2 * tq, tk), 1)
    _softmax_init(mx_scr, l_scr, acc_scr)

    def scores(j, masked):
        ks = pl.multiple_of(j * tk, tk)
        for h in range(HEADS):
            hs = slice(h * HEAD_W, (h + 1) * HEAD_W)
            qh = q_ref[0, :, hs]
            zero = jnp.zeros_like(qh)
            q2 = jnp.concatenate([jnp.where(lane < DA_DIM, qh, zero),
                                  jnp.where(lane >= DA_DIM, qh, zero)], axis=0)
            s = _dot_nt(q2, k_ref[0, pl.ds(ks, tk), hs]) * scale
            if masked:
                s = jnp.where(col0 + ks <= row_t, s, NEG)
            _score_store(h, j, s, s_scr, mx_scr)

    _causal_tiles(scores, n_full)
    _row_max_finish(mx_scr)

    def probs(j, carry):
        ks = pl.multiple_of(j * tk, tk)
        for h in range(HEADS):
            v_tile = v_ref[0, pl.ds(ks, tk), h * HEAD_W:(h + 1) * HEAD_W]
            _prob_accumulate(h, j, v_tile, s_scr, mx_scr, l_scr, acc_scr)
        return carry

    lax.fori_loop(0, n_full + 1, probs, 0)
    for h in range(HEADS):
        o2 = _softmax_out(h, l_scr, acc_scr)
        o = o2[:tq] - lam * o2[tq:]
        ms = jnp.mean(o * o, axis=-1, keepdims=True)
        o = o * lax.rsqrt(ms + EPS) * g_ref[...]
        o_ref[0, :, h * HEAD_W:(h + 1) * HEAD_W] = (o * (1.0 - lam_init)).astype(BF16)


def _diff_attn(p3, lv, g, *, lam_init, tq, tk):
    b, s, _ = p3.shape
    return pl.pallas_call(
        functools.partial(_diff_attn_kernel, tq=tq, tk=tk, lam_init=lam_init),
        out_shape=jax.ShapeDtypeStruct((b, s, BR_WIDTH), BF16),
        grid=(b, s // tq),
        in_specs=[pl.BlockSpec((1, tq, 512), lambda bi, i: (bi, i, P_AQ // 512)),
                  pl.BlockSpec((1, s, 512), lambda bi, i: (bi, 0, P_AK // 512)),
                  pl.BlockSpec((1, s, 512), lambda bi, i: (bi, 0, P_AV // 512)),
                  pl.BlockSpec((4, DA_DIM), lambda bi, i: (0, 0)),
                  pl.BlockSpec((1, HEAD_W), lambda bi, i: (0, 0))],
        out_specs=pl.BlockSpec((1, tq, BR_WIDTH), lambda bi, i: (bi, i, 0)),
        scratch_shapes=_softmax_scratch(HEADS, s // tk, 2 * tq, tk),
        compiler_params=_cparams(2),
        name="diff_attn",
    )(p3, p3, p3, lv, g)


def _nsa_compress_kernel(gk_ref, gv_ref, w1k_ref, w1v_ref, pek_ref, pev_ref,
                         w1kf_ref, w1vf_ref, w2k_ref, w2v_ref, kc_ref, vc_ref):
    def one(g_ref, w1cat_ref, pe_ref, w1f_ref, w2_ref, o_ref):
        y = _dot(g_ref[0], w1cat_ref[...])
        n = y.shape[0]
        nxt = pltpu.roll(y[:, HEAD_W:], n - 1, 0)
        c = _dot(pe_ref[...], w1f_ref[...])[0:1]
        hid = jax.nn.gelu(y[:, :HEAD_W] + nxt + c)
        o_ref[0] = _dot(hid.astype(BF16), w2_ref[...]).astype(BF16)

    one(gk_ref, w1k_ref, pek_ref, w1kf_ref, w2k_ref, kc_ref)
    one(gv_ref, w1v_ref, pev_ref, w1vf_ref, w2v_ref, vc_ref)


def _nsa_compress(gk, gv, w1k_cat, w1v_cat, pek, pev, w1k, w1v, w2k, w2v):
    b, ng, gw = gk.shape
    full = lambda shape: pl.BlockSpec(shape, lambda bi: (0,) * len(shape))
    return pl.pallas_call(
        _nsa_compress_kernel,
        out_shape=(jax.ShapeDtypeStruct((b, ng, HEAD_W), BF16),
                   jax.ShapeDtypeStruct((b, ng, HEAD_W), BF16)),
        grid=(b,),
        in_specs=[pl.BlockSpec((1, ng, gw), lambda bi: (bi, 0, 0)),
                  pl.BlockSpec((1, ng, gw), lambda bi: (bi, 0, 0)),
                  full(w1k_cat.shape), full(w1v_cat.shape), full(pek.shape), full(pev.shape),
                  full(w1k.shape), full(w1v.shape), full(w2k.shape), full(w2v.shape)],
        out_specs=(pl.BlockSpec((1, ng, HEAD_W), lambda bi: (bi, 0, 0)),
                   pl.BlockSpec((1, ng, HEAD_W), lambda bi: (bi, 0, 0))),
        compiler_params=_cparams(1),
        name="nsa_compress",
    )(gk, gv, w1k_cat, w1v_cat, pek, pev, w1k, w1v, w2k, w2v)


NSA_GROUPS = 2


def _nsa_kernel(q_ref, kc_ref, vc_ref, ks_ref, vs_ref, kw_ref, vw_ref, small_ref, ov_ref, e_ref,
                o_ref, s_scr, mx_scr, l_scr, acc_scr, *, tq, tk, seq):
    qs = pl.program_id(1) * tq
    scale = NSA_DK ** -0.5
    ns = seq // SEL_LEN
    n_sel = min(SEL_N, ns)
    r = HEADS * tq
    rg = r // NSA_GROUPS
    q4 = jnp.concatenate([q_ref[0, :, h * HEAD_W:(h + 1) * HEAD_W] for h in range(HEADS)], axis=0)
    t1 = qs + lax.broadcasted_iota(I32, (tq, 1), 0)
    t4 = qs + lax.broadcasted_iota(I32, (r, 1), 0) % tq

    kc = kc_ref[0]
    nc_pad = kc.shape[0]
    sc = _dot_nt(q4, kc) * scale
    c_end = lax.broadcasted_iota(I32, (r, nc_pad), 1) * CMP_STRIDE + (CMP_LEN - 1)
    cmask = c_end <= t4
    mx = jnp.max(jnp.where(cmask, sc, NEG), axis=-1, keepdims=True)
    e = jnp.where(cmask, jnp.exp(sc - mx), 0.0)
    den = jnp.sum(e, axis=-1, keepdims=True)
    pc = e / jnp.where(den > 0.0, den, 1.0)
    o_cmp = _dot(pc.astype(BF16), vc_ref[0])

    psum = pc[0:tq] + pc[tq:2 * tq] + pc[2 * tq:3 * tq] + pc[3 * tq:4 * tq]
    ov = ov_ref[...]
    hi = psum.astype(BF16)
    r1 = psum - hi.astype(F32)
    mid = r1.astype(BF16)
    lo = (r1 - mid.astype(F32)).astype(BF16)
    imp = _dot(hi, ov) + _dot(mid, ov) + _dot(lo, ov)

    blk = lax.broadcasted_iota(I32, (tq, LANES), 1)
    cur = t1 // SEL_LEN
    forced = (blk == 0) | (blk == cur) | (blk == cur - 1)
    visible = blk * SEL_LEN <= t1
    score = jnp.where(visible, jnp.where(forced, FORCE_SCORE, imp), NEG)
    score = jnp.where(blk < ns, score, PAD_SCORE)
    ns_pad = -(-ns // SUBLANES) * SUBLANES
    score_t = score.T[:ns_pad]
    blk_t = lax.broadcasted_iota(I32, (ns_pad, tq), 0)
    rank = jnp.zeros((ns_pad, tq), I32)
    for jp in range(ns):
        row = score_t[jp:jp + 1, :]
        later = (blk_t > jp).astype(I32)
        rank = rank + jnp.where(row > score_t, 1, jnp.where(row == score_t, later, 0))
    sel_t = jnp.where(rank < n_sel, 1.0, 0.0)
    if ns_pad < LANES:
        sel_t = jnp.concatenate([sel_t, jnp.zeros((LANES - ns_pad, tq), F32)], axis=0)
    selb = sel_t.T.astype(BF16)

    _softmax_init(mx_scr, l_scr, acc_scr)
    col0 = lax.broadcasted_iota(I32, (rg, tk), 1)
    tg = qs + lax.broadcasted_iota(I32, (rg, 1), 0) % tq
    n_tiles = qs // tk + 1

    def scores(j, masked):
        ks0 = pl.multiple_of(j * tk, tk)
        mt = _dot(selb, e_ref[j])
        mg = jnp.concatenate([mt] * (rg // tq), axis=0)
        k_tile = ks_ref[0, pl.ds(ks0, tk), :]
        for g in range(NSA_GROUPS):
            s = _dot_nt(q4[g * rg:(g + 1) * rg], k_tile) * (scale * LOG2E)
            s = jnp.where(mg > 0.5, s, NEG)
            if masked:
                s = jnp.where(col0 + ks0 <= tg, s, NEG)
            _score_store(g, j, s, s_scr, mx_scr)

    _causal_tiles(scores, n_tiles - 1)
    _row_max_finish(mx_scr)

    def probs(j, carry):
        ks0 = pl.multiple_of(j * tk, tk)
        v_tile = vs_ref[0, pl.ds(ks0, tk), :]
        for g in range(NSA_GROUPS):
            _prob_accumulate(g, j, v_tile, s_scr, mx_scr, l_scr, acc_scr)
        return carry

    lax.fori_loop(0, n_tiles, probs, 0)
    o_slc = jnp.concatenate([_softmax_out(g, l_scr, acc_scr) for g in range(NSA_GROUPS)], axis=0)

    wspan = WIN + tq
    start = pl.multiple_of(jnp.maximum(qs - WIN, 0), tq)
    sw = _dot_nt(q4, kw_ref[0, pl.ds(start, wspan), :]) * scale
    dist = t4 - (start + lax.broadcasted_iota(I32, (r, wspan), 1))
    sw = jnp.where(pltpu.bitcast(dist, jnp.uint32) < jnp.uint32(WIN), sw, NEG)
    e = jnp.exp(sw - jnp.max(sw, axis=-1, keepdims=True))
    pw = e / jnp.sum(e, axis=-1, keepdims=True)
    o_win = _dot(pw.astype(BF16), vw_ref[0, pl.ds(start, wspan), :])

    gates = small_ref[0]
    for h in range(HEADS):
        rows = slice(h * tq, (h + 1) * tq)
        g0 = gates[:, SMALL_G + 3 * h:SMALL_G + 3 * h + 1]
        g1 = gates[:, SMALL_G + 3 * h + 1:SMALL_G + 3 * h + 2]
        g2 = gates[:, SMALL_G + 3 * h + 2:SMALL_G + 3 * h + 3]
        o = g0 * o_cmp[rows] + g1 * o_slc[rows] + g2 * o_win[rows]
        o_ref[0, :, h * HEAD_W:(h + 1) * HEAD_W] = o.astype(BF16)


def _nsa(p3, kc, vc, small3, ov, emat, *, tq, tk):
    b, s, _ = p3.shape
    ng = kc.shape[1]
    col = lambda off: (lambda bi, i: (bi, 0, off // LANES))
    return pl.pallas_call(
        functools.partial(_nsa_kernel, tq=tq, tk=tk, seq=s),
        out_shape=jax.ShapeDtypeStruct((b, s, BR_WIDTH), BF16),
        grid=(b, s // tq),
        in_specs=[pl.BlockSpec((1, tq, 512), lambda bi, i: (bi, i, P_BQ // 512)),
                  pl.BlockSpec((1, ng, HEAD_W), lambda bi, i: (bi, 0, 0)),
                  pl.BlockSpec((1, ng, HEAD_W), lambda bi, i: (bi, 0, 0)),
                  pl.BlockSpec((1, s, LANES), col(P_KS)),
                  pl.BlockSpec((1, s, LANES), col(P_VS)),
                  pl.BlockSpec((1, s, LANES), col(P_KW)),
                  pl.BlockSpec((1, s, LANES), col(P_VW)),
                  pl.BlockSpec((1, tq, LANES), lambda bi, i: (bi, i, 0)),
                  pl.BlockSpec(ov.shape, lambda bi, i: (0, 0)),
                  pl.BlockSpec(emat.shape, lambda bi, i: (0, 0, 0))],
        out_specs=pl.BlockSpec((1, tq, BR_WIDTH), lambda bi, i: (bi, i, 0)),
        scratch_shapes=_softmax_scratch(NSA_GROUPS, s // tk, HEADS * tq // NSA_GROUPS, tk),
        compiler_params=_cparams(2),
        name="nsa_attn",
    )(p3, kc, vc, p3, p3, p3, p3, small3, ov, emat)


def _mla_up_kernel(p_ref, ckv_ref, tab_ref, wq_ref, wkv_ref, q_ref, kv_ref):
    q = _dot(p_ref[...], wq_ref[...])
    nn = HEADS * MLA_NOPE
    q_ref[:, :nn] = q[:, :nn].astype(BF16)
    for c in range(nn // LANES, (nn + HEADS * MLA_ROPE) // LANES):
        tile = _rope128(q[:, c * LANES:(c + 1) * LANES], tab_ref[...], MLA_ROPE // 2)
        q_ref[:, c * LANES:(c + 1) * LANES] = tile.astype(BF16)
    kv_ref[...] = _dot(ckv_ref[...], wkv_ref[...]).astype(BF16)


def _mla_up(p2, tab, wq, wkv, *, seq, ts):
    m = p2.shape[0]
    spb = seq // ts
    nq = wq.shape[1]
    nkv = wkv.shape[1]
    return pl.pallas_call(
        _mla_up_kernel,
        out_shape=(jax.ShapeDtypeStruct((m, nq), BF16), jax.ShapeDtypeStruct((m, nkv), BF16)),
        grid=(m // ts,),
        in_specs=[pl.BlockSpec((ts, 512), lambda i: (i, P_CQ // 512)),
                  pl.BlockSpec((ts, MLA_KV_LORA), lambda i: (i, P_CKV // MLA_KV_LORA)),
                  pl.BlockSpec((ts, TAB_W), lambda i: (i % spb, 2)),
                  pl.BlockSpec(wq.shape, lambda i: (0, 0)),
                  pl.BlockSpec(wkv.shape, lambda i: (0, 0))],
        out_specs=(pl.BlockSpec((ts, nq), lambda i: (i, 0)),
                   pl.BlockSpec((ts, nkv), lambda i: (i, 0))),
        compiler_params=_cparams(1),
        name="mla_up",
    )(p2, p2, tab, wq, wkv)


def _mla_attn_kernel(qn_ref, qr_ref, kn_ref, kr_ref, v_ref, o_ref, s_scr, mx_scr, l_scr, acc_scr,
                     *, tq, tk):
    qs = pl.program_id(1) * tq
    n_full = qs // tk
    scale = (MLA_NOPE + MLA_ROPE) ** -0.5 * LOG2E
    lane = lax.broadcasted_iota(I32, (tq, LANES), 1)
    row_t = qs + lax.broadcasted_iota(I32, (tq, 1), 0)
    col0 = lax.broadcasted_iota(I32, (tq, tk), 1)
    _softmax_init(mx_scr, l_scr, acc_scr)

    def scores(j, masked):
        ks = pl.multiple_of(j * tk, tk)
        kr_tile = kr_ref[0, pl.ds(ks, tk), :]
        for h in range(HEADS):
            hs = slice(h * HEAD_W, (h + 1) * HEAD_W)
            pair = qr_ref[0, :, (h // 2) * LANES:(h // 2 + 1) * LANES]
            keep = (lane < MLA_ROPE) if h % 2 == 0 else (lane >= MLA_ROPE)
            qr = jnp.where(keep, pair, jnp.zeros_like(pair))
            s = _dot_nt(jnp.concatenate([qn_ref[0, :, hs], qr], axis=1),
                        jnp.concatenate([kn_ref[0, pl.ds(ks, tk), hs], kr_tile], axis=1)) * scale
            if masked:
                s = jnp.where(col0 + ks <= row_t, s, NEG)
            _score_store(h, j, s, s_scr, mx_scr)

    _causal_tiles(scores, n_full)
    _row_max_finish(mx_scr)

    def probs(j, carry):
        ks = pl.multiple_of(j * tk, tk)
        for h in range(HEADS):
            v_tile = v_ref[0, pl.ds(ks, tk), h * HEAD_W:(h + 1) * HEAD_W]
            _prob_accumulate(h, j, v_tile, s_scr, mx_scr, l_scr, acc_scr)
        return carry

    lax.fori_loop(0, n_full + 1, probs, 0)
    for h in range(HEADS):
        o_ref[0, :, h * HEAD_W:(h + 1) * HEAD_W] = _softmax_out(h, l_scr, acc_scr).astype(BF16)


def _mla_attn(q3, kv3, p3, *, tq, tk):
    b, s, _ = q3.shape
    return pl.pallas_call(
        functools.partial(_mla_attn_kernel, tq=tq, tk=tk),
        out_shape=jax.ShapeDtypeStruct((b, s, BR_WIDTH), BF16),
        grid=(b, s // tq),
        in_specs=[pl.BlockSpec((1, tq, 512), lambda bi, i: (bi, i, 0)),
                  pl.BlockSpec((1, tq, 256), lambda bi, i: (bi, i, 2)),
                  pl.BlockSpec((1, s, 512), lambda bi, i: (bi, 0, 0)),
                  pl.BlockSpec((1, s, LANES), lambda bi, i: (bi, 0, P_KR // LANES)),
                  pl.BlockSpec((1, s, 512), lambda bi, i: (bi, 0, 1))],
        out_specs=pl.BlockSpec((1, tq, BR_WIDTH), lambda bi, i: (bi, i, 0)),
        scratch_shapes=_softmax_scratch(HEADS, s // tk, tq, tk),
        compiler_params=_cparams(2),
        name="mla_attn",
    )(q3, q3, kv3, p3, kv3)


def _sortable_key(x):
    bits = pltpu.bitcast(x + 0.0, I32)
    return bits ^ (lax.shift_right_arithmetic(bits, 31) & 0x7FFFFFFF)


def _fold_rows(x, op):
    n = x.shape[0] // SUBLANES
    return op(x.reshape(n, SUBLANES, x.shape[1]), axis=0)


def _count16(half_scr, n_tiles, pred, tq):
    def count_tile(j, cnt):
        hit = pred(half_scr[j]).astype(I16)
        parts = [hit[r:r + PACKED_ROWS] for r in range(0, hit.shape[0], PACKED_ROWS)]
        while len(parts) > 1:
            parts = [a + b for a, b in zip(parts[0::2], parts[1::2])]
        return cnt + parts[0]

    cnt = lax.fori_loop(0, n_tiles, count_tile, jnp.zeros((PACKED_ROWS, tq), I16))
    return jnp.sum(cnt.astype(I32), axis=0, keepdims=True)


def _bisect16(half_scr, n_tiles, need, tq):
    def bit_body(i, th):
        cand = th + lax.shift_left(jnp.int32(1), 15 - i)
        c16 = cand.astype(I16)
        total = _count16(half_scr, n_tiles, lambda x: x >= c16, tq)
        return jnp.where(total >= need, cand, th)

    return lax.fori_loop(0, 16, bit_body, jnp.full((1, tq), HALF_MIN, I32))


def _dsa_kernel(qt_ref, iqt_ref, iwt_ref, k_ref, ik_ref, vt_ref, o_ref,
                key_scr, half_scr, s_scr, mx_scr, l_scr, acc_scr, *, tq, tk, top):
    qs = pl.program_id(1) * tq
    n_tiles = (qs + tq - 1) // tk + 1
    scale = DSA_DIM ** -0.5 * LOG2E
    t_lane = qs + lax.broadcasted_iota(I32, (tk, tq), 1)
    krow0 = lax.broadcasted_iota(I32, (tk, tq), 0)
    half_rows = lax.broadcasted_iota(I32, (LANES, tq), 0) < IDX_DIM
    vt_per_tile = tk // VT_TILE

    def score_tile(j, carry):
        ks = pl.multiple_of(j * tk, tk)
        ikt = ik_ref[0, pl.ds(ks, tk), :]
        acc = jnp.zeros((tk, tq), F32)
        for h in range(IDX_HEADS):
            pair = iqt_ref[0, (h // 2) * LANES:(h // 2 + 1) * LANES, :]
            keep = half_rows if h % 2 == 0 else jnp.logical_not(half_rows)
            iq_h = jnp.where(keep, pair, jnp.zeros_like(pair))
            w_h = iwt_ref[0, SMALL_IW + h:SMALL_IW + h + 1, :]
            acc = acc + w_h * jnp.maximum(_dot(ikt, iq_h), 0.0)
        key = jnp.where(krow0 + ks <= t_lane, _sortable_key(acc), INT_MIN)
        key_scr[j] = key
        half_scr[j] = lax.shift_right_arithmetic(key, 16).astype(I16)
        return carry

    lax.fori_loop(0, n_tiles, score_tile, 0)

    th_hi = _bisect16(half_scr, n_tiles, top, tq)
    hi16 = th_hi.astype(I16)
    need_lo = top - _count16(half_scr, n_tiles, lambda x: x > hi16, tq)

    def low_tile(j, carry):
        key = key_scr[j]
        lo = (key & 0xFFFF) + HALF_MIN
        same_hi = lax.shift_right_arithmetic(key, 16) == th_hi
        half_scr[j] = jnp.where(same_hi, lo, HALF_MIN).astype(I16)
        return carry

    lax.fori_loop(0, n_tiles, low_tile, 0)
    th_lo = _bisect16(half_scr, n_tiles, need_lo, tq)
    theta = lax.shift_left(th_hi, 16) + (th_lo - HALF_MIN)
    theta = jnp.maximum(theta, INT_MIN + 1)

    mx_scr[...] = jnp.full(mx_scr.shape, NEG, F32)
    l_scr[...] = jnp.zeros(l_scr.shape, F32)
    acc_scr[...] = jnp.zeros(acc_scr.shape, F32)

    def scores(j, carry):
        ks = pl.multiple_of(j * tk, tk)
        sel = key_scr[j] >= theta
        for h in range(HEADS):
            hs = slice(h * HEAD_W, (h + 1) * HEAD_W)
            s = _dot(k_ref[0, pl.ds(ks, tk), hs], qt_ref[0, hs, :]) * scale
            s = jnp.where(sel, s, NEG)
            s_scr[h, j] = s
            mx_scr[h] = jnp.maximum(mx_scr[h], _fold_rows(s, jnp.max))
        return carry

    lax.fori_loop(0, n_tiles, scores, 0)
    for h in range(HEADS):
        m = jnp.max(mx_scr[h], axis=0, keepdims=True)
        mx_scr[h] = jnp.broadcast_to(m, (SUBLANES, tq))

    def probs(j, carry):
        for h in range(HEADS):
            p = jnp.exp2(s_scr[h, j] - mx_scr[h][0:1])
            l_scr[h] += _fold_rows(p, jnp.sum)
            pb = p.astype(BF16)
            for c in range(vt_per_tile):
                vt = vt_ref[0, j * vt_per_tile + c, h * HEAD_W:(h + 1) * HEAD_W, :]
                acc_scr[h] += _dot(vt, pb[c * VT_TILE:(c + 1) * VT_TILE])
        return carry

    lax.fori_loop(0, n_tiles, probs, 0)
    for h in range(HEADS):
        ot = acc_scr[h] / jnp.sum(l_scr[h], axis=0, keepdims=True)
        o_ref[0, :, h * HEAD_W:(h + 1) * HEAD_W] = ot.T.astype(BF16)


def _dsa(p3, t3, vt4, smallt, *, tq, tk, top):
    b, s, _ = p3.shape
    n_vt = vt4.shape[1]
    return pl.pallas_call(
        functools.partial(_dsa_kernel, tq=tq, tk=tk, top=top),
        out_shape=jax.ShapeDtypeStruct((b, s, BR_WIDTH), BF16),
        grid=(b, s // tq),
        in_specs=[pl.BlockSpec((1, 512, tq), lambda bi, i: (bi, T_DQ // 512, i)),
                  pl.BlockSpec((1, 512, tq), lambda bi, i: (bi, T_IQ // 512, i)),
                  pl.BlockSpec((1, LANES, tq), lambda bi, i: (bi, 0, i)),
                  pl.BlockSpec((1, s, 512), lambda bi, i: (bi, 0, P_DK // 512)),
                  pl.BlockSpec((1, s, LANES), lambda bi, i: (bi, 0, P_IK // LANES)),
                  pl.BlockSpec((1, n_vt, BR_WIDTH, VT_TILE), lambda bi, i: (bi, 0, 0, 0))],
        out_specs=pl.BlockSpec((1, tq, BR_WIDTH), lambda bi, i: (bi, i, 0)),
        scratch_shapes=[pltpu.VMEM((s // tk, tk, tq), I32),
                        pltpu.VMEM((s // tk, tk, tq), I16),
                        pltpu.VMEM((HEADS, s // tk, tk, tq), F32),
                        pltpu.VMEM((HEADS, SUBLANES, tq), F32),
                        pltpu.VMEM((HEADS, SUBLANES, tq), F32),
                        pltpu.VMEM((HEADS, HEAD_W, tq), F32)],
        compiler_params=_cparams(2),
        name="dsa_attn",
    )(t3, t3, smallt, p3, p3, vt4)


def _merge_kernel(x_ref, oa_ref, ob_ref, oc_ref, od_ref, g_ref, wb_ref, wo_ref, o_ref):
    d = x_ref.shape[1]
    acc = jnp.zeros(x_ref.shape, F32)
    for n, br_ref in enumerate((oa_ref, ob_ref, oc_ref, od_ref)):
        br = _dot(br_ref[...], wb_ref[n])
        acc = acc + g_ref[:, n * d:(n + 1) * d].astype(F32) * br
    o_ref[...] = x_ref[...] + _dot(acc.astype(BF16), wo_ref[...])


def _merge(x2, oa, ob, oc, od, gates, wb, wo, *, tm):
    m, d = x2.shape
    row = lambda w: pl.BlockSpec((tm, w), lambda i: (i, 0))
    return pl.pallas_call(
        _merge_kernel,
        out_shape=jax.ShapeDtypeStruct((m, d), F32),
        grid=(m // tm,),
        in_specs=[row(d), row(BR_WIDTH), row(BR_WIDTH), row(BR_WIDTH), row(BR_WIDTH),
                  row(gates.shape[1]),
                  pl.BlockSpec(wb.shape, lambda i: (0, 0, 0)),
                  pl.BlockSpec(wo.shape, lambda i: (0, 0))],
        out_specs=row(d),
        compiler_params=_cparams(1),
        name="merge",
    )(x2, oa, ob, oc, od, gates, wb, wo)


def _ffn_kernel(x_ref, g_ref, wg_ref, wu_ref, wd_ref, gf_ref, o_ref, h_scr, acc_scr, *, final):
    j = pl.program_id(1)

    @pl.when(j == 0)
    def _():
        x = x_ref[...]
        ms = jnp.mean(x * x, axis=-1, keepdims=True)
        h_scr[...] = (x * lax.rsqrt(ms + EPS) * g_ref[...]).astype(BF16)
        acc_scr[...] = jnp.zeros(acc_scr.shape, F32)

    h = h_scr[...]
    a = jax.nn.silu(_dot(h, wg_ref[...])) * _dot(h, wu_ref[...])
    acc_scr[...] += _dot(a.astype(BF16), wd_ref[...])

    @pl.when(j == pl.num_programs(1) - 1)
    def _():
        y = x_ref[...] + acc_scr[...]
        if final:
            ms = jnp.mean(y * y, axis=-1, keepdims=True)
            y = y * lax.rsqrt(ms + EPS) * gf_ref[...]
        o_ref[...] = y


def _ffn(x2, g, wg, wu, wd, gf, *, final, tm, tf):
    m, d = x2.shape
    dff = wg.shape[1]
    return pl.pallas_call(
        functools.partial(_ffn_kernel, final=final),
        out_shape=jax.ShapeDtypeStruct((m, d), F32),
        grid=(m // tm, dff // tf),
        in_specs=[pl.BlockSpec((tm, d), lambda i, j: (i, 0)),
                  pl.BlockSpec((1, d), lambda i, j: (0, 0)),
                  pl.BlockSpec((d, tf), lambda i, j: (0, j)),
                  pl.BlockSpec((d, tf), lambda i, j: (0, j)),
                  pl.BlockSpec((tf, d), lambda i, j: (j, 0)),
                  pl.BlockSpec((1, d), lambda i, j: (0, 0))],
        out_specs=pl.BlockSpec((tm, d), lambda i, j: (i, 0)),
        scratch_shapes=[pltpu.VMEM((tm, d), BF16), pltpu.VMEM((tm, d), F32)],
        compiler_params=_cparams(2),
        name="ffn",
    )(x2, g, wg, wu, wd, gf)


def _tiles(seq, m, dff):
    pick = lambda n, cands: next(c for c in cands if n % c == 0)
    tk = pick(seq, (512, 256))
    return dict(
        proj_tm=pick(m, (1024, 512, 256, 128)), proj_tn=1024,
        prep_ts=VT_TILE,
        diff_tq=128, mla_tq=pick(seq, (256, 128)), nsa_tq=128, dsa_tq=256, tk=tk,
        row_tm=pick(m, (512, 256, 128)),
        ffn_tf=pick(dff, (1408, 704, 256, 128)),
    )


def kernel(x, norm1_g, w_in, diff_lq1, diff_lk1, diff_lq2, diff_lk2, diff_subln_g, nsa_pe_k, nsa_w1_k, nsa_w2_k, nsa_pe_v, nsa_w1_v, nsa_w2_v, mla_q_norm_g, mla_w_uq, mla_kv_norm_g, mla_w_ukv, idx_k_norm_g, w_branch, w_out, norm2_g, w_gate_up, w_down, final_norm_g):
    b, seq, d = x.shape
    depth = w_in.shape[0]
    m = b * seq
    dff = w_down.shape[1]
    t = _tiles(seq, m, dff)
    tk = t["tk"]
    assert seq % SEL_LEN == 0 and seq >= WIN + t["nsa_tq"] and seq // SEL_LEN <= LANES
    assert seq % t["dsa_tq"] == 0 and tk % VT_TILE == 0 and tk >= min(IDX_TOPK, seq // 4)

    col_idx, gate_off, d_in = _in_proj_columns()
    assert w_in.shape[2] == d_in
    tab = jnp.concatenate([_rope_table(seq, rot, per) for rot, per in ROPE_KINDS], axis=1)

    ng = seq // CMP_STRIDE
    ns = seq // SEL_LEN
    c_start = np.arange(ng)[:, None] * CMP_STRIDE
    s_start = np.arange(LANES)[None, :] * SEL_LEN
    ov = ((c_start < s_start + SEL_LEN) & (c_start + CMP_LEN - 1 >= s_start)
          & (np.arange(LANES)[None, :] < ns))
    ov = jnp.asarray(ov, BF16)
    emat = np.arange(LANES)[:, None] == (np.arange(seq)[None, :] // SEL_LEN)
    emat = jnp.asarray(emat.reshape(LANES, seq // tk, tk).transpose(1, 0, 2), BF16)

    qd = MLA_NOPE + MLA_ROPE
    uq_idx = np.concatenate([np.concatenate([np.arange(h * qd, h * qd + MLA_NOPE) for h in range(HEADS)]),
                             np.concatenate([np.arange(h * qd + MLA_NOPE, (h + 1) * qd) for h in range(HEADS)])])
    kvd = MLA_NOPE + HEAD_W
    ukv_idx = np.concatenate([np.concatenate([np.arange(h * kvd, h * kvd + MLA_NOPE) for h in range(HEADS)]),
                              np.concatenate([np.arange(h * kvd + MLA_NOPE, (h + 1) * kvd) for h in range(HEADS)])])

    x2 = x.reshape(m, d)
    half_w1 = CMP_STRIDE * NSA_DK
    for l in range(depth):
        lam_init = 0.8 - 0.6 * math.exp(-0.3 * l)
        w_mix = _take_cols(w_in[l], col_idx).astype(BF16)
        w_gate = w_in[l][:, gate_off:].astype(BF16)
        z = _norm_matmul(x2, norm1_g[l][None], w_mix, out_dtype=F32, sigmoid=False,
                         tm=t["proj_tm"], tn=t["proj_tn"], name="in_proj")
        gates = _norm_matmul(x2, norm1_g[l][None], w_gate, out_dtype=BF16, sigmoid=True,
                             tm=t["proj_tm"], tn=t["proj_tn"], name="gate_proj")

        gq = jnp.pad(mla_q_norm_g[l], (0, 512 - MLA_Q_LORA))[None]
        gkv = mla_kv_norm_g[l][None]
        gik = jnp.concatenate([idx_k_norm_g[l], idx_k_norm_g[l]])[None]
        p2, t3, vt4, kc_tok, vc_tok, small, smallt = _prep(z, tab, gq, gkv, gik,
                                                           batch=b, seq=seq, ts=t["prep_ts"])
        p3 = p2.reshape(b, seq, P_WIDTH)
        small3 = small.reshape(b, seq, LANES)

        lv = jnp.stack([diff_lq1[l], diff_lk1[l], diff_lq2[l], diff_lk2[l]])
        o_a = _diff_attn(p3, lv, diff_subln_g[l][None], lam_init=lam_init, tq=t["diff_tq"], tk=tk)

        w1k, w1v = nsa_w1_k[l].astype(BF16), nsa_w1_v[l].astype(BF16)
        w1k_cat = jnp.concatenate([w1k[:half_w1], w1k[half_w1:]], axis=1)
        w1v_cat = jnp.concatenate([w1v[:half_w1], w1v[half_w1:]], axis=1)
        pek = jnp.broadcast_to(nsa_pe_k[l].reshape(1, -1), (8, CMP_LEN * NSA_DK)).astype(BF16)
        pev = jnp.broadcast_to(nsa_pe_v[l].reshape(1, -1), (8, CMP_LEN * NSA_DK)).astype(BF16)
        kc, vc = _nsa_compress(kc_tok.reshape(b, ng, half_w1), vc_tok.reshape(b, ng, half_w1),
                               w1k_cat, w1v_cat, pek, pev, w1k, w1v,
                               nsa_w2_k[l].astype(BF16), nsa_w2_v[l].astype(BF16))
        o_b = _nsa(p3, kc, vc, small3, ov, emat, tq=t["nsa_tq"], tk=tk)

        wq = jnp.pad(jnp.take(mla_w_uq[l], jnp.asarray(uq_idx, I32), axis=1),
                     ((0, 512 - MLA_Q_LORA), (0, 0))).astype(BF16)
        wkv = jnp.take(mla_w_ukv[l], jnp.asarray(ukv_idx, I32), axis=1).astype(BF16)
        q_c, kv_c = _mla_up(p2, tab, wq, wkv, seq=seq, ts=t["prep_ts"])
        o_c = _mla_attn(q_c.reshape(b, seq, -1), kv_c.reshape(b, seq, -1), p3, tq=t["mla_tq"], tk=tk)

        o_d = _dsa(p3, t3, vt4, smallt, tq=t["dsa_tq"], tk=tk, top=min(IDX_TOPK, seq // 4))

        x2 = _merge(x2, o_a.reshape(m, -1), o_b.reshape(m, -1), o_c.reshape(m, -1), o_d.reshape(m, -1),
                    gates, w_branch[l].astype(BF16), w_out[l].astype(BF16), tm=t["row_tm"])
        wgu = w_gate_up[l].astype(BF16)
        x2 = _ffn(x2, norm2_g[l][None], wgu[:, :dff], wgu[:, dff:], w_down[l].astype(BF16),
                  final_norm_g[None], final=(l == depth - 1), tm=t["row_tm"], tf=t["ffn_tf"])
    return x2.reshape(b, seq, d)
```

```python
import functools
import math

import numpy as np
import jax
import jax.numpy as jnp
from jax import lax
from jax.experimental import pallas as pl
from jax.experimental.pallas import tpu as pltpu

F32 = jnp.float32
BF16 = jnp.bfloat16
I32 = jnp.int32
I16 = jnp.int16

LANES = 128
SUBLANES = 8
PACKED_ROWS = 16
HALF_MIN = -32768
VMEM_LIMIT = 56 * 1024 * 1024

ROPE_THETA = 500000.0
NEG = -1e30
LOG2E = math.log2(math.e)
FORCE_SCORE = 1e9
PAD_SCORE = -3e38
EPS = 1e-6
INT_MIN = -2147483648

HEADS = 4
HEAD_W = 128
BR_WIDTH = HEADS * HEAD_W
DA_DIM = 64
NSA_DK = 128
CMP_LEN = 32
CMP_STRIDE = 16
SEL_LEN = 64
SEL_N = 16
WIN = 512
MLA_Q_LORA = 384
MLA_KV_LORA = 256
MLA_NOPE = 128
MLA_ROPE = 64
DSA_DIM = 128
IDX_HEADS = 8
IDX_DIM = 64
IDX_TOPK = 256

Z_AQ, Z_AK, Z_AV, Z_BQ, Z_DQ, Z_DK, Z_DV, Z_IQ = (i * 512 for i in range(8))
Z_CQ = 4096
Z_CKV = 4608
Z_KC, Z_KS, Z_KW, Z_VC, Z_VS, Z_VW, Z_KR, Z_IK, Z_SMALL = (4864 + i * 128 for i in range(9))
Z_WIDTH = 6144
P_AQ, P_AK, P_AV, P_BQ, P_DK, P_CQ = (i * 512 for i in range(6))
P_CKV = 3072
P_KS, P_KW, P_VS, P_VW, P_KR, P_IK = (3328 + i * 128 for i in range(6))
P_WIDTH = 4096
T_DQ, T_IQ = 0, 512
T_ROWS = 1024
VT_TILE = 256
SMALL_G = 0
SMALL_IW = 12

ROPE_KINDS = ((16, 64), (32, 128), (64, 64))
TAB_W = 3 * LANES


def _cparams(n_axes):
    return pltpu.CompilerParams(dimension_semantics=("arbitrary",) * n_axes,
                                vmem_limit_bytes=VMEM_LIMIT)


def _dot(a, b):
    return jnp.dot(a, b, preferred_element_type=F32)


def _dot_nt(a, b):
    return lax.dot_general(a, b, (((1,), (1,)), ((), ())), preferred_element_type=F32)


def _in_proj_columns():
    names = (("a_q", 512), ("a_k", 512), ("a_v", 512), ("b_q", 512),
             ("b_kc", 128), ("b_vc", 128), ("b_ks", 128), ("b_vs", 128),
             ("b_kw", 128), ("b_vw", 128), ("b_g", 12),
             ("c_q", 384), ("c_kv", 256), ("c_kr", 64),
             ("d_q", 512), ("d_k", 512), ("d_v", 512),
             ("d_iq", 512), ("d_ik", 64), ("d_iw", 8), ("gate", 4096))
    off, o = {}, 0
    for nm, n in names:
        off[nm] = (o, n)
        o += n
    idx = np.full((Z_WIDTH,), -1, np.int64)

    def put(dst, nm):
        s, n = off[nm]
        idx[dst:dst + n] = np.arange(s, s + n)

    put(Z_AQ, "a_q"); put(Z_AK, "a_k"); put(Z_AV, "a_v"); put(Z_BQ, "b_q")
    put(Z_DQ, "d_q"); put(Z_DK, "d_k"); put(Z_DV, "d_v"); put(Z_IQ, "d_iq")
    put(Z_CQ, "c_q"); put(Z_CKV, "c_kv")
    put(Z_KC, "b_kc"); put(Z_KS, "b_ks"); put(Z_KW, "b_kw")
    put(Z_VC, "b_vc"); put(Z_VS, "b_vs"); put(Z_VW, "b_vw")
    put(Z_KR, "c_kr"); put(Z_KR + 64, "c_kr")
    put(Z_IK, "d_ik"); put(Z_IK + 64, "d_ik")
    put(Z_SMALL + SMALL_G, "b_g"); put(Z_SMALL + SMALL_IW, "d_iw")
    return idx, off["gate"][0], o


def _take_cols(w, idx):
    safe = np.where(idx < 0, 0, idx)
    out = jnp.take(w, jnp.asarray(safe, I32), axis=1)
    return jnp.where(jnp.asarray(idx >= 0)[None, :], out, 0.0)


def _rope_table(seq, rot, period):
    half = rot // 2
    inv = jnp.power(jnp.float32(ROPE_THETA), -jnp.arange(0, rot, 2, dtype=F32) / rot)
    ang = jnp.arange(seq, dtype=F32)[:, None] * inv[None, :]
    cos, sin = jnp.cos(ang), jnp.sin(ang)
    lane = np.arange(LANES) % period
    in1 = lane < half
    in2 = (lane >= half) & (lane < 2 * half)
    fidx = np.where(in1, lane, np.where(in2, lane - half, 0))
    cosl, sinl = cos[:, fidx], sin[:, fidx]
    c = jnp.where(jnp.asarray(in1 | in2)[None], cosl, 1.0)
    s1 = jnp.where(jnp.asarray(in1)[None], -sinl, 0.0)
    s2 = jnp.where(jnp.asarray(in2)[None], sinl, 0.0)
    return jnp.concatenate([c, s1, s2], axis=1)


def _rope128(x, tab, half):
    return (x * tab[:, 0:LANES]
            + pltpu.roll(x, LANES - half, 1) * tab[:, LANES:2 * LANES]
            + pltpu.roll(x, half, 1) * tab[:, 2 * LANES:3 * LANES])


def _norm_matmul_kernel(x_ref, g_ref, w_ref, o_ref, h_scr, *, sigmoid):
    @pl.when(pl.program_id(1) == 0)
    def _():
        x = x_ref[...]
        ms = jnp.mean(x * x, axis=-1, keepdims=True)
        h_scr[...] = (x * lax.rsqrt(ms + EPS) * g_ref[...]).astype(BF16)

    z = _dot(h_scr[...], w_ref[...])
    if sigmoid:
        z = jax.nn.sigmoid(z)
    o_ref[...] = z.astype(o_ref.dtype)


def _norm_matmul(x2, g, w, *, out_dtype, sigmoid, tm, tn, name):
    m, d = x2.shape
    n = w.shape[1]
    return pl.pallas_call(
        functools.partial(_norm_matmul_kernel, sigmoid=sigmoid),
        out_shape=jax.ShapeDtypeStruct((m, n), out_dtype),
        grid=(m // tm, n // tn),
        in_specs=[pl.BlockSpec((tm, d), lambda i, j: (i, 0)),
                  pl.BlockSpec((1, d), lambda i, j: (0, 0)),
                  pl.BlockSpec((d, tn), lambda i, j: (0, j))],
        out_specs=pl.BlockSpec((tm, tn), lambda i, j: (i, j)),
        scratch_shapes=[pltpu.VMEM((tm, d), BF16)],
        compiler_params=_cparams(2),
        name=name,
    )(x2, g, w)


PROJ_TILE = 512


def _proj_prep_kernel(x_ref, g_ref, w_ref, tab_ref, gq_ref, gkv_ref, gik_ref,
                      p_ref, t_ref, vt_ref, kc_ref, vc_ref, small_ref, smallt_ref):
    x = x_ref[...]
    ms = jnp.mean(x * x, axis=-1, keepdims=True)
    h = (x * lax.rsqrt(ms + EPS) * g_ref[...]).astype(BF16)
    z_tiles = {}

    def z_cols(off, width):
        t = off // PROJ_TILE
        assert (off + width - 1) // PROJ_TILE == t
        if t not in z_tiles:
            z_tiles[t] = _dot(h, w_ref[:, t * PROJ_TILE:(t + 1) * PROJ_TILE])
        lo = off - t * PROJ_TILE
        return z_tiles[t][:, lo:lo + width]

    def zc(off, c=0):
        return z_cols(off + c * LANES, LANES)

    def tab(kind):
        return tab_ref[:, kind * TAB_W:(kind + 1) * TAB_W]

    def put(off, c, v):
        p_ref[:, off + c * LANES:off + (c + 1) * LANES] = v.astype(BF16)

    def rope(off, c, kind):
        return _rope128(zc(off, c), tab(kind), ROPE_KINDS[kind][0] // 2)

    for zoff, poff, kind in ((Z_AQ, P_AQ, 0), (Z_AK, P_AK, 0), (Z_BQ, P_BQ, 1), (Z_DK, P_DK, 1)):
        for c in range(4):
            put(poff, c, rope(zoff, c, kind))
    for c in range(4):
        put(P_AV, c, zc(Z_AV, c))
    put(P_VS, 0, zc(Z_VS)); put(P_VW, 0, zc(Z_VW))
    put(P_KS, 0, rope(Z_KS, 0, 1)); put(P_KW, 0, rope(Z_KW, 0, 1))
    put(P_KR, 0, rope(Z_KR, 0, 2))
    kc_ref[...] = rope(Z_KC, 0, 1).astype(BF16)
    vc_ref[...] = zc(Z_VC).astype(BF16)

    for zoff, toff, kind in ((Z_DQ, T_DQ, 1), (Z_IQ, T_IQ, 0)):
        for c in range(4):
            t_ref[0, toff + c * LANES:toff + (c + 1) * LANES, :] = rope(zoff, c, kind).T.astype(BF16)
    for c in range(4):
        vt_ref[0, 0, c * LANES:(c + 1) * LANES, :] = zc(Z_DV, c).T.astype(BF16)

    cq = z_cols(Z_CQ, 512)
    ms = jnp.sum(cq * cq, axis=-1, keepdims=True) * (1.0 / MLA_Q_LORA)
    p_ref[:, P_CQ:P_CQ + 512] = (cq * lax.rsqrt(ms + EPS) * gq_ref[...]).astype(BF16)
    ckv = z_cols(Z_CKV, MLA_KV_LORA)
    ms = jnp.mean(ckv * ckv, axis=-1, keepdims=True)
    p_ref[:, P_CKV:P_CKV + MLA_KV_LORA] = (ckv * lax.rsqrt(ms + EPS) * gkv_ref[...]).astype(BF16)

    ik = zc(Z_IK)
    ms = jnp.mean(ik * ik, axis=-1, keepdims=True)
    ikn = ik * lax.rsqrt(ms + EPS) * gik_ref[...]
    put(P_IK, 0, _rope128(ikn, tab(0), ROPE_KINDS[0][0] // 2))

    sm = zc(Z_SMALL)
    lane = lax.broadcasted_iota(I32, sm.shape, 1)
    iw_scale = IDX_HEADS ** -0.5 * IDX_DIM ** -0.5
    small = jnp.where(lane < SMALL_IW, jax.nn.sigmoid(sm), sm * iw_scale)
    small_ref[...] = small
    smallt_ref[0] = small.T


def _proj_prep(x2, g, w, tab, gq, gkv, gik, *, batch, seq, ts):
    m, d = x2.shape
    spb = seq // ts
    assert ts == VT_TILE and w.shape == (d, Z_WIDTH)
    row = lambda w: pl.BlockSpec((ts, w), lambda i: (i, 0))
    return pl.pallas_call(
        _proj_prep_kernel,
        out_shape=(jax.ShapeDtypeStruct((m, P_WIDTH), BF16),
                   jax.ShapeDtypeStruct((batch, T_ROWS, seq), BF16),
                   jax.ShapeDtypeStruct((batch, spb, BR_WIDTH, VT_TILE), BF16),
                   jax.ShapeDtypeStruct((m, LANES), BF16),
                   jax.ShapeDtypeStruct((m, LANES), BF16),
                   jax.ShapeDtypeStruct((m, LANES), F32),
                   jax.ShapeDtypeStruct((batch, LANES, seq), F32)),
        grid=(m // ts,),
        in_specs=[row(d),
                  pl.BlockSpec((1, d), lambda i: (0, 0)),
                  pl.BlockSpec((d, Z_WIDTH), lambda i: (0, 0)),
                  pl.BlockSpec((ts, 3 * TAB_W), lambda i: (i % spb, 0)),
                  pl.BlockSpec((1, 512), lambda i: (0, 0)),
                  pl.BlockSpec((1, MLA_KV_LORA), lambda i: (0, 0)),
                  pl.BlockSpec((1, LANES), lambda i: (0, 0))],
        out_specs=(row(P_WIDTH),
                   pl.BlockSpec((1, T_ROWS, ts), lambda i: (i // spb, 0, i % spb)),
                   pl.BlockSpec((1, 1, BR_WIDTH, VT_TILE), lambda i: (i // spb, i % spb, 0, 0)),
                   row(LANES), row(LANES), row(LANES),
                   pl.BlockSpec((1, LANES, ts), lambda i: (i // spb, 0, i % spb))),
        compiler_params=_cparams(1),
        name="proj_prep",
    )(x2, g, w, tab, gq, gkv, gik)


def _softmax_init(mx_scr, l_scr, acc_scr):
    mx_scr[...] = jnp.full(mx_scr.shape, NEG, F32)
    l_scr[...] = jnp.zeros(l_scr.shape, F32)
    acc_scr[...] = jnp.zeros(acc_scr.shape, F32)


def _score_store(g, j, s, s_scr, mx_scr):
    s_scr[g, j] = s
    m = s[:, 0:LANES]
    for c in range(1, s.shape[1] // LANES):
        m = jnp.maximum(m, s[:, c * LANES:(c + 1) * LANES])
    mx_scr[g] = jnp.maximum(mx_scr[g], m)


def _row_max_finish(mx_scr):
    for g in range(mx_scr.shape[0]):
        m = jnp.max(mx_scr[g], axis=-1, keepdims=True)
        mx_scr[g] = jnp.broadcast_to(m, mx_scr.shape[1:])


def _prob_accumulate(g, j, v_tile, s_scr, mx_scr, l_scr, acc_scr):
    mb = mx_scr[g]
    s = s_scr[g, j]
    ps = [jnp.exp2(s[:, c * LANES:(c + 1) * LANES] - mb) for c in range(s.shape[1] // LANES)]
    tot = ps[0]
    for p in ps[1:]:
        tot = tot + p
    l_scr[g] += tot
    acc_scr[g] += _dot(jnp.concatenate(ps, axis=1).astype(BF16), v_tile)


def _softmax_out(g, l_scr, acc_scr):
    return acc_scr[g] / jnp.sum(l_scr[g], axis=-1, keepdims=True)


def _causal_tiles(step, n_full):
    def body(j, carry):
        step(j, False)
        return carry
    lax.fori_loop(0, n_full, body, 0)
    step(n_full, True)


def _softmax_scratch(groups, n_tiles, rows, tk):
    return [pltpu.VMEM((groups, n_tiles, rows, tk), F32),
            pltpu.VMEM((groups, rows, LANES), F32),
            pltpu.VMEM((groups, rows, LANES), F32),
            pltpu.VMEM((groups, rows, HEAD_W), F32)]


def _diff_attn_kernel(q_ref, k_ref, v_ref, lv_ref, g_ref, o_ref, s_scr, mx_scr, l_scr, acc_scr,
                      *, tq, tk, lam_init):
    qs = pl.program_id(1) * tq
    n_full = qs // tk
    scale = DA_DIM ** -0.5 * LOG2E
    lv = lv_ref[...]
    lam = (jnp.exp(jnp.sum(lv[0:1] * lv[1:2], axis=-1, keepdims=True))
           - jnp.exp(jnp.sum(lv[2:3] * lv[3:4], axis=-1, keepdims=True)) + lam_init)
    lane = lax.broadcasted_iota(I32, (tq, HEAD_W), 1)
    row_t = qs + lax.broadcasted_iota(I32, (2 * tq, 1), 0) % tq
    col0 = lax.broadcasted_iota(I32, (2 * tq, tk), 1)
    _softmax_init(mx_scr, l_scr, acc_scr)

    def scores(j, masked):
        ks = pl.multiple_of(j * tk, tk)
        for h in range(HEADS):
            hs = slice(h * HEAD_W, (h + 1) * HEAD_W)
            qh = q_ref[0, :, hs]
            zero = jnp.zeros_like(qh)
            q2 = jnp.concatenate([jnp.where(lane < DA_DIM, qh, zero),
                                  jnp.where(lane >= DA_DIM, qh, zero)], axis=0)
            s = _dot_nt(q2, k_ref[0, pl.ds(ks, tk), hs]) * scale
            if masked:
                s = jnp.where(col0 + ks <= row_t, s, NEG)
            _score_store(h, j, s, s_scr, mx_scr)

    _causal_tiles(scores, n_full)
    _row_max_finish(mx_scr)

    def probs(j, carry):
        ks = pl.multiple_of(j * tk, tk)
        for h in range(HEADS):
            v_tile = v_ref[0, pl.ds(ks, tk), h * HEAD_W:(h + 1) * HEAD_W]
            _prob_accumulate(h, j, v_tile, s_scr, mx_scr, l_scr, acc_scr)
        return carry

    lax.fori_loop(0, n_full + 1, probs, 0)
    for h in range(HEADS):
        o2 = _softmax_out(h, l_scr, acc_scr)
        o = o2[:tq] - lam * o2[tq:]
        ms = jnp.mean(o * o, axis=-1, keepdims=True)
        o = o * lax.rsqrt(ms + EPS) * g_ref[...]
        o_ref[0, :, h * HEAD_W:(h + 1) * HEAD_W] = (o * (1.0 - lam_init)).astype(BF16)


def _diff_attn(p3, lv, g, *, lam_init, tq, tk):
    b, s, _ = p3.shape
    return pl.pallas_call(
        functools.partial(_diff_attn_kernel, tq=tq, tk=tk, lam_init=lam_init),
        out_shape=jax.ShapeDtypeStruct((b, s, BR_WIDTH), BF16),
        grid=(b, s // tq),
        in_specs=[pl.BlockSpec((1, tq, 512), lambda bi, i: (bi, i, P_AQ // 512)),
                  pl.BlockSpec((1, s, 512), lambda bi, i: (bi, 0, P_AK // 512)),
                  pl.BlockSpec((1, s, 512), lambda bi, i: (bi, 0, P_AV // 512)),
                  pl.BlockSpec((4, DA_DIM), lambda bi, i: (0, 0)),
                  pl.BlockSpec((1, HEAD_W), lambda bi, i: (0, 0))],
        out_specs=pl.BlockSpec((1, tq, BR_WIDTH), lambda bi, i: (bi, i, 0)),
        scratch_shapes=_softmax_scratch(HEADS, s // tk, 2 * tq, tk),
        compiler_params=_cparams(2),
        name="diff_attn",
    )(p3, p3, p3, lv, g)


def _nsa_compress_kernel(gk_ref, gv_ref, w1k_ref, w1v_ref, pek_ref, pev_ref,
                         w1kf_ref, w1vf_ref, w2k_ref, w2v_ref, kc_ref, vc_ref):
    def one(g_ref, w1cat_ref, pe_ref, w1f_ref, w2_ref, o_ref):
        y = _dot(g_ref[0], w1cat_ref[...])
        n = y.shape[0]
        nxt = pltpu.roll(y[:, HEAD_W:], n - 1, 0)
        c = _dot(pe_ref[...], w1f_ref[...])[0:1]
        hid = jax.nn.gelu(y[:, :HEAD_W] + nxt + c)
        o_ref[0] = _dot(hid.astype(BF16), w2_ref[...]).astype(BF16)

    one(gk_ref, w1k_ref, pek_ref, w1kf_ref, w2k_ref, kc_ref)
    one(gv_ref, w1v_ref, pev_ref, w1vf_ref, w2v_ref, vc_ref)


def _nsa_compress(gk, gv, w1k_cat, w1v_cat, pek, pev, w1k, w1v, w2k, w2v):
    b, ng, gw = gk.shape
    full = lambda shape: pl.BlockSpec(shape, lambda bi: (0,) * len(shape))
    return pl.pallas_call(
        _nsa_compress_kernel,
        out_shape=(jax.ShapeDtypeStruct((b, ng, HEAD_W), BF16),
                   jax.ShapeDtypeStruct((b, ng, HEAD_W), BF16)),
        grid=(b,),
        in_specs=[pl.BlockSpec((1, ng, gw), lambda bi: (bi, 0, 0)),
                  pl.BlockSpec((1, ng, gw), lambda bi: (bi, 0, 0)),
                  full(w1k_cat.shape), full(w1v_cat.shape), full(pek.shape), full(pev.shape),
                  full(w1k.shape), full(w1v.shape), full(w2k.shape), full(w2v.shape)],
        out_specs=(pl.BlockSpec((1, ng, HEAD_W), lambda bi: (bi, 0, 0)),
                   pl.BlockSpec((1, ng, HEAD_W), lambda bi: (bi, 0, 0))),
        compiler_params=_cparams(1),
        name="nsa_compress",
    )(gk, gv, w1k_cat, w1v_cat, pek, pev, w1k, w1v, w2k, w2v)


NSA_GROUPS = 2


def _nsa_kernel(q_ref, kc_ref, vc_ref, ks_ref, vs_ref, kw_ref, vw_ref, small_ref, ov_ref, e_ref,
                o_ref, s_scr, mx_scr, l_scr, acc_scr, cmp_scr, win_scr, *, tq, tk, seq):
    qs = pl.program_id(1) * tq
    scale = NSA_DK ** -0.5
    ns = seq // SEL_LEN
    n_sel = min(SEL_N, ns)
    r = HEADS * tq
    rg = r // NSA_GROUPS
    q4 = jnp.concatenate([q_ref[0, :, h * HEAD_W:(h + 1) * HEAD_W] for h in range(HEADS)], axis=0)
    t1 = qs + lax.broadcasted_iota(I32, (tq, 1), 0)
    t4 = qs + lax.broadcasted_iota(I32, (r, 1), 0) % tq

    wspan = WIN + tq
    start = pl.multiple_of(jnp.maximum(qs - WIN, 0), tq)
    sw = _dot_nt(q4, kw_ref[0, pl.ds(start, wspan), :]) * scale
    dist = t4 - (start + lax.broadcasted_iota(I32, (r, wspan), 1))
    sw = jnp.where(pltpu.bitcast(dist, jnp.uint32) < jnp.uint32(WIN), sw, NEG)
    e = jnp.exp(sw - jnp.max(sw, axis=-1, keepdims=True))
    pw = e / jnp.sum(e, axis=-1, keepdims=True)
    win_scr[...] = _dot(pw.astype(BF16), vw_ref[0, pl.ds(start, wspan), :])

    kc = kc_ref[0]
    nc_pad = kc.shape[0]
    sc = _dot_nt(q4, kc) * scale
    c_end = lax.broadcasted_iota(I32, (r, nc_pad), 1) * CMP_STRIDE + (CMP_LEN - 1)
    cmask = c_end <= t4
    mx = jnp.max(jnp.where(cmask, sc, NEG), axis=-1, keepdims=True)
    e = jnp.where(cmask, jnp.exp(sc - mx), 0.0)
    den = jnp.sum(e, axis=-1, keepdims=True)
    pc = e / jnp.where(den > 0.0, den, 1.0)
    cmp_scr[...] = _dot(pc.astype(BF16), vc_ref[0])

    psum = pc[0:tq] + pc[tq:2 * tq] + pc[2 * tq:3 * tq] + pc[3 * tq:4 * tq]
    ov = ov_ref[...]
    hi = psum.astype(BF16)
    r1 = psum - hi.astype(F32)
    mid = r1.astype(BF16)
    lo = (r1 - mid.astype(F32)).astype(BF16)
    imp = _dot(hi, ov) + _dot(mid, ov) + _dot(lo, ov)

    blk = lax.broadcasted_iota(I32, (tq, LANES), 1)
    cur = t1 // SEL_LEN
    forced = (blk == 0) | (blk == cur) | (blk == cur - 1)
    visible = blk * SEL_LEN <= t1
    score = jnp.where(visible, jnp.where(forced, FORCE_SCORE, imp), NEG)
    score = jnp.where(blk < ns, score, PAD_SCORE)
    ns_pad = -(-ns // SUBLANES) * SUBLANES
    score_t = score.T[:ns_pad]
    blk_t = lax.broadcasted_iota(I32, (ns_pad, tq), 0)
    rank = jnp.zeros((ns_pad, tq), I32)
    for jp in range(ns):
        row = score_t[jp:jp + 1, :]
        later = (blk_t > jp).astype(I32)
        rank = rank + jnp.where(row > score_t, 1, jnp.where(row == score_t, later, 0))
    sel_t = jnp.where(rank < n_sel, 1.0, 0.0)
    if ns_pad < LANES:
        sel_t = jnp.concatenate([sel_t, jnp.zeros((LANES - ns_pad, tq), F32)], axis=0)
    selb = sel_t.T.astype(BF16)

    _softmax_init(mx_scr, l_scr, acc_scr)
    col0 = lax.broadcasted_iota(I32, (rg, tk), 1)
    tg = qs + lax.broadcasted_iota(I32, (rg, 1), 0) % tq
    n_tiles = qs // tk + 1

    def scores(j, masked):
        ks0 = pl.multiple_of(j * tk, tk)
        mt = _dot(selb, e_ref[j])
        mg = jnp.concatenate([mt] * (rg // tq), axis=0)
        k_tile = ks_ref[0, pl.ds(ks0, tk), :]
        for g in range(NSA_GROUPS):
            s = _dot_nt(q4[g * rg:(g + 1) * rg], k_tile) * (scale * LOG2E)
            s = jnp.where(mg > 0.5, s, NEG)
            if masked:
                s = jnp.where(col0 + ks0 <= tg, s, NEG)
            _score_store(g, j, s, s_scr, mx_scr)

    _causal_tiles(scores, n_tiles - 1)
    _row_max_finish(mx_scr)

    def probs(j, carry):
        ks0 = pl.multiple_of(j * tk, tk)
        v_tile = vs_ref[0, pl.ds(ks0, tk), :]
        for g in range(NSA_GROUPS):
            _prob_accumulate(g, j, v_tile, s_scr, mx_scr, l_scr, acc_scr)
        return carry

    lax.fori_loop(0, n_tiles, probs, 0)
    o_slc = jnp.concatenate([_softmax_out(g, l_scr, acc_scr) for g in range(NSA_GROUPS)], axis=0)

    gates = small_ref[0]
    for h in range(HEADS):
        rows = slice(h * tq, (h + 1) * tq)
        g0 = gates[:, SMALL_G + 3 * h:SMALL_G + 3 * h + 1]
        g1 = gates[:, SMALL_G + 3 * h + 1:SMALL_G + 3 * h + 2]
        g2 = gates[:, SMALL_G + 3 * h + 2:SMALL_G + 3 * h + 3]
        o = g0 * cmp_scr[rows, :] + g1 * o_slc[rows] + g2 * win_scr[rows, :]
        o_ref[0, :, h * HEAD_W:(h + 1) * HEAD_W] = o.astype(BF16)


def _nsa(p3, kc, vc, small3, ov, emat, *, tq, tk):
    b, s, _ = p3.shape
    ng = kc.shape[1]
    col = lambda off: (lambda bi, i: (bi, 0, off // LANES))
    return pl.pallas_call(
        functools.partial(_nsa_kernel, tq=tq, tk=tk, seq=s),
        out_shape=jax.ShapeDtypeStruct((b, s, BR_WIDTH), BF16),
        grid=(b, s // tq),
        in_specs=[pl.BlockSpec((1, tq, 512), lambda bi, i: (bi, i, P_BQ // 512)),
                  pl.BlockSpec((1, ng, HEAD_W), lambda bi, i: (bi, 0, 0)),
                  pl.BlockSpec((1, ng, HEAD_W), lambda bi, i: (bi, 0, 0)),
                  pl.BlockSpec((1, s, LANES), col(P_KS)),
                  pl.BlockSpec((1, s, LANES), col(P_VS)),
                  pl.BlockSpec((1, s, LANES), col(P_KW)),
                  pl.BlockSpec((1, s, LANES), col(P_VW)),
                  pl.BlockSpec((1, tq, LANES), lambda bi, i: (bi, i, 0)),
                  pl.BlockSpec(ov.shape, lambda bi, i: (0, 0)),
                  pl.BlockSpec(emat.shape, lambda bi, i: (0, 0, 0))],
        out_specs=pl.BlockSpec((1, tq, BR_WIDTH), lambda bi, i: (bi, i, 0)),
        scratch_shapes=(_softmax_scratch(NSA_GROUPS, s // tk, HEADS * tq // NSA_GROUPS, tk)
                        + [pltpu.VMEM((HEADS * tq, HEAD_W), F32), pltpu.VMEM((HEADS * tq, HEAD_W), F32)]),
        compiler_params=_cparams(2),
        name="nsa_attn",
    )(p3, kc, vc, p3, p3, p3, p3, small3, ov, emat)


def _mla_up_kernel(p_ref, ckv_ref, tab_ref, wq_ref, wkv_ref, q_ref, kv_ref):
    q = _dot(p_ref[...], wq_ref[...])
    nn = HEADS * MLA_NOPE
    q_ref[:, :nn] = q[:, :nn].astype(BF16)
    for c in range(nn // LANES, (nn + HEADS * MLA_ROPE) // LANES):
        tile = _rope128(q[:, c * LANES:(c + 1) * LANES], tab_ref[...], MLA_ROPE // 2)
        q_ref[:, c * LANES:(c + 1) * LANES] = tile.astype(BF16)
    kv_ref[...] = _dot(ckv_ref[...], wkv_ref[...]).astype(BF16)


def _mla_up(p2, tab, wq, wkv, *, seq, ts):
    m = p2.shape[0]
    spb = seq // ts
    nq = wq.shape[1]
    nkv = wkv.shape[1]
    return pl.pallas_call(
        _mla_up_kernel,
        out_shape=(jax.ShapeDtypeStruct((m, nq), BF16), jax.ShapeDtypeStruct((m, nkv), BF16)),
        grid=(m // ts,),
        in_specs=[pl.BlockSpec((ts, 512), lambda i: (i, P_CQ // 512)),
                  pl.BlockSpec((ts, MLA_KV_LORA), lambda i: (i, P_CKV // MLA_KV_LORA)),
                  pl.BlockSpec((ts, TAB_W), lambda i: (i % spb, 2)),
                  pl.BlockSpec(wq.shape, lambda i: (0, 0)),
                  pl.BlockSpec(wkv.shape, lambda i: (0, 0))],
        out_specs=(pl.BlockSpec((ts, nq), lambda i: (i, 0)),
                   pl.BlockSpec((ts, nkv), lambda i: (i, 0))),
        compiler_params=_cparams(1),
        name="mla_up",
    )(p2, p2, tab, wq, wkv)


def _mla_attn_kernel(qn_ref, qr_ref, kn_ref, kr_ref, v_ref, o_ref, s_scr, mx_scr, l_scr, acc_scr,
                     *, tq, tk):
    qs = pl.program_id(1) * tq
    n_full = qs // tk
    scale = (MLA_NOPE + MLA_ROPE) ** -0.5 * LOG2E
    lane = lax.broadcasted_iota(I32, (tq, LANES), 1)
    row_t = qs + lax.broadcasted_iota(I32, (tq, 1), 0)
    col0 = lax.broadcasted_iota(I32, (tq, tk), 1)
    _softmax_init(mx_scr, l_scr, acc_scr)

    def scores(j, masked):
        ks = pl.multiple_of(j * tk, tk)
        kr_tile = kr_ref[0, pl.ds(ks, tk), :]
        for h in range(HEADS):
            hs = slice(h * HEAD_W, (h + 1) * HEAD_W)
            pair = qr_ref[0, :, (h // 2) * LANES:(h // 2 + 1) * LANES]
            keep = (lane < MLA_ROPE) if h % 2 == 0 else (lane >= MLA_ROPE)
            qr = jnp.where(keep, pair, jnp.zeros_like(pair))
            s = _dot_nt(jnp.concatenate([qn_ref[0, :, hs], qr], axis=1),
                        jnp.concatenate([kn_ref[0, pl.ds(ks, tk), hs], kr_tile], axis=1)) * scale
            if masked:
                s = jnp.where(col0 + ks <= row_t, s, NEG)
            _score_store(h, j, s, s_scr, mx_scr)

    _causal_tiles(scores, n_full)
    _row_max_finish(mx_scr)

    def probs(j, carry):
        ks = pl.multiple_of(j * tk, tk)
        for h in range(HEADS):
            v_tile = v_ref[0, pl.ds(ks, tk), h * HEAD_W:(h + 1) * HEAD_W]
            _prob_accumulate(h, j, v_tile, s_scr, mx_scr, l_scr, acc_scr)
        return carry

    lax.fori_loop(0, n_full + 1, probs, 0)
    for h in range(HEADS):
        o_ref[0, :, h * HEAD_W:(h + 1) * HEAD_W] = _softmax_out(h, l_scr, acc_scr).astype(BF16)


def _mla_attn(q3, kv3, p3, *, tq, tk):
    b, s, _ = q3.shape
    return pl.pallas_call(
        functools.partial(_mla_attn_kernel, tq=tq, tk=tk),
        out_shape=jax.ShapeDtypeStruct((b, s, BR_WIDTH), BF16),
        grid=(b, s // tq),
        in_specs=[pl.BlockSpec((1, tq, 512), lambda bi, i: (bi, i, 0)),
                  pl.BlockSpec((1, tq, 256), lambda bi, i: (bi, i, 2)),
                  pl.BlockSpec((1, s, 512), lambda bi, i: (bi, 0, 0)),
                  pl.BlockSpec((1, s, LANES), lambda bi, i: (bi, 0, P_KR // LANES)),
                  pl.BlockSpec((1, s, 512), lambda bi, i: (bi, 0, 1))],
        out_specs=pl.BlockSpec((1, tq, BR_WIDTH), lambda bi, i: (bi, i, 0)),
        scratch_shapes=_softmax_scratch(HEADS, s // tk, tq, tk),
        compiler_params=_cparams(2),
        name="mla_attn",
    )(q3, q3, kv3, p3, kv3)


def _sortable_key(x):
    bits = pltpu.bitcast(x + 0.0, I32)
    return bits ^ (lax.shift_right_arithmetic(bits, 31) & 0x7FFFFFFF)


def _fold_rows(x, op):
    n = x.shape[0] // SUBLANES
    return op(x.reshape(n, SUBLANES, x.shape[1]), axis=0)


def _count16(half_scr, n_tiles, pred, tq):
    def count_tile(j, cnt):
        hit = pred(half_scr[j]).astype(I16)
        parts = [hit[r:r + PACKED_ROWS] for r in range(0, hit.shape[0], PACKED_ROWS)]
        while len(parts) > 1:
            parts = [a + b for a, b in zip(parts[0::2], parts[1::2])]
        return cnt + parts[0]

    cnt = lax.fori_loop(0, n_tiles, count_tile, jnp.zeros((PACKED_ROWS, tq), I16))
    return jnp.sum(cnt.astype(I32), axis=0, keepdims=True)


def _bisect16(half_scr, n_tiles, need, tq):
    def bit_body(i, th):
        cand = th + lax.shift_left(jnp.int32(1), 15 - i)
        c16 = cand.astype(I16)
        total = _count16(half_scr, n_tiles, lambda x: x >= c16, tq)
        return jnp.where(total >= need, cand, th)

    return lax.fori_loop(0, 16, bit_body, jnp.full((1, tq), HALF_MIN, I32))


def _dsa_kernel(qt_ref, iqt_ref, iwt_ref, k_ref, ik_ref, vt_ref, o_ref,
                key_scr, half_scr, s_scr, mx_scr, l_scr, acc_scr, *, tq, tk, top):
    qs = pl.program_id(1) * tq
    n_tiles = (qs + tq - 1) // tk + 1
    scale = DSA_DIM ** -0.5 * LOG2E
    t_lane = qs + lax.broadcasted_iota(I32, (tk, tq), 1)
    krow0 = lax.broadcasted_iota(I32, (tk, tq), 0)
    half_rows = lax.broadcasted_iota(I32, (LANES, tq), 0) < IDX_DIM
    vt_per_tile = tk // VT_TILE

    def score_tile(j, carry):
        ks = pl.multiple_of(j * tk, tk)
        ikt = ik_ref[0, pl.ds(ks, tk), :]
        acc = jnp.zeros((tk, tq), F32)
        for h in range(IDX_HEADS):
            pair = iqt_ref[0, (h // 2) * LANES:(h // 2 + 1) * LANES, :]
            keep = half_rows if h % 2 == 0 else jnp.logical_not(half_rows)
            iq_h = jnp.where(keep, pair, jnp.zeros_like(pair))
            w_h = iwt_ref[0, SMALL_IW + h:SMALL_IW + h + 1, :]
            acc = acc + w_h * jnp.maximum(_dot(ikt, iq_h), 0.0)
        key = jnp.where(krow0 + ks <= t_lane, _sortable_key(acc), INT_MIN)
        key_scr[j] = key
        half_scr[j] = lax.shift_right_arithmetic(key, 16).astype(I16)
        return carry

    lax.fori_loop(0, n_tiles, score_tile, 0)

    th_hi = _bisect16(half_scr, n_tiles, top, tq)
    hi16 = th_hi.astype(I16)
    need_lo = top - _count16(half_scr, n_tiles, lambda x: x > hi16, tq)

    def low_tile(j, carry):
        key = key_scr[j]
        lo = (key & 0xFFFF) + HALF_MIN
        same_hi = lax.shift_right_arithmetic(key, 16) == th_hi
        half_scr[j] = jnp.where(same_hi, lo, HALF_MIN).astype(I16)
        return carry

    lax.fori_loop(0, n_tiles, low_tile, 0)
    th_lo = _bisect16(half_scr, n_tiles, need_lo, tq)
    theta = lax.shift_left(th_hi, 16) + (th_lo - HALF_MIN)
    theta = jnp.maximum(theta, INT_MIN + 1)

    mx_scr[...] = jnp.full(mx_scr.shape, NEG, F32)
    l_scr[...] = jnp.zeros(l_scr.shape, F32)
    acc_scr[...] = jnp.zeros(acc_scr.shape, F32)

    def scores(j, carry):
        ks = pl.multiple_of(j * tk, tk)
        sel = key_scr[j] >= theta
        for h in range(HEADS):
            hs = slice(h * HEAD_W, (h + 1) * HEAD_W)
            s = _dot(k_ref[0, pl.ds(ks, tk), hs], qt_ref[0, hs, :]) * scale
            s = jnp.where(sel, s, NEG)
            s_scr[h, j] = s
            mx_scr[h] = jnp.maximum(mx_scr[h], _fold_rows(s, jnp.max))
        return carry

    lax.fori_loop(0, n_tiles, scores, 0)
    for h in range(HEADS):
        m = jnp.max(mx_scr[h], axis=0, keepdims=True)
        mx_scr[h] = jnp.broadcast_to(m, (SUBLANES, tq))

    def probs(j, carry):
        for h in range(HEADS):
            p = jnp.exp2(s_scr[h, j] - mx_scr[h][0:1])
            l_scr[h] += _fold_rows(p, jnp.sum)
            pb = p.astype(BF16)
            for c in range(vt_per_tile):
                vt = vt_ref[0, j * vt_per_tile + c, h * HEAD_W:(h + 1) * HEAD_W, :]
                acc_scr[h] += _dot(vt, pb[c * VT_TILE:(c + 1) * VT_TILE])
        return carry

    lax.fori_loop(0, n_tiles, probs, 0)
    for h in range(HEADS):
        ot = acc_scr[h] / jnp.sum(l_scr[h], axis=0, keepdims=True)
        o_ref[0, :, h * HEAD_W:(h + 1) * HEAD_W] = ot.T.astype(BF16)


def _dsa(p3, t3, vt4, smallt, *, tq, tk, top):
    b, s, _ = p3.shape
    n_vt = vt4.shape[1]
    return pl.pallas_call(
        functools.partial(_dsa_kernel, tq=tq, tk=tk, top=top),
        out_shape=jax.ShapeDtypeStruct((b, s, BR_WIDTH), BF16),
        grid=(b, s // tq),
        in_specs=[pl.BlockSpec((1, 512, tq), lambda bi, i: (bi, T_DQ // 512, i)),
                  pl.BlockSpec((1, 512, tq), lambda bi, i: (bi, T_IQ // 512, i)),
                  pl.BlockSpec((1, LANES, tq), lambda bi, i: (bi, 0, i)),
                  pl.BlockSpec((1, s, 512), lambda bi, i: (bi, 0, P_DK // 512)),
                  pl.BlockSpec((1, s, LANES), lambda bi, i: (bi, 0, P_IK // LANES)),
                  pl.BlockSpec((1, n_vt, BR_WIDTH, VT_TILE), lambda bi, i: (bi, 0, 0, 0))],
        out_specs=pl.BlockSpec((1, tq, BR_WIDTH), lambda bi, i: (bi, i, 0)),
        scratch_shapes=[pltpu.VMEM((s // tk, tk, tq), I32),
                        pltpu.VMEM((s // tk, tk, tq), I16),
                        pltpu.VMEM((HEADS, s // tk, tk, tq), F32),
                        pltpu.VMEM((HEADS, SUBLANES, tq), F32),
                        pltpu.VMEM((HEADS, SUBLANES, tq), F32),
                        pltpu.VMEM((HEADS, HEAD_W, tq), F32)],
        compiler_params=_cparams(2),
        name="dsa_attn",
    )(t3, t3, smallt, p3, p3, vt4)


def _merge_kernel(x_ref, oa_ref, ob_ref, oc_ref, od_ref, g_ref, wb_ref, wo_ref, o_ref):
    d = x_ref.shape[1]
    acc = jnp.zeros(x_ref.shape, F32)
    for n, br_ref in enumerate((oa_ref, ob_ref, oc_ref, od_ref)):
        br = _dot(br_ref[...], wb_ref[n])
        acc = acc + g_ref[:, n * d:(n + 1) * d].astype(F32) * br
    o_ref[...] = x_ref[...] + _dot(acc.astype(BF16), wo_ref[...])


def _merge(x2, oa, ob, oc, od, gates, wb, wo, *, tm):
    m, d = x2.shape
    row = lambda w: pl.BlockSpec((tm, w), lambda i: (i, 0))
    return pl.pallas_call(
        _merge_kernel,
        out_shape=jax.ShapeDtypeStruct((m, d), F32),
        grid=(m // tm,),
        in_specs=[row(d), row(BR_WIDTH), row(BR_WIDTH), row(BR_WIDTH), row(BR_WIDTH),
                  row(gates.shape[1]),
                  pl.BlockSpec(wb.shape, lambda i: (0, 0, 0)),
                  pl.BlockSpec(wo.shape, lambda i: (0, 0))],
        out_specs=row(d),
        compiler_params=_cparams(1),
        name="merge",
    )(x2, oa, ob, oc, od, gates, wb, wo)


def _ffn_kernel(x_ref, g_ref, wg_ref, wu_ref, wd_ref, gf_ref, o_ref, h_scr, acc_scr, *, final):
    j = pl.program_id(1)

    @pl.when(j == 0)
    def _():
        x = x_ref[...]
        ms = jnp.mean(x * x, axis=-1, keepdims=True)
        h_scr[...] = (x * lax.rsqrt(ms + EPS) * g_ref[...]).astype(BF16)
        acc_scr[...] = jnp.zeros(acc_scr.shape, F32)

    h = h_scr[...]
    a = jax.nn.silu(_dot(h, wg_ref[...])) * _dot(h, wu_ref[...])
    acc_scr[...] += _dot(a.astype(BF16), wd_ref[...])

    @pl.when(j == pl.num_programs(1) - 1)
    def _():
        y = x_ref[...] + acc_scr[...]
        if final:
            ms = jnp.mean(y * y, axis=-1, keepdims=True)
            y = y * lax.rsqrt(ms + EPS) * gf_ref[...]
        o_ref[...] = y


def _ffn(x2, g, wg, wu, wd, gf, *, final, tm, tf):
    m, d = x2.shape
    dff = wg.shape[1]
    return pl.pallas_call(
        functools.partial(_ffn_kernel, final=final),
        out_shape=jax.ShapeDtypeStruct((m, d), F32),
        grid=(m // tm, dff // tf),
        in_specs=[pl.BlockSpec((tm, d), lambda i, j: (i, 0)),
                  pl.BlockSpec((1, d), lambda i, j: (0, 0)),
                  pl.BlockSpec((d, tf), lambda i, j: (0, j)),
                  pl.BlockSpec((d, tf), lambda i, j: (0, j)),
                  pl.BlockSpec((tf, d), lambda i, j: (j, 0)),
                  pl.BlockSpec((1, d), lambda i, j: (0, 0))],
        out_specs=pl.BlockSpec((tm, d), lambda i, j: (i, 0)),
        scratch_shapes=[pltpu.VMEM((tm, d), BF16), pltpu.VMEM((tm, d), F32)],
        compiler_params=_cparams(2),
        name="ffn",
    )(x2, g, wg, wu, wd, gf)


def _tiles(seq, m, dff):
    pick = lambda n, cands: next(c for c in cands if n % c == 0)
    tk = pick(seq, (512, 256))
    return dict(
        proj_tm=pick(m, (1024, 512, 256, 128)), proj_tn=1024,
        prep_ts=VT_TILE,
        diff_tq=128, mla_tq=pick(seq, (256, 128)), nsa_tq=128, dsa_tq=256, tk=tk,
        row_tm=pick(m, (512, 256, 128)),
        ffn_tf=pick(dff, (1408, 704, 256, 128)),
    )


def kernel(x, norm1_g, w_in, diff_lq1, diff_lk1, diff_lq2, diff_lk2, diff_subln_g, nsa_pe_k, nsa_w1_k, nsa_w2_k, nsa_pe_v, nsa_w1_v, nsa_w2_v, mla_q_norm_g, mla_w_uq, mla_kv_norm_g, mla_w_ukv, idx_k_norm_g, w_branch, w_out, norm2_g, w_gate_up, w_down, final_norm_g):
    b, seq, d = x.shape
    depth = w_in.shape[0]
    m = b * seq
    dff = w_down.shape[1]
    t = _tiles(seq, m, dff)
    tk = t["tk"]
    assert seq % SEL_LEN == 0 and seq >= WIN + t["nsa_tq"] and seq // SEL_LEN <= LANES
    assert seq % t["dsa_tq"] == 0 and tk % VT_TILE == 0 and tk >= min(IDX_TOPK, seq // 4)

    col_idx, gate_off, d_in = _in_proj_columns()
    assert w_in.shape[2] == d_in
    tab = jnp.concatenate([_rope_table(seq, rot, per) for rot, per in ROPE_KINDS], axis=1)

    ng = seq // CMP_STRIDE
    ns = seq // SEL_LEN
    c_start = np.arange(ng)[:, None] * CMP_STRIDE
    s_start = np.arange(LANES)[None, :] * SEL_LEN
    ov = ((c_start < s_start + SEL_LEN) & (c_start + CMP_LEN - 1 >= s_start)
          & (np.arange(LANES)[None, :] < ns))
    ov = jnp.asarray(ov, BF16)
    emat = np.arange(LANES)[:, None] == (np.arange(seq)[None, :] // SEL_LEN)
    emat = jnp.asarray(emat.reshape(LANES, seq // tk, tk).transpose(1, 0, 2), BF16)

    qd = MLA_NOPE + MLA_ROPE
    uq_idx = np.concatenate([np.concatenate([np.arange(h * qd, h * qd + MLA_NOPE) for h in range(HEADS)]),
                             np.concatenate([np.arange(h * qd + MLA_NOPE, (h + 1) * qd) for h in range(HEADS)])])
    kvd = MLA_NOPE + HEAD_W
    ukv_idx = np.concatenate([np.concatenate([np.arange(h * kvd, h * kvd + MLA_NOPE) for h in range(HEADS)]),
                              np.concatenate([np.arange(h * kvd + MLA_NOPE, (h + 1) * kvd) for h in range(HEADS)])])

    x2 = x.reshape(m, d)
    half_w1 = CMP_STRIDE * NSA_DK
    for l in range(depth):
        lam_init = 0.8 - 0.6 * math.exp(-0.3 * l)
        w_mix = _take_cols(w_in[l], col_idx).astype(BF16)
        w_gate = w_in[l][:, gate_off:].astype(BF16)
        gates = _norm_matmul(x2, norm1_g[l][None], w_gate, out_dtype=BF16, sigmoid=True,
                             tm=t["proj_tm"], tn=t["proj_tn"], name="gate_proj")

        gq = jnp.pad(mla_q_norm_g[l], (0, 512 - MLA_Q_LORA))[None]
        gkv = mla_kv_norm_g[l][None]
        gik = jnp.concatenate([idx_k_norm_g[l], idx_k_norm_g[l]])[None]
        p2, t3, vt4, kc_tok, vc_tok, small, smallt = _proj_prep(
            x2, norm1_g[l][None], w_mix, tab, gq, gkv, gik, batch=b, seq=seq, ts=t["prep_ts"])
        p3 = p2.reshape(b, seq, P_WIDTH)
        small3 = small.reshape(b, seq, LANES)

        lv = jnp.stack([diff_lq1[l], diff_lk1[l], diff_lq2[l], diff_lk2[l]])
        o_a = _diff_attn(p3, lv, diff_subln_g[l][None], lam_init=lam_init, tq=t["diff_tq"], tk=tk)

        w1k, w1v = nsa_w1_k[l].astype(BF16), nsa_w1_v[l].astype(BF16)
        w1k_cat = jnp.concatenate([w1k[:half_w1], w1k[half_w1:]], axis=1)
        w1v_cat = jnp.concatenate([w1v[:half_w1], w1v[half_w1:]], axis=1)
        pek = jnp.broadcast_to(nsa_pe_k[l].reshape(1, -1), (8, CMP_LEN * NSA_DK)).astype(BF16)
        pev = jnp.broadcast_to(nsa_pe_v[l].reshape(1, -1), (8, CMP_LEN * NSA_DK)).astype(BF16)
        kc, vc = _nsa_compress(kc_tok.reshape(b, ng, half_w1), vc_tok.reshape(b, ng, half_w1),
                               w1k_cat, w1v_cat, pek, pev, w1k, w1v,
                               nsa_w2_k[l].astype(BF16), nsa_w2_v[l].astype(BF16))
        o_b = _nsa(p3, kc, vc, small3, ov, emat, tq=t["nsa_tq"], tk=tk)

        wq = jnp.pad(jnp.take(mla_w_uq[l], jnp.asarray(uq_idx, I32), axis=1),
                     ((0, 512 - MLA_Q_LORA), (0, 0))).astype(BF16)
        wkv = jnp.take(mla_w_ukv[l], jnp.asarray(ukv_idx, I32), axis=1).astype(BF16)
        q_c, kv_c = _mla_up(p2, tab, wq, wkv, seq=seq, ts=t["prep_ts"])
        o_c = _mla_attn(q_c.reshape(b, seq, -1), kv_c.reshape(b, seq, -1), p3, tq=t["mla_tq"], tk=tk)

        o_d = _dsa(p3, t3, vt4, smallt, tq=t["dsa_tq"], tk=tk, top=min(IDX_TOPK, seq // 4))

        x2 = _merge(x2, o_a.reshape(m, -1), o_b.reshape(m, -1), o_c.reshape(m, -1), o_d.reshape(m, -1),
                    gates, w_branch[l].astype(BF16), w_out[l].astype(BF16), tm=t["row_tm"])
        wgu = w_gate_up[l].astype(BF16)
        x2 = _ffn(x2, norm2_g[l][None], wgu[:, :dff], wgu[:, dff:], w_down[l].astype(BF16),
                  final_norm_g[None], final=(l == depth - 1), tm=t["row_tm"], tf=t["ffn_tf"])
    return x2.reshape(b, seq, d)
```

```python
import functools
import math

import numpy as np
import jax
import jax.numpy as jnp
from jax import lax
from jax.experimental import pallas as pl
from jax.experimental.pallas import tpu as pltpu

F32 = jnp.float32
BF16 = jnp.bfloat16
I32 = jnp.int32
I16 = jnp.int16

LANES = 128
SUBLANES = 8
PACKED_ROWS = 16
HALF_MIN = -32768
VMEM_LIMIT = 56 * 1024 * 1024

ROPE_THETA = 500000.0
NEG = -1e30
LOG2E = math.log2(math.e)
FORCE_SCORE = 1e9
PAD_SCORE = -3e38
EPS = 1e-6
INT_MIN = -2147483648

HEADS = 4
HEAD_W = 128
BR_WIDTH = HEADS * HEAD_W
DA_DIM = 64
NSA_DK = 128
CMP_LEN = 32
CMP_STRIDE = 16
SEL_LEN = 64
SEL_N = 16
WIN = 512
MLA_Q_LORA = 384
MLA_KV_LORA = 256
MLA_NOPE = 128
MLA_ROPE = 64
DSA_DIM = 128
IDX_HEADS = 8
IDX_DIM = 64
IDX_TOPK = 256

Z_AQ, Z_AK, Z_AV, Z_BQ, Z_DQ, Z_DK, Z_DV, Z_IQ = (i * 512 for i in range(8))
Z_CQ = 4096
Z_CKV = 4608
Z_KC, Z_KS, Z_KW, Z_VC, Z_VS, Z_VW, Z_KR, Z_IK, Z_SMALL = (4864 + i * 128 for i in range(9))
Z_WIDTH = 6144
P_AQ, P_AK, P_AV, P_BQ, P_DK, P_CQ = (i * 512 for i in range(6))
P_CKV = 3072
P_KS, P_KW, P_VS, P_VW, P_KR, P_IK = (3328 + i * 128 for i in range(6))
P_WIDTH = 4096
T_DQ, T_IQ = 0, 512
T_ROWS = 1024
VT_TILE = 256
SMALL_G = 0
SMALL_IW = 12

ROPE_KINDS = ((16, 64), (32, 128), (64, 64))
TAB_W = 3 * LANES


def _cparams(n_axes):
    return pltpu.CompilerParams(dimension_semantics=("arbitrary",) * n_axes,
                                vmem_limit_bytes=VMEM_LIMIT)


def _dot(a, b):
    return jnp.dot(a, b, preferred_element_type=F32)


def _dot_nt(a, b):
    return lax.dot_general(a, b, (((1,), (1,)), ((), ())), preferred_element_type=F32)


def _in_proj_columns():
    names = (("a_q", 512), ("a_k", 512), ("a_v", 512), ("b_q", 512),
             ("b_kc", 128), ("b_vc", 128), ("b_ks", 128), ("b_vs", 128),
             ("b_kw", 128), ("b_vw", 128), ("b_g", 12),
             ("c_q", 384), ("c_kv", 256), ("c_kr", 64),
             ("d_q", 512), ("d_k", 512), ("d_v", 512),
             ("d_iq", 512), ("d_ik", 64), ("d_iw", 8), ("gate", 4096))
    off, o = {}, 0
    for nm, n in names:
        off[nm] = (o, n)
        o += n
    idx = np.full((Z_WIDTH,), -1, np.int64)

    def put(dst, nm):
        s, n = off[nm]
        idx[dst:dst + n] = np.arange(s, s + n)

    put(Z_AQ, "a_q"); put(Z_AK, "a_k"); put(Z_AV, "a_v"); put(Z_BQ, "b_q")
    put(Z_DQ, "d_q"); put(Z_DK, "d_k"); put(Z_DV, "d_v"); put(Z_IQ, "d_iq")
    put(Z_CQ, "c_q"); put(Z_CKV, "c_kv")
    put(Z_KC, "b_kc"); put(Z_KS, "b_ks"); put(Z_KW, "b_kw")
    put(Z_VC, "b_vc"); put(Z_VS, "b_vs"); put(Z_VW, "b_vw")
    put(Z_KR, "c_kr"); put(Z_KR + 64, "c_kr")
    put(Z_IK, "d_ik"); put(Z_IK + 64, "d_ik")
    put(Z_SMALL + SMALL_G, "b_g"); put(Z_SMALL + SMALL_IW, "d_iw")
    return idx, off["gate"][0], o


def _take_cols(w, idx):
    runs, i, n = [], 0, len(idx)
    while i < n:
        j = i + 1
        if idx[i] < 0:
            while j < n and idx[j] < 0:
                j += 1
            runs.append(jnp.zeros((w.shape[0], j - i), w.dtype))
        else:
            while j < n and idx[j] == idx[j - 1] + 1:
                j += 1
            runs.append(w[:, int(idx[i]):int(idx[i]) + (j - i)])
        i = j
    return jnp.concatenate(runs, axis=1)


def _rope_table(seq, rot, period):
    half = rot // 2
    inv = jnp.power(jnp.float32(ROPE_THETA), -jnp.arange(0, rot, 2, dtype=F32) / rot)
    ang = jnp.arange(seq, dtype=F32)[:, None] * inv[None, :]
    cos, sin = jnp.cos(ang), jnp.sin(ang)
    lane = np.arange(LANES) % period
    in1 = lane < half
    in2 = (lane >= half) & (lane < 2 * half)
    fidx = np.where(in1, lane, np.where(in2, lane - half, 0))
    cosl, sinl = cos[:, fidx], sin[:, fidx]
    c = jnp.where(jnp.asarray(in1 | in2)[None], cosl, 1.0)
    s1 = jnp.where(jnp.asarray(in1)[None], -sinl, 0.0)
    s2 = jnp.where(jnp.asarray(in2)[None], sinl, 0.0)
    return jnp.concatenate([c, s1, s2], axis=1)


def _rope128(x, tab, half):
    return (x * tab[:, 0:LANES]
            + pltpu.roll(x, LANES - half, 1) * tab[:, LANES:2 * LANES]
            + pltpu.roll(x, half, 1) * tab[:, 2 * LANES:3 * LANES])


def _norm_matmul_kernel(x_ref, g_ref, w_ref, o_ref, h_scr, *, sigmoid):
    @pl.when(pl.program_id(1) == 0)
    def _():
        x = x_ref[...]
        ms = jnp.mean(x * x, axis=-1, keepdims=True)
        h_scr[...] = (x * lax.rsqrt(ms + EPS) * g_ref[...]).astype(BF16)

    z = _dot(h_scr[...], w_ref[...])
    if sigmoid:
        z = jax.nn.sigmoid(z)
    o_ref[...] = z.astype(o_ref.dtype)


def _norm_matmul(x2, g, w, *, out_dtype, sigmoid, tm, tn, name):
    m, d = x2.shape
    n = w.shape[1]
    return pl.pallas_call(
        functools.partial(_norm_matmul_kernel, sigmoid=sigmoid),
        out_shape=jax.ShapeDtypeStruct((m, n), out_dtype),
        grid=(m // tm, n // tn),
        in_specs=[pl.BlockSpec((tm, d), lambda i, j: (i, 0)),
                  pl.BlockSpec((1, d), lambda i, j: (0, 0)),
                  pl.BlockSpec((d, tn), lambda i, j: (0, j))],
        out_specs=pl.BlockSpec((tm, tn), lambda i, j: (i, j)),
        scratch_shapes=[pltpu.VMEM((tm, d), BF16)],
        compiler_params=_cparams(2),
        name=name,
    )(x2, g, w)


PROJ_TILE = 512


def _proj_prep_kernel(x_ref, g_ref, w_ref, tab_ref, gq_ref, gkv_ref, gik_ref,
                      p_ref, t_ref, vt_ref, kc_ref, vc_ref, small_ref, smallt_ref):
    x = x_ref[...]
    ms = jnp.mean(x * x, axis=-1, keepdims=True)
    h = (x * lax.rsqrt(ms + EPS) * g_ref[...]).astype(BF16)
    z_tiles = {}

    def z_cols(off, width):
        t = off // PROJ_TILE
        assert (off + width - 1) // PROJ_TILE == t
        if t not in z_tiles:
            z_tiles[t] = _dot(h, w_ref[:, t * PROJ_TILE:(t + 1) * PROJ_TILE])
        lo = off - t * PROJ_TILE
        return z_tiles[t][:, lo:lo + width]

    def zc(off, c=0):
        return z_cols(off + c * LANES, LANES)

    def tab(kind):
        return tab_ref[:, kind * TAB_W:(kind + 1) * TAB_W]

    def put(off, c, v):
        p_ref[:, off + c * LANES:off + (c + 1) * LANES] = v.astype(BF16)

    def rope(off, c, kind):
        return _rope128(zc(off, c), tab(kind), ROPE_KINDS[kind][0] // 2)

    for zoff, poff, kind in ((Z_AQ, P_AQ, 0), (Z_AK, P_AK, 0), (Z_BQ, P_BQ, 1), (Z_DK, P_DK, 1)):
        for c in range(4):
            put(poff, c, rope(zoff, c, kind))
    for c in range(4):
        put(P_AV, c, zc(Z_AV, c))
    put(P_VS, 0, zc(Z_VS)); put(P_VW, 0, zc(Z_VW))
    put(P_KS, 0, rope(Z_KS, 0, 1)); put(P_KW, 0, rope(Z_KW, 0, 1))
    put(P_KR, 0, rope(Z_KR, 0, 2))
    kc_ref[...] = rope(Z_KC, 0, 1).astype(BF16)
    vc_ref[...] = zc(Z_VC).astype(BF16)

    for zoff, toff, kind in ((Z_DQ, T_DQ, 1), (Z_IQ, T_IQ, 0)):
        for c in range(4):
            t_ref[0, toff + c * LANES:toff + (c + 1) * LANES, :] = rope(zoff, c, kind).T.astype(BF16)
    for c in range(4):
        vt_ref[0, 0, c * LANES:(c + 1) * LANES, :] = zc(Z_DV, c).T.astype(BF16)

    cq = z_cols(Z_CQ, 512)
    ms = jnp.sum(cq * cq, axis=-1, keepdims=True) * (1.0 / MLA_Q_LORA)
    p_ref[:, P_CQ:P_CQ + 512] = (cq * lax.rsqrt(ms + EPS) * gq_ref[...]).astype(BF16)
    ckv = z_cols(Z_CKV, MLA_KV_LORA)
    ms = jnp.mean(ckv * ckv, axis=-1, keepdims=True)
    p_ref[:, P_CKV:P_CKV + MLA_KV_LORA] = (ckv * lax.rsqrt(ms + EPS) * gkv_ref[...]).astype(BF16)

    ik = zc(Z_IK)
    ms = jnp.mean(ik * ik, axis=-1, keepdims=True)
    ikn = ik * lax.rsqrt(ms + EPS) * gik_ref[...]
    put(P_IK, 0, _rope128(ikn, tab(0), ROPE_KINDS[0][0] // 2))

    sm = zc(Z_SMALL)
    lane = lax.broadcasted_iota(I32, sm.shape, 1)
    iw_scale = IDX_HEADS ** -0.5 * IDX_DIM ** -0.5
    small = jnp.where(lane < SMALL_IW, jax.nn.sigmoid(sm), sm * iw_scale)
    small_ref[...] = small
    smallt_ref[0] = small.T


def _proj_prep(x2, g, w, tab, gq, gkv, gik, *, batch, seq, ts):
    m, d = x2.shape
    spb = seq // ts
    assert ts == VT_TILE and w.shape == (d, Z_WIDTH)
    row = lambda w: pl.BlockSpec((ts, w), lambda i: (i, 0))
    return pl.pallas_call(
        _proj_prep_kernel,
        out_shape=(jax.ShapeDtypeStruct((m, P_WIDTH), BF16),
                   jax.ShapeDtypeStruct((batch, T_ROWS, seq), BF16),
                   jax.ShapeDtypeStruct((batch, spb, BR_WIDTH, VT_TILE), BF16),
                   jax.ShapeDtypeStruct((m, LANES), BF16),
                   jax.ShapeDtypeStruct((m, LANES), BF16),
                   jax.ShapeDtypeStruct((m, LANES), F32),
                   jax.ShapeDtypeStruct((batch, LANES, seq), F32)),
        grid=(m // ts,),
        in_specs=[row(d),
                  pl.BlockSpec((1, d), lambda i: (0, 0)),
                  pl.BlockSpec((d, Z_WIDTH), lambda i: (0, 0)),
                  pl.BlockSpec((ts, 3 * TAB_W), lambda i: (i % spb, 0)),
                  pl.BlockSpec((1, 512), lambda i: (0, 0)),
                  pl.BlockSpec((1, MLA_KV_LORA), lambda i: (0, 0)),
                  pl.BlockSpec((1, LANES), lambda i: (0, 0))],
        out_specs=(row(P_WIDTH),
                   pl.BlockSpec((1, T_ROWS, ts), lambda i: (i // spb, 0, i % spb)),
                   pl.BlockSpec((1, 1, BR_WIDTH, VT_TILE), lambda i: (i // spb, i % spb, 0, 0)),
                   row(LANES), row(LANES), row(LANES),
                   pl.BlockSpec((1, LANES, ts), lambda i: (i // spb, 0, i % spb))),
        compiler_params=_cparams(1),
        name="proj_prep",
    )(x2, g, w, tab, gq, gkv, gik)


def _softmax_init(mx_scr, l_scr, acc_scr):
    mx_scr[...] = jnp.full(mx_scr.shape, NEG, F32)
    l_scr[...] = jnp.zeros(l_scr.shape, F32)
    acc_scr[...] = jnp.zeros(acc_scr.shape, F32)


def _score_store(g, j, s, s_scr, mx_scr):
    s_scr[g, j] = s
    m = s[:, 0:LANES]
    for c in range(1, s.shape[1] // LANES):
        m = jnp.maximum(m, s[:, c * LANES:(c + 1) * LANES])
    mx_scr[g] = jnp.maximum(mx_scr[g], m)


def _row_max_finish(mx_scr):
    for g in range(mx_scr.shape[0]):
        m = jnp.max(mx_scr[g], axis=-1, keepdims=True)
        mx_scr[g] = jnp.broadcast_to(m, mx_scr.shape[1:])


def _prob_accumulate(g, j, v_tile, s_scr, mx_scr, l_scr, acc_scr):
    mb = mx_scr[g]
    s = s_scr[g, j]
    ps = [jnp.exp2(s[:, c * LANES:(c + 1) * LANES] - mb) for c in range(s.shape[1] // LANES)]
    tot = ps[0]
    for p in ps[1:]:
        tot = tot + p
    l_scr[g] += tot
    acc_scr[g] += _dot(jnp.concatenate(ps, axis=1).astype(BF16), v_tile)


def _softmax_out(g, l_scr, acc_scr):
    return acc_scr[g] / jnp.sum(l_scr[g], axis=-1, keepdims=True)


def _causal_tiles(step, n_full):
    def body(j, carry):
        step(j, False)
        return carry
    lax.fori_loop(0, n_full, body, 0)
    step(n_full, True)


def _softmax_scratch(groups, n_tiles, rows, tk):
    return [pltpu.VMEM((groups, n_tiles, rows, tk), F32),
            pltpu.VMEM((groups, rows, LANES), F32),
            pltpu.VMEM((groups, rows, LANES), F32),
            pltpu.VMEM((groups, rows, HEAD_W), F32)]


def _diff_attn_kernel(q_ref, k_ref, v_ref, lv_ref, g_ref, o_ref, s_scr, mx_scr, l_scr, acc_scr,
                      *, tq, tk, lam_init):
    qs = pl.program_id(1) * tq
    n_full = qs // tk
    scale = DA_DIM ** -0.5 * LOG2E
    lv = lv_ref[...]
    lam = (jnp.exp(jnp.sum(lv[0:1] * lv[1:2], axis=-1, keepdims=True))
           - jnp.exp(jnp.sum(lv[2:3] * lv[3:4], axis=-1, keepdims=True)) + lam_init)
    lane = lax.broadcasted_iota(I32, (tq, HEAD_W), 1)
    row_t = qs + lax.broadcasted_iota(I32, (2 * tq, 1), 0) % tq
    col0 = lax.broadcasted_iota(I32, (2 * tq, tk), 1)
    _softmax_init(mx_scr, l_scr, acc_scr)

    def scores(j, masked):
        ks = pl.multiple_of(j * tk, tk)
        for h in range(HEADS):
            hs = slice(h * HEAD_W, (h + 1) * HEAD_W)
            qh = q_ref[0, :, hs]
            zero = jnp.zeros_like(qh)
            q2 = jnp.concatenate([jnp.where(lane < DA_DIM, qh, zero),
                                  jnp.where(lane >= DA_DIM, qh, zero)], axis=0)
            s = _dot_nt(q2, k_ref[0, pl.ds(ks, tk), hs]) * scale
            if masked:
                s = jnp.where(col0 + ks <= row_t, s, NEG)
            _score_store(h, j, s, s_scr, mx_scr)

    _causal_tiles(scores, n_full)
    _row_max_finish(mx_scr)

    def probs(j, carry):
        ks = pl.multiple_of(j * tk, tk)
        for h in range(HEADS):
            v_tile = v_ref[0, pl.ds(ks, tk), h * HEAD_W:(h + 1) * HEAD_W]
            _prob_accumulate(h, j, v_tile, s_scr, mx_scr, l_scr, acc_scr)
        return carry

    lax.fori_loop(0, n_full + 1, probs, 0)
    for h in range(HEADS):
        o2 = _softmax_out(h, l_scr, acc_scr)
        o = o2[:tq] - lam * o2[tq:]
        ms = jnp.mean(o * o, axis=-1, keepdims=True)
        o = o * lax.rsqrt(ms + EPS) * g_ref[...]
        o_ref[0, :, h * HEAD_W:(h + 1) * HEAD_W] = (o * (1.0 - lam_init)).astype(BF16)


def _diff_attn(p3, lv, g, *, lam_init, tq, tk):
    b, s, _ = p3.shape
    return pl.pallas_call(
        functools.partial(_diff_attn_kernel, tq=tq, tk=tk, lam_init=lam_init),
        out_shape=jax.ShapeDtypeStruct((b, s, BR_WIDTH), BF16),
        grid=(b, s // tq),
        in_specs=[pl.BlockSpec((1, tq, 512), lambda bi, i: (bi, i, P_AQ // 512)),
                  pl.BlockSpec((1, s, 512), lambda bi, i: (bi, 0, P_AK // 512)),
                  pl.BlockSpec((1, s, 512), lambda bi, i: (bi, 0, P_AV // 512)),
                  pl.BlockSpec((4, DA_DIM), lambda bi, i: (0, 0)),
                  pl.BlockSpec((1, HEAD_W), lambda bi, i: (0, 0))],
        out_specs=pl.BlockSpec((1, tq, BR_WIDTH), lambda bi, i: (bi, i, 0)),
        scratch_shapes=_softmax_scratch(HEADS, s // tk, 2 * tq, tk),
        compiler_params=_cparams(2),
        name="diff_attn",
    )(p3, p3, p3, lv, g)


def _nsa_compress_kernel(gk_ref, gv_ref, w1k_ref, w1v_ref, pek_ref, pev_ref,
                         w1kf_ref, w1vf_ref, w2k_ref, w2v_ref, kc_ref, vc_ref):
    def one(g_ref, w1cat_ref, pe_ref, w1f_ref, w2_ref, o_ref):
        y = _dot(g_ref[0], w1cat_ref[...])
        n = y.shape[0]
        nxt = pltpu.roll(y[:, HEAD_W:], n - 1, 0)
        c = _dot(pe_ref[...], w1f_ref[...])[0:1]
        hid = jax.nn.gelu(y[:, :HEAD_W] + nxt + c)
        o_ref[0] = _dot(hid.astype(BF16), w2_ref[...]).astype(BF16)

    one(gk_ref, w1k_ref, pek_ref, w1kf_ref, w2k_ref, kc_ref)
    one(gv_ref, w1v_ref, pev_ref, w1vf_ref, w2v_ref, vc_ref)


def _nsa_compress(gk, gv, w1k_cat, w1v_cat, pek, pev, w1k, w1v, w2k, w2v):
    b, ng, gw = gk.shape
    full = lambda shape: pl.BlockSpec(shape, lambda bi: (0,) * len(shape))
    return pl.pallas_call(
        _nsa_compress_kernel,
        out_shape=(jax.ShapeDtypeStruct((b, ng, HEAD_W), BF16),
                   jax.ShapeDtypeStruct((b, ng, HEAD_W), BF16)),
        grid=(b,),
        in_specs=[pl.BlockSpec((1, ng, gw), lambda bi: (bi, 0, 0)),
                  pl.BlockSpec((1, ng, gw), lambda bi: (bi, 0, 0)),
                  full(w1k_cat.shape), full(w1v_cat.shape), full(pek.shape), full(pev.shape),
                  full(w1k.shape), full(w1v.shape), full(w2k.shape), full(w2v.shape)],
        out_specs=(pl.BlockSpec((1, ng, HEAD_W), lambda bi: (bi, 0, 0)),
                   pl.BlockSpec((1, ng, HEAD_W), lambda bi: (bi, 0, 0))),
        compiler_params=_cparams(1),
        name="nsa_compress",
    )(gk, gv, w1k_cat, w1v_cat, pek, pev, w1k, w1v, w2k, w2v)


NSA_GROUPS = 2


def _nsa_kernel(q_ref, kc_ref, vc_ref, ks_ref, vs_ref, kw_ref, vw_ref, small_ref, ov_ref, e_ref,
                o_ref, s_scr, mx_scr, l_scr, acc_scr, cmp_scr, win_scr, *, tq, tk, seq):
    qs = pl.program_id(1) * tq
    scale = NSA_DK ** -0.5
    ns = seq // SEL_LEN
    n_sel = min(SEL_N, ns)
    r = HEADS * tq
    rg = r // NSA_GROUPS
    q4 = jnp.concatenate([q_ref[0, :, h * HEAD_W:(h + 1) * HEAD_W] for h in range(HEADS)], axis=0)
    t1 = qs + lax.broadcasted_iota(I32, (tq, 1), 0)
    t4 = qs + lax.broadcasted_iota(I32, (r, 1), 0) % tq

    wspan = WIN + tq
    start = pl.multiple_of(jnp.maximum(qs - WIN, 0), tq)
    sw = _dot_nt(q4, kw_ref[0, pl.ds(start, wspan), :]) * scale
    dist = t4 - (start + lax.broadcasted_iota(I32, (r, wspan), 1))
    sw = jnp.where(pltpu.bitcast(dist, jnp.uint32) < jnp.uint32(WIN), sw, NEG)
    e = jnp.exp(sw - jnp.max(sw, axis=-1, keepdims=True))
    pw = e / jnp.sum(e, axis=-1, keepdims=True)
    win_scr[...] = _dot(pw.astype(BF16), vw_ref[0, pl.ds(start, wspan), :])

    kc = kc_ref[0]
    nc_pad = kc.shape[0]
    sc = _dot_nt(q4, kc) * scale
    c_end = lax.broadcasted_iota(I32, (r, nc_pad), 1) * CMP_STRIDE + (CMP_LEN - 1)
    cmask = c_end <= t4
    mx = jnp.max(jnp.where(cmask, sc, NEG), axis=-1, keepdims=True)
    e = jnp.where(cmask, jnp.exp(sc - mx), 0.0)
    den = jnp.sum(e, axis=-1, keepdims=True)
    pc = e / jnp.where(den > 0.0, den, 1.0)
    cmp_scr[...] = _dot(pc.astype(BF16), vc_ref[0])

    psum = pc[0:tq] + pc[tq:2 * tq] + pc[2 * tq:3 * tq] + pc[3 * tq:4 * tq]
    ov = ov_ref[...]
    hi = psum.astype(BF16)
    r1 = psum - hi.astype(F32)
    mid = r1.astype(BF16)
    lo = (r1 - mid.astype(F32)).astype(BF16)
    imp = _dot(hi, ov) + _dot(mid, ov) + _dot(lo, ov)

    blk = lax.broadcasted_iota(I32, (tq, LANES), 1)
    cur = t1 // SEL_LEN
    forced = (blk == 0) | (blk == cur) | (blk == cur - 1)
    visible = blk * SEL_LEN <= t1
    score = jnp.where(visible, jnp.where(forced, FORCE_SCORE, imp), NEG)
    score = jnp.where(blk < ns, score, PAD_SCORE)
    ns_pad = -(-ns // SUBLANES) * SUBLANES
    score_t = score.T[:ns_pad]
    blk_t = lax.broadcasted_iota(I32, (ns_pad, tq), 0)
    rank = jnp.zeros((ns_pad, tq), I32)
    for jp in range(ns):
        row = score_t[jp:jp + 1, :]
        later = (blk_t > jp).astype(I32)
        rank = rank + jnp.where(row > score_t, 1, jnp.where(row == score_t, later, 0))
    sel_t = jnp.where(rank < n_sel, 1.0, 0.0)
    if ns_pad < LANES:
        sel_t = jnp.concatenate([sel_t, jnp.zeros((LANES - ns_pad, tq), F32)], axis=0)
    selb = sel_t.T.astype(BF16)

    _softmax_init(mx_scr, l_scr, acc_scr)
    col0 = lax.broadcasted_iota(I32, (rg, tk), 1)
    tg = qs + lax.broadcasted_iota(I32, (rg, 1), 0) % tq
    n_tiles = qs // tk + 1

    def scores(j, masked):
        ks0 = pl.multiple_of(j * tk, tk)
        mt = _dot(selb, e_ref[j])
        mg = jnp.concatenate([mt] * (rg // tq), axis=0)
        k_tile = ks_ref[0, pl.ds(ks0, tk), :]
        for g in range(NSA_GROUPS):
            s = _dot_nt(q4[g * rg:(g + 1) * rg], k_tile) * (scale * LOG2E)
            s = jnp.where(mg > 0.5, s, NEG)
            if masked:
                s = jnp.where(col0 + ks0 <= tg, s, NEG)
            _score_store(g, j, s, s_scr, mx_scr)

    _causal_tiles(scores, n_tiles - 1)
    _row_max_finish(mx_scr)

    def probs(j, carry):
        ks0 = pl.multiple_of(j * tk, tk)
        v_tile = vs_ref[0, pl.ds(ks0, tk), :]
        for g in range(NSA_GROUPS):
            _prob_accumulate(g, j, v_tile, s_scr, mx_scr, l_scr, acc_scr)
        return carry

    lax.fori_loop(0, n_tiles, probs, 0)
    o_slc = jnp.concatenate([_softmax_out(g, l_scr, acc_scr) for g in range(NSA_GROUPS)], axis=0)

    gates = small_ref[0]
    for h in range(HEADS):
        rows = slice(h * tq, (h + 1) * tq)
        g0 = gates[:, SMALL_G + 3 * h:SMALL_G + 3 * h + 1]
        g1 = gates[:, SMALL_G + 3 * h + 1:SMALL_G + 3 * h + 2]
        g2 = gates[:, SMALL_G + 3 * h + 2:SMALL_G + 3 * h + 3]
        o = g0 * cmp_scr[rows, :] + g1 * o_slc[rows] + g2 * win_scr[rows, :]
        o_ref[0, :, h * HEAD_W:(h + 1) * HEAD_W] = o.astype(BF16)


def _nsa(p3, kc, vc, small3, ov, emat, *, tq, tk):
    b, s, _ = p3.shape
    ng = kc.shape[1]
    col = lambda off: (lambda bi, i: (bi, 0, off // LANES))
    return pl.pallas_call(
        functools.partial(_nsa_kernel, tq=tq, tk=tk, seq=s),
        out_shape=jax.ShapeDtypeStruct((b, s, BR_WIDTH), BF16),
        grid=(b, s // tq),
        in_specs=[pl.BlockSpec((1, tq, 512), lambda bi, i: (bi, i, P_BQ // 512)),
                  pl.BlockSpec((1, ng, HEAD_W), lambda bi, i: (bi, 0, 0)),
                  pl.BlockSpec((1, ng, HEAD_W), lambda bi, i: (bi, 0, 0)),
                  pl.BlockSpec((1, s, LANES), col(P_KS)),
                  pl.BlockSpec((1, s, LANES), col(P_VS)),
                  pl.BlockSpec((1, s, LANES), col(P_KW)),
                  pl.BlockSpec((1, s, LANES), col(P_VW)),
                  pl.BlockSpec((1, tq, LANES), lambda bi, i: (bi, i, 0)),
                  pl.BlockSpec(ov.shape, lambda bi, i: (0, 0)),
                  pl.BlockSpec(emat.shape, lambda bi, i: (0, 0, 0))],
        out_specs=pl.BlockSpec((1, tq, BR_WIDTH), lambda bi, i: (bi, i, 0)),
        scratch_shapes=(_softmax_scratch(NSA_GROUPS, s // tk, HEADS * tq // NSA_GROUPS, tk)
                        + [pltpu.VMEM((HEADS * tq, HEAD_W), F32), pltpu.VMEM((HEADS * tq, HEAD_W), F32)]),
        compiler_params=_cparams(2),
        name="nsa_attn",
    )(p3, kc, vc, p3, p3, p3, p3, small3, ov, emat)


def _mla_up_kernel(p_ref, ckv_ref, tab_ref, wq_ref, wkv_ref, q_ref, kv_ref):
    q = _dot(p_ref[...], wq_ref[...])
    nn = HEADS * MLA_NOPE
    q_ref[:, :nn] = q[:, :nn].astype(BF16)
    for c in range(nn // LANES, (nn + HEADS * MLA_ROPE) // LANES):
        tile = _rope128(q[:, c * LANES:(c + 1) * LANES], tab_ref[...], MLA_ROPE // 2)
        q_ref[:, c * LANES:(c + 1) * LANES] = tile.astype(BF16)
    kv_ref[...] = _dot(ckv_ref[...], wkv_ref[...]).astype(BF16)


def _mla_up(p2, tab, wq, wkv, *, seq, ts):
    m = p2.shape[0]
    spb = seq // ts
    nq = wq.shape[1]
    nkv = wkv.shape[1]
    return pl.pallas_call(
        _mla_up_kernel,
        out_shape=(jax.ShapeDtypeStruct((m, nq), BF16), jax.ShapeDtypeStruct((m, nkv), BF16)),
        grid=(m // ts,),
        in_specs=[pl.BlockSpec((ts, 512), lambda i: (i, P_CQ // 512)),
                  pl.BlockSpec((ts, MLA_KV_LORA), lambda i: (i, P_CKV // MLA_KV_LORA)),
                  pl.BlockSpec((ts, TAB_W), lambda i: (i % spb, 2)),
                  pl.BlockSpec(wq.shape, lambda i: (0, 0)),
                  pl.BlockSpec(wkv.shape, lambda i: (0, 0))],
        out_specs=(pl.BlockSpec((ts, nq), lambda i: (i, 0)),
                   pl.BlockSpec((ts, nkv), lambda i: (i, 0))),
        compiler_params=_cparams(1),
        name="mla_up",
    )(p2, p2, tab, wq, wkv)


def _mla_attn_kernel(qn_ref, qr_ref, kn_ref, kr_ref, v_ref, o_ref, s_scr, mx_scr, l_scr, acc_scr,
                     *, tq, tk):
    qs = pl.program_id(1) * tq
    n_full = qs // tk
    scale = (MLA_NOPE + MLA_ROPE) ** -0.5 * LOG2E
    lane = lax.broadcasted_iota(I32, (tq, LANES), 1)
    row_t = qs + lax.broadcasted_iota(I32, (tq, 1), 0)
    col0 = lax.broadcasted_iota(I32, (tq, tk), 1)
    _softmax_init(mx_scr, l_scr, acc_scr)

    def scores(j, masked):
        ks = pl.multiple_of(j * tk, tk)
        kr_tile = kr_ref[0, pl.ds(ks, tk), :]
        for h in range(HEADS):
            hs = slice(h * HEAD_W, (h + 1) * HEAD_W)
            pair = qr_ref[0, :, (h // 2) * LANES:(h // 2 + 1) * LANES]
            keep = (lane < MLA_ROPE) if h % 2 == 0 else (lane >= MLA_ROPE)
            qr = jnp.where(keep, pair, jnp.zeros_like(pair))
            s = _dot_nt(jnp.concatenate([qn_ref[0, :, hs], qr], axis=1),
                        jnp.concatenate([kn_ref[0, pl.ds(ks, tk), hs], kr_tile], axis=1)) * scale
            if masked:
                s = jnp.where(col0 + ks <= row_t, s, NEG)
            _score_store(h, j, s, s_scr, mx_scr)

    _causal_tiles(scores, n_full)
    _row_max_finish(mx_scr)

    def probs(j, carry):
        ks = pl.multiple_of(j * tk, tk)
        for h in range(HEADS):
            v_tile = v_ref[0, pl.ds(ks, tk), h * HEAD_W:(h + 1) * HEAD_W]
            _prob_accumulate(h, j, v_tile, s_scr, mx_scr, l_scr, acc_scr)
        return carry

    lax.fori_loop(0, n_full + 1, probs, 0)
    for h in range(HEADS):
        o_ref[0, :, h * HEAD_W:(h + 1) * HEAD_W] = _softmax_out(h, l_scr, acc_scr).astype(BF16)


def _mla_attn(q3, kv3, p3, *, tq, tk):
    b, s, _ = q3.shape
    return pl.pallas_call(
        functools.partial(_mla_attn_kernel, tq=tq, tk=tk),
        out_shape=jax.ShapeDtypeStruct((b, s, BR_WIDTH), BF16),
        grid=(b, s // tq),
        in_specs=[pl.BlockSpec((1, tq, 512), lambda bi, i: (bi, i, 0)),
                  pl.BlockSpec((1, tq, 256), lambda bi, i: (bi, i, 2)),
                  pl.BlockSpec((1, s, 512), lambda bi, i: (bi, 0, 0)),
                  pl.BlockSpec((1, s, LANES), lambda bi, i: (bi, 0, P_KR // LANES)),
                  pl.BlockSpec((1, s, 512), lambda bi, i: (bi, 0, 1))],
        out_specs=pl.BlockSpec((1, tq, BR_WIDTH), lambda bi, i: (bi, i, 0)),
        scratch_shapes=_softmax_scratch(HEADS, s // tk, tq, tk),
        compiler_params=_cparams(2),
        name="mla_attn",
    )(q3, q3, kv3, p3, kv3)


def _sortable_key(x):
    bits = pltpu.bitcast(x + 0.0, I32)
    return bits ^ (lax.shift_right_arithmetic(bits, 31) & 0x7FFFFFFF)


def _fold_rows(x, op):
    n = x.shape[0] // SUBLANES
    return op(x.reshape(n, SUBLANES, x.shape[1]), axis=0)


def _count16(half_scr, n_tiles, pred, tq):
    def count_tile(j, cnt):
        hit = pred(half_scr[j]).astype(I16)
        parts = [hit[r:r + PACKED_ROWS] for r in range(0, hit.shape[0], PACKED_ROWS)]
        while len(parts) > 1:
            parts = [a + b for a, b in zip(parts[0::2], parts[1::2])]
        return cnt + parts[0]

    cnt = lax.fori_loop(0, n_tiles, count_tile, jnp.zeros((PACKED_ROWS, tq), I16))
    return jnp.sum(cnt.astype(I32), axis=0, keepdims=True)


def _bisect16(half_scr, n_tiles, need, tq):
    def bit_body(i, th):
        cand = th + lax.shift_left(jnp.int32(1), 15 - i)
        c16 = cand.astype(I16)
        total = _count16(half_scr, n_tiles, lambda x: x >= c16, tq)
        return jnp.where(total >= need, cand, th)

    return lax.fori_loop(0, 16, bit_body, jnp.full((1, tq), HALF_MIN, I32))


def _dsa_kernel(qt_ref, iqt_ref, iwt_ref, k_ref, ik_ref, vt_ref, o_ref,
                key_scr, half_scr, s_scr, mx_scr, l_scr, acc_scr, *, tq, tk, top):
    qs = pl.program_id(1) * tq
    n_tiles = (qs + tq - 1) // tk + 1
    scale = DSA_DIM ** -0.5 * LOG2E
    t_lane = qs + lax.broadcasted_iota(I32, (tk, tq), 1)
    krow0 = lax.broadcasted_iota(I32, (tk, tq), 0)
    half_rows = lax.broadcasted_iota(I32, (LANES, tq), 0) < IDX_DIM
    vt_per_tile = tk // VT_TILE

    def score_tile(j, carry):
        ks = pl.multiple_of(j * tk, tk)
        ikt = ik_ref[0, pl.ds(ks, tk), :]
        acc = jnp.zeros((tk, tq), F32)
        for h in range(IDX_HEADS):
            pair = iqt_ref[0, (h // 2) * LANES:(h // 2 + 1) * LANES, :]
            keep = half_rows if h % 2 == 0 else jnp.logical_not(half_rows)
            iq_h = jnp.where(keep, pair, jnp.zeros_like(pair))
            w_h = iwt_ref[0, SMALL_IW + h:SMALL_IW + h + 1, :]
            acc = acc + w_h * jnp.maximum(_dot(ikt, iq_h), 0.0)
        key = jnp.where(krow0 + ks <= t_lane, _sortable_key(acc), INT_MIN)
        key_scr[j] = key
        half_scr[j] = lax.shift_right_arithmetic(key, 16).astype(I16)
        return carry

    lax.fori_loop(0, n_tiles, score_tile, 0)

    th_hi = _bisect16(half_scr, n_tiles, top, tq)
    hi16 = th_hi.astype(I16)
    need_lo = top - _count16(half_scr, n_tiles, lambda x: x > hi16, tq)

    def low_tile(j, carry):
        key = key_scr[j]
        lo = (key & 0xFFFF) + HALF_MIN
        same_hi = lax.shift_right_arithmetic(key, 16) == th_hi
        half_scr[j] = jnp.where(same_hi, lo, HALF_MIN).astype(I16)
        return carry

    lax.fori_loop(0, n_tiles, low_tile, 0)
    th_lo = _bisect16(half_scr, n_tiles, need_lo, tq)
    theta = lax.shift_left(th_hi, 16) + (th_lo - HALF_MIN)
    theta = jnp.maximum(theta, INT_MIN + 1)

    mx_scr[...] = jnp.full(mx_scr.shape, NEG, F32)
    l_scr[...] = jnp.zeros(l_scr.shape, F32)
    acc_scr[...] = jnp.zeros(acc_scr.shape, F32)

    def scores(j, carry):
        ks = pl.multiple_of(j * tk, tk)
        sel = key_scr[j] >= theta
        for h in range(HEADS):
            hs = slice(h * HEAD_W, (h + 1) * HEAD_W)
            s = _dot(k_ref[0, pl.ds(ks, tk), hs], qt_ref[0, hs, :]) * scale
            s = jnp.where(sel, s, NEG)
            s_scr[h, j] = s
            mx_scr[h] = jnp.maximum(mx_scr[h], _fold_rows(s, jnp.max))
        return carry

    lax.fori_loop(0, n_tiles, scores, 0)
    for h in range(HEADS):
        m = jnp.max(mx_scr[h], axis=0, keepdims=True)
        mx_scr[h] = jnp.broadcast_to(m, (SUBLANES, tq))

    def probs(j, carry):
        for h in range(HEADS):
            p = jnp.exp2(s_scr[h, j] - mx_scr[h][0:1])
            l_scr[h] += _fold_rows(p, jnp.sum)
            pb = p.astype(BF16)
            for c in range(vt_per_tile):
                vt = vt_ref[0, j * vt_per_tile + c, h * HEAD_W:(h + 1) * HEAD_W, :]
                acc_scr[h] += _dot(vt, pb[c * VT_TILE:(c + 1) * VT_TILE])
        return carry

    lax.fori_loop(0, n_tiles, probs, 0)
    for h in range(HEADS):
        ot = acc_scr[h] / jnp.sum(l_scr[h], axis=0, keepdims=True)
        o_ref[0, :, h * HEAD_W:(h + 1) * HEAD_W] = ot.T.astype(BF16)


def _dsa(p3, t3, vt4, smallt, *, tq, tk, top):
    b, s, _ = p3.shape
    n_vt = vt4.shape[1]
    return pl.pallas_call(
        functools.partial(_dsa_kernel, tq=tq, tk=tk, top=top),
        out_shape=jax.ShapeDtypeStruct((b, s, BR_WIDTH), BF16),
        grid=(b, s // tq),
        in_specs=[pl.BlockSpec((1, 512, tq), lambda bi, i: (bi, T_DQ // 512, i)),
                  pl.BlockSpec((1, 512, tq), lambda bi, i: (bi, T_IQ // 512, i)),
                  pl.BlockSpec((1, LANES, tq), lambda bi, i: (bi, 0, i)),
                  pl.BlockSpec((1, s, 512), lambda bi, i: (bi, 0, P_DK // 512)),
                  pl.BlockSpec((1, s, LANES), lambda bi, i: (bi, 0, P_IK // LANES)),
                  pl.BlockSpec((1, n_vt, BR_WIDTH, VT_TILE), lambda bi, i: (bi, 0, 0, 0))],
        out_specs=pl.BlockSpec((1, tq, BR_WIDTH), lambda bi, i: (bi, i, 0)),
        scratch_shapes=[pltpu.VMEM((s // tk, tk, tq), I32),
                        pltpu.VMEM((s // tk, tk, tq), I16),
                        pltpu.VMEM((HEADS, s // tk, tk, tq), F32),
                        pltpu.VMEM((HEADS, SUBLANES, tq), F32),
                        pltpu.VMEM((HEADS, SUBLANES, tq), F32),
                        pltpu.VMEM((HEADS, HEAD_W, tq), F32)],
        compiler_params=_cparams(2),
        name="dsa_attn",
    )(t3, t3, smallt, p3, p3, vt4)


def _merge_kernel(x_ref, oa_ref, ob_ref, oc_ref, od_ref, g_ref, wb_ref, wo_ref, o_ref):
    d = x_ref.shape[1]
    acc = jnp.zeros(x_ref.shape, F32)
    for n, br_ref in enumerate((oa_ref, ob_ref, oc_ref, od_ref)):
        br = _dot(br_ref[...], wb_ref[n])
        acc = acc + g_ref[:, n * d:(n + 1) * d].astype(F32) * br
    o_ref[...] = x_ref[...] + _dot(acc.astype(BF16), wo_ref[...])


def _merge(x2, oa, ob, oc, od, gates, wb, wo, *, tm):
    m, d = x2.shape
    row = lambda w: pl.BlockSpec((tm, w), lambda i: (i, 0))
    return pl.pallas_call(
        _merge_kernel,
        out_shape=jax.ShapeDtypeStruct((m, d), F32),
        grid=(m // tm,),
        in_specs=[row(d), row(BR_WIDTH), row(BR_WIDTH), row(BR_WIDTH), row(BR_WIDTH),
                  row(gates.shape[1]),
                  pl.BlockSpec(wb.shape, lambda i: (0, 0, 0)),
                  pl.BlockSpec(wo.shape, lambda i: (0, 0))],
        out_specs=row(d),
        compiler_params=_cparams(1),
        name="merge",
    )(x2, oa, ob, oc, od, gates, wb, wo)


def _ffn_kernel(x_ref, g_ref, wg_ref, wu_ref, wd_ref, gf_ref, o_ref, h_scr, acc_scr, *, final):
    j = pl.program_id(1)

    @pl.when(j == 0)
    def _():
        x = x_ref[...]
        ms = jnp.mean(x * x, axis=-1, keepdims=True)
        h_scr[...] = (x * lax.rsqrt(ms + EPS) * g_ref[...]).astype(BF16)
        acc_scr[...] = jnp.zeros(acc_scr.shape, F32)

    h = h_scr[...]
    a = jax.nn.silu(_dot(h, wg_ref[...])) * _dot(h, wu_ref[...])
    acc_scr[...] += _dot(a.astype(BF16), wd_ref[...])

    @pl.when(j == pl.num_programs(1) - 1)
    def _():
        y = x_ref[...] + acc_scr[...]
        if final:
            ms = jnp.mean(y * y, axis=-1, keepdims=True)
            y = y * lax.rsqrt(ms + EPS) * gf_ref[...]
        o_ref[...] = y


def _ffn(x2, g, wg, wu, wd, gf, *, final, tm, tf):
    m, d = x2.shape
    dff = wg.shape[1]
    return pl.pallas_call(
        functools.partial(_ffn_kernel, final=final),
        out_shape=jax.ShapeDtypeStruct((m, d), F32),
        grid=(m // tm, dff // tf),
        in_specs=[pl.BlockSpec((tm, d), lambda i, j: (i, 0)),
                  pl.BlockSpec((1, d), lambda i, j: (0, 0)),
                  pl.BlockSpec((d, tf), lambda i, j: (0, j)),
                  pl.BlockSpec((d, tf), lambda i, j: (0, j)),
                  pl.BlockSpec((tf, d), lambda i, j: (j, 0)),
                  pl.BlockSpec((1, d), lambda i, j: (0, 0))],
        out_specs=pl.BlockSpec((tm, d), lambda i, j: (i, 0)),
        scratch_shapes=[pltpu.VMEM((tm, d), BF16), pltpu.VMEM((tm, d), F32)],
        compiler_params=_cparams(2),
        name="ffn",
    )(x2, g, wg, wu, wd, gf)


def _tiles(seq, m, dff):
    pick = lambda n, cands: next(c for c in cands if n % c == 0)
    tk = pick(seq, (512, 256))
    return dict(
        proj_tm=pick(m, (1024, 512, 256, 128)), proj_tn=1024,
        prep_ts=VT_TILE,
        diff_tq=128, mla_tq=pick(seq, (256, 128)), nsa_tq=128, dsa_tq=256, tk=tk,
        row_tm=pick(m, (512, 256, 128)),
        ffn_tf=pick(dff, (1408, 704, 256, 128)),
    )


def kernel(x, norm1_g, w_in, diff_lq1, diff_lk1, diff_lq2, diff_lk2, diff_subln_g, nsa_pe_k, nsa_w1_k, nsa_w2_k, nsa_pe_v, nsa_w1_v, nsa_w2_v, mla_q_norm_g, mla_w_uq, mla_kv_norm_g, mla_w_ukv, idx_k_norm_g, w_branch, w_out, norm2_g, w_gate_up, w_down, final_norm_g):
    b, seq, d = x.shape
    depth = w_in.shape[0]
    m = b * seq
    dff = w_down.shape[1]
    t = _tiles(seq, m, dff)
    tk = t["tk"]
    assert seq % SEL_LEN == 0 and seq >= WIN + t["nsa_tq"] and seq // SEL_LEN <= LANES
    assert seq % t["dsa_tq"] == 0 and tk % VT_TILE == 0 and tk >= min(IDX_TOPK, seq // 4)

    col_idx, gate_off, d_in = _in_proj_columns()
    assert w_in.shape[2] == d_in
    tab = jnp.concatenate([_rope_table(seq, rot, per) for rot, per in ROPE_KINDS], axis=1)

    ng = seq // CMP_STRIDE
    ns = seq // SEL_LEN
    c_start = np.arange(ng)[:, None] * CMP_STRIDE
    s_start = np.arange(LANES)[None, :] * SEL_LEN
    ov = ((c_start < s_start + SEL_LEN) & (c_start + CMP_LEN - 1 >= s_start)
          & (np.arange(LANES)[None, :] < ns))
    ov = jnp.asarray(ov, BF16)
    emat = np.arange(LANES)[:, None] == (np.arange(seq)[None, :] // SEL_LEN)
    emat = jnp.asarray(emat.reshape(LANES, seq // tk, tk).transpose(1, 0, 2), BF16)

    qd = MLA_NOPE + MLA_ROPE
    uq_idx = np.concatenate([np.concatenate([np.arange(h * qd, h * qd + MLA_NOPE) for h in range(HEADS)]),
                             np.concatenate([np.arange(h * qd + MLA_NOPE, (h + 1) * qd) for h in range(HEADS)])])
    kvd = MLA_NOPE + HEAD_W
    ukv_idx = np.concatenate([np.concatenate([np.arange(h * kvd, h * kvd + MLA_NOPE) for h in range(HEADS)]),
                              np.concatenate([np.arange(h * kvd + MLA_NOPE, (h + 1) * kvd) for h in range(HEADS)])])

    x2 = x.reshape(m, d)
    half_w1 = CMP_STRIDE * NSA_DK
    for l in range(depth):
        lam_init = 0.8 - 0.6 * math.exp(-0.3 * l)
        w_in_l = w_in[l].astype(BF16)
        w_mix = _take_cols(w_in_l, col_idx)
        w_gate = w_in_l[:, gate_off:]
        gates = _norm_matmul(x2, norm1_g[l][None], w_gate, out_dtype=BF16, sigmoid=True,
                             tm=t["proj_tm"], tn=t["proj_tn"], name="gate_proj")

        gq = jnp.pad(mla_q_norm_g[l], (0, 512 - MLA_Q_LORA))[None]
        gkv = mla_kv_norm_g[l][None]
        gik = jnp.concatenate([idx_k_norm_g[l], idx_k_norm_g[l]])[None]
        p2, t3, vt4, kc_tok, vc_tok, small, smallt = _proj_prep(
            x2, norm1_g[l][None], w_mix, tab, gq, gkv, gik, batch=b, seq=seq, ts=t["prep_ts"])
        p3 = p2.reshape(b, seq, P_WIDTH)
        small3 = small.reshape(b, seq, LANES)

        lv = jnp.stack([diff_lq1[l], diff_lk1[l], diff_lq2[l], diff_lk2[l]])
        o_a = _diff_attn(p3, lv, diff_subln_g[l][None], lam_init=lam_init, tq=t["diff_tq"], tk=tk)

        w1k, w1v = nsa_w1_k[l].astype(BF16), nsa_w1_v[l].astype(BF16)
        w1k_cat = jnp.concatenate([w1k[:half_w1], w1k[half_w1:]], axis=1)
        w1v_cat = jnp.concatenate([w1v[:half_w1], w1v[half_w1:]], axis=1)
        pek = jnp.broadcast_to(nsa_pe_k[l].reshape(1, -1), (8, CMP_LEN * NSA_DK)).astype(BF16)
        pev = jnp.broadcast_to(nsa_pe_v[l].reshape(1, -1), (8, CMP_LEN * NSA_DK)).astype(BF16)
        kc, vc = _nsa_compress(kc_tok.reshape(b, ng, half_w1), vc_tok.reshape(b, ng, half_w1),
                               w1k_cat, w1v_cat, pek, pev, w1k, w1v,
                               nsa_w2_k[l].astype(BF16), nsa_w2_v[l].astype(BF16))
        o_b = _nsa(p3, kc, vc, small3, ov, emat, tq=t["nsa_tq"], tk=tk)

        wq = jnp.pad(_take_cols(mla_w_uq[l].astype(BF16), uq_idx), ((0, 512 - MLA_Q_LORA), (0, 0)))
        wkv = _take_cols(mla_w_ukv[l].astype(BF16), ukv_idx)
        q_c, kv_c = _mla_up(p2, tab, wq, wkv, seq=seq, ts=t["prep_ts"])
        o_c = _mla_attn(q_c.reshape(b, seq, -1), kv_c.reshape(b, seq, -1), p3, tq=t["mla_tq"], tk=tk)

        o_d = _dsa(p3, t3, vt4, smallt, tq=t["dsa_tq"], tk=tk, top=min(IDX_TOPK, seq // 4))

        x2 = _merge(x2, o_a.reshape(m, -1), o_b.reshape(m, -1), o_c.reshape(m, -1), o_d.reshape(m, -1),
                    gates, w_branch[l].astype(BF16), w_out[l].astype(BF16), tm=t["row_tm"])
        wgu = w_gate_up[l].astype(BF16)
        x2 = _ffn(x2, norm2_g[l][None], wgu[:, :dff], wgu[:, dff:], w_down[l].astype(BF16),
                  final_norm_g[None], final=(l == depth - 1), tm=t["row_tm"], tf=t["ffn_tf"])
    return x2.reshape(b, seq, d)
```

```python
import functools
import math

import numpy as np
import jax
import jax.numpy as jnp
from jax import lax
from jax.experimental import pallas as pl
from jax.experimental.pallas import tpu as pltpu

F32 = jnp.float32
BF16 = jnp.bfloat16
I32 = jnp.int32
I16 = jnp.int16

LANES = 128
SUBLANES = 8
PACKED_ROWS = 16
HALF_MIN = -32768
VMEM_LIMIT = 56 * 1024 * 1024

ROPE_THETA = 500000.0
NEG = -1e30
LOG2E = math.log2(math.e)
FORCE_SCORE = 1e9
PAD_SCORE = -3e38
EPS = 1e-6
INT_MIN = -2147483648

HEADS = 4
HEAD_W = 128
BR_WIDTH = HEADS * HEAD_W
DA_DIM = 64
NSA_DK = 128
CMP_LEN = 32
CMP_STRIDE = 16
SEL_LEN = 64
SEL_N = 16
WIN = 512
MLA_Q_LORA = 384
MLA_KV_LORA = 256
MLA_NOPE = 128
MLA_ROPE = 64
DSA_DIM = 128
IDX_HEADS = 8
IDX_DIM = 64
IDX_TOPK = 256

Z_AQ, Z_AK, Z_AV, Z_BQ, Z_DQ, Z_DK, Z_DV, Z_IQ = (i * 512 for i in range(8))
Z_CQ = 4096
Z_CKV = 4608
Z_KC, Z_KS, Z_KW, Z_VC, Z_VS, Z_VW, Z_KR, Z_IK, Z_SMALL = (4864 + i * 128 for i in range(9))
Z_WIDTH = 6144
P_AQ, P_AK, P_AV, P_BQ, P_DK, P_CQ = (i * 512 for i in range(6))
P_CKV = 3072
P_KS, P_KW, P_VS, P_VW, P_KR, P_IK = (3328 + i * 128 for i in range(6))
P_WIDTH = 4096
T_DQ, T_IQ = 0, 512
T_ROWS = 1024
VT_TILE = 256
SMALL_G = 0
SMALL_IW = 12

ROPE_KINDS = ((16, 64), (32, 128), (64, 64))
TAB_W = 3 * LANES


def _cparams(n_axes):
    return pltpu.CompilerParams(dimension_semantics=("arbitrary",) * n_axes,
                                vmem_limit_bytes=VMEM_LIMIT)


def _dot(a, b):
    return jnp.dot(a, b, preferred_element_type=F32)


def _dot_nt(a, b):
    return lax.dot_general(a, b, (((1,), (1,)), ((), ())), preferred_element_type=F32)


def _in_proj_columns():
    names = (("a_q", 512), ("a_k", 512), ("a_v", 512), ("b_q", 512),
             ("b_kc", 128), ("b_vc", 128), ("b_ks", 128), ("b_vs", 128),
             ("b_kw", 128), ("b_vw", 128), ("b_g", 12),
             ("c_q", 384), ("c_kv", 256), ("c_kr", 64),
             ("d_q", 512), ("d_k", 512), ("d_v", 512),
             ("d_iq", 512), ("d_ik", 64), ("d_iw", 8), ("gate", 4096))
    off, o = {}, 0
    for nm, n in names:
        off[nm] = (o, n)
        o += n
    idx = np.full((Z_WIDTH,), -1, np.int64)

    def put(dst, nm):
        s, n = off[nm]
        idx[dst:dst + n] = np.arange(s, s + n)

    put(Z_AQ, "a_q"); put(Z_AK, "a_k"); put(Z_AV, "a_v"); put(Z_BQ, "b_q")
    put(Z_DQ, "d_q"); put(Z_DK, "d_k"); put(Z_DV, "d_v"); put(Z_IQ, "d_iq")
    put(Z_CQ, "c_q"); put(Z_CKV, "c_kv")
    put(Z_KC, "b_kc"); put(Z_KS, "b_ks"); put(Z_KW, "b_kw")
    put(Z_VC, "b_vc"); put(Z_VS, "b_vs"); put(Z_VW, "b_vw")
    put(Z_KR, "c_kr"); put(Z_KR + 64, "c_kr")
    put(Z_IK, "d_ik"); put(Z_IK + 64, "d_ik")
    put(Z_SMALL + SMALL_G, "b_g"); put(Z_SMALL + SMALL_IW, "d_iw")
    return idx, off["gate"][0], o


def _take_cols(w, idx):
    runs, i, n = [], 0, len(idx)
    while i < n:
        j = i + 1
        if idx[i] < 0:
            while j < n and idx[j] < 0:
                j += 1
            runs.append(jnp.zeros((w.shape[0], j - i), w.dtype))
        else:
            while j < n and idx[j] == idx[j - 1] + 1:
                j += 1
            runs.append(w[:, int(idx[i]):int(idx[i]) + (j - i)])
        i = j
    return jnp.concatenate(runs, axis=1)


def _rope_table(seq, rot, period):
    half = rot // 2
    inv = jnp.power(jnp.float32(ROPE_THETA), -jnp.arange(0, rot, 2, dtype=F32) / rot)
    ang = jnp.arange(seq, dtype=F32)[:, None] * inv[None, :]
    cos, sin = jnp.cos(ang), jnp.sin(ang)
    lane = np.arange(LANES) % period
    in1 = lane < half
    in2 = (lane >= half) & (lane < 2 * half)
    fidx = np.where(in1, lane, np.where(in2, lane - half, 0))
    cosl, sinl = cos[:, fidx], sin[:, fidx]
    c = jnp.where(jnp.asarray(in1 | in2)[None], cosl, 1.0)
    s1 = jnp.where(jnp.asarray(in1)[None], -sinl, 0.0)
    s2 = jnp.where(jnp.asarray(in2)[None], sinl, 0.0)
    return jnp.concatenate([c, s1, s2], axis=1)


def _rope128(x, tab, half):
    return (x * tab[:, 0:LANES]
            + pltpu.roll(x, LANES - half, 1) * tab[:, LANES:2 * LANES]
            + pltpu.roll(x, half, 1) * tab[:, 2 * LANES:3 * LANES])


def _norm_matmul_kernel(x_ref, g_ref, w_ref, o_ref, h_scr, *, sigmoid):
    @pl.when(pl.program_id(1) == 0)
    def _():
        x = x_ref[...]
        ms = jnp.mean(x * x, axis=-1, keepdims=True)
        h_scr[...] = (x * lax.rsqrt(ms + EPS) * g_ref[...]).astype(BF16)

    z = _dot(h_scr[...], w_ref[...])
    if sigmoid:
        z = jax.nn.sigmoid(z)
    o_ref[...] = z.astype(o_ref.dtype)


def _norm_matmul(x2, g, w, *, out_dtype, sigmoid, tm, tn, name):
    m, d = x2.shape
    n = w.shape[1]
    return pl.pallas_call(
        functools.partial(_norm_matmul_kernel, sigmoid=sigmoid),
        out_shape=jax.ShapeDtypeStruct((m, n), out_dtype),
        grid=(m // tm, n // tn),
        in_specs=[pl.BlockSpec((tm, d), lambda i, j: (i, 0)),
                  pl.BlockSpec((1, d), lambda i, j: (0, 0)),
                  pl.BlockSpec((d, tn), lambda i, j: (0, j))],
        out_specs=pl.BlockSpec((tm, tn), lambda i, j: (i, j)),
        scratch_shapes=[pltpu.VMEM((tm, d), BF16)],
        compiler_params=_cparams(2),
        name=name,
    )(x2, g, w)


PROJ_TILE = 512


def _proj_prep_kernel(x_ref, g_ref, w_ref, tab_ref, gq_ref, gkv_ref, gik_ref,
                      p_ref, t_ref, vt_ref, kc_ref, vc_ref, small_ref, smallt_ref):
    x = x_ref[...]
    ms = jnp.mean(x * x, axis=-1, keepdims=True)
    h = (x * lax.rsqrt(ms + EPS) * g_ref[...]).astype(BF16)
    z_tiles = {}

    def z_cols(off, width):
        t = off // PROJ_TILE
        assert (off + width - 1) // PROJ_TILE == t
        if t not in z_tiles:
            z_tiles[t] = _dot(h, w_ref[:, t * PROJ_TILE:(t + 1) * PROJ_TILE])
        lo = off - t * PROJ_TILE
        return z_tiles[t][:, lo:lo + width]

    def zc(off, c=0):
        return z_cols(off + c * LANES, LANES)

    def tab(kind):
        return tab_ref[:, kind * TAB_W:(kind + 1) * TAB_W]

    def put(off, c, v):
        p_ref[:, off + c * LANES:off + (c + 1) * LANES] = v.astype(BF16)

    def rope(off, c, kind):
        return _rope128(zc(off, c), tab(kind), ROPE_KINDS[kind][0] // 2)

    for zoff, poff, kind in ((Z_AQ, P_AQ, 0), (Z_AK, P_AK, 0), (Z_BQ, P_BQ, 1), (Z_DK, P_DK, 1)):
        for c in range(4):
            put(poff, c, rope(zoff, c, kind))
    for c in range(4):
        put(P_AV, c, zc(Z_AV, c))
    put(P_VS, 0, zc(Z_VS)); put(P_VW, 0, zc(Z_VW))
    put(P_KS, 0, rope(Z_KS, 0, 1)); put(P_KW, 0, rope(Z_KW, 0, 1))
    put(P_KR, 0, rope(Z_KR, 0, 2))
    kc_ref[...] = rope(Z_KC, 0, 1).astype(BF16)
    vc_ref[...] = zc(Z_VC).astype(BF16)

    for zoff, toff, kind in ((Z_DQ, T_DQ, 1), (Z_IQ, T_IQ, 0)):
        for c in range(4):
            t_ref[0, toff + c * LANES:toff + (c + 1) * LANES, :] = rope(zoff, c, kind).T.astype(BF16)
    for c in range(4):
        vt_ref[0, 0, c * LANES:(c + 1) * LANES, :] = zc(Z_DV, c).T.astype(BF16)

    cq = z_cols(Z_CQ, 512)
    ms = jnp.sum(cq * cq, axis=-1, keepdims=True) * (1.0 / MLA_Q_LORA)
    p_ref[:, P_CQ:P_CQ + 512] = (cq * lax.rsqrt(ms + EPS) * gq_ref[...]).astype(BF16)
    ckv = z_cols(Z_CKV, MLA_KV_LORA)
    ms = jnp.mean(ckv * ckv, axis=-1, keepdims=True)
    p_ref[:, P_CKV:P_CKV + MLA_KV_LORA] = (ckv * lax.rsqrt(ms + EPS) * gkv_ref[...]).astype(BF16)

    ik = zc(Z_IK)
    ms = jnp.mean(ik * ik, axis=-1, keepdims=True)
    ikn = ik * lax.rsqrt(ms + EPS) * gik_ref[...]
    put(P_IK, 0, _rope128(ikn, tab(0), ROPE_KINDS[0][0] // 2))

    sm = zc(Z_SMALL)
    lane = lax.broadcasted_iota(I32, sm.shape, 1)
    iw_scale = IDX_HEADS ** -0.5 * IDX_DIM ** -0.5
    small = jnp.where(lane < SMALL_IW, jax.nn.sigmoid(sm), sm * iw_scale)
    small_ref[...] = small
    smallt_ref[0] = small.T


def _proj_prep(x2, g, w, tab, gq, gkv, gik, *, batch, seq, ts):
    m, d = x2.shape
    spb = seq // ts
    assert ts == VT_TILE and w.shape == (d, Z_WIDTH)
    row = lambda w: pl.BlockSpec((ts, w), lambda i: (i, 0))
    return pl.pallas_call(
        _proj_prep_kernel,
        out_shape=(jax.ShapeDtypeStruct((m, P_WIDTH), BF16),
                   jax.ShapeDtypeStruct((batch, T_ROWS, seq), BF16),
                   jax.ShapeDtypeStruct((batch, spb, BR_WIDTH, VT_TILE), BF16),
                   jax.ShapeDtypeStruct((m, LANES), BF16),
                   jax.ShapeDtypeStruct((m, LANES), BF16),
                   jax.ShapeDtypeStruct((m, LANES), F32),
                   jax.ShapeDtypeStruct((batch, LANES, seq), F32)),
        grid=(m // ts,),
        in_specs=[row(d),
                  pl.BlockSpec((1, d), lambda i: (0, 0)),
                  pl.BlockSpec((d, Z_WIDTH), lambda i: (0, 0)),
                  pl.BlockSpec((ts, 3 * TAB_W), lambda i: (i % spb, 0)),
                  pl.BlockSpec((1, 512), lambda i: (0, 0)),
                  pl.BlockSpec((1, MLA_KV_LORA), lambda i: (0, 0)),
                  pl.BlockSpec((1, LANES), lambda i: (0, 0))],
        out_specs=(row(P_WIDTH),
                   pl.BlockSpec((1, T_ROWS, ts), lambda i: (i // spb, 0, i % spb)),
                   pl.BlockSpec((1, 1, BR_WIDTH, VT_TILE), lambda i: (i // spb, i % spb, 0, 0)),
                   row(LANES), row(LANES), row(LANES),
                   pl.BlockSpec((1, LANES, ts), lambda i: (i // spb, 0, i % spb))),
        compiler_params=_cparams(1),
        name="proj_prep",
    )(x2, g, w, tab, gq, gkv, gik)


def _softmax_init(mx_scr, l_scr, acc_scr):
    mx_scr[...] = jnp.full(mx_scr.shape, NEG, F32)
    l_scr[...] = jnp.zeros(l_scr.shape, F32)
    acc_scr[...] = jnp.zeros(acc_scr.shape, F32)


def _score_store(g, j, s, s_scr, mx_scr):
    s_scr[g, j] = s
    m = s[:, 0:LANES]
    for c in range(1, s.shape[1] // LANES):
        m = jnp.maximum(m, s[:, c * LANES:(c + 1) * LANES])
    mx_scr[g] = jnp.maximum(mx_scr[g], m)


def _row_max_finish(mx_scr):
    for g in range(mx_scr.shape[0]):
        m = jnp.max(mx_scr[g], axis=-1, keepdims=True)
        mx_scr[g] = jnp.broadcast_to(m, mx_scr.shape[1:])


def _prob_accumulate(g, j, v_tile, s_scr, mx_scr, l_scr, acc_scr):
    mb = mx_scr[g]
    s = s_scr[g, j]
    ps = [jnp.exp2(s[:, c * LANES:(c + 1) * LANES] - mb) for c in range(s.shape[1] // LANES)]
    tot = ps[0]
    for p in ps[1:]:
        tot = tot + p
    l_scr[g] += tot
    acc_scr[g] += _dot(jnp.concatenate(ps, axis=1).astype(BF16), v_tile)


def _softmax_out(g, l_scr, acc_scr):
    return acc_scr[g] / jnp.sum(l_scr[g], axis=-1, keepdims=True)


def _paired_tiles(n, step):
    def pair(jj, carry):
        step(2 * jj)
        step(2 * jj + 1)
        return carry

    lax.fori_loop(0, n // 2, pair, 0)

    @pl.when(n % 2 == 1)
    def _():
        step(n - 1)


def _causal_tiles(step, n_full):
    _paired_tiles(n_full, lambda j: step(j, False))
    step(n_full, True)


def _softmax_scratch(groups, n_tiles, rows, tk):
    return [pltpu.VMEM((groups, n_tiles, rows, tk), F32),
            pltpu.VMEM((groups, rows, LANES), F32),
            pltpu.VMEM((groups, rows, LANES), F32),
            pltpu.VMEM((groups, rows, HEAD_W), F32)]


def _diff_attn_kernel(q_ref, k_ref, v_ref, lv_ref, g_ref, o_ref, s_scr, mx_scr, l_scr, acc_scr,
                      *, tq, tk, lam_init):
    qs = pl.program_id(1) * tq
    n_full = qs // tk
    scale = DA_DIM ** -0.5 * LOG2E
    lv = lv_ref[...]
    lam = (jnp.exp(jnp.sum(lv[0:1] * lv[1:2], axis=-1, keepdims=True))
           - jnp.exp(jnp.sum(lv[2:3] * lv[3:4], axis=-1, keepdims=True)) + lam_init)
    lane = lax.broadcasted_iota(I32, (tq, HEAD_W), 1)
    row_t = qs + lax.broadcasted_iota(I32, (2 * tq, 1), 0) % tq
    col0 = lax.broadcasted_iota(I32, (2 * tq, tk), 1)
    _softmax_init(mx_scr, l_scr, acc_scr)

    def scores(j, masked):
        ks = pl.multiple_of(j * tk, tk)
        for h in range(HEADS):
            hs = slice(h * HEAD_W, (h + 1) * HEAD_W)
            qh = q_ref[0, :, hs]
            zero = jnp.zeros_like(qh)
            q2 = jnp.concatenate([jnp.where(lane < DA_DIM, qh, zero),
                                  jnp.where(lane >= DA_DIM, qh, zero)], axis=0)
            s = _dot_nt(q2, k_ref[0, pl.ds(ks, tk), hs]) * scale
            if masked:
                s = jnp.where(col0 + ks <= row_t, s, NEG)
            _score_store(h, j, s, s_scr, mx_scr)

    _causal_tiles(scores, n_full)
    _row_max_finish(mx_scr)

    def probs(j, carry):
        ks = pl.multiple_of(j * tk, tk)
        for h in range(HEADS):
            v_tile = v_ref[0, pl.ds(ks, tk), h * HEAD_W:(h + 1) * HEAD_W]
            _prob_accumulate(h, j, v_tile, s_scr, mx_scr, l_scr, acc_scr)
        return carry

    _paired_tiles(n_full + 1, lambda j: probs(j, 0))
    for h in range(HEADS):
        o2 = _softmax_out(h, l_scr, acc_scr)
        o = o2[:tq] - lam * o2[tq:]
        ms = jnp.mean(o * o, axis=-1, keepdims=True)
        o = o * lax.rsqrt(ms + EPS) * g_ref[...]
        o_ref[0, :, h * HEAD_W:(h + 1) * HEAD_W] = (o * (1.0 - lam_init)).astype(BF16)


def _diff_attn(p3, lv, g, *, lam_init, tq, tk):
    b, s, _ = p3.shape
    return pl.pallas_call(
        functools.partial(_diff_attn_kernel, tq=tq, tk=tk, lam_init=lam_init),
        out_shape=jax.ShapeDtypeStruct((b, s, BR_WIDTH), BF16),
        grid=(b, s // tq),
        in_specs=[pl.BlockSpec((1, tq, 512), lambda bi, i: (bi, i, P_AQ // 512)),
                  pl.BlockSpec((1, s, 512), lambda bi, i: (bi, 0, P_AK // 512)),
                  pl.BlockSpec((1, s, 512), lambda bi, i: (bi, 0, P_AV // 512)),
                  pl.BlockSpec((4, DA_DIM), lambda bi, i: (0, 0)),
                  pl.BlockSpec((1, HEAD_W), lambda bi, i: (0, 0))],
        out_specs=pl.BlockSpec((1, tq, BR_WIDTH), lambda bi, i: (bi, i, 0)),
        scratch_shapes=_softmax_scratch(HEADS, s // tk, 2 * tq, tk),
        compiler_params=_cparams(2),
        name="diff_attn",
    )(p3, p3, p3, lv, g)


def _nsa_compress_kernel(gk_ref, gv_ref, w1k_ref, w1v_ref, pek_ref, pev_ref,
                         w1kf_ref, w1vf_ref, w2k_ref, w2v_ref, kc_ref, vc_ref):
    def one(g_ref, w1cat_ref, pe_ref, w1f_ref, w2_ref, o_ref):
        y = _dot(g_ref[0], w1cat_ref[...])
        n = y.shape[0]
        nxt = pltpu.roll(y[:, HEAD_W:], n - 1, 0)
        c = _dot(pe_ref[...], w1f_ref[...])[0:1]
        hid = jax.nn.gelu(y[:, :HEAD_W] + nxt + c)
        o_ref[0] = _dot(hid.astype(BF16), w2_ref[...]).astype(BF16)

    one(gk_ref, w1k_ref, pek_ref, w1kf_ref, w2k_ref, kc_ref)
    one(gv_ref, w1v_ref, pev_ref, w1vf_ref, w2v_ref, vc_ref)


def _nsa_compress(gk, gv, w1k_cat, w1v_cat, pek, pev, w1k, w1v, w2k, w2v):
    b, ng, gw = gk.shape
    full = lambda shape: pl.BlockSpec(shape, lambda bi: (0,) * len(shape))
    return pl.pallas_call(
        _nsa_compress_kernel,
        out_shape=(jax.ShapeDtypeStruct((b, ng, HEAD_W), BF16),
                   jax.ShapeDtypeStruct((b, ng, HEAD_W), BF16)),
        grid=(b,),
        in_specs=[pl.BlockSpec((1, ng, gw), lambda bi: (bi, 0, 0)),
                  pl.BlockSpec((1, ng, gw), lambda bi: (bi, 0, 0)),
                  full(w1k_cat.shape), full(w1v_cat.shape), full(pek.shape), full(pev.shape),
                  full(w1k.shape), full(w1v.shape), full(w2k.shape), full(w2v.shape)],
        out_specs=(pl.BlockSpec((1, ng, HEAD_W), lambda bi: (bi, 0, 0)),
                   pl.BlockSpec((1, ng, HEAD_W), lambda bi: (bi, 0, 0))),
        compiler_params=_cparams(1),
        name="nsa_compress",
    )(gk, gv, w1k_cat, w1v_cat, pek, pev, w1k, w1v, w2k, w2v)


NSA_GROUPS = 2


def _nsa_kernel(q_ref, kc_ref, vc_ref, ks_ref, vs_ref, kw_ref, vw_ref, small_ref, ov_ref, e_ref,
                o_ref, s_scr, mx_scr, l_scr, acc_scr, cmp_scr, win_scr, *, tq, tk, seq):
    qs = pl.program_id(1) * tq
    scale = NSA_DK ** -0.5
    ns = seq // SEL_LEN
    n_sel = min(SEL_N, ns)
    r = HEADS * tq
    rg = r // NSA_GROUPS
    q4 = jnp.concatenate([q_ref[0, :, h * HEAD_W:(h + 1) * HEAD_W] for h in range(HEADS)], axis=0)
    t1 = qs + lax.broadcasted_iota(I32, (tq, 1), 0)
    t4 = qs + lax.broadcasted_iota(I32, (r, 1), 0) % tq

    wspan = WIN + tq
    start = pl.multiple_of(jnp.maximum(qs - WIN, 0), tq)
    sw = _dot_nt(q4, kw_ref[0, pl.ds(start, wspan), :]) * scale
    dist = t4 - (start + lax.broadcasted_iota(I32, (r, wspan), 1))
    sw = jnp.where(pltpu.bitcast(dist, jnp.uint32) < jnp.uint32(WIN), sw, NEG)
    e = jnp.exp(sw - jnp.max(sw, axis=-1, keepdims=True))
    pw = e / jnp.sum(e, axis=-1, keepdims=True)
    win_scr[...] = _dot(pw.astype(BF16), vw_ref[0, pl.ds(start, wspan), :])

    kc = kc_ref[0]
    nc_pad = kc.shape[0]
    sc = _dot_nt(q4, kc) * scale
    c_end = lax.broadcasted_iota(I32, (r, nc_pad), 1) * CMP_STRIDE + (CMP_LEN - 1)
    cmask = c_end <= t4
    mx = jnp.max(jnp.where(cmask, sc, NEG), axis=-1, keepdims=True)
    e = jnp.where(cmask, jnp.exp(sc - mx), 0.0)
    den = jnp.sum(e, axis=-1, keepdims=True)
    pc = e / jnp.where(den > 0.0, den, 1.0)
    cmp_scr[...] = _dot(pc.astype(BF16), vc_ref[0])

    psum = pc[0:tq] + pc[tq:2 * tq] + pc[2 * tq:3 * tq] + pc[3 * tq:4 * tq]
    ov = ov_ref[...]
    hi = psum.astype(BF16)
    r1 = psum - hi.astype(F32)
    mid = r1.astype(BF16)
    lo = (r1 - mid.astype(F32)).astype(BF16)
    imp = _dot(hi, ov) + _dot(mid, ov) + _dot(lo, ov)

    blk = lax.broadcasted_iota(I32, (tq, LANES), 1)
    cur = t1 // SEL_LEN
    forced = (blk == 0) | (blk == cur) | (blk == cur - 1)
    visible = blk * SEL_LEN <= t1
    score = jnp.where(visible, jnp.where(forced, FORCE_SCORE, imp), NEG)
    score = jnp.where(blk < ns, score, PAD_SCORE)
    ns_pad = -(-ns // SUBLANES) * SUBLANES
    score_t = score.T[:ns_pad]
    blk_t = lax.broadcasted_iota(I32, (ns_pad, tq), 0)
    rank = jnp.zeros((ns_pad, tq), I32)
    for jp in range(ns):
        row = score_t[jp:jp + 1, :]
        later = (blk_t > jp).astype(I32)
        rank = rank + jnp.where(row > score_t, 1, jnp.where(row == score_t, later, 0))
    sel_t = jnp.where(rank < n_sel, 1.0, 0.0)
    if ns_pad < LANES:
        sel_t = jnp.concatenate([sel_t, jnp.zeros((LANES - ns_pad, tq), F32)], axis=0)
    selb = sel_t.T.astype(BF16)

    _softmax_init(mx_scr, l_scr, acc_scr)
    col0 = lax.broadcasted_iota(I32, (rg, tk), 1)
    tg = qs + lax.broadcasted_iota(I32, (rg, 1), 0) % tq
    n_tiles = qs // tk + 1

    def scores(j, masked):
        ks0 = pl.multiple_of(j * tk, tk)
        mt = _dot(selb, e_ref[j])
        mg = jnp.concatenate([mt] * (rg // tq), axis=0)
        k_tile = ks_ref[0, pl.ds(ks0, tk), :]
        for g in range(NSA_GROUPS):
            s = _dot_nt(q4[g * rg:(g + 1) * rg], k_tile) * (scale * LOG2E)
            s = jnp.where(mg > 0.5, s, NEG)
            if masked:
                s = jnp.where(col0 + ks0 <= tg, s, NEG)
            _score_store(g, j, s, s_scr, mx_scr)

    _causal_tiles(scores, n_tiles - 1)
    _row_max_finish(mx_scr)

    def probs(j, carry):
        ks0 = pl.multiple_of(j * tk, tk)
        v_tile = vs_ref[0, pl.ds(ks0, tk), :]
        for g in range(NSA_GROUPS):
            _prob_accumulate(g, j, v_tile, s_scr, mx_scr, l_scr, acc_scr)
        return carry

    _paired_tiles(n_tiles, lambda j: probs(j, 0))
    o_slc = jnp.concatenate([_softmax_out(g, l_scr, acc_scr) for g in range(NSA_GROUPS)], axis=0)

    gates = small_ref[0]
    for h in range(HEADS):
        rows = slice(h * tq, (h + 1) * tq)
        g0 = gates[:, SMALL_G + 3 * h:SMALL_G + 3 * h + 1]
        g1 = gates[:, SMALL_G + 3 * h + 1:SMALL_G + 3 * h + 2]
        g2 = gates[:, SMALL_G + 3 * h + 2:SMALL_G + 3 * h + 3]
        o = g0 * cmp_scr[rows, :] + g1 * o_slc[rows] + g2 * win_scr[rows, :]
        o_ref[0, :, h * HEAD_W:(h + 1) * HEAD_W] = o.astype(BF16)


def _nsa(p3, kc, vc, small3, ov, emat, *, tq, tk):
    b, s, _ = p3.shape
    ng = kc.shape[1]
    col = lambda off: (lambda bi, i: (bi, 0, off // LANES))
    return pl.pallas_call(
        functools.partial(_nsa_kernel, tq=tq, tk=tk, seq=s),
        out_shape=jax.ShapeDtypeStruct((b, s, BR_WIDTH), BF16),
        grid=(b, s // tq),
        in_specs=[pl.BlockSpec((1, tq, 512), lambda bi, i: (bi, i, P_BQ // 512)),
                  pl.BlockSpec((1, ng, HEAD_W), lambda bi, i: (bi, 0, 0)),
                  pl.BlockSpec((1, ng, HEAD_W), lambda bi, i: (bi, 0, 0)),
                  pl.BlockSpec((1, s, LANES), col(P_KS)),
                  pl.BlockSpec((1, s, LANES), col(P_VS)),
                  pl.BlockSpec((1, s, LANES), col(P_KW)),
                  pl.BlockSpec((1, s, LANES), col(P_VW)),
                  pl.BlockSpec((1, tq, LANES), lambda bi, i: (bi, i, 0)),
                  pl.BlockSpec(ov.shape, lambda bi, i: (0, 0)),
                  pl.BlockSpec(emat.shape, lambda bi, i: (0, 0, 0))],
        out_specs=pl.BlockSpec((1, tq, BR_WIDTH), lambda bi, i: (bi, i, 0)),
        scratch_shapes=(_softmax_scratch(NSA_GROUPS, s // tk, HEADS * tq // NSA_GROUPS, tk)
                        + [pltpu.VMEM((HEADS * tq, HEAD_W), F32), pltpu.VMEM((HEADS * tq, HEAD_W), F32)]),
        compiler_params=_cparams(2),
        name="nsa_attn",
    )(p3, kc, vc, p3, p3, p3, p3, small3, ov, emat)


def _mla_up_kernel(p_ref, ckv_ref, tab_ref, wq_ref, wkv_ref, q_ref, kv_ref):
    q = _dot(p_ref[...], wq_ref[...])
    nn = HEADS * MLA_NOPE
    q_ref[:, :nn] = q[:, :nn].astype(BF16)
    for c in range(nn // LANES, (nn + HEADS * MLA_ROPE) // LANES):
        tile = _rope128(q[:, c * LANES:(c + 1) * LANES], tab_ref[...], MLA_ROPE // 2)
        q_ref[:, c * LANES:(c + 1) * LANES] = tile.astype(BF16)
    kv_ref[...] = _dot(ckv_ref[...], wkv_ref[...]).astype(BF16)


def _mla_up(p2, tab, wq, wkv, *, seq, ts):
    m = p2.shape[0]
    spb = seq // ts
    nq = wq.shape[1]
    nkv = wkv.shape[1]
    return pl.pallas_call(
        _mla_up_kernel,
        out_shape=(jax.ShapeDtypeStruct((m, nq), BF16), jax.ShapeDtypeStruct((m, nkv), BF16)),
        grid=(m // ts,),
        in_specs=[pl.BlockSpec((ts, 512), lambda i: (i, P_CQ // 512)),
                  pl.BlockSpec((ts, MLA_KV_LORA), lambda i: (i, P_CKV // MLA_KV_LORA)),
                  pl.BlockSpec((ts, TAB_W), lambda i: (i % spb, 2)),
                  pl.BlockSpec(wq.shape, lambda i: (0, 0)),
                  pl.BlockSpec(wkv.shape, lambda i: (0, 0))],
        out_specs=(pl.BlockSpec((ts, nq), lambda i: (i, 0)),
                   pl.BlockSpec((ts, nkv), lambda i: (i, 0))),
        compiler_params=_cparams(1),
        name="mla_up",
    )(p2, p2, tab, wq, wkv)


def _mla_attn_kernel(qn_ref, qr_ref, kn_ref, kr_ref, v_ref, o_ref, s_scr, mx_scr, l_scr, acc_scr,
                     *, tq, tk):
    qs = pl.program_id(1) * tq
    n_full = qs // tk
    scale = (MLA_NOPE + MLA_ROPE) ** -0.5 * LOG2E
    lane = lax.broadcasted_iota(I32, (tq, LANES), 1)
    row_t = qs + lax.broadcasted_iota(I32, (tq, 1), 0)
    col0 = lax.broadcasted_iota(I32, (tq, tk), 1)
    _softmax_init(mx_scr, l_scr, acc_scr)

    def scores(j, masked):
        ks = pl.multiple_of(j * tk, tk)
        kr_tile = kr_ref[0, pl.ds(ks, tk), :]
        for h in range(HEADS):
            hs = slice(h * HEAD_W, (h + 1) * HEAD_W)
            pair = qr_ref[0, :, (h // 2) * LANES:(h // 2 + 1) * LANES]
            keep = (lane < MLA_ROPE) if h % 2 == 0 else (lane >= MLA_ROPE)
            qr = jnp.where(keep, pair, jnp.zeros_like(pair))
            s = _dot_nt(jnp.concatenate([qn_ref[0, :, hs], qr], axis=1),
                        jnp.concatenate([kn_ref[0, pl.ds(ks, tk), hs], kr_tile], axis=1)) * scale
            if masked:
                s = jnp.where(col0 + ks <= row_t, s, NEG)
            _score_store(h, j, s, s_scr, mx_scr)

    _causal_tiles(scores, n_full)
    _row_max_finish(mx_scr)

    def probs(j, carry):
        ks = pl.multiple_of(j * tk, tk)
        for h in range(HEADS):
            v_tile = v_ref[0, pl.ds(ks, tk), h * HEAD_W:(h + 1) * HEAD_W]
            _prob_accumulate(h, j, v_tile, s_scr, mx_scr, l_scr, acc_scr)
        return carry

    _paired_tiles(n_full + 1, lambda j: probs(j, 0))
    for h in range(HEADS):
        o_ref[0, :, h * HEAD_W:(h + 1) * HEAD_W] = _softmax_out(h, l_scr, acc_scr).astype(BF16)


def _mla_attn(q3, kv3, p3, *, tq, tk):
    b, s, _ = q3.shape
    return pl.pallas_call(
        functools.partial(_mla_attn_kernel, tq=tq, tk=tk),
        out_shape=jax.ShapeDtypeStruct((b, s, BR_WIDTH), BF16),
        grid=(b, s // tq),
        in_specs=[pl.BlockSpec((1, tq, 512), lambda bi, i: (bi, i, 0)),
                  pl.BlockSpec((1, tq, 256), lambda bi, i: (bi, i, 2)),
                  pl.BlockSpec((1, s, 512), lambda bi, i: (bi, 0, 0)),
                  pl.BlockSpec((1, s, LANES), lambda bi, i: (bi, 0, P_KR // LANES)),
                  pl.BlockSpec((1, s, 512), lambda bi, i: (bi, 0, 1))],
        out_specs=pl.BlockSpec((1, tq, BR_WIDTH), lambda bi, i: (bi, i, 0)),
        scratch_shapes=_softmax_scratch(HEADS, s // tk, tq, tk),
        compiler_params=_cparams(2),
        name="mla_attn",
    )(q3, q3, kv3, p3, kv3)


def _sortable_key(x):
    bits = pltpu.bitcast(x + 0.0, I32)
    return bits ^ (lax.shift_right_arithmetic(bits, 31) & 0x7FFFFFFF)


def _fold_rows(x, op):
    n = x.shape[0] // SUBLANES
    return op(x.reshape(n, SUBLANES, x.shape[1]), axis=0)


def _count16(half_scr, n_tiles, pred, tq):
    def count_tile(j, cnt):
        hit = pred(half_scr[j]).astype(I16)
        parts = [hit[r:r + PACKED_ROWS] for r in range(0, hit.shape[0], PACKED_ROWS)]
        while len(parts) > 1:
            parts = [a + b for a, b in zip(parts[0::2], parts[1::2])]
        return cnt + parts[0]

    cnt = lax.fori_loop(0, n_tiles, count_tile, jnp.zeros((PACKED_ROWS, tq), I16))
    return jnp.sum(cnt.astype(I32), axis=0, keepdims=True)


def _bisect16(half_scr, n_tiles, need, tq):
    def bit_body(i, th):
        cand = th + lax.shift_left(jnp.int32(1), 15 - i)
        c16 = cand.astype(I16)
        total = _count16(half_scr, n_tiles, lambda x: x >= c16, tq)
        return jnp.where(total >= need, cand, th)

    return lax.fori_loop(0, 16, bit_body, jnp.full((1, tq), HALF_MIN, I32))


def _dsa_kernel(qt_ref, iqt_ref, iwt_ref, k_ref, ik_ref, vt_ref, o_ref,
                key_scr, half_scr, s_scr, mx_scr, l_scr, acc_scr, *, tq, tk, top):
    qs = pl.program_id(1) * tq
    n_tiles = (qs + tq - 1) // tk + 1
    scale = DSA_DIM ** -0.5 * LOG2E
    t_lane = qs + lax.broadcasted_iota(I32, (tk, tq), 1)
    krow0 = lax.broadcasted_iota(I32, (tk, tq), 0)
    half_rows = lax.broadcasted_iota(I32, (LANES, tq), 0) < IDX_DIM
    vt_per_tile = tk // VT_TILE

    def score_tile(j, carry):
        ks = pl.multiple_of(j * tk, tk)
        ikt = ik_ref[0, pl.ds(ks, tk), :]
        acc = jnp.zeros((tk, tq), F32)
        for h in range(IDX_HEADS):
            pair = iqt_ref[0, (h // 2) * LANES:(h // 2 + 1) * LANES, :]
            keep = half_rows if h % 2 == 0 else jnp.logical_not(half_rows)
            iq_h = jnp.where(keep, pair, jnp.zeros_like(pair))
            w_h = iwt_ref[0, SMALL_IW + h:SMALL_IW + h + 1, :]
            acc = acc + w_h * jnp.maximum(_dot(ikt, iq_h), 0.0)
        key = jnp.where(krow0 + ks <= t_lane, _sortable_key(acc), INT_MIN)
        key_scr[j] = key
        half_scr[j] = lax.shift_right_arithmetic(key, 16).astype(I16)
        return carry

    lax.fori_loop(0, n_tiles, score_tile, 0)

    th_hi = _bisect16(half_scr, n_tiles, top, tq)
    hi16 = th_hi.astype(I16)
    need_lo = top - _count16(half_scr, n_tiles, lambda x: x > hi16, tq)

    def low_tile(j, carry):
        key = key_scr[j]
        lo = (key & 0xFFFF) + HALF_MIN
        same_hi = lax.shift_right_arithmetic(key, 16) == th_hi
        half_scr[j] = jnp.where(same_hi, lo, HALF_MIN).astype(I16)
        return carry

    lax.fori_loop(0, n_tiles, low_tile, 0)
    th_lo = _bisect16(half_scr, n_tiles, need_lo, tq)
    theta = lax.shift_left(th_hi, 16) + (th_lo - HALF_MIN)
    theta = jnp.maximum(theta, INT_MIN + 1)

    mx_scr[...] = jnp.full(mx_scr.shape, NEG, F32)
    l_scr[...] = jnp.zeros(l_scr.shape, F32)
    acc_scr[...] = jnp.zeros(acc_scr.shape, F32)

    def scores(j, carry):
        ks = pl.multiple_of(j * tk, tk)
        sel = key_scr[j] >= theta
        for h in range(HEADS):
            hs = slice(h * HEAD_W, (h + 1) * HEAD_W)
            s = _dot(k_ref[0, pl.ds(ks, tk), hs], qt_ref[0, hs, :]) * scale
            s = jnp.where(sel, s, NEG)
            s_scr[h, j] = s
            mx_scr[h] = jnp.maximum(mx_scr[h], _fold_rows(s, jnp.max))
        return carry

    _paired_tiles(n_tiles, lambda j: scores(j, 0))
    for h in range(HEADS):
        m = jnp.max(mx_scr[h], axis=0, keepdims=True)
        mx_scr[h] = jnp.broadcast_to(m, (SUBLANES, tq))

    def probs(j, carry):
        for h in range(HEADS):
            p = jnp.exp2(s_scr[h, j] - mx_scr[h][0:1])
            l_scr[h] += _fold_rows(p, jnp.sum)
            pb = p.astype(BF16)
            for c in range(vt_per_tile):
                vt = vt_ref[0, j * vt_per_tile + c, h * HEAD_W:(h + 1) * HEAD_W, :]
                acc_scr[h] += _dot(vt, pb[c * VT_TILE:(c + 1) * VT_TILE])
        return carry

    _paired_tiles(n_tiles, lambda j: probs(j, 0))
    for h in range(HEADS):
        ot = acc_scr[h] / jnp.sum(l_scr[h], axis=0, keepdims=True)
        o_ref[0, :, h * HEAD_W:(h + 1) * HEAD_W] = ot.T.astype(BF16)


def _dsa(p3, t3, vt4, smallt, *, tq, tk, top):
    b, s, _ = p3.shape
    n_vt = vt4.shape[1]
    return pl.pallas_call(
        functools.partial(_dsa_kernel, tq=tq, tk=tk, top=top),
        out_shape=jax.ShapeDtypeStruct((b, s, BR_WIDTH), BF16),
        grid=(b, s // tq),
        in_specs=[pl.BlockSpec((1, 512, tq), lambda bi, i: (bi, T_DQ // 512, i)),
                  pl.BlockSpec((1, 512, tq), lambda bi, i: (bi, T_IQ // 512, i)),
                  pl.BlockSpec((1, LANES, tq), lambda bi, i: (bi, 0, i)),
                  pl.BlockSpec((1, s, 512), lambda bi, i: (bi, 0, P_DK // 512)),
                  pl.BlockSpec((1, s, LANES), lambda bi, i: (bi, 0, P_IK // LANES)),
                  pl.BlockSpec((1, n_vt, BR_WIDTH, VT_TILE), lambda bi, i: (bi, 0, 0, 0))],
        out_specs=pl.BlockSpec((1, tq, BR_WIDTH), lambda bi, i: (bi, i, 0)),
        scratch_shapes=[pltpu.VMEM((s // tk, tk, tq), I32),
                        pltpu.VMEM((s // tk, tk, tq), I16),
                        pltpu.VMEM((HEADS, s // tk, tk, tq), F32),
                        pltpu.VMEM((HEADS, SUBLANES, tq), F32),
                        pltpu.VMEM((HEADS, SUBLANES, tq), F32),
                        pltpu.VMEM((HEADS, HEAD_W, tq), F32)],
        compiler_params=_cparams(2),
        name="dsa_attn",
    )(t3, t3, smallt, p3, p3, vt4)


def _merge_kernel(x_ref, oa_ref, ob_ref, oc_ref, od_ref, g_ref, wb_ref, wo_ref, o_ref):
    d = x_ref.shape[1]
    acc = jnp.zeros(x_ref.shape, F32)
    for n, br_ref in enumerate((oa_ref, ob_ref, oc_ref, od_ref)):
        br = _dot(br_ref[...], wb_ref[n])
        acc = acc + g_ref[:, n * d:(n + 1) * d].astype(F32) * br
    o_ref[...] = x_ref[...] + _dot(acc.astype(BF16), wo_ref[...])


def _merge(x2, oa, ob, oc, od, gates, wb, wo, *, tm):
    m, d = x2.shape
    row = lambda w: pl.BlockSpec((tm, w), lambda i: (i, 0))
    return pl.pallas_call(
        _merge_kernel,
        out_shape=jax.ShapeDtypeStruct((m, d), F32),
        grid=(m // tm,),
        in_specs=[row(d), row(BR_WIDTH), row(BR_WIDTH), row(BR_WIDTH), row(BR_WIDTH),
                  row(gates.shape[1]),
                  pl.BlockSpec(wb.shape, lambda i: (0, 0, 0)),
                  pl.BlockSpec(wo.shape, lambda i: (0, 0))],
        out_specs=row(d),
        compiler_params=_cparams(1),
        name="merge",
    )(x2, oa, ob, oc, od, gates, wb, wo)


def _ffn_kernel(x_ref, g_ref, wg_ref, wu_ref, wd_ref, gf_ref, o_ref, h_scr, acc_scr, *, final):
    j = pl.program_id(1)

    @pl.when(j == 0)
    def _():
        x = x_ref[...]
        ms = jnp.mean(x * x, axis=-1, keepdims=True)
        h_scr[...] = (x * lax.rsqrt(ms + EPS) * g_ref[...]).astype(BF16)
        acc_scr[...] = jnp.zeros(acc_scr.shape, F32)

    h = h_scr[...]
    a = jax.nn.silu(_dot(h, wg_ref[...])) * _dot(h, wu_ref[...])
    acc_scr[...] += _dot(a.astype(BF16), wd_ref[...])

    @pl.when(j == pl.num_programs(1) - 1)
    def _():
        y = x_ref[...] + acc_scr[...]
        if final:
            ms = jnp.mean(y * y, axis=-1, keepdims=True)
            y = y * lax.rsqrt(ms + EPS) * gf_ref[...]
        o_ref[...] = y


def _ffn(x2, g, wg, wu, wd, gf, *, final, tm, tf):
    m, d = x2.shape
    dff = wg.shape[1]
    return pl.pallas_call(
        functools.partial(_ffn_kernel, final=final),
        out_shape=jax.ShapeDtypeStruct((m, d), F32),
        grid=(m // tm, dff // tf),
        in_specs=[pl.BlockSpec((tm, d), lambda i, j: (i, 0)),
                  pl.BlockSpec((1, d), lambda i, j: (0, 0)),
                  pl.BlockSpec((d, tf), lambda i, j: (0, j)),
                  pl.BlockSpec((d, tf), lambda i, j: (0, j)),
                  pl.BlockSpec((tf, d), lambda i, j: (j, 0)),
                  pl.BlockSpec((1, d), lambda i, j: (0, 0))],
        out_specs=pl.BlockSpec((tm, d), lambda i, j: (i, 0)),
        scratch_shapes=[pltpu.VMEM((tm, d), BF16), pltpu.VMEM((tm, d), F32)],
        compiler_params=_cparams(2),
        name="ffn",
    )(x2, g, wg, wu, wd, gf)


def _tiles(seq, m, dff):
    pick = lambda n, cands: next(c for c in cands if n % c == 0)
    tk = pick(seq, (512, 256))
    return dict(
        proj_tm=pick(m, (1024, 512, 256, 128)), proj_tn=1024,
        prep_ts=VT_TILE,
        diff_tq=128, mla_tq=pick(seq, (256, 128)), nsa_tq=128, dsa_tq=256, tk=tk,
        row_tm=pick(m, (512, 256, 128)),
        ffn_tf=pick(dff, (1408, 704, 256, 128)),
    )


def kernel(x, norm1_g, w_in, diff_lq1, diff_lk1, diff_lq2, diff_lk2, diff_subln_g, nsa_pe_k, nsa_w1_k, nsa_w2_k, nsa_pe_v, nsa_w1_v, nsa_w2_v, mla_q_norm_g, mla_w_uq, mla_kv_norm_g, mla_w_ukv, idx_k_norm_g, w_branch, w_out, norm2_g, w_gate_up, w_down, final_norm_g):
    b, seq, d = x.shape
    depth = w_in.shape[0]
    m = b * seq
    dff = w_down.shape[1]
    t = _tiles(seq, m, dff)
    tk = t["tk"]
    assert seq % SEL_LEN == 0 and seq >= WIN + t["nsa_tq"] and seq // SEL_LEN <= LANES
    assert seq % t["dsa_tq"] == 0 and tk % VT_TILE == 0 and tk >= min(IDX_TOPK, seq // 4)

    col_idx, gate_off, d_in = _in_proj_columns()
    assert w_in.shape[2] == d_in
    tab = jnp.concatenate([_rope_table(seq, rot, per) for rot, per in ROPE_KINDS], axis=1)

    ng = seq // CMP_STRIDE
    ns = seq // SEL_LEN
    c_start = np.arange(ng)[:, None] * CMP_STRIDE
    s_start = np.arange(LANES)[None, :] * SEL_LEN
    ov = ((c_start < s_start + SEL_LEN) & (c_start + CMP_LEN - 1 >= s_start)
          & (np.arange(LANES)[None, :] < ns))
    ov = jnp.asarray(ov, BF16)
    emat = np.arange(LANES)[:, None] == (np.arange(seq)[None, :] // SEL_LEN)
    emat = jnp.asarray(emat.reshape(LANES, seq // tk, tk).transpose(1, 0, 2), BF16)

    qd = MLA_NOPE + MLA_ROPE
    uq_idx = np.concatenate([np.concatenate([np.arange(h * qd, h * qd + MLA_NOPE) for h in range(HEADS)]),
                             np.concatenate([np.arange(h * qd + MLA_NOPE, (h + 1) * qd) for h in range(HEADS)])])
    kvd = MLA_NOPE + HEAD_W
    ukv_idx = np.concatenate([np.concatenate([np.arange(h * kvd, h * kvd + MLA_NOPE) for h in range(HEADS)]),
                              np.concatenate([np.arange(h * kvd + MLA_NOPE, (h + 1) * kvd) for h in range(HEADS)])])

    x2 = x.reshape(m, d)
    half_w1 = CMP_STRIDE * NSA_DK
    for l in range(depth):
        lam_init = 0.8 - 0.6 * math.exp(-0.3 * l)
        w_in_l = w_in[l].astype(BF16)
        w_mix = _take_cols(w_in_l, col_idx)
        w_gate = w_in_l[:, gate_off:]
        gates = _norm_matmul(x2, norm1_g[l][None], w_gate, out_dtype=BF16, sigmoid=True,
                             tm=t["proj_tm"], tn=t["proj_tn"], name="gate_proj")

        gq = jnp.pad(mla_q_norm_g[l], (0, 512 - MLA_Q_LORA))[None]
        gkv = mla_kv_norm_g[l][None]
        gik = jnp.concatenate([idx_k_norm_g[l], idx_k_norm_g[l]])[None]
        p2, t3, vt4, kc_tok, vc_tok, small, smallt = _proj_prep(
            x2, norm1_g[l][None], w_mix, tab, gq, gkv, gik, batch=b, seq=seq, ts=t["prep_ts"])
        p3 = p2.reshape(b, seq, P_WIDTH)
        small3 = small.reshape(b, seq, LANES)

        lv = jnp.stack([diff_lq1[l], diff_lk1[l], diff_lq2[l], diff_lk2[l]])
        o_a = _diff_attn(p3, lv, diff_subln_g[l][None], lam_init=lam_init, tq=t["diff_tq"], tk=tk)

        w1k, w1v = nsa_w1_k[l].astype(BF16), nsa_w1_v[l].astype(BF16)
        w1k_cat = jnp.concatenate([w1k[:half_w1], w1k[half_w1:]], axis=1)
        w1v_cat = jnp.concatenate([w1v[:half_w1], w1v[half_w1:]], axis=1)
        pek = jnp.broadcast_to(nsa_pe_k[l].reshape(1, -1), (8, CMP_LEN * NSA_DK)).astype(BF16)
        pev = jnp.broadcast_to(nsa_pe_v[l].reshape(1, -1), (8, CMP_LEN * NSA_DK)).astype(BF16)
        kc, vc = _nsa_compress(kc_tok.reshape(b, ng, half_w1), vc_tok.reshape(b, ng, half_w1),
                               w1k_cat, w1v_cat, pek, pev, w1k, w1v,
                               nsa_w2_k[l].astype(BF16), nsa_w2_v[l].astype(BF16))
        o_b = _nsa(p3, kc, vc, small3, ov, emat, tq=t["nsa_tq"], tk=tk)

        wq = jnp.pad(_take_cols(mla_w_uq[l].astype(BF16), uq_idx), ((0, 512 - MLA_Q_LORA), (0, 0)))
        wkv = _take_cols(mla_w_ukv[l].astype(BF16), ukv_idx)
        q_c, kv_c = _mla_up(p2, tab, wq, wkv, seq=seq, ts=t["prep_ts"])
        o_c = _mla_attn(q_c.reshape(b, seq, -1), kv_c.reshape(b, seq, -1), p3, tq=t["mla_tq"], tk=tk)

        o_d = _dsa(p3, t3, vt4, smallt, tq=t["dsa_tq"], tk=tk, top=min(IDX_TOPK, seq // 4))

        x2 = _merge(x2, o_a.reshape(m, -1), o_b.reshape(m, -1), o_c.reshape(m, -1), o_d.reshape(m, -1),
                    gates, w_branch[l].astype(BF16), w_out[l].astype(BF16), tm=t["row_tm"])
        wgu = w_gate_up[l].astype(BF16)
        x2 = _ffn(x2, norm2_g[l][None], wgu[:, :dff], wgu[:, dff:], w_down[l].astype(BF16),
                  final_norm_g[None], final=(l == depth - 1), tm=t["row_tm"], tf=t["ffn_tf"])
    return x2.reshape(b, seq, d)
```

```python
import functools
import math

import numpy as np
import jax
import jax.numpy as jnp
from jax import lax
from jax.experimental import pallas as pl
from jax.experimental.pallas import tpu as pltpu

F32 = jnp.float32
BF16 = jnp.bfloat16
I32 = jnp.int32
I16 = jnp.int16

LANES = 128
SUBLANES = 8
PACKED_ROWS = 16
HALF_MIN = -32768
VMEM_LIMIT = 56 * 1024 * 1024

ROPE_THETA = 500000.0
NEG = -1e30
LOG2E = math.log2(math.e)
FORCE_SCORE = 1e9
PAD_SCORE = -3e38
EPS = 1e-6
INT_MIN = -2147483648

HEADS = 4
HEAD_W = 128
BR_WIDTH = HEADS * HEAD_W
DA_DIM = 64
NSA_DK = 128
CMP_LEN = 32
CMP_STRIDE = 16
SEL_LEN = 64
SEL_N = 16
WIN = 512
MLA_Q_LORA = 384
MLA_KV_LORA = 256
MLA_NOPE = 128
MLA_ROPE = 64
DSA_DIM = 128
IDX_HEADS = 8
IDX_DIM = 64
IDX_TOPK = 256

Z_AQ, Z_AK, Z_AV, Z_BQ, Z_DQ, Z_DK, Z_DV, Z_IQ = (i * 512 for i in range(8))
Z_CQ = 4096
Z_CKV = 4608
Z_KC, Z_KS, Z_KW, Z_VC, Z_VS, Z_VW, Z_KR, Z_IK, Z_SMALL = (4864 + i * 128 for i in range(9))
Z_WIDTH = 6144
P_AQ, P_AK, P_AV, P_BQ, P_DK, P_CQ = (i * 512 for i in range(6))
P_CKV = 3072
P_KS, P_KW, P_VS, P_VW, P_KR, P_IK = (3328 + i * 128 for i in range(6))
P_WIDTH = 4096
T_DQ, T_IQ = 0, 512
T_ROWS = 1024
VT_TILE = 256
SMALL_G = 0
SMALL_IW = 12

ROPE_KINDS = ((16, 64), (32, 128), (64, 64))
TAB_W = 3 * LANES


def _cparams(n_axes):
    return pltpu.CompilerParams(dimension_semantics=("arbitrary",) * n_axes,
                                vmem_limit_bytes=VMEM_LIMIT)


def _dot(a, b):
    return jnp.dot(a, b, preferred_element_type=F32)


def _dot_nt(a, b):
    return lax.dot_general(a, b, (((1,), (1,)), ((), ())), preferred_element_type=F32)


def _in_proj_columns():
    names = (("a_q", 512), ("a_k", 512), ("a_v", 512), ("b_q", 512),
             ("b_kc", 128), ("b_vc", 128), ("b_ks", 128), ("b_vs", 128),
             ("b_kw", 128), ("b_vw", 128), ("b_g", 12),
             ("c_q", 384), ("c_kv", 256), ("c_kr", 64),
             ("d_q", 512), ("d_k", 512), ("d_v", 512),
             ("d_iq", 512), ("d_ik", 64), ("d_iw", 8), ("gate", 4096))
    off, o = {}, 0
    for nm, n in names:
        off[nm] = (o, n)
        o += n
    idx = np.full((Z_WIDTH,), -1, np.int64)

    def put(dst, nm):
        s, n = off[nm]
        idx[dst:dst + n] = np.arange(s, s + n)

    put(Z_AQ, "a_q"); put(Z_AK, "a_k"); put(Z_AV, "a_v"); put(Z_BQ, "b_q")
    put(Z_DQ, "d_q"); put(Z_DK, "d_k"); put(Z_DV, "d_v"); put(Z_IQ, "d_iq")
    put(Z_CQ, "c_q"); put(Z_CKV, "c_kv")
    put(Z_KC, "b_kc"); put(Z_KS, "b_ks"); put(Z_KW, "b_kw")
    put(Z_VC, "b_vc"); put(Z_VS, "b_vs"); put(Z_VW, "b_vw")
    put(Z_KR, "c_kr"); put(Z_KR + 64, "c_kr")
    put(Z_IK, "d_ik"); put(Z_IK + 64, "d_ik")
    put(Z_SMALL + SMALL_G, "b_g"); put(Z_SMALL + SMALL_IW, "d_iw")
    return idx, off["gate"][0], o


def _take_cols(w, idx):
    runs, i, n = [], 0, len(idx)
    while i < n:
        j = i + 1
        if idx[i] < 0:
            while j < n and idx[j] < 0:
                j += 1
            runs.append(jnp.zeros((w.shape[0], j - i), w.dtype))
        else:
            while j < n and idx[j] == idx[j - 1] + 1:
                j += 1
            runs.append(w[:, int(idx[i]):int(idx[i]) + (j - i)])
        i = j
    return jnp.concatenate(runs, axis=1)


def _rope_table(seq, rot, period):
    half = rot // 2
    inv = jnp.power(jnp.float32(ROPE_THETA), -jnp.arange(0, rot, 2, dtype=F32) / rot)
    ang = jnp.arange(seq, dtype=F32)[:, None] * inv[None, :]
    cos, sin = jnp.cos(ang), jnp.sin(ang)
    lane = np.arange(LANES) % period
    in1 = lane < half
    in2 = (lane >= half) & (lane < 2 * half)
    fidx = np.where(in1, lane, np.where(in2, lane - half, 0))
    cosl, sinl = cos[:, fidx], sin[:, fidx]
    c = jnp.where(jnp.asarray(in1 | in2)[None], cosl, 1.0)
    s1 = jnp.where(jnp.asarray(in1)[None], -sinl, 0.0)
    s2 = jnp.where(jnp.asarray(in2)[None], sinl, 0.0)
    return jnp.concatenate([c, s1, s2], axis=1)


def _rope128(x, tab, half):
    return (x * tab[:, 0:LANES]
            + pltpu.roll(x, LANES - half, 1) * tab[:, LANES:2 * LANES]
            + pltpu.roll(x, half, 1) * tab[:, 2 * LANES:3 * LANES])


def _norm_matmul_kernel(x_ref, g_ref, w_ref, o_ref, h_scr, *, sigmoid):
    @pl.when(pl.program_id(1) == 0)
    def _():
        x = x_ref[...]
        ms = jnp.mean(x * x, axis=-1, keepdims=True)
        h_scr[...] = (x * lax.rsqrt(ms + EPS) * g_ref[...]).astype(BF16)

    z = _dot(h_scr[...], w_ref[...])
    if sigmoid:
        z = jax.nn.sigmoid(z)
    o_ref[...] = z.astype(o_ref.dtype)


def _norm_matmul(x2, g, w, *, out_dtype, sigmoid, tm, tn, name):
    m, d = x2.shape
    n = w.shape[1]
    return pl.pallas_call(
        functools.partial(_norm_matmul_kernel, sigmoid=sigmoid),
        out_shape=jax.ShapeDtypeStruct((m, n), out_dtype),
        grid=(m // tm, n // tn),
        in_specs=[pl.BlockSpec((tm, d), lambda i, j: (i, 0)),
                  pl.BlockSpec((1, d), lambda i, j: (0, 0)),
                  pl.BlockSpec((d, tn), lambda i, j: (0, j))],
        out_specs=pl.BlockSpec((tm, tn), lambda i, j: (i, j)),
        scratch_shapes=[pltpu.VMEM((tm, d), BF16)],
        compiler_params=_cparams(2),
        name=name,
    )(x2, g, w)


PROJ_TILE = 512


def _proj_prep_kernel(x_ref, g_ref, w_ref, tab_ref, gq_ref, gkv_ref, gik_ref,
                      p_ref, t_ref, vt_ref, kc_ref, vc_ref, small_ref, smallt_ref):
    x = x_ref[...]
    ms = jnp.mean(x * x, axis=-1, keepdims=True)
    h = (x * lax.rsqrt(ms + EPS) * g_ref[...]).astype(BF16)
    z_tiles = {}

    def z_cols(off, width):
        t = off // PROJ_TILE
        assert (off + width - 1) // PROJ_TILE == t
        if t not in z_tiles:
            z_tiles[t] = _dot(h, w_ref[:, t * PROJ_TILE:(t + 1) * PROJ_TILE])
        lo = off - t * PROJ_TILE
        return z_tiles[t][:, lo:lo + width]

    def zc(off, c=0):
        return z_cols(off + c * LANES, LANES)

    def tab(kind):
        return tab_ref[:, kind * TAB_W:(kind + 1) * TAB_W]

    def put(off, c, v):
        p_ref[:, off + c * LANES:off + (c + 1) * LANES] = v.astype(BF16)

    def rope(off, c, kind):
        return _rope128(zc(off, c), tab(kind), ROPE_KINDS[kind][0] // 2)

    for zoff, poff, kind in ((Z_AQ, P_AQ, 0), (Z_AK, P_AK, 0), (Z_BQ, P_BQ, 1), (Z_DK, P_DK, 1)):
        for c in range(4):
            put(poff, c, rope(zoff, c, kind))
    for c in range(4):
        put(P_AV, c, zc(Z_AV, c))
    put(P_VS, 0, zc(Z_VS)); put(P_VW, 0, zc(Z_VW))
    put(P_KS, 0, rope(Z_KS, 0, 1)); put(P_KW, 0, rope(Z_KW, 0, 1))
    put(P_KR, 0, rope(Z_KR, 0, 2))
    kc_ref[...] = rope(Z_KC, 0, 1).astype(BF16)
    vc_ref[...] = zc(Z_VC).astype(BF16)

    for zoff, toff, kind in ((Z_DQ, T_DQ, 1), (Z_IQ, T_IQ, 0)):
        for c in range(4):
            t_ref[0, toff + c * LANES:toff + (c + 1) * LANES, :] = rope(zoff, c, kind).T.astype(BF16)
    for c in range(4):
        vt_ref[0, 0, c * LANES:(c + 1) * LANES, :] = zc(Z_DV, c).T.astype(BF16)

    cq = z_cols(Z_CQ, 512)
    ms = jnp.sum(cq * cq, axis=-1, keepdims=True) * (1.0 / MLA_Q_LORA)
    p_ref[:, P_CQ:P_CQ + 512] = (cq * lax.rsqrt(ms + EPS) * gq_ref[...]).astype(BF16)
    ckv = z_cols(Z_CKV, MLA_KV_LORA)
    ms = jnp.mean(ckv * ckv, axis=-1, keepdims=True)
    p_ref[:, P_CKV:P_CKV + MLA_KV_LORA] = (ckv * lax.rsqrt(ms + EPS) * gkv_ref[...]).astype(BF16)

    ik = zc(Z_IK)
    ms = jnp.mean(ik * ik, axis=-1, keepdims=True)
    ikn = ik * lax.rsqrt(ms + EPS) * gik_ref[...]
    put(P_IK, 0, _rope128(ikn, tab(0), ROPE_KINDS[0][0] // 2))

    sm = zc(Z_SMALL)
    lane = lax.broadcasted_iota(I32, sm.shape, 1)
    iw_scale = IDX_HEADS ** -0.5 * IDX_DIM ** -0.5
    small = jnp.where(lane < SMALL_IW, jax.nn.sigmoid(sm), sm * iw_scale)
    small_ref[...] = small
    smallt_ref[0] = small.T


def _proj_prep(x2, g, w, tab, gq, gkv, gik, *, batch, seq, ts):
    m, d = x2.shape
    spb = seq // ts
    assert ts == VT_TILE and w.shape == (d, Z_WIDTH)
    row = lambda w: pl.BlockSpec((ts, w), lambda i: (i, 0))
    return pl.pallas_call(
        _proj_prep_kernel,
        out_shape=(jax.ShapeDtypeStruct((m, P_WIDTH), BF16),
                   jax.ShapeDtypeStruct((batch, T_ROWS, seq), BF16),
                   jax.ShapeDtypeStruct((batch, spb, BR_WIDTH, VT_TILE), BF16),
                   jax.ShapeDtypeStruct((m, LANES), BF16),
                   jax.ShapeDtypeStruct((m, LANES), BF16),
                   jax.ShapeDtypeStruct((m, LANES), F32),
                   jax.ShapeDtypeStruct((batch, LANES, seq), F32)),
        grid=(m // ts,),
        in_specs=[row(d),
                  pl.BlockSpec((1, d), lambda i: (0, 0)),
                  pl.BlockSpec((d, Z_WIDTH), lambda i: (0, 0)),
                  pl.BlockSpec((ts, 3 * TAB_W), lambda i: (i % spb, 0)),
                  pl.BlockSpec((1, 512), lambda i: (0, 0)),
                  pl.BlockSpec((1, MLA_KV_LORA), lambda i: (0, 0)),
                  pl.BlockSpec((1, LANES), lambda i: (0, 0))],
        out_specs=(row(P_WIDTH),
                   pl.BlockSpec((1, T_ROWS, ts), lambda i: (i // spb, 0, i % spb)),
                   pl.BlockSpec((1, 1, BR_WIDTH, VT_TILE), lambda i: (i // spb, i % spb, 0, 0)),
                   row(LANES), row(LANES), row(LANES),
                   pl.BlockSpec((1, LANES, ts), lambda i: (i // spb, 0, i % spb))),
        compiler_params=_cparams(1),
        name="proj_prep",
    )(x2, g, w, tab, gq, gkv, gik)


def _softmax_init(mx_scr, l_scr, acc_scr):
    mx_scr[...] = jnp.full(mx_scr.shape, NEG, F32)
    l_scr[...] = jnp.zeros(l_scr.shape, F32)
    acc_scr[...] = jnp.zeros(acc_scr.shape, F32)


def _score_store(g, j, s, s_scr, mx_scr):
    s_scr[g, j] = s
    m = s[:, 0:LANES]
    for c in range(1, s.shape[1] // LANES):
        m = jnp.maximum(m, s[:, c * LANES:(c + 1) * LANES])
    mx_scr[g] = jnp.maximum(mx_scr[g], m)


def _row_max_finish(mx_scr):
    for g in range(mx_scr.shape[0]):
        m = jnp.max(mx_scr[g], axis=-1, keepdims=True)
        mx_scr[g] = jnp.broadcast_to(m, mx_scr.shape[1:])


def _prob_accumulate(g, j, v_tile, s_scr, mx_scr, l_scr, acc_scr):
    mb = mx_scr[g]
    s = s_scr[g, j]
    ps = [jnp.exp2(s[:, c * LANES:(c + 1) * LANES] - mb) for c in range(s.shape[1] // LANES)]
    tot = ps[0]
    for p in ps[1:]:
        tot = tot + p
    l_scr[g] += tot
    acc_scr[g] += _dot(jnp.concatenate(ps, axis=1).astype(BF16), v_tile)


def _softmax_out(g, l_scr, acc_scr):
    return acc_scr[g] / jnp.sum(l_scr[g], axis=-1, keepdims=True)


def _paired_tiles(n, step):
    def pair(jj, carry):
        step(2 * jj)
        step(2 * jj + 1)
        return carry

    lax.fori_loop(0, n // 2, pair, 0)

    @pl.when(n % 2 == 1)
    def _():
        step(n - 1)


def _causal_tiles(step, n_full):
    _paired_tiles(n_full, lambda j: step(j, False))
    step(n_full, True)


def _softmax_scratch(groups, n_tiles, rows, tk):
    return [pltpu.VMEM((groups, n_tiles, rows, tk), F32),
            pltpu.VMEM((groups, rows, LANES), F32),
            pltpu.VMEM((groups, rows, LANES), F32),
            pltpu.VMEM((groups, rows, HEAD_W), F32)]


def _diff_attn_kernel(q_ref, k_ref, v_ref, lv_ref, g_ref, o_ref, s_scr, mx_scr, l_scr, acc_scr,
                      *, tq, tk, lam_init):
    qs = pl.program_id(1) * tq
    n_full = qs // tk
    scale = DA_DIM ** -0.5 * LOG2E
    lv = lv_ref[...]
    lam = (jnp.exp(jnp.sum(lv[0:1] * lv[1:2], axis=-1, keepdims=True))
           - jnp.exp(jnp.sum(lv[2:3] * lv[3:4], axis=-1, keepdims=True)) + lam_init)
    lane = lax.broadcasted_iota(I32, (tq, HEAD_W), 1)
    row_t = qs + lax.broadcasted_iota(I32, (2 * tq, 1), 0) % tq
    col0 = lax.broadcasted_iota(I32, (2 * tq, tk), 1)
    _softmax_init(mx_scr, l_scr, acc_scr)

    def scores(j, masked):
        ks = pl.multiple_of(j * tk, tk)
        for h in range(HEADS):
            hs = slice(h * HEAD_W, (h + 1) * HEAD_W)
            qh = q_ref[0, :, hs]
            zero = jnp.zeros_like(qh)
            q2 = jnp.concatenate([jnp.where(lane < DA_DIM, qh, zero),
                                  jnp.where(lane >= DA_DIM, qh, zero)], axis=0)
            s = _dot_nt(q2, k_ref[0, pl.ds(ks, tk), hs]) * scale
            if masked:
                s = jnp.where(col0 + ks <= row_t, s, NEG)
            _score_store(h, j, s, s_scr, mx_scr)

    _causal_tiles(scores, n_full)
    _row_max_finish(mx_scr)

    def probs(j, carry):
        ks = pl.multiple_of(j * tk, tk)
        for h in range(HEADS):
            v_tile = v_ref[0, pl.ds(ks, tk), h * HEAD_W:(h + 1) * HEAD_W]
            _prob_accumulate(h, j, v_tile, s_scr, mx_scr, l_scr, acc_scr)
        return carry

    _paired_tiles(n_full + 1, lambda j: probs(j, 0))
    for h in range(HEADS):
        o2 = _softmax_out(h, l_scr, acc_scr)
        o = o2[:tq] - lam * o2[tq:]
        ms = jnp.mean(o * o, axis=-1, keepdims=True)
        o = o * lax.rsqrt(ms + EPS) * g_ref[...]
        o_ref[0, :, h * HEAD_W:(h + 1) * HEAD_W] = (o * (1.0 - lam_init)).astype(BF16)


def _diff_attn(p3, lv, g, *, lam_init, tq, tk):
    b, s, _ = p3.shape
    return pl.pallas_call(
        functools.partial(_diff_attn_kernel, tq=tq, tk=tk, lam_init=lam_init),
        out_shape=jax.ShapeDtypeStruct((b, s, BR_WIDTH), BF16),
        grid=(b, s // tq),
        in_specs=[pl.BlockSpec((1, tq, 512), lambda bi, i: (bi, i, P_AQ // 512)),
                  pl.BlockSpec((1, s, 512), lambda bi, i: (bi, 0, P_AK // 512)),
                  pl.BlockSpec((1, s, 512), lambda bi, i: (bi, 0, P_AV // 512)),
                  pl.BlockSpec((4, DA_DIM), lambda bi, i: (0, 0)),
                  pl.BlockSpec((1, HEAD_W), lambda bi, i: (0, 0))],
        out_specs=pl.BlockSpec((1, tq, BR_WIDTH), lambda bi, i: (bi, i, 0)),
        scratch_shapes=_softmax_scratch(HEADS, s // tk, 2 * tq, tk),
        compiler_params=_cparams(2),
        name="diff_attn",
    )(p3, p3, p3, lv, g)


def _nsa_compress_kernel(gk_ref, gv_ref, w1k_ref, w1v_ref, pek_ref, pev_ref,
                         w1kf_ref, w1vf_ref, w2k_ref, w2v_ref, kc_ref, vc_ref):
    def one(g_ref, w1cat_ref, pe_ref, w1f_ref, w2_ref, o_ref):
        y = _dot(g_ref[0], w1cat_ref[...])
        n = y.shape[0]
        nxt = pltpu.roll(y[:, HEAD_W:], n - 1, 0)
        c = _dot(pe_ref[...], w1f_ref[...])[0:1]
        hid = jax.nn.gelu(y[:, :HEAD_W] + nxt + c)
        o_ref[0] = _dot(hid.astype(BF16), w2_ref[...]).astype(BF16)

    one(gk_ref, w1k_ref, pek_ref, w1kf_ref, w2k_ref, kc_ref)
    one(gv_ref, w1v_ref, pev_ref, w1vf_ref, w2v_ref, vc_ref)


def _nsa_compress(gk, gv, w1k_cat, w1v_cat, pek, pev, w1k, w1v, w2k, w2v):
    b, ng, gw = gk.shape
    full = lambda shape: pl.BlockSpec(shape, lambda bi: (0,) * len(shape))
    return pl.pallas_call(
        _nsa_compress_kernel,
        out_shape=(jax.ShapeDtypeStruct((b, ng, HEAD_W), BF16),
                   jax.ShapeDtypeStruct((b, ng, HEAD_W), BF16)),
        grid=(b,),
        in_specs=[pl.BlockSpec((1, ng, gw), lambda bi: (bi, 0, 0)),
                  pl.BlockSpec((1, ng, gw), lambda bi: (bi, 0, 0)),
                  full(w1k_cat.shape), full(w1v_cat.shape), full(pek.shape), full(pev.shape),
                  full(w1k.shape), full(w1v.shape), full(w2k.shape), full(w2v.shape)],
        out_specs=(pl.BlockSpec((1, ng, HEAD_W), lambda bi: (bi, 0, 0)),
                   pl.BlockSpec((1, ng, HEAD_W), lambda bi: (bi, 0, 0))),
        compiler_params=_cparams(1),
        name="nsa_compress",
    )(gk, gv, w1k_cat, w1v_cat, pek, pev, w1k, w1v, w2k, w2v)


NSA_GROUPS = 2


def _nsa_kernel(q_ref, kc_ref, vc_ref, ks_ref, vs_ref, kw_ref, vw_ref, small_ref, ov_ref, e_ref,
                o_ref, s_scr, mx_scr, l_scr, acc_scr, cmp_scr, win_scr, *, tq, tk, seq):
    qs = pl.program_id(1) * tq
    scale = NSA_DK ** -0.5
    ns = seq // SEL_LEN
    n_sel = min(SEL_N, ns)
    r = HEADS * tq
    rg = r // NSA_GROUPS
    q4 = jnp.concatenate([q_ref[0, :, h * HEAD_W:(h + 1) * HEAD_W] for h in range(HEADS)], axis=0)
    t1 = qs + lax.broadcasted_iota(I32, (tq, 1), 0)
    t4 = qs + lax.broadcasted_iota(I32, (r, 1), 0) % tq

    wspan = WIN + tq
    start = pl.multiple_of(jnp.maximum(qs - WIN, 0), tq)
    sw = _dot_nt(q4, kw_ref[0, pl.ds(start, wspan), :]) * scale
    dist = t4 - (start + lax.broadcasted_iota(I32, (r, wspan), 1))
    sw = jnp.where(pltpu.bitcast(dist, jnp.uint32) < jnp.uint32(WIN), sw, NEG)
    e = jnp.exp(sw - jnp.max(sw, axis=-1, keepdims=True))
    pw = e / jnp.sum(e, axis=-1, keepdims=True)
    win_scr[...] = _dot(pw.astype(BF16), vw_ref[0, pl.ds(start, wspan), :])

    kc = kc_ref[0]
    nc_pad = kc.shape[0]
    sc = _dot_nt(q4, kc) * scale
    c_end = lax.broadcasted_iota(I32, (r, nc_pad), 1) * CMP_STRIDE + (CMP_LEN - 1)
    cmask = c_end <= t4
    mx = jnp.max(jnp.where(cmask, sc, NEG), axis=-1, keepdims=True)
    e = jnp.where(cmask, jnp.exp(sc - mx), 0.0)
    den = jnp.sum(e, axis=-1, keepdims=True)
    pc = e / jnp.where(den > 0.0, den, 1.0)
    cmp_scr[...] = _dot(pc.astype(BF16), vc_ref[0])

    psum = pc[0:tq] + pc[tq:2 * tq] + pc[2 * tq:3 * tq] + pc[3 * tq:4 * tq]
    ov = ov_ref[...]
    hi = psum.astype(BF16)
    r1 = psum - hi.astype(F32)
    mid = r1.astype(BF16)
    lo = (r1 - mid.astype(F32)).astype(BF16)
    imp = _dot(hi, ov) + _dot(mid, ov) + _dot(lo, ov)

    blk = lax.broadcasted_iota(I32, (tq, LANES), 1)
    cur = t1 // SEL_LEN
    forced = (blk == 0) | (blk == cur) | (blk == cur - 1)
    visible = blk * SEL_LEN <= t1
    score = jnp.where(visible, jnp.where(forced, FORCE_SCORE, imp), NEG)
    score = jnp.where(blk < ns, score, PAD_SCORE)
    ns_pad = -(-ns // SUBLANES) * SUBLANES
    score_t = score.T[:ns_pad]
    blk_t = lax.broadcasted_iota(I32, (ns_pad, tq), 0)
    rank = jnp.zeros((ns_pad, tq), I32)
    for jp in range(ns):
        row = score_t[jp:jp + 1, :]
        later = (blk_t > jp).astype(I32)
        rank = rank + jnp.where(row > score_t, 1, jnp.where(row == score_t, later, 0))
    sel_t = jnp.where(rank < n_sel, 1.0, 0.0)
    if ns_pad < LANES:
        sel_t = jnp.concatenate([sel_t, jnp.zeros((LANES - ns_pad, tq), F32)], axis=0)
    selb = sel_t.T.astype(BF16)

    _softmax_init(mx_scr, l_scr, acc_scr)
    col0 = lax.broadcasted_iota(I32, (rg, tk), 1)
    tg = qs + lax.broadcasted_iota(I32, (rg, 1), 0) % tq
    n_tiles = qs // tk + 1

    def scores(j, masked):
        ks0 = pl.multiple_of(j * tk, tk)
        mt = _dot(selb, e_ref[j])
        mg = jnp.concatenate([mt] * (rg // tq), axis=0)
        k_tile = ks_ref[0, pl.ds(ks0, tk), :]
        for g in range(NSA_GROUPS):
            s = _dot_nt(q4[g * rg:(g + 1) * rg], k_tile) * (scale * LOG2E)
            s = jnp.where(mg > 0.5, s, NEG)
            if masked:
                s = jnp.where(col0 + ks0 <= tg, s, NEG)
            _score_store(g, j, s, s_scr, mx_scr)

    _causal_tiles(scores, n_tiles - 1)
    _row_max_finish(mx_scr)

    def probs(j, carry):
        ks0 = pl.multiple_of(j * tk, tk)
        v_tile = vs_ref[0, pl.ds(ks0, tk), :]
        for g in range(NSA_GROUPS):
            _prob_accumulate(g, j, v_tile, s_scr, mx_scr, l_scr, acc_scr)
        return carry

    _paired_tiles(n_tiles, lambda j: probs(j, 0))
    o_slc = jnp.concatenate([_softmax_out(g, l_scr, acc_scr) for g in range(NSA_GROUPS)], axis=0)

    gates = small_ref[0]
    for h in range(HEADS):
        rows = slice(h * tq, (h + 1) * tq)
        g0 = gates[:, SMALL_G + 3 * h:SMALL_G + 3 * h + 1]
        g1 = gates[:, SMALL_G + 3 * h + 1:SMALL_G + 3 * h + 2]
        g2 = gates[:, SMALL_G + 3 * h + 2:SMALL_G + 3 * h + 3]
        o = g0 * cmp_scr[rows, :] + g1 * o_slc[rows] + g2 * win_scr[rows, :]
        o_ref[0, :, h * HEAD_W:(h + 1) * HEAD_W] = o.astype(BF16)


def _nsa(p3, kc, vc, small3, ov, emat, *, tq, tk):
    b, s, _ = p3.shape
    ng = kc.shape[1]
    col = lambda off: (lambda bi, i: (bi, 0, off // LANES))
    return pl.pallas_call(
        functools.partial(_nsa_kernel, tq=tq, tk=tk, seq=s),
        out_shape=jax.ShapeDtypeStruct((b, s, BR_WIDTH), BF16),
        grid=(b, s // tq),
        in_specs=[pl.BlockSpec((1, tq, 512), lambda bi, i: (bi, i, P_BQ // 512)),
                  pl.BlockSpec((1, ng, HEAD_W), lambda bi, i: (bi, 0, 0)),
                  pl.BlockSpec((1, ng, HEAD_W), lambda bi, i: (bi, 0, 0)),
                  pl.BlockSpec((1, s, LANES), col(P_KS)),
                  pl.BlockSpec((1, s, LANES), col(P_VS)),
                  pl.BlockSpec((1, s, LANES), col(P_KW)),
                  pl.BlockSpec((1, s, LANES), col(P_VW)),
                  pl.BlockSpec((1, tq, LANES), lambda bi, i: (bi, i, 0)),
                  pl.BlockSpec(ov.shape, lambda bi, i: (0, 0)),
                  pl.BlockSpec(emat.shape, lambda bi, i: (0, 0, 0))],
        out_specs=pl.BlockSpec((1, tq, BR_WIDTH), lambda bi, i: (bi, i, 0)),
        scratch_shapes=(_softmax_scratch(NSA_GROUPS, s // tk, HEADS * tq // NSA_GROUPS, tk)
                        + [pltpu.VMEM((HEADS * tq, HEAD_W), F32), pltpu.VMEM((HEADS * tq, HEAD_W), F32)]),
        compiler_params=_cparams(2),
        name="nsa_attn",
    )(p3, kc, vc, p3, p3, p3, p3, small3, ov, emat)


def _mla_up_kernel(p_ref, ckv_ref, tab_ref, wq_ref, wkv_ref, q_ref, kv_ref):
    q = _dot(p_ref[...], wq_ref[...])
    nn = HEADS * MLA_NOPE
    q_ref[:, :nn] = q[:, :nn].astype(BF16)
    for c in range(nn // LANES, (nn + HEADS * MLA_ROPE) // LANES):
        tile = _rope128(q[:, c * LANES:(c + 1) * LANES], tab_ref[...], MLA_ROPE // 2)
        q_ref[:, c * LANES:(c + 1) * LANES] = tile.astype(BF16)
    kv_ref[...] = _dot(ckv_ref[...], wkv_ref[...]).astype(BF16)


def _mla_up(p2, tab, wq, wkv, *, seq, ts):
    m = p2.shape[0]
    spb = seq // ts
    nq = wq.shape[1]
    nkv = wkv.shape[1]
    return pl.pallas_call(
        _mla_up_kernel,
        out_shape=(jax.ShapeDtypeStruct((m, nq), BF16), jax.ShapeDtypeStruct((m, nkv), BF16)),
        grid=(m // ts,),
        in_specs=[pl.BlockSpec((ts, 512), lambda i: (i, P_CQ // 512)),
                  pl.BlockSpec((ts, MLA_KV_LORA), lambda i: (i, P_CKV // MLA_KV_LORA)),
                  pl.BlockSpec((ts, TAB_W), lambda i: (i % spb, 2)),
                  pl.BlockSpec(wq.shape, lambda i: (0, 0)),
                  pl.BlockSpec(wkv.shape, lambda i: (0, 0))],
        out_specs=(pl.BlockSpec((ts, nq), lambda i: (i, 0)),
                   pl.BlockSpec((ts, nkv), lambda i: (i, 0))),
        compiler_params=_cparams(1),
        name="mla_up",
    )(p2, p2, tab, wq, wkv)


def _mla_attn_kernel(qn_ref, qr_ref, kn_ref, kr_ref, v_ref, o_ref, s_scr, mx_scr, l_scr, acc_scr,
                     *, tq, tk):
    qs = pl.program_id(1) * tq
    n_full = qs // tk
    scale = (MLA_NOPE + MLA_ROPE) ** -0.5 * LOG2E
    lane = lax.broadcasted_iota(I32, (tq, LANES), 1)
    row_t = qs + lax.broadcasted_iota(I32, (tq, 1), 0)
    col0 = lax.broadcasted_iota(I32, (tq, tk), 1)
    _softmax_init(mx_scr, l_scr, acc_scr)

    def scores(j, masked):
        ks = pl.multiple_of(j * tk, tk)
        kr_tile = kr_ref[0, pl.ds(ks, tk), :]
        for h in range(HEADS):
            hs = slice(h * HEAD_W, (h + 1) * HEAD_W)
            pair = qr_ref[0, :, (h // 2) * LANES:(h // 2 + 1) * LANES]
            keep = (lane < MLA_ROPE) if h % 2 == 0 else (lane >= MLA_ROPE)
            qr = jnp.where(keep, pair, jnp.zeros_like(pair))
            s = _dot_nt(jnp.concatenate([qn_ref[0, :, hs], qr], axis=1),
                        jnp.concatenate([kn_ref[0, pl.ds(ks, tk), hs], kr_tile], axis=1)) * scale
            if masked:
                s = jnp.where(col0 + ks <= row_t, s, NEG)
            _score_store(h, j, s, s_scr, mx_scr)

    _causal_tiles(scores, n_full)
    _row_max_finish(mx_scr)

    def probs(j, carry):
        ks = pl.multiple_of(j * tk, tk)
        for h in range(HEADS):
            v_tile = v_ref[0, pl.ds(ks, tk), h * HEAD_W:(h + 1) * HEAD_W]
            _prob_accumulate(h, j, v_tile, s_scr, mx_scr, l_scr, acc_scr)
        return carry

    _paired_tiles(n_full + 1, lambda j: probs(j, 0))
    for h in range(HEADS):
        o_ref[0, :, h * HEAD_W:(h + 1) * HEAD_W] = _softmax_out(h, l_scr, acc_scr).astype(BF16)


def _mla_attn(q3, kv3, p3, *, tq, tk):
    b, s, _ = q3.shape
    return pl.pallas_call(
        functools.partial(_mla_attn_kernel, tq=tq, tk=tk),
        out_shape=jax.ShapeDtypeStruct((b, s, BR_WIDTH), BF16),
        grid=(b, s // tq),
        in_specs=[pl.BlockSpec((1, tq, 512), lambda bi, i: (bi, i, 0)),
                  pl.BlockSpec((1, tq, 256), lambda bi, i: (bi, i, 2)),
                  pl.BlockSpec((1, s, 512), lambda bi, i: (bi, 0, 0)),
                  pl.BlockSpec((1, s, LANES), lambda bi, i: (bi, 0, P_KR // LANES)),
                  pl.BlockSpec((1, s, 512), lambda bi, i: (bi, 0, 1))],
        out_specs=pl.BlockSpec((1, tq, BR_WIDTH), lambda bi, i: (bi, i, 0)),
        scratch_shapes=_softmax_scratch(HEADS, s // tk, tq, tk),
        compiler_params=_cparams(2),
        name="mla_attn",
    )(q3, q3, kv3, p3, kv3)


def _sortable_key(x):
    bits = pltpu.bitcast(x + 0.0, I32)
    return bits ^ (lax.shift_right_arithmetic(bits, 31) & 0x7FFFFFFF)


def _fold_rows(x, op):
    n = x.shape[0] // SUBLANES
    return op(x.reshape(n, SUBLANES, x.shape[1]), axis=0)


def _count16(half_scr, n_tiles, pred, tq):
    def count_tile(j, cnt):
        hit = pred(half_scr[j]).astype(I16)
        parts = [hit[r:r + PACKED_ROWS] for r in range(0, hit.shape[0], PACKED_ROWS)]
        while len(parts) > 1:
            parts = [a + b for a, b in zip(parts[0::2], parts[1::2])]
        return cnt + parts[0]

    cnt = lax.fori_loop(0, n_tiles, count_tile, jnp.zeros((PACKED_ROWS, tq), I16))
    return jnp.sum(cnt.astype(I32), axis=0, keepdims=True)


def _bisect16(half_scr, n_tiles, need, tq):
    def bit_body(i, th):
        cand = th + lax.shift_left(jnp.int32(1), 15 - i)
        c16 = cand.astype(I16)
        total = _count16(half_scr, n_tiles, lambda x: x >= c16, tq)
        return jnp.where(total >= need, cand, th)

    return lax.fori_loop(0, 16, bit_body, jnp.full((1, tq), HALF_MIN, I32))


def _dsa_kernel(qt_ref, iqt_ref, iwt_ref, k_ref, ik_ref, vt_ref, o_ref,
                key_scr, half_scr, s_scr, mx_scr, l_scr, acc_scr, *, tq, tk, top):
    qs = pl.program_id(1) * tq
    n_tiles = (qs + tq - 1) // tk + 1
    scale = DSA_DIM ** -0.5 * LOG2E
    t_lane = qs + lax.broadcasted_iota(I32, (tk, tq), 1)
    krow0 = lax.broadcasted_iota(I32, (tk, tq), 0)
    half_rows = lax.broadcasted_iota(I32, (LANES, tq), 0) < IDX_DIM
    vt_per_tile = tk // VT_TILE

    def score_tile(j, carry):
        ks = pl.multiple_of(j * tk, tk)
        ikt = ik_ref[0, pl.ds(ks, tk), :]
        acc = jnp.zeros((tk, tq), F32)
        for h in range(IDX_HEADS):
            pair = iqt_ref[0, (h // 2) * LANES:(h // 2 + 1) * LANES, :]
            keep = half_rows if h % 2 == 0 else jnp.logical_not(half_rows)
            iq_h = jnp.where(keep, pair, jnp.zeros_like(pair))
            w_h = iwt_ref[0, SMALL_IW + h:SMALL_IW + h + 1, :]
            acc = acc + w_h * jnp.maximum(_dot(ikt, iq_h), 0.0)
        key = jnp.where(krow0 + ks <= t_lane, _sortable_key(acc), INT_MIN)
        key_scr[j] = key
        half_scr[j] = lax.shift_right_arithmetic(key, 16).astype(I16)
        return carry

    _paired_tiles(n_tiles, lambda j: score_tile(j, 0))

    th_hi = _bisect16(half_scr, n_tiles, top, tq)
    hi16 = th_hi.astype(I16)
    need_lo = top - _count16(half_scr, n_tiles, lambda x: x > hi16, tq)

    def low_tile(j, carry):
        key = key_scr[j]
        lo = (key & 0xFFFF) + HALF_MIN
        same_hi = lax.shift_right_arithmetic(key, 16) == th_hi
        half_scr[j] = jnp.where(same_hi, lo, HALF_MIN).astype(I16)
        return carry

    lax.fori_loop(0, n_tiles, low_tile, 0)
    th_lo = _bisect16(half_scr, n_tiles, need_lo, tq)
    theta = lax.shift_left(th_hi, 16) + (th_lo - HALF_MIN)

    def count32(pred):
        def body(j, cnt):
            return cnt + _fold_rows(pred(key_scr[j], krow0 + j * tk).astype(I32), jnp.sum)

        cnt = lax.fori_loop(0, n_tiles, body, jnp.zeros((SUBLANES, tq), I32))
        return jnp.sum(cnt, axis=0, keepdims=True)

    tied = (count32(lambda k, pos: k >= theta) > top) & (theta > INT_MIN)

    @pl.when(jnp.max(tied.astype(I32)) > 0)
    def _():
        need_eq = top - count32(lambda k, pos: k > theta)
        pos_bits = max(1, (tk * key_scr.shape[0] - 1).bit_length())

        def bit_body(i, last):
            cand = last + lax.shift_left(jnp.int32(1), pos_bits - 1 - i)
            below = count32(lambda k, pos: (k == theta) & (pos < cand))
            return jnp.where(below < need_eq, cand, last)

        last = lax.fori_loop(0, pos_bits, bit_body, jnp.zeros((1, tq), I32))

        def demote(j, carry):
            k = key_scr[j]
            drop = (k == theta) & (krow0 + j * tk > last) & tied
            key_scr[j] = jnp.where(drop, k - 1, k)
            return carry

        lax.fori_loop(0, n_tiles, demote, 0)

    theta = jnp.maximum(theta, INT_MIN + 1)

    mx_scr[...] = jnp.full(mx_scr.shape, NEG, F32)
    l_scr[...] = jnp.zeros(l_scr.shape, F32)
    acc_scr[...] = jnp.zeros(acc_scr.shape, F32)

    def scores(j, carry):
        ks = pl.multiple_of(j * tk, tk)
        sel = key_scr[j] >= theta
        for h in range(HEADS):
            hs = slice(h * HEAD_W, (h + 1) * HEAD_W)
            s = _dot(k_ref[0, pl.ds(ks, tk), hs], qt_ref[0, hs, :]) * scale
            s = jnp.where(sel, s, NEG)
            s_scr[h, j] = s
            mx_scr[h] = jnp.maximum(mx_scr[h], _fold_rows(s, jnp.max))
        return carry

    _paired_tiles(n_tiles, lambda j: scores(j, 0))
    for h in range(HEADS):
        m = jnp.max(mx_scr[h], axis=0, keepdims=True)
        mx_scr[h] = jnp.broadcast_to(m, (SUBLANES, tq))

    def probs(j, carry):
        for h in range(HEADS):
            p = jnp.exp2(s_scr[h, j] - mx_scr[h][0:1])
            l_scr[h] += _fold_rows(p, jnp.sum)
            pb = p.astype(BF16)
            for c in range(vt_per_tile):
                vt = vt_ref[0, j * vt_per_tile + c, h * HEAD_W:(h + 1) * HEAD_W, :]
                acc_scr[h] += _dot(vt, pb[c * VT_TILE:(c + 1) * VT_TILE])
        return carry

    _paired_tiles(n_tiles, lambda j: probs(j, 0))
    for h in range(HEADS):
        ot = acc_scr[h] / jnp.sum(l_scr[h], axis=0, keepdims=True)
        o_ref[0, :, h * HEAD_W:(h + 1) * HEAD_W] = ot.T.astype(BF16)


def _dsa(p3, t3, vt4, smallt, *, tq, tk, top):
    b, s, _ = p3.shape
    n_vt = vt4.shape[1]
    return pl.pallas_call(
        functools.partial(_dsa_kernel, tq=tq, tk=tk, top=top),
        out_shape=jax.ShapeDtypeStruct((b, s, BR_WIDTH), BF16),
        grid=(b, s // tq),
        in_specs=[pl.BlockSpec((1, 512, tq), lambda bi, i: (bi, T_DQ // 512, i)),
                  pl.BlockSpec((1, 512, tq), lambda bi, i: (bi, T_IQ // 512, i)),
                  pl.BlockSpec((1, LANES, tq), lambda bi, i: (bi, 0, i)),
                  pl.BlockSpec((1, s, 512), lambda bi, i: (bi, 0, P_DK // 512)),
                  pl.BlockSpec((1, s, LANES), lambda bi, i: (bi, 0, P_IK // LANES)),
                  pl.BlockSpec((1, n_vt, BR_WIDTH, VT_TILE), lambda bi, i: (bi, 0, 0, 0))],
        out_specs=pl.BlockSpec((1, tq, BR_WIDTH), lambda bi, i: (bi, i, 0)),
        scratch_shapes=[pltpu.VMEM((s // tk, tk, tq), I32),
                        pltpu.VMEM((s // tk, tk, tq), I16),
                        pltpu.VMEM((HEADS, s // tk, tk, tq), F32),
                        pltpu.VMEM((HEADS, SUBLANES, tq), F32),
                        pltpu.VMEM((HEADS, SUBLANES, tq), F32),
                        pltpu.VMEM((HEADS, HEAD_W, tq), F32)],
        compiler_params=_cparams(2),
        name="dsa_attn",
    )(t3, t3, smallt, p3, p3, vt4)


def _merge_kernel(x_ref, oa_ref, ob_ref, oc_ref, od_ref, g_ref, wb_ref, wo_ref, o_ref):
    d = x_ref.shape[1]
    acc = jnp.zeros(x_ref.shape, F32)
    for n, br_ref in enumerate((oa_ref, ob_ref, oc_ref, od_ref)):
        br = _dot(br_ref[...], wb_ref[n])
        acc = acc + g_ref[:, n * d:(n + 1) * d].astype(F32) * br
    o_ref[...] = x_ref[...] + _dot(acc.astype(BF16), wo_ref[...])


def _merge(x2, oa, ob, oc, od, gates, wb, wo, *, tm):
    m, d = x2.shape
    row = lambda w: pl.BlockSpec((tm, w), lambda i: (i, 0))
    return pl.pallas_call(
        _merge_kernel,
        out_shape=jax.ShapeDtypeStruct((m, d), F32),
        grid=(m // tm,),
        in_specs=[row(d), row(BR_WIDTH), row(BR_WIDTH), row(BR_WIDTH), row(BR_WIDTH),
                  row(gates.shape[1]),
                  pl.BlockSpec(wb.shape, lambda i: (0, 0, 0)),
                  pl.BlockSpec(wo.shape, lambda i: (0, 0))],
        out_specs=row(d),
        compiler_params=_cparams(1),
        name="merge",
    )(x2, oa, ob, oc, od, gates, wb, wo)


def _ffn_kernel(x_ref, g_ref, wg_ref, wu_ref, wd_ref, gf_ref, o_ref, h_scr, acc_scr, *, final):
    j = pl.program_id(1)

    @pl.when(j == 0)
    def _():
        x = x_ref[...]
        ms = jnp.mean(x * x, axis=-1, keepdims=True)
        h_scr[...] = (x * lax.rsqrt(ms + EPS) * g_ref[...]).astype(BF16)
        acc_scr[...] = jnp.zeros(acc_scr.shape, F32)

    h = h_scr[...]
    a = jax.nn.silu(_dot(h, wg_ref[...])) * _dot(h, wu_ref[...])
    acc_scr[...] += _dot(a.astype(BF16), wd_ref[...])

    @pl.when(j == pl.num_programs(1) - 1)
    def _():
        y = x_ref[...] + acc_scr[...]
        if final:
            ms = jnp.mean(y * y, axis=-1, keepdims=True)
            y = y * lax.rsqrt(ms + EPS) * gf_ref[...]
        o_ref[...] = y


def _ffn(x2, g, wg, wu, wd, gf, *, final, tm, tf):
    m, d = x2.shape
    dff = wg.shape[1]
    return pl.pallas_call(
        functools.partial(_ffn_kernel, final=final),
        out_shape=jax.ShapeDtypeStruct((m, d), F32),
        grid=(m // tm, dff // tf),
        in_specs=[pl.BlockSpec((tm, d), lambda i, j: (i, 0)),
                  pl.BlockSpec((1, d), lambda i, j: (0, 0)),
                  pl.BlockSpec((d, tf), lambda i, j: (0, j)),
                  pl.BlockSpec((d, tf), lambda i, j: (0, j)),
                  pl.BlockSpec((tf, d), lambda i, j: (j, 0)),
                  pl.BlockSpec((1, d), lambda i, j: (0, 0))],
        out_specs=pl.BlockSpec((tm, d), lambda i, j: (i, 0)),
        scratch_shapes=[pltpu.VMEM((tm, d), BF16), pltpu.VMEM((tm, d), F32)],
        compiler_params=_cparams(2),
        name="ffn",
    )(x2, g, wg, wu, wd, gf)


def _tiles(seq, m, dff):
    pick = lambda n, cands: next(c for c in cands if n % c == 0)
    tk = pick(seq, (512, 256))
    return dict(
        proj_tm=pick(m, (1024, 512, 256, 128)), proj_tn=1024,
        prep_ts=VT_TILE,
        diff_tq=128, mla_tq=pick(seq, (256, 128)), nsa_tq=128, dsa_tq=256, tk=tk,
        row_tm=pick(m, (512, 256, 128)),
        ffn_tf=pick(dff, (1408, 704, 256, 128)),
    )


def kernel(x, norm1_g, w_in, diff_lq1, diff_lk1, diff_lq2, diff_lk2, diff_subln_g, nsa_pe_k, nsa_w1_k, nsa_w2_k, nsa_pe_v, nsa_w1_v, nsa_w2_v, mla_q_norm_g, mla_w_uq, mla_kv_norm_g, mla_w_ukv, idx_k_norm_g, w_branch, w_out, norm2_g, w_gate_up, w_down, final_norm_g):
    b, seq, d = x.shape
    depth = w_in.shape[0]
    m = b * seq
    dff = w_down.shape[1]
    t = _tiles(seq, m, dff)
    tk = t["tk"]
    assert seq % SEL_LEN == 0 and seq >= WIN + t["nsa_tq"] and seq // SEL_LEN <= LANES
    assert seq % t["dsa_tq"] == 0 and tk % VT_TILE == 0 and tk >= min(IDX_TOPK, seq // 4)

    col_idx, gate_off, d_in = _in_proj_columns()
    assert w_in.shape[2] == d_in
    tab = jnp.concatenate([_rope_table(seq, rot, per) for rot, per in ROPE_KINDS], axis=1)

    ng = seq // CMP_STRIDE
    ns = seq // SEL_LEN
    c_start = np.arange(ng)[:, None] * CMP_STRIDE
    s_start = np.arange(LANES)[None, :] * SEL_LEN
    ov = ((c_start < s_start + SEL_LEN) & (c_start + CMP_LEN - 1 >= s_start)
          & (np.arange(LANES)[None, :] < ns))
    ov = jnp.asarray(ov, BF16)
    emat = np.arange(LANES)[:, None] == (np.arange(seq)[None, :] // SEL_LEN)
    emat = jnp.asarray(emat.reshape(LANES, seq // tk, tk).transpose(1, 0, 2), BF16)

    qd = MLA_NOPE + MLA_ROPE
    uq_idx = np.concatenate([np.concatenate([np.arange(h * qd, h * qd + MLA_NOPE) for h in range(HEADS)]),
                             np.concatenate([np.arange(h * qd + MLA_NOPE, (h + 1) * qd) for h in range(HEADS)])])
    kvd = MLA_NOPE + HEAD_W
    ukv_idx = np.concatenate([np.concatenate([np.arange(h * kvd, h * kvd + MLA_NOPE) for h in range(HEADS)]),
                              np.concatenate([np.arange(h * kvd + MLA_NOPE, (h + 1) * kvd) for h in range(HEADS)])])

    x2 = x.reshape(m, d)
    half_w1 = CMP_STRIDE * NSA_DK
    for l in range(depth):
        lam_init = 0.8 - 0.6 * math.exp(-0.3 * l)
        w_in_l = w_in[l].astype(BF16)
        w_mix = _take_cols(w_in_l, col_idx)
        w_gate = w_in_l[:, gate_off:]
        gates = _norm_matmul(x2, norm1_g[l][None], w_gate, out_dtype=BF16, sigmoid=True,
                             tm=t["proj_tm"], tn=t["proj_tn"], name="gate_proj")

        gq = jnp.pad(mla_q_norm_g[l], (0, 512 - MLA_Q_LORA))[None]
        gkv = mla_kv_norm_g[l][None]
        gik = jnp.concatenate([idx_k_norm_g[l], idx_k_norm_g[l]])[None]
        p2, t3, vt4, kc_tok, vc_tok, small, smallt = _proj_prep(
            x2, norm1_g[l][None], w_mix, tab, gq, gkv, gik, batch=b, seq=seq, ts=t["prep_ts"])
        p3 = p2.reshape(b, seq, P_WIDTH)
        small3 = small.reshape(b, seq, LANES)

        lv = jnp.stack([diff_lq1[l], diff_lk1[l], diff_lq2[l], diff_lk2[l]])
        o_a = _diff_attn(p3, lv, diff_subln_g[l][None], lam_init=lam_init, tq=t["diff_tq"], tk=tk)

        w1k, w1v = nsa_w1_k[l].astype(BF16), nsa_w1_v[l].astype(BF16)
        w1k_cat = jnp.concatenate([w1k[:half_w1], w1k[half_w1:]], axis=1)
        w1v_cat = jnp.concatenate([w1v[:half_w1], w1v[half_w1:]], axis=1)
        pek = jnp.broadcast_to(nsa_pe_k[l].reshape(1, -1), (8, CMP_LEN * NSA_DK)).astype(BF16)
        pev = jnp.broadcast_to(nsa_pe_v[l].reshape(1, -1), (8, CMP_LEN * NSA_DK)).astype(BF16)
        kc, vc = _nsa_compress(kc_tok.reshape(b, ng, half_w1), vc_tok.reshape(b, ng, half_w1),
                               w1k_cat, w1v_cat, pek, pev, w1k, w1v,
                               nsa_w2_k[l].astype(BF16), nsa_w2_v[l].astype(BF16))
        o_b = _nsa(p3, kc, vc, small3, ov, emat, tq=t["nsa_tq"], tk=tk)

        wq = jnp.pad(_take_cols(mla_w_uq[l].astype(BF16), uq_idx), ((0, 512 - MLA_Q_LORA), (0, 0)))
        wkv = _take_cols(mla_w_ukv[l].astype(BF16), ukv_idx)
        q_c, kv_c = _mla_up(p2, tab, wq, wkv, seq=seq, ts=t["prep_ts"])
        o_c = _mla_attn(q_c.reshape(b, seq, -1), kv_c.reshape(b, seq, -1), p3, tq=t["mla_tq"], tk=tk)

        o_d = _dsa(p3, t3, vt4, smallt, tq=t["dsa_tq"], tk=tk, top=min(IDX_TOPK, seq // 4))

        x2 = _merge(x2, o_a.reshape(m, -1), o_b.reshape(m, -1), o_c.reshape(m, -1), o_d.reshape(m, -1),
                    gates, w_branch[l].astype(BF16), w_out[l].astype(BF16), tm=t["row_tm"])
        wgu = w_gate_up[l].astype(BF16)
        x2 = _ffn(x2, norm2_g[l][None], wgu[:, :dff], wgu[:, dff:], w_down[l].astype(BF16),
                  final_norm_g[None], final=(l == depth - 1), tm=t["row_tm"], tf=t["ffn_tf"])
    return x2.reshape(b, seq, d)
```

```python
import functools
import math

import numpy as np
import jax
import jax.numpy as jnp
from jax import lax
from jax.experimental import pallas as pl
from jax.experimental.pallas import tpu as pltpu

F32 = jnp.float32
BF16 = jnp.bfloat16
I32 = jnp.int32
I16 = jnp.int16

LANES = 128
SUBLANES = 8
PACKED_ROWS = 16
HALF_MIN = -32768
VMEM_LIMIT = 56 * 1024 * 1024

ROPE_THETA = 500000.0
NEG = -1e30
LOG2E = math.log2(math.e)
FORCE_SCORE = 1e9
PAD_SCORE = -3e38
EPS = 1e-6
INT_MIN = -2147483648

HEADS = 4
HEAD_W = 128
BR_WIDTH = HEADS * HEAD_W
DA_DIM = 64
NSA_DK = 128
CMP_LEN = 32
CMP_STRIDE = 16
SEL_LEN = 64
SEL_N = 16
WIN = 512
MLA_Q_LORA = 384
MLA_KV_LORA = 256
MLA_NOPE = 128
MLA_ROPE = 64
DSA_DIM = 128
IDX_HEADS = 8
IDX_DIM = 64
IDX_TOPK = 256

Z_AQ, Z_AK, Z_AV, Z_BQ, Z_DQ, Z_DK, Z_DV, Z_IQ = (i * 512 for i in range(8))
Z_CQ = 4096
Z_CKV = 4608
Z_KC, Z_KS, Z_KW, Z_VC, Z_VS, Z_VW, Z_KR, Z_IK, Z_SMALL = (4864 + i * 128 for i in range(9))
Z_WIDTH = 6144
P_AQ, P_AK, P_AV, P_BQ, P_DK, P_CQ = (i * 512 for i in range(6))
P_CKV = 3072
P_KS, P_KW, P_VS, P_VW, P_KR, P_IK = (3328 + i * 128 for i in range(6))
P_WIDTH = 4096
T_DQ, T_IQ = 0, 512
T_ROWS = 1024
VT_TILE = 256
SMALL_G = 0
SMALL_IW = 12

ROPE_KINDS = ((16, 64), (32, 128), (64, 64))
TAB_W = 3 * LANES


def _cparams(n_axes):
    return pltpu.CompilerParams(dimension_semantics=("arbitrary",) * n_axes,
                                vmem_limit_bytes=VMEM_LIMIT)


def _dot(a, b):
    return jnp.dot(a, b, preferred_element_type=F32)


def _dot_nt(a, b):
    return lax.dot_general(a, b, (((1,), (1,)), ((), ())), preferred_element_type=F32)


def _in_proj_columns():
    names = (("a_q", 512), ("a_k", 512), ("a_v", 512), ("b_q", 512),
             ("b_kc", 128), ("b_vc", 128), ("b_ks", 128), ("b_vs", 128),
             ("b_kw", 128), ("b_vw", 128), ("b_g", 12),
             ("c_q", 384), ("c_kv", 256), ("c_kr", 64),
             ("d_q", 512), ("d_k", 512), ("d_v", 512),
             ("d_iq", 512), ("d_ik", 64), ("d_iw", 8), ("gate", 4096))
    off, o = {}, 0
    for nm, n in names:
        off[nm] = (o, n)
        o += n
    idx = np.full((Z_WIDTH,), -1, np.int64)

    def put(dst, nm):
        s, n = off[nm]
        idx[dst:dst + n] = np.arange(s, s + n)

    put(Z_AQ, "a_q"); put(Z_AK, "a_k"); put(Z_AV, "a_v"); put(Z_BQ, "b_q")
    put(Z_DQ, "d_q"); put(Z_DK, "d_k"); put(Z_DV, "d_v"); put(Z_IQ, "d_iq")
    put(Z_CQ, "c_q"); put(Z_CKV, "c_kv")
    put(Z_KC, "b_kc"); put(Z_KS, "b_ks"); put(Z_KW, "b_kw")
    put(Z_VC, "b_vc"); put(Z_VS, "b_vs"); put(Z_VW, "b_vw")
    put(Z_KR, "c_kr"); put(Z_KR + 64, "c_kr")
    put(Z_IK, "d_ik"); put(Z_IK + 64, "d_ik")
    put(Z_SMALL + SMALL_G, "b_g"); put(Z_SMALL + SMALL_IW, "d_iw")
    return idx, off["gate"][0], o


def _take_cols(w, idx):
    runs, i, n = [], 0, len(idx)
    while i < n:
        j = i + 1
        if idx[i] < 0:
            while j < n and idx[j] < 0:
                j += 1
            runs.append(jnp.zeros((w.shape[0], j - i), w.dtype))
        else:
            while j < n and idx[j] == idx[j - 1] + 1:
                j += 1
            runs.append(w[:, int(idx[i]):int(idx[i]) + (j - i)])
        i = j
    return jnp.concatenate(runs, axis=1)


def _rope_table(seq, rot, period):
    half = rot // 2
    inv = jnp.power(jnp.float32(ROPE_THETA), -jnp.arange(0, rot, 2, dtype=F32) / rot)
    ang = jnp.arange(seq, dtype=F32)[:, None] * inv[None, :]
    cos, sin = jnp.cos(ang), jnp.sin(ang)
    lane = np.arange(LANES) % period
    in1 = lane < half
    in2 = (lane >= half) & (lane < 2 * half)
    fidx = np.where(in1, lane, np.where(in2, lane - half, 0))
    cosl, sinl = cos[:, fidx], sin[:, fidx]
    c = jnp.where(jnp.asarray(in1 | in2)[None], cosl, 1.0)
    s1 = jnp.where(jnp.asarray(in1)[None], -sinl, 0.0)
    s2 = jnp.where(jnp.asarray(in2)[None], sinl, 0.0)
    return jnp.concatenate([c, s1, s2], axis=1)


def _rope128(x, tab, half):
    return (x * tab[:, 0:LANES]
            + pltpu.roll(x, LANES - half, 1) * tab[:, LANES:2 * LANES]
            + pltpu.roll(x, half, 1) * tab[:, 2 * LANES:3 * LANES])


def _norm_matmul_kernel(x_ref, g_ref, w_ref, o_ref, h_scr, *, sigmoid):
    @pl.when(pl.program_id(1) == 0)
    def _():
        x = x_ref[...]
        ms = jnp.mean(x * x, axis=-1, keepdims=True)
        h_scr[...] = (x * lax.rsqrt(ms + EPS) * g_ref[...]).astype(BF16)

    z = _dot(h_scr[...], w_ref[...])
    if sigmoid:
        z = jax.nn.sigmoid(z)
    o_ref[...] = z.astype(o_ref.dtype)


def _norm_matmul(x2, g, w, *, out_dtype, sigmoid, tm, tn, name):
    m, d = x2.shape
    n = w.shape[1]
    return pl.pallas_call(
        functools.partial(_norm_matmul_kernel, sigmoid=sigmoid),
        out_shape=jax.ShapeDtypeStruct((m, n), out_dtype),
        grid=(m // tm, n // tn),
        in_specs=[pl.BlockSpec((tm, d), lambda i, j: (i, 0)),
                  pl.BlockSpec((1, d), lambda i, j: (0, 0)),
                  pl.BlockSpec((d, tn), lambda i, j: (0, j))],
        out_specs=pl.BlockSpec((tm, tn), lambda i, j: (i, j)),
        scratch_shapes=[pltpu.VMEM((tm, d), BF16)],
        compiler_params=_cparams(2),
        name=name,
    )(x2, g, w)


PROJ_TILE = 512


def _proj_prep_kernel(x_ref, g_ref, w_ref, tab_ref, gq_ref, gkv_ref, gik_ref,
                      p_ref, t_ref, vt_ref, kc_ref, vc_ref, small_ref, smallt_ref):
    x = x_ref[...]
    ms = jnp.mean(x * x, axis=-1, keepdims=True)
    h = (x * lax.rsqrt(ms + EPS) * g_ref[...]).astype(BF16)
    z_tiles = {}

    def z_cols(off, width):
        t = off // PROJ_TILE
        assert (off + width - 1) // PROJ_TILE == t
        if t not in z_tiles:
            z_tiles[t] = _dot(h, w_ref[:, t * PROJ_TILE:(t + 1) * PROJ_TILE])
        lo = off - t * PROJ_TILE
        return z_tiles[t][:, lo:lo + width]

    def zc(off, c=0):
        return z_cols(off + c * LANES, LANES)

    def tab(kind):
        return tab_ref[:, kind * TAB_W:(kind + 1) * TAB_W]

    def put(off, c, v):
        p_ref[:, off + c * LANES:off + (c + 1) * LANES] = v.astype(BF16)

    def rope(off, c, kind):
        return _rope128(zc(off, c), tab(kind), ROPE_KINDS[kind][0] // 2)

    for zoff, poff, kind in ((Z_AQ, P_AQ, 0), (Z_AK, P_AK, 0), (Z_BQ, P_BQ, 1), (Z_DK, P_DK, 1)):
        for c in range(4):
            put(poff, c, rope(zoff, c, kind))
    for c in range(4):
        put(P_AV, c, zc(Z_AV, c))
    put(P_VS, 0, zc(Z_VS)); put(P_VW, 0, zc(Z_VW))
    put(P_KS, 0, rope(Z_KS, 0, 1)); put(P_KW, 0, rope(Z_KW, 0, 1))
    put(P_KR, 0, rope(Z_KR, 0, 2))
    kc_ref[...] = rope(Z_KC, 0, 1).astype(BF16)
    vc_ref[...] = zc(Z_VC).astype(BF16)

    for zoff, toff, kind in ((Z_DQ, T_DQ, 1), (Z_IQ, T_IQ, 0)):
        for c in range(4):
            t_ref[0, toff + c * LANES:toff + (c + 1) * LANES, :] = rope(zoff, c, kind).T.astype(BF16)
    for c in range(4):
        vt_ref[0, 0, c * LANES:(c + 1) * LANES, :] = zc(Z_DV, c).T.astype(BF16)

    cq = z_cols(Z_CQ, 512)
    ms = jnp.sum(cq * cq, axis=-1, keepdims=True) * (1.0 / MLA_Q_LORA)
    p_ref[:, P_CQ:P_CQ + 512] = (cq * lax.rsqrt(ms + EPS) * gq_ref[...]).astype(BF16)
    ckv = z_cols(Z_CKV, MLA_KV_LORA)
    ms = jnp.mean(ckv * ckv, axis=-1, keepdims=True)
    p_ref[:, P_CKV:P_CKV + MLA_KV_LORA] = (ckv * lax.rsqrt(ms + EPS) * gkv_ref[...]).astype(BF16)

    ik = zc(Z_IK)
    ms = jnp.mean(ik * ik, axis=-1, keepdims=True)
    ikn = ik * lax.rsqrt(ms + EPS) * gik_ref[...]
    put(P_IK, 0, _rope128(ikn, tab(0), ROPE_KINDS[0][0] // 2))

    sm = zc(Z_SMALL)
    lane = lax.broadcasted_iota(I32, sm.shape, 1)
    iw_scale = IDX_HEADS ** -0.5 * IDX_DIM ** -0.5
    small = jnp.where(lane < SMALL_IW, jax.nn.sigmoid(sm), sm * iw_scale)
    small_ref[...] = small
    smallt_ref[0] = small.T


def _proj_prep(x2, g, w, tab, gq, gkv, gik, *, batch, seq, ts):
    m, d = x2.shape
    spb = seq // ts
    assert ts == VT_TILE and w.shape == (d, Z_WIDTH)
    row = lambda w: pl.BlockSpec((ts, w), lambda i: (i, 0))
    return pl.pallas_call(
        _proj_prep_kernel,
        out_shape=(jax.ShapeDtypeStruct((m, P_WIDTH), BF16),
                   jax.ShapeDtypeStruct((batch, T_ROWS, seq), BF16),
                   jax.ShapeDtypeStruct((batch, spb, BR_WIDTH, VT_TILE), BF16),
                   jax.ShapeDtypeStruct((m, LANES), BF16),
                   jax.ShapeDtypeStruct((m, LANES), BF16),
                   jax.ShapeDtypeStruct((m, LANES), F32),
                   jax.ShapeDtypeStruct((batch, LANES, seq), F32)),
        grid=(m // ts,),
        in_specs=[row(d),
                  pl.BlockSpec((1, d), lambda i: (0, 0)),
                  pl.BlockSpec((d, Z_WIDTH), lambda i: (0, 0)),
                  pl.BlockSpec((ts, 3 * TAB_W), lambda i: (i % spb, 0)),
                  pl.BlockSpec((1, 512), lambda i: (0, 0)),
                  pl.BlockSpec((1, MLA_KV_LORA), lambda i: (0, 0)),
                  pl.BlockSpec((1, LANES), lambda i: (0, 0))],
        out_specs=(row(P_WIDTH),
                   pl.BlockSpec((1, T_ROWS, ts), lambda i: (i // spb, 0, i % spb)),
                   pl.BlockSpec((1, 1, BR_WIDTH, VT_TILE), lambda i: (i // spb, i % spb, 0, 0)),
                   row(LANES), row(LANES), row(LANES),
                   pl.BlockSpec((1, LANES, ts), lambda i: (i // spb, 0, i % spb))),
        compiler_params=_cparams(1),
        name="proj_prep",
    )(x2, g, w, tab, gq, gkv, gik)


def _softmax_init(mx_scr, l_scr, acc_scr):
    mx_scr[...] = jnp.full(mx_scr.shape, NEG, F32)
    l_scr[...] = jnp.zeros(l_scr.shape, F32)
    acc_scr[...] = jnp.zeros(acc_scr.shape, F32)


def _score_store(g, j, s, s_scr, mx_scr):
    s_scr[g, j] = s
    m = s[:, 0:LANES]
    for c in range(1, s.shape[1] // LANES):
        m = jnp.maximum(m, s[:, c * LANES:(c + 1) * LANES])
    mx_scr[g] = jnp.maximum(mx_scr[g], m)


def _row_max_finish(mx_scr):
    for g in range(mx_scr.shape[0]):
        m = jnp.max(mx_scr[g], axis=-1, keepdims=True)
        mx_scr[g] = jnp.broadcast_to(m, mx_scr.shape[1:])


def _prob_accumulate(g, j, v_tile, s_scr, mx_scr, l_scr, acc_scr):
    mb = mx_scr[g]
    s = s_scr[g, j]
    ps = [jnp.exp2(s[:, c * LANES:(c + 1) * LANES] - mb) for c in range(s.shape[1] // LANES)]
    tot = ps[0]
    for p in ps[1:]:
        tot = tot + p
    l_scr[g] += tot
    acc_scr[g] += _dot(jnp.concatenate(ps, axis=1).astype(BF16), v_tile)


def _softmax_out(g, l_scr, acc_scr):
    return acc_scr[g] / jnp.sum(l_scr[g], axis=-1, keepdims=True)


def _paired_tiles(n, step):
    def pair(jj, carry):
        step(2 * jj)
        step(2 * jj + 1)
        return carry

    lax.fori_loop(0, n // 2, pair, 0)

    @pl.when(n % 2 == 1)
    def _():
        step(n - 1)


def _causal_tiles(step, n_full):
    _paired_tiles(n_full, lambda j: step(j, False))
    step(n_full, True)


def _softmax_scratch(groups, n_tiles, rows, tk):
    return [pltpu.VMEM((groups, n_tiles, rows, tk), F32),
            pltpu.VMEM((groups, rows, LANES), F32),
            pltpu.VMEM((groups, rows, LANES), F32),
            pltpu.VMEM((groups, rows, HEAD_W), F32)]


def _diff_attn_kernel(q_ref, k_ref, v_ref, lv_ref, g_ref, o_ref, s_scr, mx_scr, l_scr, acc_scr,
                      *, tq, tk, lam_init):
    qs = pl.program_id(1) * tq
    n_full = qs // tk
    scale = DA_DIM ** -0.5 * LOG2E
    lv = lv_ref[...]
    lam = (jnp.exp(jnp.sum(lv[0:1] * lv[1:2], axis=-1, keepdims=True))
           - jnp.exp(jnp.sum(lv[2:3] * lv[3:4], axis=-1, keepdims=True)) + lam_init)
    lane = lax.broadcasted_iota(I32, (tq, HEAD_W), 1)
    row_t = qs + lax.broadcasted_iota(I32, (2 * tq, 1), 0) % tq
    col0 = lax.broadcasted_iota(I32, (2 * tq, tk), 1)
    _softmax_init(mx_scr, l_scr, acc_scr)

    def scores(j, masked):
        ks = pl.multiple_of(j * tk, tk)
        for h in range(HEADS):
            hs = slice(h * HEAD_W, (h + 1) * HEAD_W)
            qh = q_ref[0, :, hs]
            zero = jnp.zeros_like(qh)
            q2 = jnp.concatenate([jnp.where(lane < DA_DIM, qh, zero),
                                  jnp.where(lane >= DA_DIM, qh, zero)], axis=0)
            s = _dot_nt(q2, k_ref[0, pl.ds(ks, tk), hs]) * scale
            if masked:
                s = jnp.where(col0 + ks <= row_t, s, NEG)
            _score_store(h, j, s, s_scr, mx_scr)

    _causal_tiles(scores, n_full)
    _row_max_finish(mx_scr)

    def probs(j, carry):
        ks = pl.multiple_of(j * tk, tk)
        for h in range(HEADS):
            v_tile = v_ref[0, pl.ds(ks, tk), h * HEAD_W:(h + 1) * HEAD_W]
            _prob_accumulate(h, j, v_tile, s_scr, mx_scr, l_scr, acc_scr)
        return carry

    _paired_tiles(n_full + 1, lambda j: probs(j, 0))
    for h in range(HEADS):
        o2 = _softmax_out(h, l_scr, acc_scr)
        o = o2[:tq] - lam * o2[tq:]
        ms = jnp.mean(o * o, axis=-1, keepdims=True)
        o = o * lax.rsqrt(ms + EPS) * g_ref[...]
        o_ref[0, :, h * HEAD_W:(h + 1) * HEAD_W] = (o * (1.0 - lam_init)).astype(BF16)


def _diff_attn(p3, lv, g, *, lam_init, tq, tk):
    b, s, _ = p3.shape
    return pl.pallas_call(
        functools.partial(_diff_attn_kernel, tq=tq, tk=tk, lam_init=lam_init),
        out_shape=jax.ShapeDtypeStruct((b, s, BR_WIDTH), BF16),
        grid=(b, s // tq),
        in_specs=[pl.BlockSpec((1, tq, 512), lambda bi, i: (bi, i, P_AQ // 512)),
                  pl.BlockSpec((1, s, 512), lambda bi, i: (bi, 0, P_AK // 512)),
                  pl.BlockSpec((1, s, 512), lambda bi, i: (bi, 0, P_AV // 512)),
                  pl.BlockSpec((4, DA_DIM), lambda bi, i: (0, 0)),
                  pl.BlockSpec((1, HEAD_W), lambda bi, i: (0, 0))],
        out_specs=pl.BlockSpec((1, tq, BR_WIDTH), lambda bi, i: (bi, i, 0)),
        scratch_shapes=_softmax_scratch(HEADS, s // tk, 2 * tq, tk),
        compiler_params=_cparams(2),
        name="diff_attn",
    )(p3, p3, p3, lv, g)


def _nsa_compress_kernel(gk_ref, gv_ref, w1k_ref, w1v_ref, pek_ref, pev_ref,
                         w1kf_ref, w1vf_ref, w2k_ref, w2v_ref, kc_ref, vc_ref):
    def one(g_ref, w1cat_ref, pe_ref, w1f_ref, w2_ref, o_ref):
        y = _dot(g_ref[0], w1cat_ref[...])
        n = y.shape[0]
        nxt = pltpu.roll(y[:, HEAD_W:], n - 1, 0)
        c = _dot(pe_ref[...], w1f_ref[...])[0:1]
        hid = jax.nn.gelu(y[:, :HEAD_W] + nxt + c)
        o_ref[0] = _dot(hid.astype(BF16), w2_ref[...]).astype(BF16)

    one(gk_ref, w1k_ref, pek_ref, w1kf_ref, w2k_ref, kc_ref)
    one(gv_ref, w1v_ref, pev_ref, w1vf_ref, w2v_ref, vc_ref)


def _nsa_compress(gk, gv, w1k_cat, w1v_cat, pek, pev, w1k, w1v, w2k, w2v):
    b, ng, gw = gk.shape
    full = lambda shape: pl.BlockSpec(shape, lambda bi: (0,) * len(shape))
    return pl.pallas_call(
        _nsa_compress_kernel,
        out_shape=(jax.ShapeDtypeStruct((b, ng, HEAD_W), BF16),
                   jax.ShapeDtypeStruct((b, ng, HEAD_W), BF16)),
        grid=(b,),
        in_specs=[pl.BlockSpec((1, ng, gw), lambda bi: (bi, 0, 0)),
                  pl.BlockSpec((1, ng, gw), lambda bi: (bi, 0, 0)),
                  full(w1k_cat.shape), full(w1v_cat.shape), full(pek.shape), full(pev.shape),
                  full(w1k.shape), full(w1v.shape), full(w2k.shape), full(w2v.shape)],
        out_specs=(pl.BlockSpec((1, ng, HEAD_W), lambda bi: (bi, 0, 0)),
                   pl.BlockSpec((1, ng, HEAD_W), lambda bi: (bi, 0, 0))),
        compiler_params=_cparams(1),
        name="nsa_compress",
    )(gk, gv, w1k_cat, w1v_cat, pek, pev, w1k, w1v, w2k, w2v)


NSA_GROUPS = 2


def _nsa_kernel(q_ref, kc_ref, vc_ref, ks_ref, vs_ref, kw_ref, vw_ref, small_ref, ov_ref, e_ref,
                o_ref, s_scr, mx_scr, l_scr, acc_scr, cmp_scr, win_scr, *, tq, tk, seq):
    qs = pl.program_id(1) * tq
    scale = NSA_DK ** -0.5
    ns = seq // SEL_LEN
    n_sel = min(SEL_N, ns)
    r = HEADS * tq
    rg = r // NSA_GROUPS
    q4 = jnp.concatenate([q_ref[0, :, h * HEAD_W:(h + 1) * HEAD_W] for h in range(HEADS)], axis=0)
    t1 = qs + lax.broadcasted_iota(I32, (tq, 1), 0)
    t4 = qs + lax.broadcasted_iota(I32, (r, 1), 0) % tq

    wspan = WIN + tq
    start = pl.multiple_of(jnp.maximum(qs - WIN, 0), tq)
    sw = _dot_nt(q4, kw_ref[0, pl.ds(start, wspan), :]) * scale
    dist = t4 - (start + lax.broadcasted_iota(I32, (r, wspan), 1))
    sw = jnp.where(pltpu.bitcast(dist, jnp.uint32) < jnp.uint32(WIN), sw, NEG)
    e = jnp.exp(sw - jnp.max(sw, axis=-1, keepdims=True))
    pw = e / jnp.sum(e, axis=-1, keepdims=True)
    win_scr[...] = _dot(pw.astype(BF16), vw_ref[0, pl.ds(start, wspan), :])

    kc = kc_ref[0]
    nc_pad = kc.shape[0]
    sc = _dot_nt(q4, kc) * scale
    c_end = lax.broadcasted_iota(I32, (r, nc_pad), 1) * CMP_STRIDE + (CMP_LEN - 1)
    cmask = c_end <= t4
    mx = jnp.max(jnp.where(cmask, sc, NEG), axis=-1, keepdims=True)
    e = jnp.where(cmask, jnp.exp(sc - mx), 0.0)
    den = jnp.sum(e, axis=-1, keepdims=True)
    pc = e / jnp.where(den > 0.0, den, 1.0)
    cmp_scr[...] = _dot(pc.astype(BF16), vc_ref[0])

    psum = pc[0:tq] + pc[tq:2 * tq] + pc[2 * tq:3 * tq] + pc[3 * tq:4 * tq]
    ov = ov_ref[...]
    hi = psum.astype(BF16)
    r1 = psum - hi.astype(F32)
    mid = r1.astype(BF16)
    lo = (r1 - mid.astype(F32)).astype(BF16)
    imp = _dot(hi, ov) + _dot(mid, ov) + _dot(lo, ov)

    blk = lax.broadcasted_iota(I32, (tq, LANES), 1)
    cur = t1 // SEL_LEN
    forced = (blk == 0) | (blk == cur) | (blk == cur - 1)
    visible = blk * SEL_LEN <= t1
    score = jnp.where(visible, jnp.where(forced, FORCE_SCORE, imp), NEG)
    score = jnp.where(blk < ns, score, PAD_SCORE)
    ns_pad = -(-ns // SUBLANES) * SUBLANES
    score_t = score.T[:ns_pad]
    blk_t = lax.broadcasted_iota(I32, (ns_pad, tq), 0)
    rank = jnp.zeros((ns_pad, tq), I32)
    for jp in range(ns):
        row = score_t[jp:jp + 1, :]
        later = (blk_t > jp).astype(I32)
        rank = rank + jnp.where(row > score_t, 1, jnp.where(row == score_t, later, 0))
    sel_t = jnp.where(rank < n_sel, 1.0, 0.0)
    if ns_pad < LANES:
        sel_t = jnp.concatenate([sel_t, jnp.zeros((LANES - ns_pad, tq), F32)], axis=0)
    selb = sel_t.T.astype(BF16)

    _softmax_init(mx_scr, l_scr, acc_scr)
    col0 = lax.broadcasted_iota(I32, (rg, tk), 1)
    tg = qs + lax.broadcasted_iota(I32, (rg, 1), 0) % tq
    n_tiles = qs // tk + 1

    def scores(j, masked):
        ks0 = pl.multiple_of(j * tk, tk)
        mt = _dot(selb, e_ref[j])
        mg = jnp.concatenate([mt] * (rg // tq), axis=0)
        k_tile = ks_ref[0, pl.ds(ks0, tk), :]
        for g in range(NSA_GROUPS):
            s = _dot_nt(q4[g * rg:(g + 1) * rg], k_tile) * (scale * LOG2E)
            s = jnp.where(mg > 0.5, s, NEG)
            if masked:
                s = jnp.where(col0 + ks0 <= tg, s, NEG)
            _score_store(g, j, s, s_scr, mx_scr)

    _causal_tiles(scores, n_tiles - 1)
    _row_max_finish(mx_scr)

    def probs(j, carry):
        ks0 = pl.multiple_of(j * tk, tk)
        v_tile = vs_ref[0, pl.ds(ks0, tk), :]
        for g in range(NSA_GROUPS):
            _prob_accumulate(g, j, v_tile, s_scr, mx_scr, l_scr, acc_scr)
        return carry

    _paired_tiles(n_tiles, lambda j: probs(j, 0))
    o_slc = jnp.concatenate([_softmax_out(g, l_scr, acc_scr) for g in range(NSA_GROUPS)], axis=0)

    gates = small_ref[0]
    for h in range(HEADS):
        rows = slice(h * tq, (h + 1) * tq)
        g0 = gates[:, SMALL_G + 3 * h:SMALL_G + 3 * h + 1]
        g1 = gates[:, SMALL_G + 3 * h + 1:SMALL_G + 3 * h + 2]
        g2 = gates[:, SMALL_G + 3 * h + 2:SMALL_G + 3 * h + 3]
        o = g0 * cmp_scr[rows, :] + g1 * o_slc[rows] + g2 * win_scr[rows, :]
        o_ref[0, :, h * HEAD_W:(h + 1) * HEAD_W] = o.astype(BF16)


def _nsa(p3, kc, vc, small3, ov, emat, *, tq, tk):
    b, s, _ = p3.shape
    ng = kc.shape[1]
    col = lambda off: (lambda bi, i: (bi, 0, off // LANES))
    return pl.pallas_call(
        functools.partial(_nsa_kernel, tq=tq, tk=tk, seq=s),
        out_shape=jax.ShapeDtypeStruct((b, s, BR_WIDTH), BF16),
        grid=(b, s // tq),
        in_specs=[pl.BlockSpec((1, tq, 512), lambda bi, i: (bi, i, P_BQ // 512)),
                  pl.BlockSpec((1, ng, HEAD_W), lambda bi, i: (bi, 0, 0)),
                  pl.BlockSpec((1, ng, HEAD_W), lambda bi, i: (bi, 0, 0)),
                  pl.BlockSpec((1, s, LANES), col(P_KS)),
                  pl.BlockSpec((1, s, LANES), col(P_VS)),
                  pl.BlockSpec((1, s, LANES), col(P_KW)),
                  pl.BlockSpec((1, s, LANES), col(P_VW)),
                  pl.BlockSpec((1, tq, LANES), lambda bi, i: (bi, i, 0)),
                  pl.BlockSpec(ov.shape, lambda bi, i: (0, 0)),
                  pl.BlockSpec(emat.shape, lambda bi, i: (0, 0, 0))],
        out_specs=pl.BlockSpec((1, tq, BR_WIDTH), lambda bi, i: (bi, i, 0)),
        scratch_shapes=(_softmax_scratch(NSA_GROUPS, s // tk, HEADS * tq // NSA_GROUPS, tk)
                        + [pltpu.VMEM((HEADS * tq, HEAD_W), F32), pltpu.VMEM((HEADS * tq, HEAD_W), F32)]),
        compiler_params=_cparams(2),
        name="nsa_attn",
    )(p3, kc, vc, p3, p3, p3, p3, small3, ov, emat)


def _mla_up_kernel(p_ref, ckv_ref, tab_ref, wq_ref, wkv_ref, q_ref, kv_ref):
    q = _dot(p_ref[...], wq_ref[...])
    nn = HEADS * MLA_NOPE
    q_ref[:, :nn] = q[:, :nn].astype(BF16)
    for c in range(nn // LANES, (nn + HEADS * MLA_ROPE) // LANES):
        tile = _rope128(q[:, c * LANES:(c + 1) * LANES], tab_ref[...], MLA_ROPE // 2)
        q_ref[:, c * LANES:(c + 1) * LANES] = tile.astype(BF16)
    kv_ref[...] = _dot(ckv_ref[...], wkv_ref[...]).astype(BF16)


def _mla_up(p2, tab, wq, wkv, *, seq, ts):
    m = p2.shape[0]
    spb = seq // ts
    nq = wq.shape[1]
    nkv = wkv.shape[1]
    return pl.pallas_call(
        _mla_up_kernel,
        out_shape=(jax.ShapeDtypeStruct((m, nq), BF16), jax.ShapeDtypeStruct((m, nkv), BF16)),
        grid=(m // ts,),
        in_specs=[pl.BlockSpec((ts, 512), lambda i: (i, P_CQ // 512)),
                  pl.BlockSpec((ts, MLA_KV_LORA), lambda i: (i, P_CKV // MLA_KV_LORA)),
                  pl.BlockSpec((ts, TAB_W), lambda i: (i % spb, 2)),
                  pl.BlockSpec(wq.shape, lambda i: (0, 0)),
                  pl.BlockSpec(wkv.shape, lambda i: (0, 0))],
        out_specs=(pl.BlockSpec((ts, nq), lambda i: (i, 0)),
                   pl.BlockSpec((ts, nkv), lambda i: (i, 0))),
        compiler_params=_cparams(1),
        name="mla_up",
    )(p2, p2, tab, wq, wkv)


def _mla_attn_kernel(qn_ref, qr_ref, kn_ref, kr_ref, v_ref, o_ref, s_scr, mx_scr, l_scr, acc_scr,
                     *, tq, tk):
    qs = pl.program_id(1) * tq
    n_full = qs // tk
    scale = (MLA_NOPE + MLA_ROPE) ** -0.5 * LOG2E
    lane = lax.broadcasted_iota(I32, (tq, LANES), 1)
    row_t = qs + lax.broadcasted_iota(I32, (tq, 1), 0)
    col0 = lax.broadcasted_iota(I32, (tq, tk), 1)
    _softmax_init(mx_scr, l_scr, acc_scr)

    def scores(j, masked):
        ks = pl.multiple_of(j * tk, tk)
        kr_tile = kr_ref[0, pl.ds(ks, tk), :]
        for h in range(HEADS):
            hs = slice(h * HEAD_W, (h + 1) * HEAD_W)
            pair = qr_ref[0, :, (h // 2) * LANES:(h // 2 + 1) * LANES]
            keep = (lane < MLA_ROPE) if h % 2 == 0 else (lane >= MLA_ROPE)
            qr = jnp.where(keep, pair, jnp.zeros_like(pair))
            s = _dot_nt(jnp.concatenate([qn_ref[0, :, hs], qr], axis=1),
                        jnp.concatenate([kn_ref[0, pl.ds(ks, tk), hs], kr_tile], axis=1)) * scale
            if masked:
                s = jnp.where(col0 + ks <= row_t, s, NEG)
            _score_store(h, j, s, s_scr, mx_scr)

    _causal_tiles(scores, n_full)
    _row_max_finish(mx_scr)

    def probs(j, carry):
        ks = pl.multiple_of(j * tk, tk)
        for h in range(HEADS):
            v_tile = v_ref[0, pl.ds(ks, tk), h * HEAD_W:(h + 1) * HEAD_W]
            _prob_accumulate(h, j, v_tile, s_scr, mx_scr, l_scr, acc_scr)
        return carry

    _paired_tiles(n_full + 1, lambda j: probs(j, 0))
    for h in range(HEADS):
        o_ref[0, :, h * HEAD_W:(h + 1) * HEAD_W] = _softmax_out(h, l_scr, acc_scr).astype(BF16)


def _mla_attn(q3, kv3, p3, *, tq, tk):
    b, s, _ = q3.shape
    return pl.pallas_call(
        functools.partial(_mla_attn_kernel, tq=tq, tk=tk),
        out_shape=jax.ShapeDtypeStruct((b, s, BR_WIDTH), BF16),
        grid=(b, s // tq),
        in_specs=[pl.BlockSpec((1, tq, 512), lambda bi, i: (bi, i, 0)),
                  pl.BlockSpec((1, tq, 256), lambda bi, i: (bi, i, 2)),
                  pl.BlockSpec((1, s, 512), lambda bi, i: (bi, 0, 0)),
                  pl.BlockSpec((1, s, LANES), lambda bi, i: (bi, 0, P_KR // LANES)),
                  pl.BlockSpec((1, s, 512), lambda bi, i: (bi, 0, 1))],
        out_specs=pl.BlockSpec((1, tq, BR_WIDTH), lambda bi, i: (bi, i, 0)),
        scratch_shapes=_softmax_scratch(HEADS, s // tk, tq, tk),
        compiler_params=_cparams(2),
        name="mla_attn",
    )(q3, q3, kv3, p3, kv3)


def _sortable_key(x):
    bits = pltpu.bitcast(x + 0.0, I32)
    return bits ^ (lax.shift_right_arithmetic(bits, 31) & 0x7FFFFFFF)


def _fold_rows(x, op):
    n = x.shape[0] // SUBLANES
    return op(x.reshape(n, SUBLANES, x.shape[1]), axis=0)


def _count16(half_scr, n_tiles, pred, tq):
    def count_tile(j, cnt):
        hit = pred(half_scr[j]).astype(I16)
        parts = [hit[r:r + PACKED_ROWS] for r in range(0, hit.shape[0], PACKED_ROWS)]
        while len(parts) > 1:
            parts = [a + b for a, b in zip(parts[0::2], parts[1::2])]
        return cnt + parts[0]

    cnt = lax.fori_loop(0, n_tiles, count_tile, jnp.zeros((PACKED_ROWS, tq), I16))
    return jnp.sum(cnt.astype(I32), axis=0, keepdims=True)


def _bisect16(half_scr, n_tiles, need, tq):
    def bit_body(i, th):
        cand = th + lax.shift_left(jnp.int32(1), 15 - i)
        c16 = cand.astype(I16)
        total = _count16(half_scr, n_tiles, lambda x: x >= c16, tq)
        return jnp.where(total >= need, cand, th)

    return lax.fori_loop(0, 16, bit_body, jnp.full((1, tq), HALF_MIN, I32))


def _dsa_kernel(qt_ref, iqt_ref, iwt_ref, k_ref, ik_ref, vt_ref, tri_ref, o_ref,
                key_scr, half_scr, s_scr, mx_scr, l_scr, acc_scr, *, tq, tk, top):
    qs = pl.program_id(1) * tq
    n_tiles = (qs + tq - 1) // tk + 1
    scale = DSA_DIM ** -0.5 * LOG2E
    t_lane = qs + lax.broadcasted_iota(I32, (tk, tq), 1)
    krow0 = lax.broadcasted_iota(I32, (tk, tq), 0)
    half_rows = lax.broadcasted_iota(I32, (LANES, tq), 0) < IDX_DIM
    vt_per_tile = tk // VT_TILE

    def score_tile(j, carry):
        ks = pl.multiple_of(j * tk, tk)
        ikt = ik_ref[0, pl.ds(ks, tk), :]
        acc = jnp.zeros((tk, tq), F32)
        for h in range(IDX_HEADS):
            pair = iqt_ref[0, (h // 2) * LANES:(h // 2 + 1) * LANES, :]
            keep = half_rows if h % 2 == 0 else jnp.logical_not(half_rows)
            iq_h = jnp.where(keep, pair, jnp.zeros_like(pair))
            w_h = iwt_ref[0, SMALL_IW + h:SMALL_IW + h + 1, :]
            acc = acc + w_h * jnp.maximum(_dot(ikt, iq_h), 0.0)
        key = jnp.where(krow0 + ks <= t_lane, _sortable_key(acc), INT_MIN)
        key_scr[j] = key
        half_scr[j] = lax.shift_right_arithmetic(key, 16).astype(I16)
        return carry

    _paired_tiles(n_tiles, lambda j: score_tile(j, 0))

    th_hi = _bisect16(half_scr, n_tiles, top, tq)
    hi16 = th_hi.astype(I16)
    need_lo = top - _count16(half_scr, n_tiles, lambda x: x > hi16, tq)

    def low_tile(j, carry):
        key = key_scr[j]
        lo = (key & 0xFFFF) + HALF_MIN
        same_hi = lax.shift_right_arithmetic(key, 16) == th_hi
        half_scr[j] = jnp.where(same_hi, lo, HALF_MIN).astype(I16)
        return carry

    lax.fori_loop(0, n_tiles, low_tile, 0)
    th_lo = _bisect16(half_scr, n_tiles, need_lo, tq)
    theta = lax.shift_left(th_hi, 16) + (th_lo - HALF_MIN)

    def count32(pred):
        def body(j, cnt):
            return cnt + _fold_rows(pred(key_scr[j], krow0 + j * tk).astype(I32), jnp.sum)

        cnt = lax.fori_loop(0, n_tiles, body, jnp.zeros((SUBLANES, tq), I32))
        return jnp.sum(cnt, axis=0, keepdims=True)

    tied = (count32(lambda k, pos: k >= theta) > top) & (theta > INT_MIN)

    @pl.when(jnp.max(tied.astype(I32)) > 0)
    def _():
        need_eq = (top - count32(lambda k, pos: k > theta)).astype(F32)

        def demote(j, seen):
            k = key_scr[j]
            eq = k == theta
            eqf = jnp.where(eq, 1.0, 0.0)
            before = _dot(tri_ref[...], eqf.astype(BF16)) + seen
            key_scr[j] = jnp.where(eq & (before >= need_eq) & tied, k - 1, k)
            return seen + jnp.sum(_fold_rows(eqf, jnp.sum), axis=0, keepdims=True)

        lax.fori_loop(0, n_tiles, demote, jnp.zeros((1, tq), F32))

    theta = jnp.maximum(theta, INT_MIN + 1)

    mx_scr[...] = jnp.full(mx_scr.shape, NEG, F32)
    l_scr[...] = jnp.zeros(l_scr.shape, F32)
    acc_scr[...] = jnp.zeros(acc_scr.shape, F32)

    def scores(j, carry):
        ks = pl.multiple_of(j * tk, tk)
        sel = key_scr[j] >= theta
        for h in range(HEADS):
            hs = slice(h * HEAD_W, (h + 1) * HEAD_W)
            s = _dot(k_ref[0, pl.ds(ks, tk), hs], qt_ref[0, hs, :]) * scale
            s = jnp.where(sel, s, NEG)
            s_scr[h, j] = s
            mx_scr[h] = jnp.maximum(mx_scr[h], _fold_rows(s, jnp.max))
        return carry

    _paired_tiles(n_tiles, lambda j: scores(j, 0))
    for h in range(HEADS):
        m = jnp.max(mx_scr[h], axis=0, keepdims=True)
        mx_scr[h] = jnp.broadcast_to(m, (SUBLANES, tq))

    def probs(j, carry):
        for h in range(HEADS):
            p = jnp.exp2(s_scr[h, j] - mx_scr[h][0:1])
            l_scr[h] += _fold_rows(p, jnp.sum)
            pb = p.astype(BF16)
            for c in range(vt_per_tile):
                vt = vt_ref[0, j * vt_per_tile + c, h * HEAD_W:(h + 1) * HEAD_W, :]
                acc_scr[h] += _dot(vt, pb[c * VT_TILE:(c + 1) * VT_TILE])
        return carry

    _paired_tiles(n_tiles, lambda j: probs(j, 0))
    for h in range(HEADS):
        ot = acc_scr[h] / jnp.sum(l_scr[h], axis=0, keepdims=True)
        o_ref[0, :, h * HEAD_W:(h + 1) * HEAD_W] = ot.T.astype(BF16)


def _dsa(p3, t3, vt4, smallt, *, tq, tk, top):
    b, s, _ = p3.shape
    n_vt = vt4.shape[1]
    tri = jnp.asarray(np.tril(np.ones((tk, tk), np.float32), -1), BF16)
    return pl.pallas_call(
        functools.partial(_dsa_kernel, tq=tq, tk=tk, top=top),
        out_shape=jax.ShapeDtypeStruct((b, s, BR_WIDTH), BF16),
        grid=(b, s // tq),
        in_specs=[pl.BlockSpec((1, 512, tq), lambda bi, i: (bi, T_DQ // 512, i)),
                  pl.BlockSpec((1, 512, tq), lambda bi, i: (bi, T_IQ // 512, i)),
                  pl.BlockSpec((1, LANES, tq), lambda bi, i: (bi, 0, i)),
                  pl.BlockSpec((1, s, 512), lambda bi, i: (bi, 0, P_DK // 512)),
                  pl.BlockSpec((1, s, LANES), lambda bi, i: (bi, 0, P_IK // LANES)),
                  pl.BlockSpec((1, n_vt, BR_WIDTH, VT_TILE), lambda bi, i: (bi, 0, 0, 0)),
                  pl.BlockSpec((tk, tk), lambda bi, i: (0, 0))],
        out_specs=pl.BlockSpec((1, tq, BR_WIDTH), lambda bi, i: (bi, i, 0)),
        scratch_shapes=[pltpu.VMEM((s // tk, tk, tq), I32),
                        pltpu.VMEM((s // tk, tk, tq), I16),
                        pltpu.VMEM((HEADS, s // tk, tk, tq), F32),
                        pltpu.VMEM((HEADS, SUBLANES, tq), F32),
                        pltpu.VMEM((HEADS, SUBLANES, tq), F32),
                        pltpu.VMEM((HEADS, HEAD_W, tq), F32)],
        compiler_params=_cparams(2),
        name="dsa_attn",
    )(t3, t3, smallt, p3, p3, vt4, tri)


def _merge_kernel(x_ref, oa_ref, ob_ref, oc_ref, od_ref, g_ref, wb_ref, wo_ref, o_ref):
    d = x_ref.shape[1]
    acc = jnp.zeros(x_ref.shape, F32)
    for n, br_ref in enumerate((oa_ref, ob_ref, oc_ref, od_ref)):
        br = _dot(br_ref[...], wb_ref[n])
        acc = acc + g_ref[:, n * d:(n + 1) * d].astype(F32) * br
    o_ref[...] = x_ref[...] + _dot(acc.astype(BF16), wo_ref[...])


def _merge(x2, oa, ob, oc, od, gates, wb, wo, *, tm):
    m, d = x2.shape
    row = lambda w: pl.BlockSpec((tm, w), lambda i: (i, 0))
    return pl.pallas_call(
        _merge_kernel,
        out_shape=jax.ShapeDtypeStruct((m, d), F32),
        grid=(m // tm,),
        in_specs=[row(d), row(BR_WIDTH), row(BR_WIDTH), row(BR_WIDTH), row(BR_WIDTH),
                  row(gates.shape[1]),
                  pl.BlockSpec(wb.shape, lambda i: (0, 0, 0)),
                  pl.BlockSpec(wo.shape, lambda i: (0, 0))],
        out_specs=row(d),
        compiler_params=_cparams(1),
        name="merge",
    )(x2, oa, ob, oc, od, gates, wb, wo)


def _ffn_kernel(x_ref, g_ref, wg_ref, wu_ref, wd_ref, gf_ref, o_ref, h_scr, acc_scr, *, final):
    j = pl.program_id(1)

    @pl.when(j == 0)
    def _():
        x = x_ref[...]
        ms = jnp.mean(x * x, axis=-1, keepdims=True)
        h_scr[...] = (x * lax.rsqrt(ms + EPS) * g_ref[...]).astype(BF16)
        acc_scr[...] = jnp.zeros(acc_scr.shape, F32)

    h = h_scr[...]
    a = jax.nn.silu(_dot(h, wg_ref[...])) * _dot(h, wu_ref[...])
    acc_scr[...] += _dot(a.astype(BF16), wd_ref[...])

    @pl.when(j == pl.num_programs(1) - 1)
    def _():
        y = x_ref[...] + acc_scr[...]
        if final:
            ms = jnp.mean(y * y, axis=-1, keepdims=True)
            y = y * lax.rsqrt(ms + EPS) * gf_ref[...]
        o_ref[...] = y


def _ffn(x2, g, wg, wu, wd, gf, *, final, tm, tf):
    m, d = x2.shape
    dff = wg.shape[1]
    return pl.pallas_call(
        functools.partial(_ffn_kernel, final=final),
        out_shape=jax.ShapeDtypeStruct((m, d), F32),
        grid=(m // tm, dff // tf),
        in_specs=[pl.BlockSpec((tm, d), lambda i, j: (i, 0)),
                  pl.BlockSpec((1, d), lambda i, j: (0, 0)),
                  pl.BlockSpec((d, tf), lambda i, j: (0, j)),
                  pl.BlockSpec((d, tf), lambda i, j: (0, j)),
                  pl.BlockSpec((tf, d), lambda i, j: (j, 0)),
                  pl.BlockSpec((1, d), lambda i, j: (0, 0))],
        out_specs=pl.BlockSpec((tm, d), lambda i, j: (i, 0)),
        scratch_shapes=[pltpu.VMEM((tm, d), BF16), pltpu.VMEM((tm, d), F32)],
        compiler_params=_cparams(2),
        name="ffn",
    )(x2, g, wg, wu, wd, gf)


def _tiles(seq, m, dff):
    pick = lambda n, cands: next(c for c in cands if n % c == 0)
    tk = pick(seq, (512, 256))
    return dict(
        proj_tm=pick(m, (1024, 512, 256, 128)), proj_tn=1024,
        prep_ts=VT_TILE,
        diff_tq=128, mla_tq=pick(seq, (256, 128)), nsa_tq=128, dsa_tq=256, tk=tk,
        row_tm=pick(m, (512, 256, 128)),
        ffn_tf=pick(dff, (1408, 704, 256, 128)),
    )


def kernel(x, norm1_g, w_in, diff_lq1, diff_lk1, diff_lq2, diff_lk2, diff_subln_g, nsa_pe_k, nsa_w1_k, nsa_w2_k, nsa_pe_v, nsa_w1_v, nsa_w2_v, mla_q_norm_g, mla_w_uq, mla_kv_norm_g, mla_w_ukv, idx_k_norm_g, w_branch, w_out, norm2_g, w_gate_up, w_down, final_norm_g):
    b, seq, d = x.shape
    depth = w_in.shape[0]
    m = b * seq
    dff = w_down.shape[1]
    t = _tiles(seq, m, dff)
    tk = t["tk"]
    assert seq % SEL_LEN == 0 and seq >= WIN + t["nsa_tq"] and seq // SEL_LEN <= LANES
    assert seq % t["dsa_tq"] == 0 and tk % VT_TILE == 0 and tk >= min(IDX_TOPK, seq // 4)

    col_idx, gate_off, d_in = _in_proj_columns()
    assert w_in.shape[2] == d_in
    tab = jnp.concatenate([_rope_table(seq, rot, per) for rot, per in ROPE_KINDS], axis=1)

    ng = seq // CMP_STRIDE
    ns = seq // SEL_LEN
    c_start = np.arange(ng)[:, None] * CMP_STRIDE
    s_start = np.arange(LANES)[None, :] * SEL_LEN
    ov = ((c_start < s_start + SEL_LEN) & (c_start + CMP_LEN - 1 >= s_start)
          & (np.arange(LANES)[None, :] < ns))
    ov = jnp.asarray(ov, BF16)
    emat = np.arange(LANES)[:, None] == (np.arange(seq)[None, :] // SEL_LEN)
    emat = jnp.asarray(emat.reshape(LANES, seq // tk, tk).transpose(1, 0, 2), BF16)

    qd = MLA_NOPE + MLA_ROPE
    uq_idx = np.concatenate([np.concatenate([np.arange(h * qd, h * qd + MLA_NOPE) for h in range(HEADS)]),
                             np.concatenate([np.arange(h * qd + MLA_NOPE, (h + 1) * qd) for h in range(HEADS)])])
    kvd = MLA_NOPE + HEAD_W
    ukv_idx = np.concatenate([np.concatenate([np.arange(h * kvd, h * kvd + MLA_NOPE) for h in range(HEADS)]),
                              np.concatenate([np.arange(h * kvd + MLA_NOPE, (h + 1) * kvd) for h in range(HEADS)])])

    x2 = x.reshape(m, d)
    half_w1 = CMP_STRIDE * NSA_DK
    for l in range(depth):
        lam_init = 0.8 - 0.6 * math.exp(-0.3 * l)
        w_in_l = w_in[l].astype(BF16)
        w_mix = _take_cols(w_in_l, col_idx)
        w_gate = w_in_l[:, gate_off:]
        gates = _norm_matmul(x2, norm1_g[l][None], w_gate, out_dtype=BF16, sigmoid=True,
                             tm=t["proj_tm"], tn=t["proj_tn"], name="gate_proj")

        gq = jnp.pad(mla_q_norm_g[l], (0, 512 - MLA_Q_LORA))[None]
        gkv = mla_kv_norm_g[l][None]
        gik = jnp.concatenate([idx_k_norm_g[l], idx_k_norm_g[l]])[None]
        p2, t3, vt4, kc_tok, vc_tok, small, smallt = _proj_prep(
            x2, norm1_g[l][None], w_mix, tab, gq, gkv, gik, batch=b, seq=seq, ts=t["prep_ts"])
        p3 = p2.reshape(b, seq, P_WIDTH)
        small3 = small.reshape(b, seq, LANES)

        lv = jnp.stack([diff_lq1[l], diff_lk1[l], diff_lq2[l], diff_lk2[l]])
        o_a = _diff_attn(p3, lv, diff_subln_g[l][None], lam_init=lam_init, tq=t["diff_tq"], tk=tk)

        w1k, w1v = nsa_w1_k[l].astype(BF16), nsa_w1_v[l].astype(BF16)
        w1k_cat = jnp.concatenate([w1k[:half_w1], w1k[half_w1:]], axis=1)
        w1v_cat = jnp.concatenate([w1v[:half_w1], w1v[half_w1:]], axis=1)
        pek = jnp.broadcast_to(nsa_pe_k[l].reshape(1, -1), (8, CMP_LEN * NSA_DK)).astype(BF16)
        pev = jnp.broadcast_to(nsa_pe_v[l].reshape(1, -1), (8, CMP_LEN * NSA_DK)).astype(BF16)
        kc, vc = _nsa_compress(kc_tok.reshape(b, ng, half_w1), vc_tok.reshape(b, ng, half_w1),
                               w1k_cat, w1v_cat, pek, pev, w1k, w1v,
                               nsa_w2_k[l].astype(BF16), nsa_w2_v[l].astype(BF16))
        o_b = _nsa(p3, kc, vc, small3, ov, emat, tq=t["nsa_tq"], tk=tk)

        wq = jnp.pad(_take_cols(mla_w_uq[l].astype(BF16), uq_idx), ((0, 512 - MLA_Q_LORA), (0, 0)))
        wkv = _take_cols(mla_w_ukv[l].astype(BF16), ukv_idx)
        q_c, kv_c = _mla_up(p2, tab, wq, wkv, seq=seq, ts=t["prep_ts"])
        o_c = _mla_attn(q_c.reshape(b, seq, -1), kv_c.reshape(b, seq, -1), p3, tq=t["mla_tq"], tk=tk)

        o_d = _dsa(p3, t3, vt4, smallt, tq=t["dsa_tq"], tk=tk, top=min(IDX_TOPK, seq // 4))

        x2 = _merge(x2, o_a.reshape(m, -1), o_b.reshape(m, -1), o_c.reshape(m, -1), o_d.reshape(m, -1),
                    gates, w_branch[l].astype(BF16), w_out[l].astype(BF16), tm=t["row_tm"])
        wgu = w_gate_up[l].astype(BF16)
        x2 = _ffn(x2, norm2_g[l][None], wgu[:, :dff], wgu[:, dff:], w_down[l].astype(BF16),
                  final_norm_g[None], final=(l == depth - 1), tm=t["row_tm"], tf=t["ffn_tf"])
    return x2.reshape(b, seq, d)
```

```python
import functools
import math

import numpy as np
import jax
import jax.numpy as jnp
from jax import lax
from jax.experimental import pallas as pl
from jax.experimental.pallas import tpu as pltpu

F32 = jnp.float32
BF16 = jnp.bfloat16
I32 = jnp.int32
I16 = jnp.int16

LANES = 128
SUBLANES = 8
PACKED_ROWS = 16
HALF_MIN = -32768
VMEM_LIMIT = 56 * 1024 * 1024

ROPE_THETA = 500000.0
NEG = -1e30
LOG2E = math.log2(math.e)
FORCE_SCORE = 1e9
PAD_SCORE = -3e38
EPS = 1e-6
INT_MIN = -2147483648

HEADS = 4
HEAD_W = 128
BR_WIDTH = HEADS * HEAD_W
DA_DIM = 64
NSA_DK = 128
CMP_LEN = 32
CMP_STRIDE = 16
SEL_LEN = 64
SEL_N = 16
WIN = 512
MLA_Q_LORA = 384
MLA_KV_LORA = 256
MLA_NOPE = 128
MLA_ROPE = 64
DSA_DIM = 128
IDX_HEADS = 8
IDX_DIM = 64
IDX_TOPK = 256

Z_AQ, Z_AK, Z_AV, Z_BQ, Z_DQ, Z_DK, Z_DV, Z_IQ = (i * 512 for i in range(8))
Z_CQ = 4096
Z_CKV = 4608
Z_KC, Z_KS, Z_KW, Z_VC, Z_VS, Z_VW, Z_KR, Z_IK, Z_SMALL = (4864 + i * 128 for i in range(9))
Z_WIDTH = 6144
P_AQ, P_AK, P_AV, P_BQ, P_DK, P_CQ = (i * 512 for i in range(6))
P_CKV = 3072
P_KS, P_KW, P_VS, P_VW, P_KR, P_IK = (3328 + i * 128 for i in range(6))
P_WIDTH = 4096
T_DQ, T_IQ = 0, 512
T_ROWS = 1024
VT_TILE = 256
SMALL_G = 0
SMALL_IW = 12

ROPE_KINDS = ((16, 64), (32, 128), (64, 64))
TAB_W = 3 * LANES


def _cparams(n_axes):
    return pltpu.CompilerParams(dimension_semantics=("arbitrary",) * n_axes,
                                vmem_limit_bytes=VMEM_LIMIT)


def _dot(a, b):
    return jnp.dot(a, b, preferred_element_type=F32)


def _dot_nt(a, b):
    return lax.dot_general(a, b, (((1,), (1,)), ((), ())), preferred_element_type=F32)


def _in_proj_columns():
    names = (("a_q", 512), ("a_k", 512), ("a_v", 512), ("b_q", 512),
             ("b_kc", 128), ("b_vc", 128), ("b_ks", 128), ("b_vs", 128),
             ("b_kw", 128), ("b_vw", 128), ("b_g", 12),
             ("c_q", 384), ("c_kv", 256), ("c_kr", 64),
             ("d_q", 512), ("d_k", 512), ("d_v", 512),
             ("d_iq", 512), ("d_ik", 64), ("d_iw", 8), ("gate", 4096))
    off, o = {}, 0
    for nm, n in names:
        off[nm] = (o, n)
        o += n
    idx = np.full((Z_WIDTH,), -1, np.int64)

    def put(dst, nm):
        s, n = off[nm]
        idx[dst:dst + n] = np.arange(s, s + n)

    put(Z_AQ, "a_q"); put(Z_AK, "a_k"); put(Z_AV, "a_v"); put(Z_BQ, "b_q")
    put(Z_DQ, "d_q"); put(Z_DK, "d_k"); put(Z_DV, "d_v"); put(Z_IQ, "d_iq")
    put(Z_CQ, "c_q"); put(Z_CKV, "c_kv")
    put(Z_KC, "b_kc"); put(Z_KS, "b_ks"); put(Z_KW, "b_kw")
    put(Z_VC, "b_vc"); put(Z_VS, "b_vs"); put(Z_VW, "b_vw")
    put(Z_KR, "c_kr"); put(Z_KR + 64, "c_kr")
    put(Z_IK, "d_ik"); put(Z_IK + 64, "d_ik")
    put(Z_SMALL + SMALL_G, "b_g"); put(Z_SMALL + SMALL_IW, "d_iw")
    return idx, off["gate"][0], o


def _take_cols(w, idx):
    runs, i, n = [], 0, len(idx)
    while i < n:
        j = i + 1
        if idx[i] < 0:
            while j < n and idx[j] < 0:
                j += 1
            runs.append(jnp.zeros((w.shape[0], j - i), w.dtype))
        else:
            while j < n and idx[j] == idx[j - 1] + 1:
                j += 1
            runs.append(w[:, int(idx[i]):int(idx[i]) + (j - i)])
        i = j
    return jnp.concatenate(runs, axis=1)


def _rope_table(seq, rot, period):
    half = rot // 2
    inv = jnp.power(jnp.float32(ROPE_THETA), -jnp.arange(0, rot, 2, dtype=F32) / rot)
    ang = jnp.arange(seq, dtype=F32)[:, None] * inv[None, :]
    cos, sin = jnp.cos(ang), jnp.sin(ang)
    lane = np.arange(LANES) % period
    in1 = lane < half
    in2 = (lane >= half) & (lane < 2 * half)
    fidx = np.where(in1, lane, np.where(in2, lane - half, 0))
    cosl, sinl = cos[:, fidx], sin[:, fidx]
    c = jnp.where(jnp.asarray(in1 | in2)[None], cosl, 1.0)
    s1 = jnp.where(jnp.asarray(in1)[None], -sinl, 0.0)
    s2 = jnp.where(jnp.asarray(in2)[None], sinl, 0.0)
    return jnp.concatenate([c, s1, s2], axis=1)


def _rope128(x, tab, half):
    return (x * tab[:, 0:LANES]
            + pltpu.roll(x, LANES - half, 1) * tab[:, LANES:2 * LANES]
            + pltpu.roll(x, half, 1) * tab[:, 2 * LANES:3 * LANES])


def _norm_matmul_kernel(x_ref, g_ref, w_ref, o_ref, h_scr, *, sigmoid):
    @pl.when(pl.program_id(1) == 0)
    def _():
        x = x_ref[...]
        ms = jnp.mean(x * x, axis=-1, keepdims=True)
        h_scr[...] = (x * lax.rsqrt(ms + EPS) * g_ref[...]).astype(BF16)

    z = _dot(h_scr[...], w_ref[...])
    if sigmoid:
        z = jax.nn.sigmoid(z)
    o_ref[...] = z.astype(o_ref.dtype)


def _norm_matmul(x2, g, w, *, out_dtype, sigmoid, tm, tn, name):
    m, d = x2.shape
    n = w.shape[1]
    return pl.pallas_call(
        functools.partial(_norm_matmul_kernel, sigmoid=sigmoid),
        out_shape=jax.ShapeDtypeStruct((m, n), out_dtype),
        grid=(m // tm, n // tn),
        in_specs=[pl.BlockSpec((tm, d), lambda i, j: (i, 0)),
                  pl.BlockSpec((1, d), lambda i, j: (0, 0)),
                  pl.BlockSpec((d, tn), lambda i, j: (0, j))],
        out_specs=pl.BlockSpec((tm, tn), lambda i, j: (i, j)),
        scratch_shapes=[pltpu.VMEM((tm, d), BF16)],
        compiler_params=_cparams(2),
        name=name,
    )(x2, g, w)


PROJ_TILE = 512


def _proj_prep_kernel(x_ref, g_ref, w_ref, tab_ref, gq_ref, gkv_ref, gik_ref,
                      p_ref, t_ref, vt_ref, kc_ref, vc_ref, small_ref, smallt_ref):
    x = x_ref[...]
    ms = jnp.mean(x * x, axis=-1, keepdims=True)
    h = (x * lax.rsqrt(ms + EPS) * g_ref[...]).astype(BF16)
    z_tiles = {}

    def z_cols(off, width):
        t = off // PROJ_TILE
        assert (off + width - 1) // PROJ_TILE == t
        if t not in z_tiles:
            z_tiles[t] = _dot(h, w_ref[:, t * PROJ_TILE:(t + 1) * PROJ_TILE])
        lo = off - t * PROJ_TILE
        return z_tiles[t][:, lo:lo + width]

    def zc(off, c=0):
        return z_cols(off + c * LANES, LANES)

    def tab(kind):
        return tab_ref[:, kind * TAB_W:(kind + 1) * TAB_W]

    def put(off, c, v):
        p_ref[:, off + c * LANES:off + (c + 1) * LANES] = v.astype(BF16)

    def rope(off, c, kind):
        return _rope128(zc(off, c), tab(kind), ROPE_KINDS[kind][0] // 2)

    for zoff, poff, kind in ((Z_AQ, P_AQ, 0), (Z_AK, P_AK, 0), (Z_BQ, P_BQ, 1), (Z_DK, P_DK, 1)):
        for c in range(4):
            put(poff, c, rope(zoff, c, kind))
    for c in range(4):
        put(P_AV, c, zc(Z_AV, c))
    put(P_VS, 0, zc(Z_VS)); put(P_VW, 0, zc(Z_VW))
    put(P_KS, 0, rope(Z_KS, 0, 1)); put(P_KW, 0, rope(Z_KW, 0, 1))
    put(P_KR, 0, rope(Z_KR, 0, 2))
    kc_ref[...] = rope(Z_KC, 0, 1).astype(BF16)
    vc_ref[...] = zc(Z_VC).astype(BF16)

    for zoff, toff, kind in ((Z_DQ, T_DQ, 1), (Z_IQ, T_IQ, 0)):
        for c in range(4):
            t_ref[0, toff + c * LANES:toff + (c + 1) * LANES, :] = rope(zoff, c, kind).T.astype(BF16)
    for c in range(4):
        vt_ref[0, 0, c * LANES:(c + 1) * LANES, :] = zc(Z_DV, c).T.astype(BF16)

    cq = z_cols(Z_CQ, 512)
    ms = jnp.sum(cq * cq, axis=-1, keepdims=True) * (1.0 / MLA_Q_LORA)
    p_ref[:, P_CQ:P_CQ + 512] = (cq * lax.rsqrt(ms + EPS) * gq_ref[...]).astype(BF16)
    ckv = z_cols(Z_CKV, MLA_KV_LORA)
    ms = jnp.mean(ckv * ckv, axis=-1, keepdims=True)
    p_ref[:, P_CKV:P_CKV + MLA_KV_LORA] = (ckv * lax.rsqrt(ms + EPS) * gkv_ref[...]).astype(BF16)

    ik = zc(Z_IK)
    ms = jnp.mean(ik * ik, axis=-1, keepdims=True)
    ikn = ik * lax.rsqrt(ms + EPS) * gik_ref[...]
    put(P_IK, 0, _rope128(ikn, tab(0), ROPE_KINDS[0][0] // 2))

    sm = zc(Z_SMALL)
    lane = lax.broadcasted_iota(I32, sm.shape, 1)
    iw_scale = IDX_HEADS ** -0.5 * IDX_DIM ** -0.5
    small = jnp.where(lane < SMALL_IW, jax.nn.sigmoid(sm), sm * iw_scale)
    small_ref[...] = small
    smallt_ref[0] = small.T


def _proj_prep(x2, g, w, tab, gq, gkv, gik, *, batch, seq, ts):
    m, d = x2.shape
    spb = seq // ts
    assert ts == VT_TILE and w.shape == (d, Z_WIDTH)
    row = lambda w: pl.BlockSpec((ts, w), lambda i: (i, 0))
    return pl.pallas_call(
        _proj_prep_kernel,
        out_shape=(jax.ShapeDtypeStruct((m, P_WIDTH), BF16),
                   jax.ShapeDtypeStruct((batch, T_ROWS, seq), BF16),
                   jax.ShapeDtypeStruct((batch, spb, BR_WIDTH, VT_TILE), BF16),
                   jax.ShapeDtypeStruct((m, LANES), BF16),
                   jax.ShapeDtypeStruct((m, LANES), BF16),
                   jax.ShapeDtypeStruct((m, LANES), F32),
                   jax.ShapeDtypeStruct((batch, LANES, seq), F32)),
        grid=(m // ts,),
        in_specs=[row(d),
                  pl.BlockSpec((1, d), lambda i: (0, 0)),
                  pl.BlockSpec((d, Z_WIDTH), lambda i: (0, 0)),
                  pl.BlockSpec((ts, 3 * TAB_W), lambda i: (i % spb, 0)),
                  pl.BlockSpec((1, 512), lambda i: (0, 0)),
                  pl.BlockSpec((1, MLA_KV_LORA), lambda i: (0, 0)),
                  pl.BlockSpec((1, LANES), lambda i: (0, 0))],
        out_specs=(row(P_WIDTH),
                   pl.BlockSpec((1, T_ROWS, ts), lambda i: (i // spb, 0, i % spb)),
                   pl.BlockSpec((1, 1, BR_WIDTH, VT_TILE), lambda i: (i // spb, i % spb, 0, 0)),
                   row(LANES), row(LANES), row(LANES),
                   pl.BlockSpec((1, LANES, ts), lambda i: (i // spb, 0, i % spb))),
        compiler_params=_cparams(1),
        name="proj_prep",
    )(x2, g, w, tab, gq, gkv, gik)


def _softmax_init(mx_scr, l_scr, acc_scr):
    mx_scr[...] = jnp.full(mx_scr.shape, NEG, F32)
    l_scr[...] = jnp.zeros(l_scr.shape, F32)
    acc_scr[...] = jnp.zeros(acc_scr.shape, F32)


def _score_store(g, j, s, s_scr, mx_scr):
    s_scr[g, j] = s
    m = s[:, 0:LANES]
    for c in range(1, s.shape[1] // LANES):
        m = jnp.maximum(m, s[:, c * LANES:(c + 1) * LANES])
    mx_scr[g] = jnp.maximum(mx_scr[g], m)


def _row_max_finish(mx_scr):
    for g in range(mx_scr.shape[0]):
        m = jnp.max(mx_scr[g], axis=-1, keepdims=True)
        mx_scr[g] = jnp.broadcast_to(m, mx_scr.shape[1:])


def _prob_accumulate(g, j, v_tile, s_scr, mx_scr, l_scr, acc_scr):
    mb = mx_scr[g]
    s = s_scr[g, j]
    ps = [jnp.exp2(s[:, c * LANES:(c + 1) * LANES] - mb) for c in range(s.shape[1] // LANES)]
    tot = ps[0]
    for p in ps[1:]:
        tot = tot + p
    l_scr[g] += tot
    acc_scr[g] += _dot(jnp.concatenate(ps, axis=1).astype(BF16), v_tile)


def _softmax_out(g, l_scr, acc_scr):
    return acc_scr[g] / jnp.sum(l_scr[g], axis=-1, keepdims=True)


def _paired_tiles(n, step):
    def pair(jj, carry):
        step(2 * jj)
        step(2 * jj + 1)
        return carry

    lax.fori_loop(0, n // 2, pair, 0)

    @pl.when(n % 2 == 1)
    def _():
        step(n - 1)


def _causal_tiles(step, n_full):
    _paired_tiles(n_full, lambda j: step(j, False))
    step(n_full, True)


def _softmax_scratch(groups, n_tiles, rows, tk):
    return [pltpu.VMEM((groups, n_tiles, rows, tk), F32),
            pltpu.VMEM((groups, rows, LANES), F32),
            pltpu.VMEM((groups, rows, LANES), F32),
            pltpu.VMEM((groups, rows, HEAD_W), F32)]


def _diff_attn_kernel(q_ref, k_ref, v_ref, lv_ref, g_ref, o_ref, s_scr, mx_scr, l_scr, acc_scr,
                      *, tq, tk, lam_init):
    qs = pl.program_id(1) * tq
    n_full = qs // tk
    scale = DA_DIM ** -0.5 * LOG2E
    lv = lv_ref[...]
    lam = (jnp.exp(jnp.sum(lv[0:1] * lv[1:2], axis=-1, keepdims=True))
           - jnp.exp(jnp.sum(lv[2:3] * lv[3:4], axis=-1, keepdims=True)) + lam_init)
    lane = lax.broadcasted_iota(I32, (tq, HEAD_W), 1)
    row_t = qs + lax.broadcasted_iota(I32, (2 * tq, 1), 0) % tq
    col0 = lax.broadcasted_iota(I32, (2 * tq, tk), 1)
    _softmax_init(mx_scr, l_scr, acc_scr)

    def scores(j, masked):
        ks = pl.multiple_of(j * tk, tk)
        for h in range(HEADS):
            hs = slice(h * HEAD_W, (h + 1) * HEAD_W)
            qh = q_ref[0, :, hs]
            zero = jnp.zeros_like(qh)
            q2 = jnp.concatenate([jnp.where(lane < DA_DIM, qh, zero),
                                  jnp.where(lane >= DA_DIM, qh, zero)], axis=0)
            s = _dot_nt(q2, k_ref[0, pl.ds(ks, tk), hs]) * scale
            if masked:
                s = jnp.where(col0 + ks <= row_t, s, NEG)
            _score_store(h, j, s, s_scr, mx_scr)

    _causal_tiles(scores, n_full)
    _row_max_finish(mx_scr)

    def probs(j, carry):
        ks = pl.multiple_of(j * tk, tk)
        for h in range(HEADS):
            v_tile = v_ref[0, pl.ds(ks, tk), h * HEAD_W:(h + 1) * HEAD_W]
            _prob_accumulate(h, j, v_tile, s_scr, mx_scr, l_scr, acc_scr)
        return carry

    _paired_tiles(n_full + 1, lambda j: probs(j, 0))
    for h in range(HEADS):
        o2 = _softmax_out(h, l_scr, acc_scr)
        o = o2[:tq] - lam * o2[tq:]
        ms = jnp.mean(o * o, axis=-1, keepdims=True)
        o = o * lax.rsqrt(ms + EPS) * g_ref[...]
        o_ref[0, :, h * HEAD_W:(h + 1) * HEAD_W] = (o * (1.0 - lam_init)).astype(BF16)


def _diff_attn(p3, lv, g, *, lam_init, tq, tk):
    b, s, _ = p3.shape
    return pl.pallas_call(
        functools.partial(_diff_attn_kernel, tq=tq, tk=tk, lam_init=lam_init),
        out_shape=jax.ShapeDtypeStruct((b, s, BR_WIDTH), BF16),
        grid=(b, s // tq),
        in_specs=[pl.BlockSpec((1, tq, 512), lambda bi, i: (bi, i, P_AQ // 512)),
                  pl.BlockSpec((1, s, 512), lambda bi, i: (bi, 0, P_AK // 512)),
                  pl.BlockSpec((1, s, 512), lambda bi, i: (bi, 0, P_AV // 512)),
                  pl.BlockSpec((4, DA_DIM), lambda bi, i: (0, 0)),
                  pl.BlockSpec((1, HEAD_W), lambda bi, i: (0, 0))],
        out_specs=pl.BlockSpec((1, tq, BR_WIDTH), lambda bi, i: (bi, i, 0)),
        scratch_shapes=_softmax_scratch(HEADS, s // tk, 2 * tq, tk),
        compiler_params=_cparams(2),
        name="diff_attn",
    )(p3, p3, p3, lv, g)


def _nsa_compress_kernel(gk_ref, gv_ref, w1k_ref, w1v_ref, pek_ref, pev_ref,
                         w1kf_ref, w1vf_ref, w2k_ref, w2v_ref, kc_ref, vc_ref):
    def one(g_ref, w1cat_ref, pe_ref, w1f_ref, w2_ref, o_ref):
        y = _dot(g_ref[0], w1cat_ref[...])
        n = y.shape[0]
        nxt = pltpu.roll(y[:, HEAD_W:], n - 1, 0)
        c = _dot(pe_ref[...], w1f_ref[...])[0:1]
        hid = jax.nn.gelu(y[:, :HEAD_W] + nxt + c)
        o_ref[0] = _dot(hid.astype(BF16), w2_ref[...]).astype(BF16)

    one(gk_ref, w1k_ref, pek_ref, w1kf_ref, w2k_ref, kc_ref)
    one(gv_ref, w1v_ref, pev_ref, w1vf_ref, w2v_ref, vc_ref)


def _nsa_compress(gk, gv, w1k_cat, w1v_cat, pek, pev, w1k, w1v, w2k, w2v):
    b, ng, gw = gk.shape
    full = lambda shape: pl.BlockSpec(shape, lambda bi: (0,) * len(shape))
    return pl.pallas_call(
        _nsa_compress_kernel,
        out_shape=(jax.ShapeDtypeStruct((b, ng, HEAD_W), BF16),
                   jax.ShapeDtypeStruct((b, ng, HEAD_W), BF16)),
        grid=(b,),
        in_specs=[pl.BlockSpec((1, ng, gw), lambda bi: (bi, 0, 0)),
                  pl.BlockSpec((1, ng, gw), lambda bi: (bi, 0, 0)),
                  full(w1k_cat.shape), full(w1v_cat.shape), full(pek.shape), full(pev.shape),
                  full(w1k.shape), full(w1v.shape), full(w2k.shape), full(w2v.shape)],
        out_specs=(pl.BlockSpec((1, ng, HEAD_W), lambda bi: (bi, 0, 0)),
                   pl.BlockSpec((1, ng, HEAD_W), lambda bi: (bi, 0, 0))),
        compiler_params=_cparams(1),
        name="nsa_compress",
    )(gk, gv, w1k_cat, w1v_cat, pek, pev, w1k, w1v, w2k, w2v)


NSA_GROUPS = 2


def _nsa_kernel(q_ref, kc_ref, vc_ref, ks_ref, vs_ref, kw_ref, vw_ref, small_ref, ov_ref, e_ref,
                o_ref, s_scr, mx_scr, l_scr, acc_scr, cmp_scr, win_scr, *, tq, tk, seq):
    qs = pl.program_id(1) * tq
    scale = NSA_DK ** -0.5
    ns = seq // SEL_LEN
    n_sel = min(SEL_N, ns)
    r = HEADS * tq
    rg = r // NSA_GROUPS
    q4 = jnp.concatenate([q_ref[0, :, h * HEAD_W:(h + 1) * HEAD_W] for h in range(HEADS)], axis=0)
    t1 = qs + lax.broadcasted_iota(I32, (tq, 1), 0)
    t4 = qs + lax.broadcasted_iota(I32, (r, 1), 0) % tq

    wspan = WIN + tq
    start = pl.multiple_of(jnp.maximum(qs - WIN, 0), tq)
    sw = _dot_nt(q4, kw_ref[0, pl.ds(start, wspan), :]) * scale
    dist = t4 - (start + lax.broadcasted_iota(I32, (r, wspan), 1))
    sw = jnp.where(pltpu.bitcast(dist, jnp.uint32) < jnp.uint32(WIN), sw, NEG)
    e = jnp.exp(sw - jnp.max(sw, axis=-1, keepdims=True))
    pw = e / jnp.sum(e, axis=-1, keepdims=True)
    win_scr[...] = _dot(pw.astype(BF16), vw_ref[0, pl.ds(start, wspan), :])

    kc = kc_ref[0]
    nc_pad = kc.shape[0]
    sc = _dot_nt(q4, kc) * scale
    c_end = lax.broadcasted_iota(I32, (r, nc_pad), 1) * CMP_STRIDE + (CMP_LEN - 1)
    cmask = c_end <= t4
    mx = jnp.max(jnp.where(cmask, sc, NEG), axis=-1, keepdims=True)
    e = jnp.where(cmask, jnp.exp(sc - mx), 0.0)
    den = jnp.sum(e, axis=-1, keepdims=True)
    pc = e / jnp.where(den > 0.0, den, 1.0)
    cmp_scr[...] = _dot(pc.astype(BF16), vc_ref[0])

    psum = pc[0:tq] + pc[tq:2 * tq] + pc[2 * tq:3 * tq] + pc[3 * tq:4 * tq]
    ov = ov_ref[...]
    hi = psum.astype(BF16)
    r1 = psum - hi.astype(F32)
    mid = r1.astype(BF16)
    lo = (r1 - mid.astype(F32)).astype(BF16)
    imp = _dot(hi, ov) + _dot(mid, ov) + _dot(lo, ov)

    blk = lax.broadcasted_iota(I32, (tq, LANES), 1)
    cur = t1 // SEL_LEN
    forced = (blk == 0) | (blk == cur) | (blk == cur - 1)
    visible = blk * SEL_LEN <= t1
    score = jnp.where(visible, jnp.where(forced, FORCE_SCORE, imp), NEG)
    score = jnp.where(blk < ns, score, PAD_SCORE)
    ns_pad = -(-ns // SUBLANES) * SUBLANES
    score_t = score.T[:ns_pad]
    blk_t = lax.broadcasted_iota(I32, (ns_pad, tq), 0)
    rank = jnp.zeros((ns_pad, tq), I32)
    for jp in range(ns):
        row = score_t[jp:jp + 1, :]
        later = (blk_t > jp).astype(I32)
        rank = rank + jnp.where(row > score_t, 1, jnp.where(row == score_t, later, 0))
    sel_t = jnp.where(rank < n_sel, 1.0, 0.0)
    if ns_pad < LANES:
        sel_t = jnp.concatenate([sel_t, jnp.zeros((LANES - ns_pad, tq), F32)], axis=0)
    selb = sel_t.T.astype(BF16)

    _softmax_init(mx_scr, l_scr, acc_scr)
    col0 = lax.broadcasted_iota(I32, (rg, tk), 1)
    tg = qs + lax.broadcasted_iota(I32, (rg, 1), 0) % tq
    n_tiles = qs // tk + 1

    def scores(j, masked):
        ks0 = pl.multiple_of(j * tk, tk)
        mt = _dot(selb, e_ref[j])
        mg = jnp.concatenate([mt] * (rg // tq), axis=0)
        k_tile = ks_ref[0, pl.ds(ks0, tk), :]
        for g in range(NSA_GROUPS):
            s = _dot_nt(q4[g * rg:(g + 1) * rg], k_tile) * (scale * LOG2E)
            s = jnp.where(mg > 0.5, s, NEG)
            if masked:
                s = jnp.where(col0 + ks0 <= tg, s, NEG)
            _score_store(g, j, s, s_scr, mx_scr)

    _causal_tiles(scores, n_tiles - 1)
    _row_max_finish(mx_scr)

    def probs(j, carry):
        ks0 = pl.multiple_of(j * tk, tk)
        v_tile = vs_ref[0, pl.ds(ks0, tk), :]
        for g in range(NSA_GROUPS):
            _prob_accumulate(g, j, v_tile, s_scr, mx_scr, l_scr, acc_scr)
        return carry

    _paired_tiles(n_tiles, lambda j: probs(j, 0))
    o_slc = jnp.concatenate([_softmax_out(g, l_scr, acc_scr) for g in range(NSA_GROUPS)], axis=0)

    gates = small_ref[0]
    for h in range(HEADS):
        rows = slice(h * tq, (h + 1) * tq)
        g0 = gates[:, SMALL_G + 3 * h:SMALL_G + 3 * h + 1]
        g1 = gates[:, SMALL_G + 3 * h + 1:SMALL_G + 3 * h + 2]
        g2 = gates[:, SMALL_G + 3 * h + 2:SMALL_G + 3 * h + 3]
        o = g0 * cmp_scr[rows, :] + g1 * o_slc[rows] + g2 * win_scr[rows, :]
        o_ref[0, :, h * HEAD_W:(h + 1) * HEAD_W] = o.astype(BF16)


def _nsa(p3, kc, vc, small3, ov, emat, *, tq, tk):
    b, s, _ = p3.shape
    ng = kc.shape[1]
    col = lambda off: (lambda bi, i: (bi, 0, off // LANES))
    return pl.pallas_call(
        functools.partial(_nsa_kernel, tq=tq, tk=tk, seq=s),
        out_shape=jax.ShapeDtypeStruct((b, s, BR_WIDTH), BF16),
        grid=(b, s // tq),
        in_specs=[pl.BlockSpec((1, tq, 512), lambda bi, i: (bi, i, P_BQ // 512)),
                  pl.BlockSpec((1, ng, HEAD_W), lambda bi, i: (bi, 0, 0)),
                  pl.BlockSpec((1, ng, HEAD_W), lambda bi, i: (bi, 0, 0)),
                  pl.BlockSpec((1, s, LANES), col(P_KS)),
                  pl.BlockSpec((1, s, LANES), col(P_VS)),
                  pl.BlockSpec((1, s, LANES), col(P_KW)),
                  pl.BlockSpec((1, s, LANES), col(P_VW)),
                  pl.BlockSpec((1, tq, LANES), lambda bi, i: (bi, i, 0)),
                  pl.BlockSpec(ov.shape, lambda bi, i: (0, 0)),
                  pl.BlockSpec(emat.shape, lambda bi, i: (0, 0, 0))],
        out_specs=pl.BlockSpec((1, tq, BR_WIDTH), lambda bi, i: (bi, i, 0)),
        scratch_shapes=(_softmax_scratch(NSA_GROUPS, s // tk, HEADS * tq // NSA_GROUPS, tk)
                        + [pltpu.VMEM((HEADS * tq, HEAD_W), F32), pltpu.VMEM((HEADS * tq, HEAD_W), F32)]),
        compiler_params=_cparams(2),
        name="nsa_attn",
    )(p3, kc, vc, p3, p3, p3, p3, small3, ov, emat)


def _mla_up_kernel(p_ref, ckv_ref, tab_ref, wq_ref, wkv_ref, q_ref, kv_ref):
    q = _dot(p_ref[...], wq_ref[...])
    nn = HEADS * MLA_NOPE
    q_ref[:, :nn] = q[:, :nn].astype(BF16)
    for c in range(nn // LANES, (nn + HEADS * MLA_ROPE) // LANES):
        tile = _rope128(q[:, c * LANES:(c + 1) * LANES], tab_ref[...], MLA_ROPE // 2)
        q_ref[:, c * LANES:(c + 1) * LANES] = tile.astype(BF16)
    kv_ref[...] = _dot(ckv_ref[...], wkv_ref[...]).astype(BF16)


def _mla_up(p2, tab, wq, wkv, *, seq, ts):
    m = p2.shape[0]
    spb = seq // ts
    nq = wq.shape[1]
    nkv = wkv.shape[1]
    return pl.pallas_call(
        _mla_up_kernel,
        out_shape=(jax.ShapeDtypeStruct((m, nq), BF16), jax.ShapeDtypeStruct((m, nkv), BF16)),
        grid=(m // ts,),
        in_specs=[pl.BlockSpec((ts, 512), lambda i: (i, P_CQ // 512)),
                  pl.BlockSpec((ts, MLA_KV_LORA), lambda i: (i, P_CKV // MLA_KV_LORA)),
                  pl.BlockSpec((ts, TAB_W), lambda i: (i % spb, 2)),
                  pl.BlockSpec(wq.shape, lambda i: (0, 0)),
                  pl.BlockSpec(wkv.shape, lambda i: (0, 0))],
        out_specs=(pl.BlockSpec((ts, nq), lambda i: (i, 0)),
                   pl.BlockSpec((ts, nkv), lambda i: (i, 0))),
        compiler_params=_cparams(1),
        name="mla_up",
    )(p2, p2, tab, wq, wkv)


def _mla_attn_kernel(qn_ref, qr_ref, kn_ref, kr_ref, v_ref, o_ref, s_scr, mx_scr, l_scr, acc_scr,
                     *, tq, tk):
    qs = pl.program_id(1) * tq
    n_full = qs // tk
    scale = (MLA_NOPE + MLA_ROPE) ** -0.5 * LOG2E
    lane = lax.broadcasted_iota(I32, (tq, LANES), 1)
    row_t = qs + lax.broadcasted_iota(I32, (tq, 1), 0)
    col0 = lax.broadcasted_iota(I32, (tq, tk), 1)
    _softmax_init(mx_scr, l_scr, acc_scr)

    def scores(j, masked):
        ks = pl.multiple_of(j * tk, tk)
        kr_tile = kr_ref[0, pl.ds(ks, tk), :]
        for h in range(HEADS):
            hs = slice(h * HEAD_W, (h + 1) * HEAD_W)
            pair = qr_ref[0, :, (h // 2) * LANES:(h // 2 + 1) * LANES]
            keep = (lane < MLA_ROPE) if h % 2 == 0 else (lane >= MLA_ROPE)
            qr = jnp.where(keep, pair, jnp.zeros_like(pair))
            s = _dot_nt(jnp.concatenate([qn_ref[0, :, hs], qr], axis=1),
                        jnp.concatenate([kn_ref[0, pl.ds(ks, tk), hs], kr_tile], axis=1)) * scale
            if masked:
                s = jnp.where(col0 + ks <= row_t, s, NEG)
            _score_store(h, j, s, s_scr, mx_scr)

    _causal_tiles(scores, n_full)
    _row_max_finish(mx_scr)

    def probs(j, carry):
        ks = pl.multiple_of(j * tk, tk)
        for h in range(HEADS):
            v_tile = v_ref[0, pl.ds(ks, tk), h * HEAD_W:(h + 1) * HEAD_W]
            _prob_accumulate(h, j, v_tile, s_scr, mx_scr, l_scr, acc_scr)
        return carry

    _paired_tiles(n_full + 1, lambda j: probs(j, 0))
    for h in range(HEADS):
        o_ref[0, :, h * HEAD_W:(h + 1) * HEAD_W] = _softmax_out(h, l_scr, acc_scr).astype(BF16)


def _mla_attn(q3, kv3, p3, *, tq, tk):
    b, s, _ = q3.shape
    return pl.pallas_call(
        functools.partial(_mla_attn_kernel, tq=tq, tk=tk),
        out_shape=jax.ShapeDtypeStruct((b, s, BR_WIDTH), BF16),
        grid=(b, s // tq),
        in_specs=[pl.BlockSpec((1, tq, 512), lambda bi, i: (bi, i, 0)),
                  pl.BlockSpec((1, tq, 256), lambda bi, i: (bi, i, 2)),
                  pl.BlockSpec((1, s, 512), lambda bi, i: (bi, 0, 0)),
                  pl.BlockSpec((1, s, LANES), lambda bi, i: (bi, 0, P_KR // LANES)),
                  pl.BlockSpec((1, s, 512), lambda bi, i: (bi, 0, 1))],
        out_specs=pl.BlockSpec((1, tq, BR_WIDTH), lambda bi, i: (bi, i, 0)),
        scratch_shapes=_softmax_scratch(HEADS, s // tk, tq, tk),
        compiler_params=_cparams(2),
        name="mla_attn",
    )(q3, q3, kv3, p3, kv3)


def _sortable_key(x):
    bits = pltpu.bitcast(x + 0.0, I32)
    return bits ^ (lax.shift_right_arithmetic(bits, 31) & 0x7FFFFFFF)


def _fold_rows(x, op):
    n = x.shape[0] // SUBLANES
    return op(x.reshape(n, SUBLANES, x.shape[1]), axis=0)


def _count16(half_scr, n_tiles, pred, tq):
    def count_tile(j, cnt):
        hit = pred(half_scr[j]).astype(I16)
        parts = [hit[r:r + PACKED_ROWS] for r in range(0, hit.shape[0], PACKED_ROWS)]
        while len(parts) > 1:
            parts = [a + b for a, b in zip(parts[0::2], parts[1::2])]
        return cnt + parts[0]

    cnt = lax.fori_loop(0, n_tiles, count_tile, jnp.zeros((PACKED_ROWS, tq), I16))
    return jnp.sum(cnt.astype(I32), axis=0, keepdims=True)


def _bisect16(half_scr, n_tiles, need, tq):
    def bit_body(i, carry):
        th, tot = carry
        cand = th + lax.shift_left(jnp.int32(1), 15 - i)
        c16 = cand.astype(I16)
        total = _count16(half_scr, n_tiles, lambda x: x >= c16, tq)
        ok = total >= need
        return jnp.where(ok, cand, th), jnp.where(ok, total, tot)

    return lax.fori_loop(0, 16, bit_body, (jnp.full((1, tq), HALF_MIN, I32), jnp.full((1, tq), -1, I32)))


def _dsa_kernel(qt_ref, iqt_ref, iwt_ref, k_ref, ik_ref, vt_ref, tri_ref, o_ref,
                key_scr, half_scr, s_scr, mx_scr, l_scr, acc_scr, *, tq, tk, top):
    qs = pl.program_id(1) * tq
    n_tiles = (qs + tq - 1) // tk + 1
    scale = DSA_DIM ** -0.5 * LOG2E
    t_lane = qs + lax.broadcasted_iota(I32, (tk, tq), 1)
    krow0 = lax.broadcasted_iota(I32, (tk, tq), 0)
    half_rows = lax.broadcasted_iota(I32, (LANES, tq), 0) < IDX_DIM
    vt_per_tile = tk // VT_TILE

    def score_tile(j, carry):
        ks = pl.multiple_of(j * tk, tk)
        ikt = ik_ref[0, pl.ds(ks, tk), :]
        acc = jnp.zeros((tk, tq), F32)
        for h in range(IDX_HEADS):
            pair = iqt_ref[0, (h // 2) * LANES:(h // 2 + 1) * LANES, :]
            keep = half_rows if h % 2 == 0 else jnp.logical_not(half_rows)
            iq_h = jnp.where(keep, pair, jnp.zeros_like(pair))
            w_h = iwt_ref[0, SMALL_IW + h:SMALL_IW + h + 1, :]
            acc = acc + w_h * jnp.maximum(_dot(ikt, iq_h), 0.0)
        key = jnp.where(krow0 + ks <= t_lane, _sortable_key(acc), INT_MIN)
        key_scr[j] = key
        half_scr[j] = lax.shift_right_arithmetic(key, 16).astype(I16)
        return carry

    _paired_tiles(n_tiles, lambda j: score_tile(j, 0))

    th_hi, n_ge_hi = _bisect16(half_scr, n_tiles, top, tq)
    hi16 = th_hi.astype(I16)
    n_gt_hi = _count16(half_scr, n_tiles, lambda x: x > hi16, tq)
    need_lo = top - n_gt_hi

    def low_tile(j, carry):
        key = key_scr[j]
        lo = (key & 0xFFFF) + HALF_MIN
        same_hi = lax.shift_right_arithmetic(key, 16) == th_hi
        half_scr[j] = jnp.where(same_hi, lo, HALF_MIN).astype(I16)
        return carry

    lax.fori_loop(0, n_tiles, low_tile, 0)
    th_lo, n_ge_lo = _bisect16(half_scr, n_tiles, need_lo, tq)
    theta = lax.shift_left(th_hi, 16) + (th_lo - HALF_MIN)
    n_ge_lo = jnp.where(n_ge_lo >= 0, n_ge_lo, n_ge_hi - n_gt_hi)
    tied = (n_ge_lo > need_lo) & (n_ge_hi >= 0)

    def count32(pred):
        def body(j, cnt):
            return cnt + _fold_rows(pred(key_scr[j], krow0 + j * tk).astype(I32), jnp.sum)

        cnt = lax.fori_loop(0, n_tiles, body, jnp.zeros((SUBLANES, tq), I32))
        return jnp.sum(cnt, axis=0, keepdims=True)

    @pl.when(jnp.max(tied.astype(I32)) > 0)
    def _():
        need_eq = (top - count32(lambda k, pos: k > theta)).astype(F32)

        def demote(j, seen):
            k = key_scr[j]
            eq = k == theta
            eqf = jnp.where(eq, 1.0, 0.0)
            before = _dot(tri_ref[...], eqf.astype(BF16)) + seen
            key_scr[j] = jnp.where(eq & (before >= need_eq) & tied, k - 1, k)
            return seen + jnp.sum(_fold_rows(eqf, jnp.sum), axis=0, keepdims=True)

        lax.fori_loop(0, n_tiles, demote, jnp.zeros((1, tq), F32))

    theta = jnp.maximum(theta, INT_MIN + 1)

    mx_scr[...] = jnp.full(mx_scr.shape, NEG, F32)
    l_scr[...] = jnp.zeros(l_scr.shape, F32)
    acc_scr[...] = jnp.zeros(acc_scr.shape, F32)

    def scores(j, carry):
        ks = pl.multiple_of(j * tk, tk)
        sel = key_scr[j] >= theta
        for h in range(HEADS):
            hs = slice(h * HEAD_W, (h + 1) * HEAD_W)
            s = _dot(k_ref[0, pl.ds(ks, tk), hs], qt_ref[0, hs, :]) * scale
            s = jnp.where(sel, s, NEG)
            s_scr[h, j] = s
            mx_scr[h] = jnp.maximum(mx_scr[h], _fold_rows(s, jnp.max))
        return carry

    _paired_tiles(n_tiles, lambda j: scores(j, 0))
    for h in range(HEADS):
        m = jnp.max(mx_scr[h], axis=0, keepdims=True)
        mx_scr[h] = jnp.broadcast_to(m, (SUBLANES, tq))

    def probs(j, carry):
        for h in range(HEADS):
            p = jnp.exp2(s_scr[h, j] - mx_scr[h][0:1])
            l_scr[h] += _fold_rows(p, jnp.sum)
            pb = p.astype(BF16)
            for c in range(vt_per_tile):
                vt = vt_ref[0, j * vt_per_tile + c, h * HEAD_W:(h + 1) * HEAD_W, :]
                acc_scr[h] += _dot(vt, pb[c * VT_TILE:(c + 1) * VT_TILE])
        return carry

    _paired_tiles(n_tiles, lambda j: probs(j, 0))
    for h in range(HEADS):
        ot = acc_scr[h] / jnp.sum(l_scr[h], axis=0, keepdims=True)
        o_ref[0, :, h * HEAD_W:(h + 1) * HEAD_W] = ot.T.astype(BF16)


def _dsa(p3, t3, vt4, smallt, *, tq, tk, top):
    b, s, _ = p3.shape
    n_vt = vt4.shape[1]
    tri = jnp.asarray(np.tril(np.ones((tk, tk), np.float32), -1), BF16)
    return pl.pallas_call(
        functools.partial(_dsa_kernel, tq=tq, tk=tk, top=top),
        out_shape=jax.ShapeDtypeStruct((b, s, BR_WIDTH), BF16),
        grid=(b, s // tq),
        in_specs=[pl.BlockSpec((1, 512, tq), lambda bi, i: (bi, T_DQ // 512, i)),
                  pl.BlockSpec((1, 512, tq), lambda bi, i: (bi, T_IQ // 512, i)),
                  pl.BlockSpec((1, LANES, tq), lambda bi, i: (bi, 0, i)),
                  pl.BlockSpec((1, s, 512), lambda bi, i: (bi, 0, P_DK // 512)),
                  pl.BlockSpec((1, s, LANES), lambda bi, i: (bi, 0, P_IK // LANES)),
                  pl.BlockSpec((1, n_vt, BR_WIDTH, VT_TILE), lambda bi, i: (bi, 0, 0, 0)),
                  pl.BlockSpec((tk, tk), lambda bi, i: (0, 0))],
        out_specs=pl.BlockSpec((1, tq, BR_WIDTH), lambda bi, i: (bi, i, 0)),
        scratch_shapes=[pltpu.VMEM((s // tk, tk, tq), I32),
                        pltpu.VMEM((s // tk, tk, tq), I16),
                        pltpu.VMEM((HEADS, s // tk, tk, tq), F32),
                        pltpu.VMEM((HEADS, SUBLANES, tq), F32),
                        pltpu.VMEM((HEADS, SUBLANES, tq), F32),
                        pltpu.VMEM((HEADS, HEAD_W, tq), F32)],
        compiler_params=_cparams(2),
        name="dsa_attn",
    )(t3, t3, smallt, p3, p3, vt4, tri)


def _merge_kernel(x_ref, oa_ref, ob_ref, oc_ref, od_ref, g_ref, wb_ref, wo_ref, o_ref):
    d = x_ref.shape[1]
    acc = jnp.zeros(x_ref.shape, F32)
    for n, br_ref in enumerate((oa_ref, ob_ref, oc_ref, od_ref)):
        br = _dot(br_ref[...], wb_ref[n])
        acc = acc + g_ref[:, n * d:(n + 1) * d].astype(F32) * br
    o_ref[...] = x_ref[...] + _dot(acc.astype(BF16), wo_ref[...])


def _merge(x2, oa, ob, oc, od, gates, wb, wo, *, tm):
    m, d = x2.shape
    row = lambda w: pl.BlockSpec((tm, w), lambda i: (i, 0))
    return pl.pallas_call(
        _merge_kernel,
        out_shape=jax.ShapeDtypeStruct((m, d), F32),
        grid=(m // tm,),
        in_specs=[row(d), row(BR_WIDTH), row(BR_WIDTH), row(BR_WIDTH), row(BR_WIDTH),
                  row(gates.shape[1]),
                  pl.BlockSpec(wb.shape, lambda i: (0, 0, 0)),
                  pl.BlockSpec(wo.shape, lambda i: (0, 0))],
        out_specs=row(d),
        compiler_params=_cparams(1),
        name="merge",
    )(x2, oa, ob, oc, od, gates, wb, wo)


def _ffn_kernel(x_ref, g_ref, wg_ref, wu_ref, wd_ref, gf_ref, o_ref, h_scr, acc_scr, *, final):
    j = pl.program_id(1)

    @pl.when(j == 0)
    def _():
        x = x_ref[...]
        ms = jnp.mean(x * x, axis=-1, keepdims=True)
        h_scr[...] = (x * lax.rsqrt(ms + EPS) * g_ref[...]).astype(BF16)
        acc_scr[...] = jnp.zeros(acc_scr.shape, F32)

    h = h_scr[...]
    a = jax.nn.silu(_dot(h, wg_ref[...])) * _dot(h, wu_ref[...])
    acc_scr[...] += _dot(a.astype(BF16), wd_ref[...])

    @pl.when(j == pl.num_programs(1) - 1)
    def _():
        y = x_ref[...] + acc_scr[...]
        if final:
            ms = jnp.mean(y * y, axis=-1, keepdims=True)
            y = y * lax.rsqrt(ms + EPS) * gf_ref[...]
        o_ref[...] = y


def _ffn(x2, g, wg, wu, wd, gf, *, final, tm, tf):
    m, d = x2.shape
    dff = wg.shape[1]
    return pl.pallas_call(
        functools.partial(_ffn_kernel, final=final),
        out_shape=jax.ShapeDtypeStruct((m, d), F32),
        grid=(m // tm, dff // tf),
        in_specs=[pl.BlockSpec((tm, d), lambda i, j: (i, 0)),
                  pl.BlockSpec((1, d), lambda i, j: (0, 0)),
                  pl.BlockSpec((d, tf), lambda i, j: (0, j)),
                  pl.BlockSpec((d, tf), lambda i, j: (0, j)),
                  pl.BlockSpec((tf, d), lambda i, j: (j, 0)),
                  pl.BlockSpec((1, d), lambda i, j: (0, 0))],
        out_specs=pl.BlockSpec((tm, d), lambda i, j: (i, 0)),
        scratch_shapes=[pltpu.VMEM((tm, d), BF16), pltpu.VMEM((tm, d), F32)],
        compiler_params=_cparams(2),
        name="ffn",
    )(x2, g, wg, wu, wd, gf)


def _tiles(seq, m, dff):
    pick = lambda n, cands: next(c for c in cands if n % c == 0)
    tk = pick(seq, (512, 256))
    return dict(
        proj_tm=pick(m, (1024, 512, 256, 128)), proj_tn=1024,
        prep_ts=VT_TILE,
        diff_tq=128, mla_tq=pick(seq, (256, 128)), nsa_tq=256, dsa_tq=256, tk=tk,
        row_tm=pick(m, (512, 256, 128)),
        ffn_tf=pick(dff, (1408, 704, 256, 128)),
    )


def kernel(x, norm1_g, w_in, diff_lq1, diff_lk1, diff_lq2, diff_lk2, diff_subln_g, nsa_pe_k, nsa_w1_k, nsa_w2_k, nsa_pe_v, nsa_w1_v, nsa_w2_v, mla_q_norm_g, mla_w_uq, mla_kv_norm_g, mla_w_ukv, idx_k_norm_g, w_branch, w_out, norm2_g, w_gate_up, w_down, final_norm_g):
    b, seq, d = x.shape
    depth = w_in.shape[0]
    m = b * seq
    dff = w_down.shape[1]
    t = _tiles(seq, m, dff)
    tk = t["tk"]
    assert seq % SEL_LEN == 0 and seq >= WIN + t["nsa_tq"] and seq // SEL_LEN <= LANES
    assert seq % t["dsa_tq"] == 0 and tk % VT_TILE == 0 and tk >= min(IDX_TOPK, seq // 4)

    col_idx, gate_off, d_in = _in_proj_columns()
    assert w_in.shape[2] == d_in
    tab = jnp.concatenate([_rope_table(seq, rot, per) for rot, per in ROPE_KINDS], axis=1)

    ng = seq // CMP_STRIDE
    ns = seq // SEL_LEN
    c_start = np.arange(ng)[:, None] * CMP_STRIDE
    s_start = np.arange(LANES)[None, :] * SEL_LEN
    ov = ((c_start < s_start + SEL_LEN) & (c_start + CMP_LEN - 1 >= s_start)
          & (np.arange(LANES)[None, :] < ns))
    ov = jnp.asarray(ov, BF16)
    emat = np.arange(LANES)[:, None] == (np.arange(seq)[None, :] // SEL_LEN)
    emat = jnp.asarray(emat.reshape(LANES, seq // tk, tk).transpose(1, 0, 2), BF16)

    qd = MLA_NOPE + MLA_ROPE
    uq_idx = np.concatenate([np.concatenate([np.arange(h * qd, h * qd + MLA_NOPE) for h in range(HEADS)]),
                             np.concatenate([np.arange(h * qd + MLA_NOPE, (h + 1) * qd) for h in range(HEADS)])])
    kvd = MLA_NOPE + HEAD_W
    ukv_idx = np.concatenate([np.concatenate([np.arange(h * kvd, h * kvd + MLA_NOPE) for h in range(HEADS)]),
                              np.concatenate([np.arange(h * kvd + MLA_NOPE, (h + 1) * kvd) for h in range(HEADS)])])

    x2 = x.reshape(m, d)
    half_w1 = CMP_STRIDE * NSA_DK
    for l in range(depth):
        lam_init = 0.8 - 0.6 * math.exp(-0.3 * l)
        w_in_l = w_in[l].astype(BF16)
        w_mix = _take_cols(w_in_l, col_idx)
        w_gate = w_in_l[:, gate_off:]
        gates = _norm_matmul(x2, norm1_g[l][None], w_gate, out_dtype=BF16, sigmoid=True,
                             tm=t["proj_tm"], tn=t["proj_tn"], name="gate_proj")

        gq = jnp.pad(mla_q_norm_g[l], (0, 512 - MLA_Q_LORA))[None]
        gkv = mla_kv_norm_g[l][None]
        gik = jnp.concatenate([idx_k_norm_g[l], idx_k_norm_g[l]])[None]
        p2, t3, vt4, kc_tok, vc_tok, small, smallt = _proj_prep(
            x2, norm1_g[l][None], w_mix, tab, gq, gkv, gik, batch=b, seq=seq, ts=t["prep_ts"])
        p3 = p2.reshape(b, seq, P_WIDTH)
        small3 = small.reshape(b, seq, LANES)

        lv = jnp.stack([diff_lq1[l], diff_lk1[l], diff_lq2[l], diff_lk2[l]])
        o_a = _diff_attn(p3, lv, diff_subln_g[l][None], lam_init=lam_init, tq=t["diff_tq"], tk=tk)

        w1k, w1v = nsa_w1_k[l].astype(BF16), nsa_w1_v[l].astype(BF16)
        w1k_cat = jnp.concatenate([w1k[:half_w1], w1k[half_w1:]], axis=1)
        w1v_cat = jnp.concatenate([w1v[:half_w1], w1v[half_w1:]], axis=1)
        pek = jnp.broadcast_to(nsa_pe_k[l].reshape(1, -1), (8, CMP_LEN * NSA_DK)).astype(BF16)
        pev = jnp.broadcast_to(nsa_pe_v[l].reshape(1, -1), (8, CMP_LEN * NSA_DK)).astype(BF16)
        kc, vc = _nsa_compress(kc_tok.reshape(b, ng, half_w1), vc_tok.reshape(b, ng, half_w1),
                               w1k_cat, w1v_cat, pek, pev, w1k, w1v,
                               nsa_w2_k[l].astype(BF16), nsa_w2_v[l].astype(BF16))
        o_b = _nsa(p3, kc, vc, small3, ov, emat, tq=t["nsa_tq"], tk=tk)

        wq = jnp.pad(_take_cols(mla_w_uq[l].astype(BF16), uq_idx), ((0, 512 - MLA_Q_LORA), (0, 0)))
        wkv = _take_cols(mla_w_ukv[l].astype(BF16), ukv_idx)
        q_c, kv_c = _mla_up(p2, tab, wq, wkv, seq=seq, ts=t["prep_ts"])
        o_c = _mla_attn(q_c.reshape(b, seq, -1), kv_c.reshape(b, seq, -1), p3, tq=t["mla_tq"], tk=tk)

        o_d = _dsa(p3, t3, vt4, smallt, tq=t["dsa_tq"], tk=tk, top=min(IDX_TOPK, seq // 4))

        x2 = _merge(x2, o_a.reshape(m, -1), o_b.reshape(m, -1), o_c.reshape(m, -1), o_d.reshape(m, -1),
                    gates, w_branch[l].astype(BF16), w_out[l].astype(BF16), tm=t["row_tm"])
        wgu = w_gate_up[l].astype(BF16)
        x2 = _ffn(x2, norm2_g[l][None], wgu[:, :dff], wgu[:, dff:], w_down[l].astype(BF16),
                  final_norm_g[None], final=(l == depth - 1), tm=t["row_tm"], tf=t["ffn_tf"])
    return x2.reshape(b, seq, d)
```

```python
import functools
import math

import numpy as np
import jax
import jax.numpy as jnp
from jax import lax
from jax.experimental import pallas as pl
from jax.experimental.pallas import tpu as pltpu

F32 = jnp.float32
BF16 = jnp.bfloat16
I32 = jnp.int32
I16 = jnp.int16

LANES = 128
SUBLANES = 8
PACKED_ROWS = 16
HALF_MIN = -32768
VMEM_LIMIT = 56 * 1024 * 1024

ROPE_THETA = 500000.0
NEG = -1e30
LOG2E = math.log2(math.e)
FORCE_SCORE = 1e9
PAD_SCORE = -3e38
EPS = 1e-6
INT_MIN = -2147483648

HEADS = 4
HEAD_W = 128
BR_WIDTH = HEADS * HEAD_W
DA_DIM = 64
NSA_DK = 128
CMP_LEN = 32
CMP_STRIDE = 16
SEL_LEN = 64
SEL_N = 16
WIN = 512
MLA_Q_LORA = 384
MLA_KV_LORA = 256
MLA_NOPE = 128
MLA_ROPE = 64
DSA_DIM = 128
IDX_HEADS = 8
IDX_DIM = 64
IDX_TOPK = 256

Z_AQ, Z_AK, Z_AV, Z_BQ, Z_DQ, Z_DK, Z_DV, Z_IQ = (i * 512 for i in range(8))
Z_CQ = 4096
Z_CKV = 4608
Z_KC, Z_KS, Z_KW, Z_VC, Z_VS, Z_VW, Z_KR, Z_IK, Z_SMALL = (4864 + i * 128 for i in range(9))
Z_WIDTH = 6144
P_AQ, P_AK, P_AV, P_BQ, P_DK, P_CQ = (i * 512 for i in range(6))
P_CKV = 3072
P_KS, P_KW, P_VS, P_VW, P_KR, P_IK = (3328 + i * 128 for i in range(6))
P_WIDTH = 4096
T_DQ, T_IQ = 0, 512
T_ROWS = 1024
VT_TILE = 256
SMALL_G = 0
SMALL_IW = 12

ROPE_KINDS = ((16, 64), (32, 128), (64, 64))
TAB_W = 3 * LANES


def _cparams(n_axes):
    return pltpu.CompilerParams(dimension_semantics=("arbitrary",) * n_axes,
                                vmem_limit_bytes=VMEM_LIMIT)


def _dot(a, b):
    return jnp.dot(a, b, preferred_element_type=F32)


def _dot_nt(a, b):
    return lax.dot_general(a, b, (((1,), (1,)), ((), ())), preferred_element_type=F32)


def _in_proj_columns():
    names = (("a_q", 512), ("a_k", 512), ("a_v", 512), ("b_q", 512),
             ("b_kc", 128), ("b_vc", 128), ("b_ks", 128), ("b_vs", 128),
             ("b_kw", 128), ("b_vw", 128), ("b_g", 12),
             ("c_q", 384), ("c_kv", 256), ("c_kr", 64),
             ("d_q", 512), ("d_k", 512), ("d_v", 512),
             ("d_iq", 512), ("d_ik", 64), ("d_iw", 8), ("gate", 4096))
    off, o = {}, 0
    for nm, n in names:
        off[nm] = (o, n)
        o += n
    idx = np.full((Z_WIDTH,), -1, np.int64)

    def put(dst, nm):
        s, n = off[nm]
        idx[dst:dst + n] = np.arange(s, s + n)

    put(Z_AQ, "a_q"); put(Z_AK, "a_k"); put(Z_AV, "a_v"); put(Z_BQ, "b_q")
    put(Z_DQ, "d_q"); put(Z_DK, "d_k"); put(Z_DV, "d_v"); put(Z_IQ, "d_iq")
    put(Z_CQ, "c_q"); put(Z_CKV, "c_kv")
    put(Z_KC, "b_kc"); put(Z_KS, "b_ks"); put(Z_KW, "b_kw")
    put(Z_VC, "b_vc"); put(Z_VS, "b_vs"); put(Z_VW, "b_vw")
    put(Z_KR, "c_kr"); put(Z_KR + 64, "c_kr")
    put(Z_IK, "d_ik"); put(Z_IK + 64, "d_ik")
    put(Z_SMALL + SMALL_G, "b_g"); put(Z_SMALL + SMALL_IW, "d_iw")
    return idx, off["gate"][0], o


def _take_cols(w, idx):
    runs, i, n = [], 0, len(idx)
    while i < n:
        j = i + 1
        if idx[i] < 0:
            while j < n and idx[j] < 0:
                j += 1
            runs.append(jnp.zeros((w.shape[0], j - i), w.dtype))
        else:
            while j < n and idx[j] == idx[j - 1] + 1:
                j += 1
            runs.append(w[:, int(idx[i]):int(idx[i]) + (j - i)])
        i = j
    return jnp.concatenate(runs, axis=1)


def _rope_table(seq, rot, period):
    half = rot // 2
    inv = jnp.power(jnp.float32(ROPE_THETA), -jnp.arange(0, rot, 2, dtype=F32) / rot)
    ang = jnp.arange(seq, dtype=F32)[:, None] * inv[None, :]
    cos, sin = jnp.cos(ang), jnp.sin(ang)
    lane = np.arange(LANES) % period
    in1 = lane < half
    in2 = (lane >= half) & (lane < 2 * half)
    fidx = np.where(in1, lane, np.where(in2, lane - half, 0))
    cosl, sinl = cos[:, fidx], sin[:, fidx]
    c = jnp.where(jnp.asarray(in1 | in2)[None], cosl, 1.0)
    s1 = jnp.where(jnp.asarray(in1)[None], -sinl, 0.0)
    s2 = jnp.where(jnp.asarray(in2)[None], sinl, 0.0)
    return jnp.concatenate([c, s1, s2], axis=1)


def _rope128(x, tab, half):
    return (x * tab[:, 0:LANES]
            + pltpu.roll(x, LANES - half, 1) * tab[:, LANES:2 * LANES]
            + pltpu.roll(x, half, 1) * tab[:, 2 * LANES:3 * LANES])


def _norm_matmul_kernel(x_ref, g_ref, w_ref, o_ref, h_scr, *, sigmoid):
    @pl.when(pl.program_id(1) == 0)
    def _():
        x = x_ref[...]
        ms = jnp.mean(x * x, axis=-1, keepdims=True)
        h_scr[...] = (x * lax.rsqrt(ms + EPS) * g_ref[...]).astype(BF16)

    z = _dot(h_scr[...], w_ref[...])
    if sigmoid:
        z = jax.nn.sigmoid(z)
    o_ref[...] = z.astype(o_ref.dtype)


def _norm_matmul(x2, g, w, *, out_dtype, sigmoid, tm, tn, name):
    m, d = x2.shape
    n = w.shape[1]
    return pl.pallas_call(
        functools.partial(_norm_matmul_kernel, sigmoid=sigmoid),
        out_shape=jax.ShapeDtypeStruct((m, n), out_dtype),
        grid=(m // tm, n // tn),
        in_specs=[pl.BlockSpec((tm, d), lambda i, j: (i, 0)),
                  pl.BlockSpec((1, d), lambda i, j: (0, 0)),
                  pl.BlockSpec((d, tn), lambda i, j: (0, j))],
        out_specs=pl.BlockSpec((tm, tn), lambda i, j: (i, j)),
        scratch_shapes=[pltpu.VMEM((tm, d), BF16)],
        compiler_params=_cparams(2),
        name=name,
    )(x2, g, w)


PROJ_TILE = 512


def _proj_prep_kernel(x_ref, g_ref, w_ref, tab_ref, gq_ref, gkv_ref, gik_ref,
                      p_ref, t_ref, vt_ref, kc_ref, vc_ref, small_ref, smallt_ref):
    x = x_ref[...]
    ms = jnp.mean(x * x, axis=-1, keepdims=True)
    h = (x * lax.rsqrt(ms + EPS) * g_ref[...]).astype(BF16)
    z_tiles = {}

    def z_cols(off, width):
        t = off // PROJ_TILE
        assert (off + width - 1) // PROJ_TILE == t
        if t not in z_tiles:
            z_tiles[t] = _dot(h, w_ref[:, t * PROJ_TILE:(t + 1) * PROJ_TILE])
        lo = off - t * PROJ_TILE
        return z_tiles[t][:, lo:lo + width]

    def zc(off, c=0):
        return z_cols(off + c * LANES, LANES)

    def tab(kind):
        return tab_ref[:, kind * TAB_W:(kind + 1) * TAB_W]

    def put(off, c, v):
        p_ref[:, off + c * LANES:off + (c + 1) * LANES] = v.astype(BF16)

    def rope(off, c, kind):
        return _rope128(zc(off, c), tab(kind), ROPE_KINDS[kind][0] // 2)

    for zoff, poff, kind in ((Z_AQ, P_AQ, 0), (Z_AK, P_AK, 0), (Z_BQ, P_BQ, 1), (Z_DK, P_DK, 1)):
        for c in range(4):
            put(poff, c, rope(zoff, c, kind))
    for c in range(4):
        put(P_AV, c, zc(Z_AV, c))
    put(P_VS, 0, zc(Z_VS)); put(P_VW, 0, zc(Z_VW))
    put(P_KS, 0, rope(Z_KS, 0, 1)); put(P_KW, 0, rope(Z_KW, 0, 1))
    put(P_KR, 0, rope(Z_KR, 0, 2))
    kc_ref[...] = rope(Z_KC, 0, 1).astype(BF16)
    vc_ref[...] = zc(Z_VC).astype(BF16)

    for zoff, toff, kind in ((Z_DQ, T_DQ, 1), (Z_IQ, T_IQ, 0)):
        for c in range(4):
            t_ref[0, toff + c * LANES:toff + (c + 1) * LANES, :] = rope(zoff, c, kind).T.astype(BF16)
    for c in range(4):
        vt_ref[0, 0, c * LANES:(c + 1) * LANES, :] = zc(Z_DV, c).T.astype(BF16)

    cq = z_cols(Z_CQ, 512)
    ms = jnp.sum(cq * cq, axis=-1, keepdims=True) * (1.0 / MLA_Q_LORA)
    p_ref[:, P_CQ:P_CQ + 512] = (cq * lax.rsqrt(ms + EPS) * gq_ref[...]).astype(BF16)
    ckv = z_cols(Z_CKV, MLA_KV_LORA)
    ms = jnp.mean(ckv * ckv, axis=-1, keepdims=True)
    p_ref[:, P_CKV:P_CKV + MLA_KV_LORA] = (ckv * lax.rsqrt(ms + EPS) * gkv_ref[...]).astype(BF16)

    ik = zc(Z_IK)
    ms = jnp.mean(ik * ik, axis=-1, keepdims=True)
    ikn = ik * lax.rsqrt(ms + EPS) * gik_ref[...]
    put(P_IK, 0, _rope128(ikn, tab(0), ROPE_KINDS[0][0] // 2))

    sm = zc(Z_SMALL)
    lane = lax.broadcasted_iota(I32, sm.shape, 1)
    iw_scale = IDX_HEADS ** -0.5 * IDX_DIM ** -0.5
    small = jnp.where(lane < SMALL_IW, jax.nn.sigmoid(sm), sm * iw_scale)
    small_ref[...] = small
    smallt_ref[0] = small.T


def _proj_prep(x2, g, w, tab, gq, gkv, gik, *, batch, seq, ts):
    m, d = x2.shape
    spb = seq // ts
    assert ts == VT_TILE and w.shape == (d, Z_WIDTH)
    row = lambda w: pl.BlockSpec((ts, w), lambda i: (i, 0))
    return pl.pallas_call(
        _proj_prep_kernel,
        out_shape=(jax.ShapeDtypeStruct((m, P_WIDTH), BF16),
                   jax.ShapeDtypeStruct((batch, T_ROWS, seq), BF16),
                   jax.ShapeDtypeStruct((batch, spb, BR_WIDTH, VT_TILE), BF16),
                   jax.ShapeDtypeStruct((m, LANES), BF16),
                   jax.ShapeDtypeStruct((m, LANES), BF16),
                   jax.ShapeDtypeStruct((m, LANES), F32),
                   jax.ShapeDtypeStruct((batch, LANES, seq), F32)),
        grid=(m // ts,),
        in_specs=[row(d),
                  pl.BlockSpec((1, d), lambda i: (0, 0)),
                  pl.BlockSpec((d, Z_WIDTH), lambda i: (0, 0)),
                  pl.BlockSpec((ts, 3 * TAB_W), lambda i: (i % spb, 0)),
                  pl.BlockSpec((1, 512), lambda i: (0, 0)),
                  pl.BlockSpec((1, MLA_KV_LORA), lambda i: (0, 0)),
                  pl.BlockSpec((1, LANES), lambda i: (0, 0))],
        out_specs=(row(P_WIDTH),
                   pl.BlockSpec((1, T_ROWS, ts), lambda i: (i // spb, 0, i % spb)),
                   pl.BlockSpec((1, 1, BR_WIDTH, VT_TILE), lambda i: (i // spb, i % spb, 0, 0)),
                   row(LANES), row(LANES), row(LANES),
                   pl.BlockSpec((1, LANES, ts), lambda i: (i // spb, 0, i % spb))),
        compiler_params=_cparams(1),
        name="proj_prep",
    )(x2, g, w, tab, gq, gkv, gik)


def _softmax_init(mx_scr, l_scr, acc_scr):
    mx_scr[...] = jnp.full(mx_scr.shape, NEG, F32)
    l_scr[...] = jnp.zeros(l_scr.shape, F32)
    acc_scr[...] = jnp.zeros(acc_scr.shape, F32)


def _score_store(g, j, s, s_scr, mx_scr):
    s_scr[g, j] = s
    m = s[:, 0:LANES]
    for c in range(1, s.shape[1] // LANES):
        m = jnp.maximum(m, s[:, c * LANES:(c + 1) * LANES])
    mx_scr[g] = jnp.maximum(mx_scr[g], m)


def _row_max_finish(mx_scr):
    for g in range(mx_scr.shape[0]):
        m = jnp.max(mx_scr[g], axis=-1, keepdims=True)
        mx_scr[g] = jnp.broadcast_to(m, mx_scr.shape[1:])


def _prob_accumulate(g, j, v_tile, s_scr, mx_scr, l_scr, acc_scr):
    mb = mx_scr[g]
    s = s_scr[g, j]
    ps = [jnp.exp2(s[:, c * LANES:(c + 1) * LANES] - mb) for c in range(s.shape[1] // LANES)]
    tot = ps[0]
    for p in ps[1:]:
        tot = tot + p
    l_scr[g] += tot
    acc_scr[g] += _dot(jnp.concatenate(ps, axis=1).astype(BF16), v_tile)


def _softmax_out(g, l_scr, acc_scr):
    return acc_scr[g] / jnp.sum(l_scr[g], axis=-1, keepdims=True)


def _paired_tiles(n, step):
    def pair(jj, carry):
        step(2 * jj)
        step(2 * jj + 1)
        return carry

    lax.fori_loop(0, n // 2, pair, 0)

    @pl.when(n % 2 == 1)
    def _():
        step(n - 1)


def _causal_tiles(step, n_full):
    _paired_tiles(n_full, lambda j: step(j, False))
    step(n_full, True)


def _softmax_scratch(groups, n_tiles, rows, tk):
    return [pltpu.VMEM((groups, n_tiles, rows, tk), F32),
            pltpu.VMEM((groups, rows, LANES), F32),
            pltpu.VMEM((groups, rows, LANES), F32),
            pltpu.VMEM((groups, rows, HEAD_W), F32)]


def _diff_attn_kernel(q_ref, k_ref, v_ref, lv_ref, g_ref, o_ref, s_scr, mx_scr, l_scr, acc_scr,
                      *, tq, tk, lam_init):
    qs = pl.program_id(1) * tq
    n_full = qs // tk
    scale = DA_DIM ** -0.5 * LOG2E
    lv = lv_ref[...]
    lam = (jnp.exp(jnp.sum(lv[0:1] * lv[1:2], axis=-1, keepdims=True))
           - jnp.exp(jnp.sum(lv[2:3] * lv[3:4], axis=-1, keepdims=True)) + lam_init)
    lane = lax.broadcasted_iota(I32, (tq, HEAD_W), 1)
    row_t = qs + lax.broadcasted_iota(I32, (2 * tq, 1), 0) % tq
    col0 = lax.broadcasted_iota(I32, (2 * tq, tk), 1)
    _softmax_init(mx_scr, l_scr, acc_scr)

    def scores(j, masked):
        ks = pl.multiple_of(j * tk, tk)
        for h in range(HEADS):
            hs = slice(h * HEAD_W, (h + 1) * HEAD_W)
            qh = q_ref[0, :, hs]
            zero = jnp.zeros_like(qh)
            q2 = jnp.concatenate([jnp.where(lane < DA_DIM, qh, zero),
                                  jnp.where(lane >= DA_DIM, qh, zero)], axis=0)
            s = _dot_nt(q2, k_ref[0, pl.ds(ks, tk), hs]) * scale
            if masked:
                s = jnp.where(col0 + ks <= row_t, s, NEG)
            _score_store(h, j, s, s_scr, mx_scr)

    _causal_tiles(scores, n_full)
    _row_max_finish(mx_scr)

    def probs(j, carry):
        ks = pl.multiple_of(j * tk, tk)
        for h in range(HEADS):
            v_tile = v_ref[0, pl.ds(ks, tk), h * HEAD_W:(h + 1) * HEAD_W]
            _prob_accumulate(h, j, v_tile, s_scr, mx_scr, l_scr, acc_scr)
        return carry

    _paired_tiles(n_full + 1, lambda j: probs(j, 0))
    for h in range(HEADS):
        o2 = _softmax_out(h, l_scr, acc_scr)
        o = o2[:tq] - lam * o2[tq:]
        ms = jnp.mean(o * o, axis=-1, keepdims=True)
        o = o * lax.rsqrt(ms + EPS) * g_ref[...]
        o_ref[0, :, h * HEAD_W:(h + 1) * HEAD_W] = (o * (1.0 - lam_init)).astype(BF16)


def _diff_attn(p3, lv, g, *, lam_init, tq, tk):
    b, s, _ = p3.shape
    return pl.pallas_call(
        functools.partial(_diff_attn_kernel, tq=tq, tk=tk, lam_init=lam_init),
        out_shape=jax.ShapeDtypeStruct((b, s, BR_WIDTH), BF16),
        grid=(b, s // tq),
        in_specs=[pl.BlockSpec((1, tq, 512), lambda bi, i: (bi, i, P_AQ // 512)),
                  pl.BlockSpec((1, s, 512), lambda bi, i: (bi, 0, P_AK // 512)),
                  pl.BlockSpec((1, s, 512), lambda bi, i: (bi, 0, P_AV // 512)),
                  pl.BlockSpec((4, DA_DIM), lambda bi, i: (0, 0)),
                  pl.BlockSpec((1, HEAD_W), lambda bi, i: (0, 0))],
        out_specs=pl.BlockSpec((1, tq, BR_WIDTH), lambda bi, i: (bi, i, 0)),
        scratch_shapes=_softmax_scratch(HEADS, s // tk, 2 * tq, tk),
        compiler_params=_cparams(2),
        name="diff_attn",
    )(p3, p3, p3, lv, g)


def _nsa_compress_kernel(gk_ref, gv_ref, w1k_ref, w1v_ref, pek_ref, pev_ref,
                         w1kf_ref, w1vf_ref, w2k_ref, w2v_ref, kc_ref, vc_ref):
    def one(g_ref, w1cat_ref, pe_ref, w1f_ref, w2_ref, o_ref):
        y = _dot(g_ref[0], w1cat_ref[...])
        n = y.shape[0]
        nxt = pltpu.roll(y[:, HEAD_W:], n - 1, 0)
        c = _dot(pe_ref[...], w1f_ref[...])[0:1]
        hid = jax.nn.gelu(y[:, :HEAD_W] + nxt + c)
        o_ref[0] = _dot(hid.astype(BF16), w2_ref[...]).astype(BF16)

    one(gk_ref, w1k_ref, pek_ref, w1kf_ref, w2k_ref, kc_ref)
    one(gv_ref, w1v_ref, pev_ref, w1vf_ref, w2v_ref, vc_ref)


def _nsa_compress(gk, gv, w1k_cat, w1v_cat, pek, pev, w1k, w1v, w2k, w2v):
    b, ng, gw = gk.shape
    full = lambda shape: pl.BlockSpec(shape, lambda bi: (0,) * len(shape))
    return pl.pallas_call(
        _nsa_compress_kernel,
        out_shape=(jax.ShapeDtypeStruct((b, ng, HEAD_W), BF16),
                   jax.ShapeDtypeStruct((b, ng, HEAD_W), BF16)),
        grid=(b,),
        in_specs=[pl.BlockSpec((1, ng, gw), lambda bi: (bi, 0, 0)),
                  pl.BlockSpec((1, ng, gw), lambda bi: (bi, 0, 0)),
                  full(w1k_cat.shape), full(w1v_cat.shape), full(pek.shape), full(pev.shape),
                  full(w1k.shape), full(w1v.shape), full(w2k.shape), full(w2v.shape)],
        out_specs=(pl.BlockSpec((1, ng, HEAD_W), lambda bi: (bi, 0, 0)),
                   pl.BlockSpec((1, ng, HEAD_W), lambda bi: (bi, 0, 0))),
        compiler_params=_cparams(1),
        name="nsa_compress",
    )(gk, gv, w1k_cat, w1v_cat, pek, pev, w1k, w1v, w2k, w2v)


NSA_GROUPS = 2


def _nsa_kernel(q_ref, kc_ref, vc_ref, ks_ref, vs_ref, kw_ref, vw_ref, small_ref, ov_ref, e_ref,
                o_ref, s_scr, mx_scr, l_scr, acc_scr, cmp_scr, win_scr, *, tq, tk, seq):
    qs = pl.program_id(1) * tq
    scale = NSA_DK ** -0.5
    ns = seq // SEL_LEN
    n_sel = min(SEL_N, ns)
    r = HEADS * tq
    rg = r // NSA_GROUPS
    q4 = jnp.concatenate([q_ref[0, :, h * HEAD_W:(h + 1) * HEAD_W] for h in range(HEADS)], axis=0)
    t1 = qs + lax.broadcasted_iota(I32, (tq, 1), 0)
    t4 = qs + lax.broadcasted_iota(I32, (r, 1), 0) % tq

    wspan = WIN + tq
    start = pl.multiple_of(jnp.maximum(qs - WIN, 0), tq)
    sw = _dot_nt(q4, kw_ref[0, pl.ds(start, wspan), :]) * scale
    dist = t4 - (start + lax.broadcasted_iota(I32, (r, wspan), 1))
    sw = jnp.where(pltpu.bitcast(dist, jnp.uint32) < jnp.uint32(WIN), sw, NEG)
    e = jnp.exp(sw - jnp.max(sw, axis=-1, keepdims=True))
    pw = e / jnp.sum(e, axis=-1, keepdims=True)
    win_scr[...] = _dot(pw.astype(BF16), vw_ref[0, pl.ds(start, wspan), :])

    kc = kc_ref[0]
    nc_pad = kc.shape[0]
    sc = _dot_nt(q4, kc) * scale
    c_end = lax.broadcasted_iota(I32, (r, nc_pad), 1) * CMP_STRIDE + (CMP_LEN - 1)
    cmask = c_end <= t4
    mx = jnp.max(jnp.where(cmask, sc, NEG), axis=-1, keepdims=True)
    e = jnp.where(cmask, jnp.exp(sc - mx), 0.0)
    den = jnp.sum(e, axis=-1, keepdims=True)
    pc = e / jnp.where(den > 0.0, den, 1.0)
    cmp_scr[...] = _dot(pc.astype(BF16), vc_ref[0])

    psum = pc[0:tq] + pc[tq:2 * tq] + pc[2 * tq:3 * tq] + pc[3 * tq:4 * tq]
    ov = ov_ref[...]
    hi = psum.astype(BF16)
    r1 = psum - hi.astype(F32)
    mid = r1.astype(BF16)
    lo = (r1 - mid.astype(F32)).astype(BF16)
    imp = _dot(hi, ov) + _dot(mid, ov) + _dot(lo, ov)

    blk = lax.broadcasted_iota(I32, (tq, LANES), 1)
    cur = t1 // SEL_LEN
    forced = (blk == 0) | (blk == cur) | (blk == cur - 1)
    visible = blk * SEL_LEN <= t1
    score = jnp.where(visible, jnp.where(forced, FORCE_SCORE, imp), NEG)
    score = jnp.where(blk < ns, score, PAD_SCORE)
    ns_pad = -(-ns // SUBLANES) * SUBLANES
    score_t = score.T[:ns_pad]
    blk_t = lax.broadcasted_iota(I32, (ns_pad, tq), 0)
    rank = jnp.zeros((ns_pad, tq), I32)
    for jp in range(ns):
        row = score_t[jp:jp + 1, :]
        later = (blk_t > jp).astype(I32)
        rank = rank + jnp.where(row > score_t, 1, jnp.where(row == score_t, later, 0))
    sel_t = jnp.where(rank < n_sel, 1.0, 0.0)
    if ns_pad < LANES:
        sel_t = jnp.concatenate([sel_t, jnp.zeros((LANES - ns_pad, tq), F32)], axis=0)
    selb = sel_t.T.astype(BF16)

    _softmax_init(mx_scr, l_scr, acc_scr)
    col0 = lax.broadcasted_iota(I32, (rg, tk), 1)
    tg = qs + lax.broadcasted_iota(I32, (rg, 1), 0) % tq
    n_tiles = qs // tk + 1

    def scores(j, masked):
        ks0 = pl.multiple_of(j * tk, tk)
        mt = _dot(selb, e_ref[j])
        mg = jnp.concatenate([mt] * (rg // tq), axis=0)
        k_tile = ks_ref[0, pl.ds(ks0, tk), :]
        for g in range(NSA_GROUPS):
            s = _dot_nt(q4[g * rg:(g + 1) * rg], k_tile) * (scale * LOG2E)
            s = jnp.where(mg > 0.5, s, NEG)
            if masked:
                s = jnp.where(col0 + ks0 <= tg, s, NEG)
            _score_store(g, j, s, s_scr, mx_scr)

    _causal_tiles(scores, n_tiles - 1)
    _row_max_finish(mx_scr)

    def probs(j, carry):
        ks0 = pl.multiple_of(j * tk, tk)
        v_tile = vs_ref[0, pl.ds(ks0, tk), :]
        for g in range(NSA_GROUPS):
            _prob_accumulate(g, j, v_tile, s_scr, mx_scr, l_scr, acc_scr)
        return carry

    _paired_tiles(n_tiles, lambda j: probs(j, 0))
    o_slc = jnp.concatenate([_softmax_out(g, l_scr, acc_scr) for g in range(NSA_GROUPS)], axis=0)

    gates = small_ref[0]
    for h in range(HEADS):
        rows = slice(h * tq, (h + 1) * tq)
        g0 = gates[:, SMALL_G + 3 * h:SMALL_G + 3 * h + 1]
        g1 = gates[:, SMALL_G + 3 * h + 1:SMALL_G + 3 * h + 2]
        g2 = gates[:, SMALL_G + 3 * h + 2:SMALL_G + 3 * h + 3]
        o = g0 * cmp_scr[rows, :] + g1 * o_slc[rows] + g2 * win_scr[rows, :]
        o_ref[0, :, h * HEAD_W:(h + 1) * HEAD_W] = o.astype(BF16)


def _nsa(p3, kc, vc, small3, ov, emat, *, tq, tk):
    b, s, _ = p3.shape
    ng = kc.shape[1]
    col = lambda off: (lambda bi, i: (bi, 0, off // LANES))
    return pl.pallas_call(
        functools.partial(_nsa_kernel, tq=tq, tk=tk, seq=s),
        out_shape=jax.ShapeDtypeStruct((b, s, BR_WIDTH), BF16),
        grid=(b, s // tq),
        in_specs=[pl.BlockSpec((1, tq, 512), lambda bi, i: (bi, i, P_BQ // 512)),
                  pl.BlockSpec((1, ng, HEAD_W), lambda bi, i: (bi, 0, 0)),
                  pl.BlockSpec((1, ng, HEAD_W), lambda bi, i: (bi, 0, 0)),
                  pl.BlockSpec((1, s, LANES), col(P_KS)),
                  pl.BlockSpec((1, s, LANES), col(P_VS)),
                  pl.BlockSpec((1, s, LANES), col(P_KW)),
                  pl.BlockSpec((1, s, LANES), col(P_VW)),
                  pl.BlockSpec((1, tq, LANES), lambda bi, i: (bi, i, 0)),
                  pl.BlockSpec(ov.shape, lambda bi, i: (0, 0)),
                  pl.BlockSpec(emat.shape, lambda bi, i: (0, 0, 0))],
        out_specs=pl.BlockSpec((1, tq, BR_WIDTH), lambda bi, i: (bi, i, 0)),
        scratch_shapes=(_softmax_scratch(NSA_GROUPS, s // tk, HEADS * tq // NSA_GROUPS, tk)
                        + [pltpu.VMEM((HEADS * tq, HEAD_W), F32), pltpu.VMEM((HEADS * tq, HEAD_W), F32)]),
        compiler_params=_cparams(2),
        name="nsa_attn",
    )(p3, kc, vc, p3, p3, p3, p3, small3, ov, emat)


def _mla_up_kernel(p_ref, ckv_ref, tab_ref, wq_ref, wkv_ref, q_ref, kv_ref):
    q = _dot(p_ref[...], wq_ref[...])
    nn = HEADS * MLA_NOPE
    q_ref[:, :nn] = q[:, :nn].astype(BF16)
    for c in range(nn // LANES, (nn + HEADS * MLA_ROPE) // LANES):
        tile = _rope128(q[:, c * LANES:(c + 1) * LANES], tab_ref[...], MLA_ROPE // 2)
        q_ref[:, c * LANES:(c + 1) * LANES] = tile.astype(BF16)
    kv_ref[...] = _dot(ckv_ref[...], wkv_ref[...]).astype(BF16)


def _mla_up(p2, tab, wq, wkv, *, seq, ts):
    m = p2.shape[0]
    spb = seq // ts
    nq = wq.shape[1]
    nkv = wkv.shape[1]
    return pl.pallas_call(
        _mla_up_kernel,
        out_shape=(jax.ShapeDtypeStruct((m, nq), BF16), jax.ShapeDtypeStruct((m, nkv), BF16)),
        grid=(m // ts,),
        in_specs=[pl.BlockSpec((ts, 512), lambda i: (i, P_CQ // 512)),
                  pl.BlockSpec((ts, MLA_KV_LORA), lambda i: (i, P_CKV // MLA_KV_LORA)),
                  pl.BlockSpec((ts, TAB_W), lambda i: (i % spb, 2)),
                  pl.BlockSpec(wq.shape, lambda i: (0, 0)),
                  pl.BlockSpec(wkv.shape, lambda i: (0, 0))],
        out_specs=(pl.BlockSpec((ts, nq), lambda i: (i, 0)),
                   pl.BlockSpec((ts, nkv), lambda i: (i, 0))),
        compiler_params=_cparams(1),
        name="mla_up",
    )(p2, p2, tab, wq, wkv)


def _mla_attn_kernel(qn_ref, qr_ref, kn_ref, kr_ref, v_ref, o_ref, s_scr, mx_scr, l_scr, acc_scr,
                     *, tq, tk):
    qs = pl.program_id(1) * tq
    n_full = qs // tk
    scale = (MLA_NOPE + MLA_ROPE) ** -0.5 * LOG2E
    lane = lax.broadcasted_iota(I32, (tq, LANES), 1)
    row_t = qs + lax.broadcasted_iota(I32, (tq, 1), 0)
    col0 = lax.broadcasted_iota(I32, (tq, tk), 1)
    _softmax_init(mx_scr, l_scr, acc_scr)

    def scores(j, masked):
        ks = pl.multiple_of(j * tk, tk)
        kr_tile = kr_ref[0, pl.ds(ks, tk), :]
        for h in range(HEADS):
            hs = slice(h * HEAD_W, (h + 1) * HEAD_W)
            pair = qr_ref[0, :, (h // 2) * LANES:(h // 2 + 1) * LANES]
            keep = (lane < MLA_ROPE) if h % 2 == 0 else (lane >= MLA_ROPE)
            qr = jnp.where(keep, pair, jnp.zeros_like(pair))
            s = _dot_nt(jnp.concatenate([qn_ref[0, :, hs], qr], axis=1),
                        jnp.concatenate([kn_ref[0, pl.ds(ks, tk), hs], kr_tile], axis=1)) * scale
            if masked:
                s = jnp.where(col0 + ks <= row_t, s, NEG)
            _score_store(h, j, s, s_scr, mx_scr)

    _causal_tiles(scores, n_full)
    _row_max_finish(mx_scr)

    def probs(j, carry):
        ks = pl.multiple_of(j * tk, tk)
        for h in range(HEADS):
            v_tile = v_ref[0, pl.ds(ks, tk), h * HEAD_W:(h + 1) * HEAD_W]
            _prob_accumulate(h, j, v_tile, s_scr, mx_scr, l_scr, acc_scr)
        return carry

    _paired_tiles(n_full + 1, lambda j: probs(j, 0))
    for h in range(HEADS):
        o_ref[0, :, h * HEAD_W:(h + 1) * HEAD_W] = _softmax_out(h, l_scr, acc_scr).astype(BF16)


def _mla_attn(q3, kv3, p3, *, tq, tk):
    b, s, _ = q3.shape
    return pl.pallas_call(
        functools.partial(_mla_attn_kernel, tq=tq, tk=tk),
        out_shape=jax.ShapeDtypeStruct((b, s, BR_WIDTH), BF16),
        grid=(b, s // tq),
        in_specs=[pl.BlockSpec((1, tq, 512), lambda bi, i: (bi, i, 0)),
                  pl.BlockSpec((1, tq, 256), lambda bi, i: (bi, i, 2)),
                  pl.BlockSpec((1, s, 512), lambda bi, i: (bi, 0, 0)),
                  pl.BlockSpec((1, s, LANES), lambda bi, i: (bi, 0, P_KR // LANES)),
                  pl.BlockSpec((1, s, 512), lambda bi, i: (bi, 0, 1))],
        out_specs=pl.BlockSpec((1, tq, BR_WIDTH), lambda bi, i: (bi, i, 0)),
        scratch_shapes=_softmax_scratch(HEADS, s // tk, tq, tk),
        compiler_params=_cparams(2),
        name="mla_attn",
    )(q3, q3, kv3, p3, kv3)


def _sortable_key(x):
    bits = pltpu.bitcast(x + 0.0, I32)
    return bits ^ (lax.shift_right_arithmetic(bits, 31) & 0x7FFFFFFF)


def _fold_rows(x, op):
    n = x.shape[0] // SUBLANES
    return op(x.reshape(n, SUBLANES, x.shape[1]), axis=0)


def _count16(half_scr, n_tiles, pred, tq):
    def count_tile(j, cnt):
        hit = pred(half_scr[j]).astype(I16)
        parts = [hit[r:r + PACKED_ROWS] for r in range(0, hit.shape[0], PACKED_ROWS)]
        while len(parts) > 1:
            parts = [a + b for a, b in zip(parts[0::2], parts[1::2])]
        return cnt + parts[0]

    cnt = lax.fori_loop(0, n_tiles, count_tile, jnp.zeros((PACKED_ROWS, tq), I16))
    return jnp.sum(cnt.astype(I32), axis=0, keepdims=True)


def _bisect16(half_scr, n_tiles, need, tq):
    def bit_body(i, carry):
        th, tot = carry
        cand = th + lax.shift_left(jnp.int32(1), 15 - i)
        c16 = cand.astype(I16)
        total = _count16(half_scr, n_tiles, lambda x: x >= c16, tq)
        ok = total >= need
        return jnp.where(ok, cand, th), jnp.where(ok, total, tot)

    return lax.fori_loop(0, 16, bit_body, (jnp.full((1, tq), HALF_MIN, I32), jnp.full((1, tq), -1, I32)))


def _dsa_kernel(qt_ref, iqt_ref, iwt_ref, k_ref, ik_ref, vt_ref, tri_ref, o_ref,
                key_scr, half_scr, s_scr, mx_scr, l_scr, acc_scr, *, tq, tk, top):
    qs = pl.program_id(1) * tq
    n_tiles = (qs + tq - 1) // tk + 1
    scale = DSA_DIM ** -0.5 * LOG2E
    t_lane = qs + lax.broadcasted_iota(I32, (tk, tq), 1)
    krow0 = lax.broadcasted_iota(I32, (tk, tq), 0)
    half_rows = lax.broadcasted_iota(I32, (LANES, tq), 0) < IDX_DIM
    vt_per_tile = tk // VT_TILE

    def score_tile(j, masked):
        ks = pl.multiple_of(j * tk, tk)
        ikt = ik_ref[0, pl.ds(ks, tk), :]
        acc = jnp.zeros((tk, tq), F32)
        for h in range(IDX_HEADS):
            pair = iqt_ref[0, (h // 2) * LANES:(h // 2 + 1) * LANES, :]
            keep = half_rows if h % 2 == 0 else jnp.logical_not(half_rows)
            iq_h = jnp.where(keep, pair, jnp.zeros_like(pair))
            w_h = iwt_ref[0, SMALL_IW + h:SMALL_IW + h + 1, :]
            acc = acc + w_h * jnp.maximum(_dot(ikt, iq_h), 0.0)
        key = _sortable_key(acc)
        if masked:
            key = jnp.where(krow0 + ks <= t_lane, key, INT_MIN)
        key_scr[j] = key
        half_scr[j] = lax.shift_right_arithmetic(key, 16).astype(I16)

    _causal_tiles(score_tile, n_tiles - 1)

    th_hi, n_ge_hi = _bisect16(half_scr, n_tiles, top, tq)
    hi16 = th_hi.astype(I16)
    n_gt_hi = _count16(half_scr, n_tiles, lambda x: x > hi16, tq)
    need_lo = top - n_gt_hi

    def low_tile(j, carry):
        key = key_scr[j]
        lo = (key & 0xFFFF) + HALF_MIN
        same_hi = lax.shift_right_arithmetic(key, 16) == th_hi
        half_scr[j] = jnp.where(same_hi, lo, HALF_MIN).astype(I16)
        return carry

    lax.fori_loop(0, n_tiles, low_tile, 0)
    th_lo, n_ge_lo = _bisect16(half_scr, n_tiles, need_lo, tq)
    theta = lax.shift_left(th_hi, 16) + (th_lo - HALF_MIN)
    n_ge_lo = jnp.where(n_ge_lo >= 0, n_ge_lo, n_ge_hi - n_gt_hi)
    tied = (n_ge_lo > need_lo) & (n_ge_hi >= 0)

    @pl.when(jnp.max(tied.astype(I32)) > 0)
    def _():
        lo16 = th_lo.astype(I16)
        n_gt = n_gt_hi + _count16(half_scr, n_tiles, lambda x: x > lo16, tq)
        keep_eq = jnp.where(tied, top - n_gt, tk * key_scr.shape[0]).astype(F32)

        def demote(j, seen):
            k = key_scr[j]
            eq = k == theta
            eqf = jnp.where(eq, 1.0, 0.0)
            before = _dot(tri_ref[...], eqf.astype(BF16)) + seen
            key_scr[j] = jnp.where(jnp.where(eq, before, -1.0) >= keep_eq, k - 1, k)
            return seen + jnp.sum(_fold_rows(eqf, jnp.sum), axis=0, keepdims=True)

        lax.fori_loop(0, n_tiles, demote, jnp.zeros((1, tq), F32))

    theta = jnp.maximum(theta, INT_MIN + 1)

    mx_scr[...] = jnp.full(mx_scr.shape, NEG, F32)
    l_scr[...] = jnp.zeros(l_scr.shape, F32)
    acc_scr[...] = jnp.zeros(acc_scr.shape, F32)

    def scores(j, carry):
        ks = pl.multiple_of(j * tk, tk)
        sel = key_scr[j] >= theta
        for h in range(HEADS):
            hs = slice(h * HEAD_W, (h + 1) * HEAD_W)
            s = _dot(k_ref[0, pl.ds(ks, tk), hs], qt_ref[0, hs, :]) * scale
            s = jnp.where(sel, s, NEG)
            s_scr[h, j] = s
            mx_scr[h] = jnp.maximum(mx_scr[h], _fold_rows(s, jnp.max))
        return carry

    _paired_tiles(n_tiles, lambda j: scores(j, 0))
    for h in range(HEADS):
        m = jnp.max(mx_scr[h], axis=0, keepdims=True)
        mx_scr[h] = jnp.broadcast_to(m, (SUBLANES, tq))

    def probs(j, carry):
        for h in range(HEADS):
            p = jnp.exp2(s_scr[h, j] - mx_scr[h][0:1])
            l_scr[h] += _fold_rows(p, jnp.sum)
            pb = p.astype(BF16)
            for c in range(vt_per_tile):
                vt = vt_ref[0, j * vt_per_tile + c, h * HEAD_W:(h + 1) * HEAD_W, :]
                acc_scr[h] += _dot(vt, pb[c * VT_TILE:(c + 1) * VT_TILE])
        return carry

    _paired_tiles(n_tiles, lambda j: probs(j, 0))
    for h in range(HEADS):
        ot = acc_scr[h] / jnp.sum(l_scr[h], axis=0, keepdims=True)
        o_ref[0, :, h * HEAD_W:(h + 1) * HEAD_W] = ot.T.astype(BF16)


def _dsa(p3, t3, vt4, smallt, *, tq, tk, top):
    b, s, _ = p3.shape
    n_vt = vt4.shape[1]
    tri = jnp.asarray(np.tril(np.ones((tk, tk), np.float32), -1), BF16)
    return pl.pallas_call(
        functools.partial(_dsa_kernel, tq=tq, tk=tk, top=top),
        out_shape=jax.ShapeDtypeStruct((b, s, BR_WIDTH), BF16),
        grid=(b, s // tq),
        in_specs=[pl.BlockSpec((1, 512, tq), lambda bi, i: (bi, T_DQ // 512, i)),
                  pl.BlockSpec((1, 512, tq), lambda bi, i: (bi, T_IQ // 512, i)),
                  pl.BlockSpec((1, LANES, tq), lambda bi, i: (bi, 0, i)),
                  pl.BlockSpec((1, s, 512), lambda bi, i: (bi, 0, P_DK // 512)),
                  pl.BlockSpec((1, s, LANES), lambda bi, i: (bi, 0, P_IK // LANES)),
                  pl.BlockSpec((1, n_vt, BR_WIDTH, VT_TILE), lambda bi, i: (bi, 0, 0, 0)),
                  pl.BlockSpec((tk, tk), lambda bi, i: (0, 0))],
        out_specs=pl.BlockSpec((1, tq, BR_WIDTH), lambda bi, i: (bi, i, 0)),
        scratch_shapes=[pltpu.VMEM((s // tk, tk, tq), I32),
                        pltpu.VMEM((s // tk, tk, tq), I16),
                        pltpu.VMEM((HEADS, s // tk, tk, tq), F32),
                        pltpu.VMEM((HEADS, SUBLANES, tq), F32),
                        pltpu.VMEM((HEADS, SUBLANES, tq), F32),
                        pltpu.VMEM((HEADS, HEAD_W, tq), F32)],
        compiler_params=_cparams(2),
        name="dsa_attn",
    )(t3, t3, smallt, p3, p3, vt4, tri)


def _merge_kernel(x_ref, oa_ref, ob_ref, oc_ref, od_ref, g_ref, wb_ref, wo_ref, o_ref):
    d = x_ref.shape[1]
    acc = jnp.zeros(x_ref.shape, F32)
    for n, br_ref in enumerate((oa_ref, ob_ref, oc_ref, od_ref)):
        br = _dot(br_ref[...], wb_ref[n])
        acc = acc + g_ref[:, n * d:(n + 1) * d].astype(F32) * br
    o_ref[...] = x_ref[...] + _dot(acc.astype(BF16), wo_ref[...])


def _merge(x2, oa, ob, oc, od, gates, wb, wo, *, tm):
    m, d = x2.shape
    row = lambda w: pl.BlockSpec((tm, w), lambda i: (i, 0))
    return pl.pallas_call(
        _merge_kernel,
        out_shape=jax.ShapeDtypeStruct((m, d), F32),
        grid=(m // tm,),
        in_specs=[row(d), row(BR_WIDTH), row(BR_WIDTH), row(BR_WIDTH), row(BR_WIDTH),
                  row(gates.shape[1]),
                  pl.BlockSpec(wb.shape, lambda i: (0, 0, 0)),
                  pl.BlockSpec(wo.shape, lambda i: (0, 0))],
        out_specs=row(d),
        compiler_params=_cparams(1),
        name="merge",
    )(x2, oa, ob, oc, od, gates, wb, wo)


def _ffn_kernel(x_ref, g_ref, wg_ref, wu_ref, wd_ref, gf_ref, o_ref, h_scr, acc_scr, *, final):
    j = pl.program_id(1)

    @pl.when(j == 0)
    def _():
        x = x_ref[...]
        ms = jnp.mean(x * x, axis=-1, keepdims=True)
        h_scr[...] = (x * lax.rsqrt(ms + EPS) * g_ref[...]).astype(BF16)
        acc_scr[...] = jnp.zeros(acc_scr.shape, F32)

    h = h_scr[...]
    a = jax.nn.silu(_dot(h, wg_ref[...])) * _dot(h, wu_ref[...])
    acc_scr[...] += _dot(a.astype(BF16), wd_ref[...])

    @pl.when(j == pl.num_programs(1) - 1)
    def _():
        y = x_ref[...] + acc_scr[...]
        if final:
            ms = jnp.mean(y * y, axis=-1, keepdims=True)
            y = y * lax.rsqrt(ms + EPS) * gf_ref[...]
        o_ref[...] = y


def _ffn(x2, g, wg, wu, wd, gf, *, final, tm, tf):
    m, d = x2.shape
    dff = wg.shape[1]
    return pl.pallas_call(
        functools.partial(_ffn_kernel, final=final),
        out_shape=jax.ShapeDtypeStruct((m, d), F32),
        grid=(m // tm, dff // tf),
        in_specs=[pl.BlockSpec((tm, d), lambda i, j: (i, 0)),
                  pl.BlockSpec((1, d), lambda i, j: (0, 0)),
                  pl.BlockSpec((d, tf), lambda i, j: (0, j)),
                  pl.BlockSpec((d, tf), lambda i, j: (0, j)),
                  pl.BlockSpec((tf, d), lambda i, j: (j, 0)),
                  pl.BlockSpec((1, d), lambda i, j: (0, 0))],
        out_specs=pl.BlockSpec((tm, d), lambda i, j: (i, 0)),
        scratch_shapes=[pltpu.VMEM((tm, d), BF16), pltpu.VMEM((tm, d), F32)],
        compiler_params=_cparams(2),
        name="ffn",
    )(x2, g, wg, wu, wd, gf)


def _tiles(seq, m, dff):
    pick = lambda n, cands: next(c for c in cands if n % c == 0)
    tk = pick(seq, (512, 256))
    return dict(
        proj_tm=pick(m, (1024, 512, 256, 128)), proj_tn=1024,
        prep_ts=VT_TILE,
        diff_tq=128, mla_tq=pick(seq, (256, 128)), nsa_tq=256, dsa_tq=256, tk=tk,
        row_tm=pick(m, (512, 256, 128)),
        ffn_tf=pick(dff, (1408, 704, 256, 128)),
    )


def kernel(x, norm1_g, w_in, diff_lq1, diff_lk1, diff_lq2, diff_lk2, diff_subln_g, nsa_pe_k, nsa_w1_k, nsa_w2_k, nsa_pe_v, nsa_w1_v, nsa_w2_v, mla_q_norm_g, mla_w_uq, mla_kv_norm_g, mla_w_ukv, idx_k_norm_g, w_branch, w_out, norm2_g, w_gate_up, w_down, final_norm_g):
    b, seq, d = x.shape
    depth = w_in.shape[0]
    m = b * seq
    dff = w_down.shape[1]
    t = _tiles(seq, m, dff)
    tk = t["tk"]
    assert seq % SEL_LEN == 0 and seq >= WIN + t["nsa_tq"] and seq // SEL_LEN <= LANES
    assert seq % t["dsa_tq"] == 0 and tk % VT_TILE == 0 and tk >= min(IDX_TOPK, seq // 4)

    col_idx, gate_off, d_in = _in_proj_columns()
    assert w_in.shape[2] == d_in
    tab = jnp.concatenate([_rope_table(seq, rot, per) for rot, per in ROPE_KINDS], axis=1)

    ng = seq // CMP_STRIDE
    ns = seq // SEL_LEN
    c_start = np.arange(ng)[:, None] * CMP_STRIDE
    s_start = np.arange(LANES)[None, :] * SEL_LEN
    ov = ((c_start < s_start + SEL_LEN) & (c_start + CMP_LEN - 1 >= s_start)
          & (np.arange(LANES)[None, :] < ns))
    ov = jnp.asarray(ov, BF16)
    emat = np.arange(LANES)[:, None] == (np.arange(seq)[None, :] // SEL_LEN)
    emat = jnp.asarray(emat.reshape(LANES, seq // tk, tk).transpose(1, 0, 2), BF16)

    qd = MLA_NOPE + MLA_ROPE
    uq_idx = np.concatenate([np.concatenate([np.arange(h * qd, h * qd + MLA_NOPE) for h in range(HEADS)]),
                             np.concatenate([np.arange(h * qd + MLA_NOPE, (h + 1) * qd) for h in range(HEADS)])])
    kvd = MLA_NOPE + HEAD_W
    ukv_idx = np.concatenate([np.concatenate([np.arange(h * kvd, h * kvd + MLA_NOPE) for h in range(HEADS)]),
                              np.concatenate([np.arange(h * kvd + MLA_NOPE, (h + 1) * kvd) for h in range(HEADS)])])

    x2 = x.reshape(m, d)
    half_w1 = CMP_STRIDE * NSA_DK
    for l in range(depth):
        lam_init = 0.8 - 0.6 * math.exp(-0.3 * l)
        w_in_l = w_in[l].astype(BF16)
        w_mix = _take_cols(w_in_l, col_idx)
        w_gate = w_in_l[:, gate_off:]
        gates = _norm_matmul(x2, norm1_g[l][None], w_gate, out_dtype=BF16, sigmoid=True,
                             tm=t["proj_tm"], tn=t["proj_tn"], name="gate_proj")

        gq = jnp.pad(mla_q_norm_g[l], (0, 512 - MLA_Q_LORA))[None]
        gkv = mla_kv_norm_g[l][None]
        gik = jnp.concatenate([idx_k_norm_g[l], idx_k_norm_g[l]])[None]
        p2, t3, vt4, kc_tok, vc_tok, small, smallt = _proj_prep(
            x2, norm1_g[l][None], w_mix, tab, gq, gkv, gik, batch=b, seq=seq, ts=t["prep_ts"])
        p3 = p2.reshape(b, seq, P_WIDTH)
        small3 = small.reshape(b, seq, LANES)

        lv = jnp.stack([diff_lq1[l], diff_lk1[l], diff_lq2[l], diff_lk2[l]])
        o_a = _diff_attn(p3, lv, diff_subln_g[l][None], lam_init=lam_init, tq=t["diff_tq"], tk=tk)

        w1k, w1v = nsa_w1_k[l].astype(BF16), nsa_w1_v[l].astype(BF16)
        w1k_cat = jnp.concatenate([w1k[:half_w1], w1k[half_w1:]], axis=1)
        w1v_cat = jnp.concatenate([w1v[:half_w1], w1v[half_w1:]], axis=1)
        pek = jnp.broadcast_to(nsa_pe_k[l].reshape(1, -1), (8, CMP_LEN * NSA_DK)).astype(BF16)
        pev = jnp.broadcast_to(nsa_pe_v[l].reshape(1, -1), (8, CMP_LEN * NSA_DK)).astype(BF16)
        kc, vc = _nsa_compress(kc_tok.reshape(b, ng, half_w1), vc_tok.reshape(b, ng, half_w1),
                               w1k_cat, w1v_cat, pek, pev, w1k, w1v,
                               nsa_w2_k[l].astype(BF16), nsa_w2_v[l].astype(BF16))
        o_b = _nsa(p3, kc, vc, small3, ov, emat, tq=t["nsa_tq"], tk=tk)

        wq = jnp.pad(_take_cols(mla_w_uq[l].astype(BF16), uq_idx), ((0, 512 - MLA_Q_LORA), (0, 0)))
        wkv = _take_cols(mla_w_ukv[l].astype(BF16), ukv_idx)
        q_c, kv_c = _mla_up(p2, tab, wq, wkv, seq=seq, ts=t["prep_ts"])
        o_c = _mla_attn(q_c.reshape(b, seq, -1), kv_c.reshape(b, seq, -1), p3, tq=t["mla_tq"], tk=tk)

        o_d = _dsa(p3, t3, vt4, smallt, tq=t["dsa_tq"], tk=tk, top=min(IDX_TOPK, seq // 4))

        x2 = _merge(x2, o_a.reshape(m, -1), o_b.reshape(m, -1), o_c.reshape(m, -1), o_d.reshape(m, -1),
                    gates, w_branch[l].astype(BF16), w_out[l].astype(BF16), tm=t["row_tm"])
        wgu = w_gate_up[l].astype(BF16)
        x2 = _ffn(x2, norm2_g[l][None], wgu[:, :dff], wgu[:, dff:], w_down[l].astype(BF16),
                  final_norm_g[None], final=(l == depth - 1), tm=t["row_tm"], tf=t["ffn_tf"])
    return x2.reshape(b, seq, d)
```

```python
import functools
import math

import numpy as np
import jax
import jax.numpy as jnp
from jax import lax
from jax.experimental import pallas as pl
from jax.experimental.pallas import tpu as pltpu

F32 = jnp.float32
BF16 = jnp.bfloat16
I32 = jnp.int32
I16 = jnp.int16

LANES = 128
SUBLANES = 8
PACKED_ROWS = 16
HALF_MIN = -32768
VMEM_LIMIT = 56 * 1024 * 1024

ROPE_THETA = 500000.0
NEG = -1e30
LOG2E = math.log2(math.e)
FORCE_SCORE = 1e9
PAD_SCORE = -3e38
EPS = 1e-6
INT_MIN = -2147483648

HEADS = 4
HEAD_W = 128
BR_WIDTH = HEADS * HEAD_W
DA_DIM = 64
NSA_DK = 128
CMP_LEN = 32
CMP_STRIDE = 16
SEL_LEN = 64
SEL_N = 16
WIN = 512
MLA_Q_LORA = 384
MLA_KV_LORA = 256
MLA_NOPE = 128
MLA_ROPE = 64
DSA_DIM = 128
IDX_HEADS = 8
IDX_DIM = 64
IDX_TOPK = 256

Z_AQ, Z_AK, Z_AV, Z_BQ, Z_DQ, Z_DK, Z_DV, Z_IQ = (i * 512 for i in range(8))
Z_CQ = 4096
Z_CKV = 4608
Z_KC, Z_KS, Z_KW, Z_VC, Z_VS, Z_VW, Z_KR, Z_IK, Z_SMALL = (4864 + i * 128 for i in range(9))
Z_WIDTH = 6144
P_AQ, P_AK, P_AV, P_BQ, P_DK, P_CQ = (i * 512 for i in range(6))
P_CKV = 3072
P_KS, P_KW, P_VS, P_VW, P_KR, P_IK = (3328 + i * 128 for i in range(6))
P_WIDTH = 4096
T_DQ, T_IQ = 0, 512
T_ROWS = 1024
VT_TILE = 256
SMALL_G = 0
SMALL_IW = 12

ROPE_KINDS = ((16, 64), (32, 128), (64, 64))
TAB_W = 3 * LANES


def _cparams(n_axes):
    return pltpu.CompilerParams(dimension_semantics=("arbitrary",) * n_axes,
                                vmem_limit_bytes=VMEM_LIMIT)


def _dot(a, b):
    return jnp.dot(a, b, preferred_element_type=F32)


def _dot_nt(a, b):
    return lax.dot_general(a, b, (((1,), (1,)), ((), ())), preferred_element_type=F32)


def _in_proj_columns():
    names = (("a_q", 512), ("a_k", 512), ("a_v", 512), ("b_q", 512),
             ("b_kc", 128), ("b_vc", 128), ("b_ks", 128), ("b_vs", 128),
             ("b_kw", 128), ("b_vw", 128), ("b_g", 12),
             ("c_q", 384), ("c_kv", 256), ("c_kr", 64),
             ("d_q", 512), ("d_k", 512), ("d_v", 512),
             ("d_iq", 512), ("d_ik", 64), ("d_iw", 8), ("gate", 4096))
    off, o = {}, 0
    for nm, n in names:
        off[nm] = (o, n)
        o += n
    idx = np.full((Z_WIDTH,), -1, np.int64)

    def put(dst, nm):
        s, n = off[nm]
        idx[dst:dst + n] = np.arange(s, s + n)

    put(Z_AQ, "a_q"); put(Z_AK, "a_k"); put(Z_AV, "a_v"); put(Z_BQ, "b_q")
    put(Z_DQ, "d_q"); put(Z_DK, "d_k"); put(Z_DV, "d_v"); put(Z_IQ, "d_iq")
    put(Z_CQ, "c_q"); put(Z_CKV, "c_kv")
    put(Z_KC, "b_kc"); put(Z_KS, "b_ks"); put(Z_KW, "b_kw")
    put(Z_VC, "b_vc"); put(Z_VS, "b_vs"); put(Z_VW, "b_vw")
    put(Z_KR, "c_kr"); put(Z_KR + 64, "c_kr")
    put(Z_IK, "d_ik"); put(Z_IK + 64, "d_ik")
    put(Z_SMALL + SMALL_G, "b_g"); put(Z_SMALL + SMALL_IW, "d_iw")
    return idx, off["gate"][0], o


def _take_cols(w, idx):
    runs, i, n = [], 0, len(idx)
    while i < n:
        j = i + 1
        if idx[i] < 0:
            while j < n and idx[j] < 0:
                j += 1
            runs.append(jnp.zeros((w.shape[0], j - i), w.dtype))
        else:
            while j < n and idx[j] == idx[j - 1] + 1:
                j += 1
            runs.append(w[:, int(idx[i]):int(idx[i]) + (j - i)])
        i = j
    return jnp.concatenate(runs, axis=1)


def _rope_table(seq, rot, period):
    half = rot // 2
    inv = jnp.power(jnp.float32(ROPE_THETA), -jnp.arange(0, rot, 2, dtype=F32) / rot)
    ang = jnp.arange(seq, dtype=F32)[:, None] * inv[None, :]
    cos, sin = jnp.cos(ang), jnp.sin(ang)
    lane = np.arange(LANES) % period
    in1 = lane < half
    in2 = (lane >= half) & (lane < 2 * half)
    fidx = np.where(in1, lane, np.where(in2, lane - half, 0))
    cosl, sinl = cos[:, fidx], sin[:, fidx]
    c = jnp.where(jnp.asarray(in1 | in2)[None], cosl, 1.0)
    s1 = jnp.where(jnp.asarray(in1)[None], -sinl, 0.0)
    s2 = jnp.where(jnp.asarray(in2)[None], sinl, 0.0)
    return jnp.concatenate([c, s1, s2], axis=1)


def _rope128(x, tab, half):
    return (x * tab[:, 0:LANES]
            + pltpu.roll(x, LANES - half, 1) * tab[:, LANES:2 * LANES]
            + pltpu.roll(x, half, 1) * tab[:, 2 * LANES:3 * LANES])


PROJ_TILE = 512


def _proj_prep_kernel(x_ref, g_ref, w_ref, tab_ref, gq_ref, gkv_ref, gik_ref,
                      p_ref, t_ref, vt_ref, kc_ref, vc_ref, small_ref, smallt_ref):
    x = x_ref[...]
    ms = jnp.mean(x * x, axis=-1, keepdims=True)
    h = (x * lax.rsqrt(ms + EPS) * g_ref[...]).astype(BF16)
    z_tiles = {}

    def z_cols(off, width):
        t = off // PROJ_TILE
        assert (off + width - 1) // PROJ_TILE == t
        if t not in z_tiles:
            z_tiles[t] = _dot(h, w_ref[:, t * PROJ_TILE:(t + 1) * PROJ_TILE])
        lo = off - t * PROJ_TILE
        return z_tiles[t][:, lo:lo + width]

    def zc(off, c=0):
        return z_cols(off + c * LANES, LANES)

    def tab(kind):
        return tab_ref[:, kind * TAB_W:(kind + 1) * TAB_W]

    def put(off, c, v):
        p_ref[:, off + c * LANES:off + (c + 1) * LANES] = v.astype(BF16)

    def rope(off, c, kind):
        return _rope128(zc(off, c), tab(kind), ROPE_KINDS[kind][0] // 2)

    for zoff, poff, kind in ((Z_AQ, P_AQ, 0), (Z_AK, P_AK, 0), (Z_BQ, P_BQ, 1), (Z_DK, P_DK, 1)):
        for c in range(4):
            put(poff, c, rope(zoff, c, kind))
    for c in range(4):
        put(P_AV, c, zc(Z_AV, c))
    put(P_VS, 0, zc(Z_VS)); put(P_VW, 0, zc(Z_VW))
    put(P_KS, 0, rope(Z_KS, 0, 1)); put(P_KW, 0, rope(Z_KW, 0, 1))
    put(P_KR, 0, rope(Z_KR, 0, 2))
    kc_ref[...] = rope(Z_KC, 0, 1).astype(BF16)
    vc_ref[...] = zc(Z_VC).astype(BF16)

    for zoff, toff, kind in ((Z_DQ, T_DQ, 1), (Z_IQ, T_IQ, 0)):
        for c in range(4):
            t_ref[0, toff + c * LANES:toff + (c + 1) * LANES, :] = rope(zoff, c, kind).T.astype(BF16)
    for c in range(4):
        vt_ref[0, 0, c * LANES:(c + 1) * LANES, :] = zc(Z_DV, c).T.astype(BF16)

    cq = z_cols(Z_CQ, 512)
    ms = jnp.sum(cq * cq, axis=-1, keepdims=True) * (1.0 / MLA_Q_LORA)
    p_ref[:, P_CQ:P_CQ + 512] = (cq * lax.rsqrt(ms + EPS) * gq_ref[...]).astype(BF16)
    ckv = z_cols(Z_CKV, MLA_KV_LORA)
    ms = jnp.mean(ckv * ckv, axis=-1, keepdims=True)
    p_ref[:, P_CKV:P_CKV + MLA_KV_LORA] = (ckv * lax.rsqrt(ms + EPS) * gkv_ref[...]).astype(BF16)

    ik = zc(Z_IK)
    ms = jnp.mean(ik * ik, axis=-1, keepdims=True)
    ikn = ik * lax.rsqrt(ms + EPS) * gik_ref[...]
    put(P_IK, 0, _rope128(ikn, tab(0), ROPE_KINDS[0][0] // 2))

    sm = zc(Z_SMALL)
    lane = lax.broadcasted_iota(I32, sm.shape, 1)
    iw_scale = IDX_HEADS ** -0.5 * IDX_DIM ** -0.5
    small = jnp.where(lane < SMALL_IW, jax.nn.sigmoid(sm), sm * iw_scale)
    small_ref[...] = small
    smallt_ref[0] = small.T


def _proj_prep(x2, g, w, tab, gq, gkv, gik, *, batch, seq, ts):
    m, d = x2.shape
    spb = seq // ts
    assert ts == VT_TILE and w.shape == (d, Z_WIDTH)
    row = lambda w: pl.BlockSpec((ts, w), lambda i: (i, 0))
    return pl.pallas_call(
        _proj_prep_kernel,
        out_shape=(jax.ShapeDtypeStruct((m, P_WIDTH), BF16),
                   jax.ShapeDtypeStruct((batch, T_ROWS, seq), BF16),
                   jax.ShapeDtypeStruct((batch, spb, BR_WIDTH, VT_TILE), BF16),
                   jax.ShapeDtypeStruct((m, LANES), BF16),
                   jax.ShapeDtypeStruct((m, LANES), BF16),
                   jax.ShapeDtypeStruct((m, LANES), F32),
                   jax.ShapeDtypeStruct((batch, LANES, seq), F32)),
        grid=(m // ts,),
        in_specs=[row(d),
                  pl.BlockSpec((1, d), lambda i: (0, 0)),
                  pl.BlockSpec((d, Z_WIDTH), lambda i: (0, 0)),
                  pl.BlockSpec((ts, 3 * TAB_W), lambda i: (i % spb, 0)),
                  pl.BlockSpec((1, 512), lambda i: (0, 0)),
                  pl.BlockSpec((1, MLA_KV_LORA), lambda i: (0, 0)),
                  pl.BlockSpec((1, LANES), lambda i: (0, 0))],
        out_specs=(row(P_WIDTH),
                   pl.BlockSpec((1, T_ROWS, ts), lambda i: (i // spb, 0, i % spb)),
                   pl.BlockSpec((1, 1, BR_WIDTH, VT_TILE), lambda i: (i // spb, i % spb, 0, 0)),
                   row(LANES), row(LANES), row(LANES),
                   pl.BlockSpec((1, LANES, ts), lambda i: (i // spb, 0, i % spb))),
        compiler_params=_cparams(1),
        name="proj_prep",
    )(x2, g, w, tab, gq, gkv, gik)


def _softmax_init(mx_scr, l_scr, acc_scr):
    mx_scr[...] = jnp.full(mx_scr.shape, NEG, F32)
    l_scr[...] = jnp.zeros(l_scr.shape, F32)
    acc_scr[...] = jnp.zeros(acc_scr.shape, F32)


def _score_store(g, j, s, s_scr, mx_scr):
    s_scr[g, j] = s
    m = s[:, 0:LANES]
    for c in range(1, s.shape[1] // LANES):
        m = jnp.maximum(m, s[:, c * LANES:(c + 1) * LANES])
    mx_scr[g] = jnp.maximum(mx_scr[g], m)


def _row_max_finish(mx_scr):
    for g in range(mx_scr.shape[0]):
        m = jnp.max(mx_scr[g], axis=-1, keepdims=True)
        mx_scr[g] = jnp.broadcast_to(m, mx_scr.shape[1:])


def _prob_accumulate(g, j, v_tile, s_scr, mx_scr, l_scr, acc_scr):
    mb = mx_scr[g]
    s = s_scr[g, j]
    ps = [jnp.exp2(s[:, c * LANES:(c + 1) * LANES] - mb) for c in range(s.shape[1] // LANES)]
    tot = ps[0]
    for p in ps[1:]:
        tot = tot + p
    l_scr[g] += tot
    acc_scr[g] += _dot(jnp.concatenate(ps, axis=1).astype(BF16), v_tile)


def _softmax_out(g, l_scr, acc_scr):
    return acc_scr[g] / jnp.sum(l_scr[g], axis=-1, keepdims=True)


def _paired_tiles(n, step):
    def pair(jj, carry):
        step(2 * jj)
        step(2 * jj + 1)
        return carry

    lax.fori_loop(0, n // 2, pair, 0)

    @pl.when(n % 2 == 1)
    def _():
        step(n - 1)


def _causal_tiles(step, n_full):
    _paired_tiles(n_full, lambda j: step(j, False))
    step(n_full, True)


def _softmax_scratch(groups, n_tiles, rows, tk):
    return [pltpu.VMEM((groups, n_tiles, rows, tk), F32),
            pltpu.VMEM((groups, rows, LANES), F32),
            pltpu.VMEM((groups, rows, LANES), F32),
            pltpu.VMEM((groups, rows, HEAD_W), F32)]


def _diff_attn_kernel(q_ref, k_ref, v_ref, lv_ref, g_ref, o_ref, s_scr, mx_scr, l_scr, acc_scr,
                      *, tq, tk, lam_init):
    qs = pl.program_id(1) * tq
    n_full = qs // tk
    scale = DA_DIM ** -0.5 * LOG2E
    lv = lv_ref[...]
    lam = (jnp.exp(jnp.sum(lv[0:1] * lv[1:2], axis=-1, keepdims=True))
           - jnp.exp(jnp.sum(lv[2:3] * lv[3:4], axis=-1, keepdims=True)) + lam_init)
    lane = lax.broadcasted_iota(I32, (tq, HEAD_W), 1)
    row_t = qs + lax.broadcasted_iota(I32, (2 * tq, 1), 0) % tq
    col0 = lax.broadcasted_iota(I32, (2 * tq, tk), 1)
    _softmax_init(mx_scr, l_scr, acc_scr)

    def scores(j, masked):
        ks = pl.multiple_of(j * tk, tk)
        for h in range(HEADS):
            hs = slice(h * HEAD_W, (h + 1) * HEAD_W)
            qh = q_ref[0, :, hs]
            zero = jnp.zeros_like(qh)
            q2 = jnp.concatenate([jnp.where(lane < DA_DIM, qh, zero),
                                  jnp.where(lane >= DA_DIM, qh, zero)], axis=0)
            s = _dot_nt(q2, k_ref[0, pl.ds(ks, tk), hs]) * scale
            if masked:
                s = jnp.where(col0 + ks <= row_t, s, NEG)
            _score_store(h, j, s, s_scr, mx_scr)

    _causal_tiles(scores, n_full)
    _row_max_finish(mx_scr)

    def probs(j, carry):
        ks = pl.multiple_of(j * tk, tk)
        for h in range(HEADS):
            v_tile = v_ref[0, pl.ds(ks, tk), h * HEAD_W:(h + 1) * HEAD_W]
            _prob_accumulate(h, j, v_tile, s_scr, mx_scr, l_scr, acc_scr)
        return carry

    _paired_tiles(n_full + 1, lambda j: probs(j, 0))
    for h in range(HEADS):
        o2 = _softmax_out(h, l_scr, acc_scr)
        o = o2[:tq] - lam * o2[tq:]
        ms = jnp.mean(o * o, axis=-1, keepdims=True)
        o = o * lax.rsqrt(ms + EPS) * g_ref[...]
        o_ref[0, :, h * HEAD_W:(h + 1) * HEAD_W] = (o * (1.0 - lam_init)).astype(BF16)


def _diff_attn(p3, lv, g, *, lam_init, tq, tk):
    b, s, _ = p3.shape
    return pl.pallas_call(
        functools.partial(_diff_attn_kernel, tq=tq, tk=tk, lam_init=lam_init),
        out_shape=jax.ShapeDtypeStruct((b, s, BR_WIDTH), BF16),
        grid=(b, s // tq),
        in_specs=[pl.BlockSpec((1, tq, 512), lambda bi, i: (bi, i, P_AQ // 512)),
                  pl.BlockSpec((1, s, 512), lambda bi, i: (bi, 0, P_AK // 512)),
                  pl.BlockSpec((1, s, 512), lambda bi, i: (bi, 0, P_AV // 512)),
                  pl.BlockSpec((4, DA_DIM), lambda bi, i: (0, 0)),
                  pl.BlockSpec((1, HEAD_W), lambda bi, i: (0, 0))],
        out_specs=pl.BlockSpec((1, tq, BR_WIDTH), lambda bi, i: (bi, i, 0)),
        scratch_shapes=_softmax_scratch(HEADS, s // tk, 2 * tq, tk),
        compiler_params=_cparams(2),
        name="diff_attn",
    )(p3, p3, p3, lv, g)


def _nsa_compress_kernel(gk_ref, gv_ref, w1k_ref, w1v_ref, pek_ref, pev_ref,
                         w1kf_ref, w1vf_ref, w2k_ref, w2v_ref, kc_ref, vc_ref):
    def one(g_ref, w1cat_ref, pe_ref, w1f_ref, w2_ref, o_ref):
        y = _dot(g_ref[0], w1cat_ref[...])
        n = y.shape[0]
        nxt = pltpu.roll(y[:, HEAD_W:], n - 1, 0)
        c = _dot(pe_ref[...], w1f_ref[...])[0:1]
        hid = jax.nn.gelu(y[:, :HEAD_W] + nxt + c)
        o_ref[0] = _dot(hid.astype(BF16), w2_ref[...]).astype(BF16)

    one(gk_ref, w1k_ref, pek_ref, w1kf_ref, w2k_ref, kc_ref)
    one(gv_ref, w1v_ref, pev_ref, w1vf_ref, w2v_ref, vc_ref)


def _nsa_compress(gk, gv, w1k_cat, w1v_cat, pek, pev, w1k, w1v, w2k, w2v):
    b, ng, gw = gk.shape
    full = lambda shape: pl.BlockSpec(shape, lambda bi: (0,) * len(shape))
    return pl.pallas_call(
        _nsa_compress_kernel,
        out_shape=(jax.ShapeDtypeStruct((b, ng, HEAD_W), BF16),
                   jax.ShapeDtypeStruct((b, ng, HEAD_W), BF16)),
        grid=(b,),
        in_specs=[pl.BlockSpec((1, ng, gw), lambda bi: (bi, 0, 0)),
                  pl.BlockSpec((1, ng, gw), lambda bi: (bi, 0, 0)),
                  full(w1k_cat.shape), full(w1v_cat.shape), full(pek.shape), full(pev.shape),
                  full(w1k.shape), full(w1v.shape), full(w2k.shape), full(w2v.shape)],
        out_specs=(pl.BlockSpec((1, ng, HEAD_W), lambda bi: (bi, 0, 0)),
                   pl.BlockSpec((1, ng, HEAD_W), lambda bi: (bi, 0, 0))),
        compiler_params=_cparams(1),
        name="nsa_compress",
    )(gk, gv, w1k_cat, w1v_cat, pek, pev, w1k, w1v, w2k, w2v)


NSA_GROUPS = 2


def _nsa_kernel(q_ref, kc_ref, vc_ref, ks_ref, vs_ref, kw_ref, vw_ref, small_ref, ov_ref, e_ref,
                o_ref, s_scr, mx_scr, l_scr, acc_scr, cmp_scr, win_scr, *, tq, tk, seq):
    qs = pl.program_id(1) * tq
    scale = NSA_DK ** -0.5
    ns = seq // SEL_LEN
    n_sel = min(SEL_N, ns)
    r = HEADS * tq
    rg = r // NSA_GROUPS
    q4 = jnp.concatenate([q_ref[0, :, h * HEAD_W:(h + 1) * HEAD_W] for h in range(HEADS)], axis=0)
    t1 = qs + lax.broadcasted_iota(I32, (tq, 1), 0)
    t4 = qs + lax.broadcasted_iota(I32, (r, 1), 0) % tq

    wspan = WIN + tq
    start = pl.multiple_of(jnp.maximum(qs - WIN, 0), tq)
    sw = _dot_nt(q4, kw_ref[0, pl.ds(start, wspan), :]) * scale
    dist = t4 - (start + lax.broadcasted_iota(I32, (r, wspan), 1))
    sw = jnp.where(pltpu.bitcast(dist, jnp.uint32) < jnp.uint32(WIN), sw, NEG)
    e = jnp.exp(sw - jnp.max(sw, axis=-1, keepdims=True))
    pw = e / jnp.sum(e, axis=-1, keepdims=True)
    win_scr[...] = _dot(pw.astype(BF16), vw_ref[0, pl.ds(start, wspan), :])

    kc = kc_ref[0]
    nc_pad = kc.shape[0]
    sc = _dot_nt(q4, kc) * scale
    c_end = lax.broadcasted_iota(I32, (r, nc_pad), 1) * CMP_STRIDE + (CMP_LEN - 1)
    cmask = c_end <= t4
    mx = jnp.max(jnp.where(cmask, sc, NEG), axis=-1, keepdims=True)
    e = jnp.where(cmask, jnp.exp(sc - mx), 0.0)
    den = jnp.sum(e, axis=-1, keepdims=True)
    pc = e / jnp.where(den > 0.0, den, 1.0)
    cmp_scr[...] = _dot(pc.astype(BF16), vc_ref[0])

    psum = pc[0:tq] + pc[tq:2 * tq] + pc[2 * tq:3 * tq] + pc[3 * tq:4 * tq]
    ov = ov_ref[...]
    hi = psum.astype(BF16)
    r1 = psum - hi.astype(F32)
    mid = r1.astype(BF16)
    lo = (r1 - mid.astype(F32)).astype(BF16)
    imp = _dot(hi, ov) + _dot(mid, ov) + _dot(lo, ov)

    blk = lax.broadcasted_iota(I32, (tq, LANES), 1)
    cur = t1 // SEL_LEN
    forced = (blk == 0) | (blk == cur) | (blk == cur - 1)
    visible = blk * SEL_LEN <= t1
    score = jnp.where(visible, jnp.where(forced, FORCE_SCORE, imp), NEG)
    score = jnp.where(blk < ns, score, PAD_SCORE)
    ns_pad = -(-ns // SUBLANES) * SUBLANES
    score_t = score.T[:ns_pad]
    blk_t = lax.broadcasted_iota(I32, (ns_pad, tq), 0)
    rank = jnp.zeros((ns_pad, tq), I32)
    for jp in range(ns):
        row = score_t[jp:jp + 1, :]
        later = (blk_t > jp).astype(I32)
        rank = rank + jnp.where(row > score_t, 1, jnp.where(row == score_t, later, 0))
    sel_t = jnp.where(rank < n_sel, 1.0, 0.0)
    if ns_pad < LANES:
        sel_t = jnp.concatenate([sel_t, jnp.zeros((LANES - ns_pad, tq), F32)], axis=0)
    selb = sel_t.T.astype(BF16)

    _softmax_init(mx_scr, l_scr, acc_scr)
    col0 = lax.broadcasted_iota(I32, (rg, tk), 1)
    tg = qs + lax.broadcasted_iota(I32, (rg, 1), 0) % tq
    n_tiles = qs // tk + 1

    def scores(j, masked):
        ks0 = pl.multiple_of(j * tk, tk)
        mt = _dot(selb, e_ref[j])
        mg = jnp.concatenate([mt] * (rg // tq), axis=0)
        k_tile = ks_ref[0, pl.ds(ks0, tk), :]
        for g in range(NSA_GROUPS):
            s = _dot_nt(q4[g * rg:(g + 1) * rg], k_tile) * (scale * LOG2E)
            s = jnp.where(mg > 0.5, s, NEG)
            if masked:
                s = jnp.where(col0 + ks0 <= tg, s, NEG)
            _score_store(g, j, s, s_scr, mx_scr)

    _causal_tiles(scores, n_tiles - 1)
    _row_max_finish(mx_scr)

    def probs(j, carry):
        ks0 = pl.multiple_of(j * tk, tk)
        v_tile = vs_ref[0, pl.ds(ks0, tk), :]
        for g in range(NSA_GROUPS):
            _prob_accumulate(g, j, v_tile, s_scr, mx_scr, l_scr, acc_scr)
        return carry

    _paired_tiles(n_tiles, lambda j: probs(j, 0))
    o_slc = jnp.concatenate([_softmax_out(g, l_scr, acc_scr) for g in range(NSA_GROUPS)], axis=0)

    gates = small_ref[0]
    for h in range(HEADS):
        rows = slice(h * tq, (h + 1) * tq)
        g0 = gates[:, SMALL_G + 3 * h:SMALL_G + 3 * h + 1]
        g1 = gates[:, SMALL_G + 3 * h + 1:SMALL_G + 3 * h + 2]
        g2 = gates[:, SMALL_G + 3 * h + 2:SMALL_G + 3 * h + 3]
        o = g0 * cmp_scr[rows, :] + g1 * o_slc[rows] + g2 * win_scr[rows, :]
        o_ref[0, :, h * HEAD_W:(h + 1) * HEAD_W] = o.astype(BF16)


def _nsa(p3, kc, vc, small3, ov, emat, *, tq, tk):
    b, s, _ = p3.shape
    ng = kc.shape[1]
    col = lambda off: (lambda bi, i: (bi, 0, off // LANES))
    return pl.pallas_call(
        functools.partial(_nsa_kernel, tq=tq, tk=tk, seq=s),
        out_shape=jax.ShapeDtypeStruct((b, s, BR_WIDTH), BF16),
        grid=(b, s // tq),
        in_specs=[pl.BlockSpec((1, tq, 512), lambda bi, i: (bi, i, P_BQ // 512)),
                  pl.BlockSpec((1, ng, HEAD_W), lambda bi, i: (bi, 0, 0)),
                  pl.BlockSpec((1, ng, HEAD_W), lambda bi, i: (bi, 0, 0)),
                  pl.BlockSpec((1, s, LANES), col(P_KS)),
                  pl.BlockSpec((1, s, LANES), col(P_VS)),
                  pl.BlockSpec((1, s, LANES), col(P_KW)),
                  pl.BlockSpec((1, s, LANES), col(P_VW)),
                  pl.BlockSpec((1, tq, LANES), lambda bi, i: (bi, i, 0)),
                  pl.BlockSpec(ov.shape, lambda bi, i: (0, 0)),
                  pl.BlockSpec(emat.shape, lambda bi, i: (0, 0, 0))],
        out_specs=pl.BlockSpec((1, tq, BR_WIDTH), lambda bi, i: (bi, i, 0)),
        scratch_shapes=(_softmax_scratch(NSA_GROUPS, s // tk, HEADS * tq // NSA_GROUPS, tk)
                        + [pltpu.VMEM((HEADS * tq, HEAD_W), F32), pltpu.VMEM((HEADS * tq, HEAD_W), F32)]),
        compiler_params=_cparams(2),
        name="nsa_attn",
    )(p3, kc, vc, p3, p3, p3, p3, small3, ov, emat)


def _mla_up_kernel(p_ref, ckv_ref, tab_ref, wq_ref, wkv_ref, q_ref, kv_ref):
    q = _dot(p_ref[...], wq_ref[...])
    nn = HEADS * MLA_NOPE
    q_ref[:, :nn] = q[:, :nn].astype(BF16)
    for c in range(nn // LANES, (nn + HEADS * MLA_ROPE) // LANES):
        tile = _rope128(q[:, c * LANES:(c + 1) * LANES], tab_ref[...], MLA_ROPE // 2)
        q_ref[:, c * LANES:(c + 1) * LANES] = tile.astype(BF16)
    kv_ref[...] = _dot(ckv_ref[...], wkv_ref[...]).astype(BF16)


def _mla_up(p2, tab, wq, wkv, *, seq, ts):
    m = p2.shape[0]
    spb = seq // ts
    nq = wq.shape[1]
    nkv = wkv.shape[1]
    return pl.pallas_call(
        _mla_up_kernel,
        out_shape=(jax.ShapeDtypeStruct((m, nq), BF16), jax.ShapeDtypeStruct((m, nkv), BF16)),
        grid=(m // ts,),
        in_specs=[pl.BlockSpec((ts, 512), lambda i: (i, P_CQ // 512)),
                  pl.BlockSpec((ts, MLA_KV_LORA), lambda i: (i, P_CKV // MLA_KV_LORA)),
                  pl.BlockSpec((ts, TAB_W), lambda i: (i % spb, 2)),
                  pl.BlockSpec(wq.shape, lambda i: (0, 0)),
                  pl.BlockSpec(wkv.shape, lambda i: (0, 0))],
        out_specs=(pl.BlockSpec((ts, nq), lambda i: (i, 0)),
                   pl.BlockSpec((ts, nkv), lambda i: (i, 0))),
        compiler_params=_cparams(1),
        name="mla_up",
    )(p2, p2, tab, wq, wkv)


def _mla_attn_kernel(qn_ref, qr_ref, kn_ref, kr_ref, v_ref, o_ref, s_scr, mx_scr, l_scr, acc_scr,
                     *, tq, tk):
    qs = pl.program_id(1) * tq
    n_full = qs // tk
    scale = (MLA_NOPE + MLA_ROPE) ** -0.5 * LOG2E
    lane = lax.broadcasted_iota(I32, (tq, LANES), 1)
    row_t = qs + lax.broadcasted_iota(I32, (tq, 1), 0)
    col0 = lax.broadcasted_iota(I32, (tq, tk), 1)
    _softmax_init(mx_scr, l_scr, acc_scr)

    def scores(j, masked):
        ks = pl.multiple_of(j * tk, tk)
        kr_tile = kr_ref[0, pl.ds(ks, tk), :]
        for h in range(HEADS):
            hs = slice(h * HEAD_W, (h + 1) * HEAD_W)
            pair = qr_ref[0, :, (h // 2) * LANES:(h // 2 + 1) * LANES]
            keep = (lane < MLA_ROPE) if h % 2 == 0 else (lane >= MLA_ROPE)
            qr = jnp.where(keep, pair, jnp.zeros_like(pair))
            s = _dot_nt(jnp.concatenate([qn_ref[0, :, hs], qr], axis=1),
                        jnp.concatenate([kn_ref[0, pl.ds(ks, tk), hs], kr_tile], axis=1)) * scale
            if masked:
                s = jnp.where(col0 + ks <= row_t, s, NEG)
            _score_store(h, j, s, s_scr, mx_scr)

    _causal_tiles(scores, n_full)
    _row_max_finish(mx_scr)

    def probs(j, carry):
        ks = pl.multiple_of(j * tk, tk)
        for h in range(HEADS):
            v_tile = v_ref[0, pl.ds(ks, tk), h * HEAD_W:(h + 1) * HEAD_W]
            _prob_accumulate(h, j, v_tile, s_scr, mx_scr, l_scr, acc_scr)
        return carry

    _paired_tiles(n_full + 1, lambda j: probs(j, 0))
    for h in range(HEADS):
        o_ref[0, :, h * HEAD_W:(h + 1) * HEAD_W] = _softmax_out(h, l_scr, acc_scr).astype(BF16)


def _mla_attn(q3, kv3, p3, *, tq, tk):
    b, s, _ = q3.shape
    return pl.pallas_call(
        functools.partial(_mla_attn_kernel, tq=tq, tk=tk),
        out_shape=jax.ShapeDtypeStruct((b, s, BR_WIDTH), BF16),
        grid=(b, s // tq),
        in_specs=[pl.BlockSpec((1, tq, 512), lambda bi, i: (bi, i, 0)),
                  pl.BlockSpec((1, tq, 256), lambda bi, i: (bi, i, 2)),
                  pl.BlockSpec((1, s, 512), lambda bi, i: (bi, 0, 0)),
                  pl.BlockSpec((1, s, LANES), lambda bi, i: (bi, 0, P_KR // LANES)),
                  pl.BlockSpec((1, s, 512), lambda bi, i: (bi, 0, 1))],
        out_specs=pl.BlockSpec((1, tq, BR_WIDTH), lambda bi, i: (bi, i, 0)),
        scratch_shapes=_softmax_scratch(HEADS, s // tk, tq, tk),
        compiler_params=_cparams(2),
        name="mla_attn",
    )(q3, q3, kv3, p3, kv3)


def _sortable_key(x):
    bits = pltpu.bitcast(x + 0.0, I32)
    return bits ^ (lax.shift_right_arithmetic(bits, 31) & 0x7FFFFFFF)


def _fold_rows(x, op):
    n = x.shape[0] // SUBLANES
    return op(x.reshape(n, SUBLANES, x.shape[1]), axis=0)


def _count16(half_scr, n_tiles, pred, tq):
    def count_tile(j, cnt):
        hit = pred(half_scr[j]).astype(I16)
        parts = [hit[r:r + PACKED_ROWS] for r in range(0, hit.shape[0], PACKED_ROWS)]
        while len(parts) > 1:
            parts = [a + b for a, b in zip(parts[0::2], parts[1::2])]
        return cnt + parts[0]

    cnt = lax.fori_loop(0, n_tiles, count_tile, jnp.zeros((PACKED_ROWS, tq), I16))
    return jnp.sum(cnt.astype(I32), axis=0, keepdims=True)


def _bisect16(half_scr, n_tiles, need, tq):
    def bit_body(i, carry):
        th, tot = carry
        cand = th + lax.shift_left(jnp.int32(1), 15 - i)
        c16 = cand.astype(I16)
        total = _count16(half_scr, n_tiles, lambda x: x >= c16, tq)
        ok = total >= need
        return jnp.where(ok, cand, th), jnp.where(ok, total, tot)

    return lax.fori_loop(0, 16, bit_body, (jnp.full((1, tq), HALF_MIN, I32), jnp.full((1, tq), -1, I32)))


def _dsa_kernel(qt_ref, iqt_ref, iwt_ref, k_ref, ik_ref, vt_ref, tri_ref, o_ref,
                key_scr, half_scr, s_scr, mx_scr, l_scr, acc_scr, *, tq, tk, top):
    qs = pl.program_id(1) * tq
    n_tiles = (qs + tq - 1) // tk + 1
    scale = DSA_DIM ** -0.5 * LOG2E
    t_lane = qs + lax.broadcasted_iota(I32, (tk, tq), 1)
    krow0 = lax.broadcasted_iota(I32, (tk, tq), 0)
    half_rows = lax.broadcasted_iota(I32, (LANES, tq), 0) < IDX_DIM
    vt_per_tile = tk // VT_TILE

    def score_tile(j, masked):
        ks = pl.multiple_of(j * tk, tk)
        ikt = ik_ref[0, pl.ds(ks, tk), :]
        acc = jnp.zeros((tk, tq), F32)
        for h in range(IDX_HEADS):
            pair = iqt_ref[0, (h // 2) * LANES:(h // 2 + 1) * LANES, :]
            keep = half_rows if h % 2 == 0 else jnp.logical_not(half_rows)
            iq_h = jnp.where(keep, pair, jnp.zeros_like(pair))
            w_h = iwt_ref[0, SMALL_IW + h:SMALL_IW + h + 1, :]
            acc = acc + w_h * jnp.maximum(_dot(ikt, iq_h), 0.0)
        key = _sortable_key(acc)
        if masked:
            key = jnp.where(krow0 + ks <= t_lane, key, INT_MIN)
        key_scr[j] = key
        half_scr[j] = lax.shift_right_arithmetic(key, 16).astype(I16)

    _causal_tiles(score_tile, n_tiles - 1)

    th_hi, n_ge_hi = _bisect16(half_scr, n_tiles, top, tq)
    hi16 = th_hi.astype(I16)
    n_gt_hi = _count16(half_scr, n_tiles, lambda x: x > hi16, tq)
    need_lo = top - n_gt_hi

    def low_tile(j, carry):
        key = key_scr[j]
        lo = (key & 0xFFFF) + HALF_MIN
        same_hi = lax.shift_right_arithmetic(key, 16) == th_hi
        half_scr[j] = jnp.where(same_hi, lo, HALF_MIN).astype(I16)
        return carry

    lax.fori_loop(0, n_tiles, low_tile, 0)
    th_lo, n_ge_lo = _bisect16(half_scr, n_tiles, need_lo, tq)
    theta = lax.shift_left(th_hi, 16) + (th_lo - HALF_MIN)
    n_ge_lo = jnp.where(n_ge_lo >= 0, n_ge_lo, n_ge_hi - n_gt_hi)
    tied = (n_ge_lo > need_lo) & (n_ge_hi >= 0)

    @pl.when(jnp.max(tied.astype(I32)) > 0)
    def _():
        lo16 = th_lo.astype(I16)
        n_gt = n_gt_hi + _count16(half_scr, n_tiles, lambda x: x > lo16, tq)
        keep_eq = jnp.where(tied, top - n_gt, tk * key_scr.shape[0]).astype(F32)

        def demote(j, seen):
            k = key_scr[j]
            eq = k == theta
            eqf = jnp.where(eq, 1.0, 0.0)
            before = _dot(tri_ref[...], eqf.astype(BF16)) + seen
            key_scr[j] = jnp.where(jnp.where(eq, before, -1.0) >= keep_eq, k - 1, k)
            return seen + jnp.sum(_fold_rows(eqf, jnp.sum), axis=0, keepdims=True)

        lax.fori_loop(0, n_tiles, demote, jnp.zeros((1, tq), F32))

    theta = jnp.maximum(theta, INT_MIN + 1)

    mx_scr[...] = jnp.full(mx_scr.shape, NEG, F32)
    l_scr[...] = jnp.zeros(l_scr.shape, F32)
    acc_scr[...] = jnp.zeros(acc_scr.shape, F32)

    def scores(j, carry):
        ks = pl.multiple_of(j * tk, tk)
        sel = key_scr[j] >= theta
        for h in range(HEADS):
            hs = slice(h * HEAD_W, (h + 1) * HEAD_W)
            s = _dot(k_ref[0, pl.ds(ks, tk), hs], qt_ref[0, hs, :]) * scale
            s = jnp.where(sel, s, NEG)
            s_scr[h, j] = s
            mx_scr[h] = jnp.maximum(mx_scr[h], _fold_rows(s, jnp.max))
        return carry

    _paired_tiles(n_tiles, lambda j: scores(j, 0))
    for h in range(HEADS):
        m = jnp.max(mx_scr[h], axis=0, keepdims=True)
        mx_scr[h] = jnp.broadcast_to(m, (SUBLANES, tq))

    def probs(j, carry):
        for h in range(HEADS):
            p = jnp.exp2(s_scr[h, j] - mx_scr[h][0:1])
            l_scr[h] += _fold_rows(p, jnp.sum)
            pb = p.astype(BF16)
            for c in range(vt_per_tile):
                vt = vt_ref[0, j * vt_per_tile + c, h * HEAD_W:(h + 1) * HEAD_W, :]
                acc_scr[h] += _dot(vt, pb[c * VT_TILE:(c + 1) * VT_TILE])
        return carry

    _paired_tiles(n_tiles, lambda j: probs(j, 0))
    for h in range(HEADS):
        ot = acc_scr[h] / jnp.sum(l_scr[h], axis=0, keepdims=True)
        o_ref[0, :, h * HEAD_W:(h + 1) * HEAD_W] = ot.T.astype(BF16)


def _dsa(p3, t3, vt4, smallt, *, tq, tk, top):
    b, s, _ = p3.shape
    n_vt = vt4.shape[1]
    tri = jnp.asarray(np.tril(np.ones((tk, tk), np.float32), -1), BF16)
    return pl.pallas_call(
        functools.partial(_dsa_kernel, tq=tq, tk=tk, top=top),
        out_shape=jax.ShapeDtypeStruct((b, s, BR_WIDTH), BF16),
        grid=(b, s // tq),
        in_specs=[pl.BlockSpec((1, 512, tq), lambda bi, i: (bi, T_DQ // 512, i)),
                  pl.BlockSpec((1, 512, tq), lambda bi, i: (bi, T_IQ // 512, i)),
                  pl.BlockSpec((1, LANES, tq), lambda bi, i: (bi, 0, i)),
                  pl.BlockSpec((1, s, 512), lambda bi, i: (bi, 0, P_DK // 512)),
                  pl.BlockSpec((1, s, LANES), lambda bi, i: (bi, 0, P_IK // LANES)),
                  pl.BlockSpec((1, n_vt, BR_WIDTH, VT_TILE), lambda bi, i: (bi, 0, 0, 0)),
                  pl.BlockSpec((tk, tk), lambda bi, i: (0, 0))],
        out_specs=pl.BlockSpec((1, tq, BR_WIDTH), lambda bi, i: (bi, i, 0)),
        scratch_shapes=[pltpu.VMEM((s // tk, tk, tq), I32),
                        pltpu.VMEM((s // tk, tk, tq), I16),
                        pltpu.VMEM((HEADS, s // tk, tk, tq), F32),
                        pltpu.VMEM((HEADS, SUBLANES, tq), F32),
                        pltpu.VMEM((HEADS, SUBLANES, tq), F32),
                        pltpu.VMEM((HEADS, HEAD_W, tq), F32)],
        compiler_params=_cparams(2),
        name="dsa_attn",
    )(t3, t3, smallt, p3, p3, vt4, tri)


def _merge_kernel(x_ref, g_ref, oa_ref, ob_ref, oc_ref, od_ref, wg_ref, wb_ref, wo_ref, o_ref):
    x = x_ref[...]
    d = x.shape[1]
    ms = jnp.mean(x * x, axis=-1, keepdims=True)
    h = (x * lax.rsqrt(ms + EPS) * g_ref[...]).astype(BF16)
    acc = jnp.zeros(x.shape, F32)
    for n, br_ref in enumerate((oa_ref, ob_ref, oc_ref, od_ref)):
        gate = jax.nn.sigmoid(_dot(h, wg_ref[:, n * d:(n + 1) * d]))
        acc = acc + gate * _dot(br_ref[...], wb_ref[n])
    o_ref[...] = x + _dot(acc.astype(BF16), wo_ref[...])


def _merge(x2, g, oa, ob, oc, od, wg, wb, wo, *, tm):
    m, d = x2.shape
    row = lambda w: pl.BlockSpec((tm, w), lambda i: (i, 0))
    return pl.pallas_call(
        _merge_kernel,
        out_shape=jax.ShapeDtypeStruct((m, d), F32),
        grid=(m // tm,),
        in_specs=[row(d), pl.BlockSpec((1, d), lambda i: (0, 0)),
                  row(BR_WIDTH), row(BR_WIDTH), row(BR_WIDTH), row(BR_WIDTH),
                  pl.BlockSpec(wg.shape, lambda i: (0, 0)),
                  pl.BlockSpec(wb.shape, lambda i: (0, 0, 0)),
                  pl.BlockSpec(wo.shape, lambda i: (0, 0))],
        out_specs=row(d),
        compiler_params=_cparams(1),
        name="merge",
    )(x2, g, oa, ob, oc, od, wg, wb, wo)


def _ffn_kernel(x_ref, g_ref, wg_ref, wu_ref, wd_ref, gf_ref, o_ref, h_scr, acc_scr, *, final):
    j = pl.program_id(1)

    @pl.when(j == 0)
    def _():
        x = x_ref[...]
        ms = jnp.mean(x * x, axis=-1, keepdims=True)
        h_scr[...] = (x * lax.rsqrt(ms + EPS) * g_ref[...]).astype(BF16)
        acc_scr[...] = jnp.zeros(acc_scr.shape, F32)

    h = h_scr[...]
    a = jax.nn.silu(_dot(h, wg_ref[...])) * _dot(h, wu_ref[...])
    acc_scr[...] += _dot(a.astype(BF16), wd_ref[...])

    @pl.when(j == pl.num_programs(1) - 1)
    def _():
        y = x_ref[...] + acc_scr[...]
        if final:
            ms = jnp.mean(y * y, axis=-1, keepdims=True)
            y = y * lax.rsqrt(ms + EPS) * gf_ref[...]
        o_ref[...] = y


def _ffn(x2, g, wg, wu, wd, gf, *, final, tm, tf):
    m, d = x2.shape
    dff = wg.shape[1]
    return pl.pallas_call(
        functools.partial(_ffn_kernel, final=final),
        out_shape=jax.ShapeDtypeStruct((m, d), F32),
        grid=(m // tm, dff // tf),
        in_specs=[pl.BlockSpec((tm, d), lambda i, j: (i, 0)),
                  pl.BlockSpec((1, d), lambda i, j: (0, 0)),
                  pl.BlockSpec((d, tf), lambda i, j: (0, j)),
                  pl.BlockSpec((d, tf), lambda i, j: (0, j)),
                  pl.BlockSpec((tf, d), lambda i, j: (j, 0)),
                  pl.BlockSpec((1, d), lambda i, j: (0, 0))],
        out_specs=pl.BlockSpec((tm, d), lambda i, j: (i, 0)),
        scratch_shapes=[pltpu.VMEM((tm, d), BF16), pltpu.VMEM((tm, d), F32)],
        compiler_params=_cparams(2),
        name="ffn",
    )(x2, g, wg, wu, wd, gf)


def _tiles(seq, m, dff):
    pick = lambda n, cands: next(c for c in cands if n % c == 0)
    tk = pick(seq, (512, 256))
    return dict(
        merge_tm=pick(m, (256, 128)),
        prep_ts=VT_TILE,
        diff_tq=128, mla_tq=pick(seq, (256, 128)), nsa_tq=256, dsa_tq=256, tk=tk,
        row_tm=pick(m, (512, 256, 128)),
        ffn_tf=pick(dff, (1408, 704, 256, 128)),
    )


def kernel(x, norm1_g, w_in, diff_lq1, diff_lk1, diff_lq2, diff_lk2, diff_subln_g, nsa_pe_k, nsa_w1_k, nsa_w2_k, nsa_pe_v, nsa_w1_v, nsa_w2_v, mla_q_norm_g, mla_w_uq, mla_kv_norm_g, mla_w_ukv, idx_k_norm_g, w_branch, w_out, norm2_g, w_gate_up, w_down, final_norm_g):
    b, seq, d = x.shape
    depth = w_in.shape[0]
    m = b * seq
    dff = w_down.shape[1]
    t = _tiles(seq, m, dff)
    tk = t["tk"]
    assert seq % SEL_LEN == 0 and seq >= WIN + t["nsa_tq"] and seq // SEL_LEN <= LANES
    assert seq % t["dsa_tq"] == 0 and tk % VT_TILE == 0 and tk >= min(IDX_TOPK, seq // 4)

    col_idx, gate_off, d_in = _in_proj_columns()
    assert w_in.shape[2] == d_in
    tab = jnp.concatenate([_rope_table(seq, rot, per) for rot, per in ROPE_KINDS], axis=1)

    ng = seq // CMP_STRIDE
    ns = seq // SEL_LEN
    c_start = np.arange(ng)[:, None] * CMP_STRIDE
    s_start = np.arange(LANES)[None, :] * SEL_LEN
    ov = ((c_start < s_start + SEL_LEN) & (c_start + CMP_LEN - 1 >= s_start)
          & (np.arange(LANES)[None, :] < ns))
    ov = jnp.asarray(ov, BF16)
    emat = np.arange(LANES)[:, None] == (np.arange(seq)[None, :] // SEL_LEN)
    emat = jnp.asarray(emat.reshape(LANES, seq // tk, tk).transpose(1, 0, 2), BF16)

    qd = MLA_NOPE + MLA_ROPE
    uq_idx = np.concatenate([np.concatenate([np.arange(h * qd, h * qd + MLA_NOPE) for h in range(HEADS)]),
                             np.concatenate([np.arange(h * qd + MLA_NOPE, (h + 1) * qd) for h in range(HEADS)])])
    kvd = MLA_NOPE + HEAD_W
    ukv_idx = np.concatenate([np.concatenate([np.arange(h * kvd, h * kvd + MLA_NOPE) for h in range(HEADS)]),
                              np.concatenate([np.arange(h * kvd + MLA_NOPE, (h + 1) * kvd) for h in range(HEADS)])])

    x2 = x.reshape(m, d)
    half_w1 = CMP_STRIDE * NSA_DK
    for l in range(depth):
        lam_init = 0.8 - 0.6 * math.exp(-0.3 * l)
        w_in_l = w_in[l].astype(BF16)
        w_mix = _take_cols(w_in_l, col_idx)
        w_gate = w_in_l[:, gate_off:]

        gq = jnp.pad(mla_q_norm_g[l], (0, 512 - MLA_Q_LORA))[None]
        gkv = mla_kv_norm_g[l][None]
        gik = jnp.concatenate([idx_k_norm_g[l], idx_k_norm_g[l]])[None]
        p2, t3, vt4, kc_tok, vc_tok, small, smallt = _proj_prep(
            x2, norm1_g[l][None], w_mix, tab, gq, gkv, gik, batch=b, seq=seq, ts=t["prep_ts"])
        p3 = p2.reshape(b, seq, P_WIDTH)
        small3 = small.reshape(b, seq, LANES)

        lv = jnp.stack([diff_lq1[l], diff_lk1[l], diff_lq2[l], diff_lk2[l]])
        o_a = _diff_attn(p3, lv, diff_subln_g[l][None], lam_init=lam_init, tq=t["diff_tq"], tk=tk)

        w1k, w1v = nsa_w1_k[l].astype(BF16), nsa_w1_v[l].astype(BF16)
        w1k_cat = jnp.concatenate([w1k[:half_w1], w1k[half_w1:]], axis=1)
        w1v_cat = jnp.concatenate([w1v[:half_w1], w1v[half_w1:]], axis=1)
        pek = jnp.broadcast_to(nsa_pe_k[l].reshape(1, -1), (8, CMP_LEN * NSA_DK)).astype(BF16)
        pev = jnp.broadcast_to(nsa_pe_v[l].reshape(1, -1), (8, CMP_LEN * NSA_DK)).astype(BF16)
        kc, vc = _nsa_compress(kc_tok.reshape(b, ng, half_w1), vc_tok.reshape(b, ng, half_w1),
                               w1k_cat, w1v_cat, pek, pev, w1k, w1v,
                               nsa_w2_k[l].astype(BF16), nsa_w2_v[l].astype(BF16))
        o_b = _nsa(p3, kc, vc, small3, ov, emat, tq=t["nsa_tq"], tk=tk)

        wq = jnp.pad(_take_cols(mla_w_uq[l].astype(BF16), uq_idx), ((0, 512 - MLA_Q_LORA), (0, 0)))
        wkv = _take_cols(mla_w_ukv[l].astype(BF16), ukv_idx)
        q_c, kv_c = _mla_up(p2, tab, wq, wkv, seq=seq, ts=t["prep_ts"])
        o_c = _mla_attn(q_c.reshape(b, seq, -1), kv_c.reshape(b, seq, -1), p3, tq=t["mla_tq"], tk=tk)

        o_d = _dsa(p3, t3, vt4, smallt, tq=t["dsa_tq"], tk=tk, top=min(IDX_TOPK, seq // 4))

        x2 = _merge(x2, norm1_g[l][None],
                    o_a.reshape(m, -1), o_b.reshape(m, -1), o_c.reshape(m, -1), o_d.reshape(m, -1),
                    w_gate, w_branch[l].astype(BF16), w_out[l].astype(BF16), tm=t["merge_tm"])
        wgu = w_gate_up[l].astype(BF16)
        x2 = _ffn(x2, norm2_g[l][None], wgu[:, :dff], wgu[:, dff:], w_down[l].astype(BF16),
                  final_norm_g[None], final=(l == depth - 1), tm=t["row_tm"], tf=t["ffn_tf"])
    return x2.reshape(b, seq, d)
```

```python
import functools
import math

import numpy as np
import jax
import jax.numpy as jnp
from jax import lax
from jax.experimental import pallas as pl
from jax.experimental.pallas import tpu as pltpu

F32 = jnp.float32
BF16 = jnp.bfloat16
I32 = jnp.int32
I16 = jnp.int16

LANES = 128
SUBLANES = 8
PACKED_ROWS = 16
HALF_MIN = -32768
VMEM_LIMIT = 56 * 1024 * 1024

ROPE_THETA = 500000.0
NEG = -1e30
LOG2E = math.log2(math.e)
FORCE_SCORE = 1e9
PAD_SCORE = -3e38
EPS = 1e-6
INT_MIN = -2147483648

HEADS = 4
HEAD_W = 128
BR_WIDTH = HEADS * HEAD_W
DA_DIM = 64
NSA_DK = 128
CMP_LEN = 32
CMP_STRIDE = 16
SEL_LEN = 64
SEL_N = 16
WIN = 512
MLA_Q_LORA = 384
MLA_KV_LORA = 256
MLA_NOPE = 128
MLA_ROPE = 64
DSA_DIM = 128
IDX_HEADS = 8
IDX_DIM = 64
IDX_TOPK = 256

Z_AQ, Z_AK, Z_AV, Z_BQ, Z_DQ, Z_DK, Z_DV, Z_IQ = (i * 512 for i in range(8))
Z_CQ = 4096
Z_CKV = 4608
Z_KC, Z_KS, Z_KW, Z_VC, Z_VS, Z_VW, Z_KR, Z_IK, Z_SMALL = (4864 + i * 128 for i in range(9))
Z_WIDTH = 6144
P_AQ, P_AK, P_AV, P_BQ, P_DK = (i * 512 for i in range(5))
P_KS, P_KW, P_VS, P_VW, P_KR, P_IK = (2560 + i * 128 for i in range(6))
P_WIDTH = 3328
T_DQ, T_IQ = 0, 512
T_ROWS = 1024
VT_TILE = 256
SMALL_G = 0
SMALL_IW = 12

ROPE_KINDS = ((16, 64), (32, 128), (64, 64))
TAB_W = 3 * LANES


def _cparams(n_axes):
    return pltpu.CompilerParams(dimension_semantics=("arbitrary",) * n_axes,
                                vmem_limit_bytes=VMEM_LIMIT)


def _dot(a, b):
    return jnp.dot(a, b, preferred_element_type=F32)


def _dot_nt(a, b):
    return lax.dot_general(a, b, (((1,), (1,)), ((), ())), preferred_element_type=F32)


def _in_proj_columns():
    names = (("a_q", 512), ("a_k", 512), ("a_v", 512), ("b_q", 512),
             ("b_kc", 128), ("b_vc", 128), ("b_ks", 128), ("b_vs", 128),
             ("b_kw", 128), ("b_vw", 128), ("b_g", 12),
             ("c_q", 384), ("c_kv", 256), ("c_kr", 64),
             ("d_q", 512), ("d_k", 512), ("d_v", 512),
             ("d_iq", 512), ("d_ik", 64), ("d_iw", 8), ("gate", 4096))
    off, o = {}, 0
    for nm, n in names:
        off[nm] = (o, n)
        o += n
    idx = np.full((Z_WIDTH,), -1, np.int64)

    def put(dst, nm):
        s, n = off[nm]
        idx[dst:dst + n] = np.arange(s, s + n)

    put(Z_AQ, "a_q"); put(Z_AK, "a_k"); put(Z_AV, "a_v"); put(Z_BQ, "b_q")
    put(Z_DQ, "d_q"); put(Z_DK, "d_k"); put(Z_DV, "d_v"); put(Z_IQ, "d_iq")
    put(Z_CQ, "c_q"); put(Z_CKV, "c_kv")
    put(Z_KC, "b_kc"); put(Z_KS, "b_ks"); put(Z_KW, "b_kw")
    put(Z_VC, "b_vc"); put(Z_VS, "b_vs"); put(Z_VW, "b_vw")
    put(Z_KR, "c_kr"); put(Z_KR + 64, "c_kr")
    put(Z_IK, "d_ik"); put(Z_IK + 64, "d_ik")
    put(Z_SMALL + SMALL_G, "b_g"); put(Z_SMALL + SMALL_IW, "d_iw")
    return idx, off["gate"][0], o


def _take_cols(w, idx):
    runs, i, n = [], 0, len(idx)
    while i < n:
        j = i + 1
        if idx[i] < 0:
            while j < n and idx[j] < 0:
                j += 1
            runs.append(jnp.zeros((w.shape[0], j - i), w.dtype))
        else:
            while j < n and idx[j] == idx[j - 1] + 1:
                j += 1
            runs.append(w[:, int(idx[i]):int(idx[i]) + (j - i)])
        i = j
    return jnp.concatenate(runs, axis=1)


def _rope_table(seq, rot, period):
    half = rot // 2
    inv = jnp.power(jnp.float32(ROPE_THETA), -jnp.arange(0, rot, 2, dtype=F32) / rot)
    ang = jnp.arange(seq, dtype=F32)[:, None] * inv[None, :]
    cos, sin = jnp.cos(ang), jnp.sin(ang)
    lane = np.arange(LANES) % period
    in1 = lane < half
    in2 = (lane >= half) & (lane < 2 * half)
    fidx = np.where(in1, lane, np.where(in2, lane - half, 0))
    cosl, sinl = cos[:, fidx], sin[:, fidx]
    c = jnp.where(jnp.asarray(in1 | in2)[None], cosl, 1.0)
    s1 = jnp.where(jnp.asarray(in1)[None], -sinl, 0.0)
    s2 = jnp.where(jnp.asarray(in2)[None], sinl, 0.0)
    return jnp.concatenate([c, s1, s2], axis=1)


def _rope128(x, tab, half):
    return (x * tab[:, 0:LANES]
            + pltpu.roll(x, LANES - half, 1) * tab[:, LANES:2 * LANES]
            + pltpu.roll(x, half, 1) * tab[:, 2 * LANES:3 * LANES])


PROJ_TILE = 512


def _proj_prep_kernel(x_ref, g_ref, w_ref, tab_ref, gq_ref, gkv_ref, gik_ref, wq_ref, wkv_ref,
                      p_ref, t_ref, vt_ref, kc_ref, vc_ref, small_ref, smallt_ref, q_ref, kv_ref):
    x = x_ref[...]
    ms = jnp.mean(x * x, axis=-1, keepdims=True)
    h = (x * lax.rsqrt(ms + EPS) * g_ref[...]).astype(BF16)
    z_tiles = {}

    def z_cols(off, width):
        t = off // PROJ_TILE
        assert (off + width - 1) // PROJ_TILE == t
        if t not in z_tiles:
            z_tiles[t] = _dot(h, w_ref[:, t * PROJ_TILE:(t + 1) * PROJ_TILE])
        lo = off - t * PROJ_TILE
        return z_tiles[t][:, lo:lo + width]

    def zc(off, c=0):
        return z_cols(off + c * LANES, LANES)

    def tab(kind):
        return tab_ref[:, kind * TAB_W:(kind + 1) * TAB_W]

    def put(off, c, v):
        p_ref[:, off + c * LANES:off + (c + 1) * LANES] = v.astype(BF16)

    def rope(off, c, kind):
        return _rope128(zc(off, c), tab(kind), ROPE_KINDS[kind][0] // 2)

    for zoff, poff, kind in ((Z_AQ, P_AQ, 0), (Z_AK, P_AK, 0), (Z_BQ, P_BQ, 1), (Z_DK, P_DK, 1)):
        for c in range(4):
            put(poff, c, rope(zoff, c, kind))
    for c in range(4):
        put(P_AV, c, zc(Z_AV, c))
    put(P_VS, 0, zc(Z_VS)); put(P_VW, 0, zc(Z_VW))
    put(P_KS, 0, rope(Z_KS, 0, 1)); put(P_KW, 0, rope(Z_KW, 0, 1))
    put(P_KR, 0, rope(Z_KR, 0, 2))
    kc_ref[...] = rope(Z_KC, 0, 1).astype(BF16)
    vc_ref[...] = zc(Z_VC).astype(BF16)

    for zoff, toff, kind in ((Z_DQ, T_DQ, 1), (Z_IQ, T_IQ, 0)):
        for c in range(4):
            t_ref[0, toff + c * LANES:toff + (c + 1) * LANES, :] = rope(zoff, c, kind).T.astype(BF16)
    for c in range(4):
        vt_ref[0, 0, c * LANES:(c + 1) * LANES, :] = zc(Z_DV, c).T.astype(BF16)

    cq = z_cols(Z_CQ, 512)
    ms = jnp.sum(cq * cq, axis=-1, keepdims=True) * (1.0 / MLA_Q_LORA)
    q = _dot((cq * lax.rsqrt(ms + EPS) * gq_ref[...]).astype(BF16), wq_ref[...])
    nn = HEADS * MLA_NOPE
    q_ref[:, :nn] = q[:, :nn].astype(BF16)
    for c in range(nn // LANES, (nn + HEADS * MLA_ROPE) // LANES):
        tile = _rope128(q[:, c * LANES:(c + 1) * LANES], tab(2), MLA_ROPE // 2)
        q_ref[:, c * LANES:(c + 1) * LANES] = tile.astype(BF16)
    ckv = z_cols(Z_CKV, MLA_KV_LORA)
    ms = jnp.mean(ckv * ckv, axis=-1, keepdims=True)
    kv_ref[...] = _dot((ckv * lax.rsqrt(ms + EPS) * gkv_ref[...]).astype(BF16), wkv_ref[...]).astype(BF16)

    ik = zc(Z_IK)
    ms = jnp.mean(ik * ik, axis=-1, keepdims=True)
    ikn = ik * lax.rsqrt(ms + EPS) * gik_ref[...]
    put(P_IK, 0, _rope128(ikn, tab(0), ROPE_KINDS[0][0] // 2))

    sm = zc(Z_SMALL)
    lane = lax.broadcasted_iota(I32, sm.shape, 1)
    iw_scale = IDX_HEADS ** -0.5 * IDX_DIM ** -0.5
    small = jnp.where(lane < SMALL_IW, jax.nn.sigmoid(sm), sm * iw_scale)
    small_ref[...] = small
    smallt_ref[0] = small.T


def _proj_prep(x2, g, w, tab, gq, gkv, gik, wq, wkv, *, batch, seq, ts):
    m, d = x2.shape
    spb = seq // ts
    assert ts == VT_TILE and w.shape == (d, Z_WIDTH)
    row = lambda w: pl.BlockSpec((ts, w), lambda i: (i, 0))
    const = lambda a: pl.BlockSpec(a.shape, lambda i: (0,) * a.ndim)
    return pl.pallas_call(
        _proj_prep_kernel,
        out_shape=(jax.ShapeDtypeStruct((m, P_WIDTH), BF16),
                   jax.ShapeDtypeStruct((batch, T_ROWS, seq), BF16),
                   jax.ShapeDtypeStruct((batch, spb, BR_WIDTH, VT_TILE), BF16),
                   jax.ShapeDtypeStruct((m, LANES), BF16),
                   jax.ShapeDtypeStruct((m, LANES), BF16),
                   jax.ShapeDtypeStruct((m, LANES), F32),
                   jax.ShapeDtypeStruct((batch, LANES, seq), F32),
                   jax.ShapeDtypeStruct((m, wq.shape[1]), BF16),
                   jax.ShapeDtypeStruct((m, wkv.shape[1]), BF16)),
        grid=(m // ts,),
        in_specs=[row(d),
                  pl.BlockSpec((1, d), lambda i: (0, 0)),
                  pl.BlockSpec((d, Z_WIDTH), lambda i: (0, 0)),
                  pl.BlockSpec((ts, 3 * TAB_W), lambda i: (i % spb, 0)),
                  pl.BlockSpec((1, 512), lambda i: (0, 0)),
                  pl.BlockSpec((1, MLA_KV_LORA), lambda i: (0, 0)),
                  pl.BlockSpec((1, LANES), lambda i: (0, 0)),
                  const(wq), const(wkv)],
        out_specs=(row(P_WIDTH),
                   pl.BlockSpec((1, T_ROWS, ts), lambda i: (i // spb, 0, i % spb)),
                   pl.BlockSpec((1, 1, BR_WIDTH, VT_TILE), lambda i: (i // spb, i % spb, 0, 0)),
                   row(LANES), row(LANES), row(LANES),
                   pl.BlockSpec((1, LANES, ts), lambda i: (i // spb, 0, i % spb)),
                   row(wq.shape[1]), row(wkv.shape[1])),
        compiler_params=_cparams(1),
        name="proj_prep",
    )(x2, g, w, tab, gq, gkv, gik, wq, wkv)


def _softmax_init(mx_scr, l_scr, acc_scr):
    mx_scr[...] = jnp.full(mx_scr.shape, NEG, F32)
    l_scr[...] = jnp.zeros(l_scr.shape, F32)
    acc_scr[...] = jnp.zeros(acc_scr.shape, F32)


def _score_store(g, j, s, s_scr, mx_scr):
    s_scr[g, j] = s
    m = s[:, 0:LANES]
    for c in range(1, s.shape[1] // LANES):
        m = jnp.maximum(m, s[:, c * LANES:(c + 1) * LANES])
    mx_scr[g] = jnp.maximum(mx_scr[g], m)


def _row_max_finish(mx_scr):
    for g in range(mx_scr.shape[0]):
        m = jnp.max(mx_scr[g], axis=-1, keepdims=True)
        mx_scr[g] = jnp.broadcast_to(m, mx_scr.shape[1:])


def _prob_accumulate(g, j, v_tile, s_scr, mx_scr, l_scr, acc_scr):
    mb = mx_scr[g]
    s = s_scr[g, j]
    ps = [jnp.exp2(s[:, c * LANES:(c + 1) * LANES] - mb) for c in range(s.shape[1] // LANES)]
    tot = ps[0]
    for p in ps[1:]:
        tot = tot + p
    l_scr[g] += tot
    acc_scr[g] += _dot(jnp.concatenate(ps, axis=1).astype(BF16), v_tile)


def _softmax_out(g, l_scr, acc_scr):
    return acc_scr[g] / jnp.sum(l_scr[g], axis=-1, keepdims=True)


def _paired_tiles(n, step):
    def pair(jj, carry):
        step(2 * jj)
        step(2 * jj + 1)
        return carry

    lax.fori_loop(0, n // 2, pair, 0)

    @pl.when(n % 2 == 1)
    def _():
        step(n - 1)


def _causal_tiles(step, n_full):
    _paired_tiles(n_full, lambda j: step(j, False))
    step(n_full, True)


def _softmax_scratch(groups, n_tiles, rows, tk):
    return [pltpu.VMEM((groups, n_tiles, rows, tk), F32),
            pltpu.VMEM((groups, rows, LANES), F32),
            pltpu.VMEM((groups, rows, LANES), F32),
            pltpu.VMEM((groups, rows, HEAD_W), F32)]


def _diff_attn_kernel(q_ref, k_ref, v_ref, lv_ref, g_ref, o_ref, s_scr, mx_scr, l_scr, acc_scr,
                      *, tq, tk, lam_init):
    qs = pl.program_id(1) * tq
    n_full = qs // tk
    scale = DA_DIM ** -0.5 * LOG2E
    lv = lv_ref[...]
    lam = (jnp.exp(jnp.sum(lv[0:1] * lv[1:2], axis=-1, keepdims=True))
           - jnp.exp(jnp.sum(lv[2:3] * lv[3:4], axis=-1, keepdims=True)) + lam_init)
    lane = lax.broadcasted_iota(I32, (tq, HEAD_W), 1)
    row_t = qs + lax.broadcasted_iota(I32, (2 * tq, 1), 0) % tq
    col0 = lax.broadcasted_iota(I32, (2 * tq, tk), 1)
    _softmax_init(mx_scr, l_scr, acc_scr)

    def scores(j, masked):
        ks = pl.multiple_of(j * tk, tk)
        for h in range(HEADS):
            hs = slice(h * HEAD_W, (h + 1) * HEAD_W)
            qh = q_ref[0, :, hs]
            zero = jnp.zeros_like(qh)
            q2 = jnp.concatenate([jnp.where(lane < DA_DIM, qh, zero),
                                  jnp.where(lane >= DA_DIM, qh, zero)], axis=0)
            s = _dot_nt(q2, k_ref[0, pl.ds(ks, tk), hs]) * scale
            if masked:
                s = jnp.where(col0 + ks <= row_t, s, NEG)
            _score_store(h, j, s, s_scr, mx_scr)

    _causal_tiles(scores, n_full)
    _row_max_finish(mx_scr)

    def probs(j, carry):
        ks = pl.multiple_of(j * tk, tk)
        for h in range(HEADS):
            v_tile = v_ref[0, pl.ds(ks, tk), h * HEAD_W:(h + 1) * HEAD_W]
            _prob_accumulate(h, j, v_tile, s_scr, mx_scr, l_scr, acc_scr)
        return carry

    _paired_tiles(n_full + 1, lambda j: probs(j, 0))
    for h in range(HEADS):
        o2 = _softmax_out(h, l_scr, acc_scr)
        o = o2[:tq] - lam * o2[tq:]
        ms = jnp.mean(o * o, axis=-1, keepdims=True)
        o = o * lax.rsqrt(ms + EPS) * g_ref[...]
        o_ref[0, :, h * HEAD_W:(h + 1) * HEAD_W] = (o * (1.0 - lam_init)).astype(BF16)


def _diff_attn(p3, lv, g, *, lam_init, tq, tk):
    b, s, _ = p3.shape
    return pl.pallas_call(
        functools.partial(_diff_attn_kernel, tq=tq, tk=tk, lam_init=lam_init),
        out_shape=jax.ShapeDtypeStruct((b, s, BR_WIDTH), BF16),
        grid=(b, s // tq),
        in_specs=[pl.BlockSpec((1, tq, 512), lambda bi, i: (bi, i, P_AQ // 512)),
                  pl.BlockSpec((1, s, 512), lambda bi, i: (bi, 0, P_AK // 512)),
                  pl.BlockSpec((1, s, 512), lambda bi, i: (bi, 0, P_AV // 512)),
                  pl.BlockSpec((4, DA_DIM), lambda bi, i: (0, 0)),
                  pl.BlockSpec((1, HEAD_W), lambda bi, i: (0, 0))],
        out_specs=pl.BlockSpec((1, tq, BR_WIDTH), lambda bi, i: (bi, i, 0)),
        scratch_shapes=_softmax_scratch(HEADS, s // tk, 2 * tq, tk),
        compiler_params=_cparams(2),
        name="diff_attn",
    )(p3, p3, p3, lv, g)


def _nsa_compress_kernel(gk_ref, gv_ref, w1k_ref, w1v_ref, pek_ref, pev_ref,
                         w1kf_ref, w1vf_ref, w2k_ref, w2v_ref, kc_ref, vc_ref):
    def one(g_ref, w1cat_ref, pe_ref, w1f_ref, w2_ref, o_ref):
        y = _dot(g_ref[0], w1cat_ref[...])
        n = y.shape[0]
        nxt = pltpu.roll(y[:, HEAD_W:], n - 1, 0)
        c = _dot(pe_ref[...], w1f_ref[...])[0:1]
        hid = jax.nn.gelu(y[:, :HEAD_W] + nxt + c)
        o_ref[0] = _dot(hid.astype(BF16), w2_ref[...]).astype(BF16)

    one(gk_ref, w1k_ref, pek_ref, w1kf_ref, w2k_ref, kc_ref)
    one(gv_ref, w1v_ref, pev_ref, w1vf_ref, w2v_ref, vc_ref)


def _nsa_compress(gk, gv, w1k_cat, w1v_cat, pek, pev, w1k, w1v, w2k, w2v):
    b, ng, gw = gk.shape
    full = lambda shape: pl.BlockSpec(shape, lambda bi: (0,) * len(shape))
    return pl.pallas_call(
        _nsa_compress_kernel,
        out_shape=(jax.ShapeDtypeStruct((b, ng, HEAD_W), BF16),
                   jax.ShapeDtypeStruct((b, ng, HEAD_W), BF16)),
        grid=(b,),
        in_specs=[pl.BlockSpec((1, ng, gw), lambda bi: (bi, 0, 0)),
                  pl.BlockSpec((1, ng, gw), lambda bi: (bi, 0, 0)),
                  full(w1k_cat.shape), full(w1v_cat.shape), full(pek.shape), full(pev.shape),
                  full(w1k.shape), full(w1v.shape), full(w2k.shape), full(w2v.shape)],
        out_specs=(pl.BlockSpec((1, ng, HEAD_W), lambda bi: (bi, 0, 0)),
                   pl.BlockSpec((1, ng, HEAD_W), lambda bi: (bi, 0, 0))),
        compiler_params=_cparams(1),
        name="nsa_compress",
    )(gk, gv, w1k_cat, w1v_cat, pek, pev, w1k, w1v, w2k, w2v)


NSA_GROUPS = 2


def _nsa_kernel(q_ref, kc_ref, vc_ref, ks_ref, vs_ref, kw_ref, vw_ref, small_ref, ov_ref, e_ref,
                o_ref, s_scr, mx_scr, l_scr, acc_scr, cmp_scr, win_scr, *, tq, tk, seq):
    qs = pl.program_id(1) * tq
    scale = NSA_DK ** -0.5
    ns = seq // SEL_LEN
    n_sel = min(SEL_N, ns)
    r = HEADS * tq
    rg = r // NSA_GROUPS
    q4 = jnp.concatenate([q_ref[0, :, h * HEAD_W:(h + 1) * HEAD_W] for h in range(HEADS)], axis=0)
    t1 = qs + lax.broadcasted_iota(I32, (tq, 1), 0)
    t4 = qs + lax.broadcasted_iota(I32, (r, 1), 0) % tq

    wspan = WIN + tq
    start = pl.multiple_of(jnp.maximum(qs - WIN, 0), tq)
    sw = _dot_nt(q4, kw_ref[0, pl.ds(start, wspan), :]) * scale
    dist = t4 - (start + lax.broadcasted_iota(I32, (r, wspan), 1))
    sw = jnp.where(pltpu.bitcast(dist, jnp.uint32) < jnp.uint32(WIN), sw, NEG)
    e = jnp.exp(sw - jnp.max(sw, axis=-1, keepdims=True))
    pw = e / jnp.sum(e, axis=-1, keepdims=True)
    win_scr[...] = _dot(pw.astype(BF16), vw_ref[0, pl.ds(start, wspan), :])

    kc = kc_ref[0]
    nc_pad = kc.shape[0]
    sc = _dot_nt(q4, kc) * scale
    c_end = lax.broadcasted_iota(I32, (r, nc_pad), 1) * CMP_STRIDE + (CMP_LEN - 1)
    cmask = c_end <= t4
    mx = jnp.max(jnp.where(cmask, sc, NEG), axis=-1, keepdims=True)
    e = jnp.where(cmask, jnp.exp(sc - mx), 0.0)
    den = jnp.sum(e, axis=-1, keepdims=True)
    pc = e / jnp.where(den > 0.0, den, 1.0)
    cmp_scr[...] = _dot(pc.astype(BF16), vc_ref[0])

    psum = pc[0:tq] + pc[tq:2 * tq] + pc[2 * tq:3 * tq] + pc[3 * tq:4 * tq]
    ov = ov_ref[...]
    hi = psum.astype(BF16)
    r1 = psum - hi.astype(F32)
    mid = r1.astype(BF16)
    lo = (r1 - mid.astype(F32)).astype(BF16)
    imp = _dot(hi, ov) + _dot(mid, ov) + _dot(lo, ov)

    blk = lax.broadcasted_iota(I32, (tq, LANES), 1)
    cur = t1 // SEL_LEN
    forced = (blk == 0) | (blk == cur) | (blk == cur - 1)
    visible = blk * SEL_LEN <= t1
    score = jnp.where(visible, jnp.where(forced, FORCE_SCORE, imp), NEG)
    score = jnp.where(blk < ns, score, PAD_SCORE)
    ns_pad = -(-ns // SUBLANES) * SUBLANES
    score_t = score.T[:ns_pad]
    blk_t = lax.broadcasted_iota(I32, (ns_pad, tq), 0)
    rank = jnp.zeros((ns_pad, tq), I32)
    for jp in range(ns):
        row = score_t[jp:jp + 1, :]
        later = (blk_t > jp).astype(I32)
        rank = rank + jnp.where(row > score_t, 1, jnp.where(row == score_t, later, 0))
    sel_t = jnp.where(rank < n_sel, 1.0, 0.0)
    if ns_pad < LANES:
        sel_t = jnp.concatenate([sel_t, jnp.zeros((LANES - ns_pad, tq), F32)], axis=0)
    selb = sel_t.T.astype(BF16)

    _softmax_init(mx_scr, l_scr, acc_scr)
    col0 = lax.broadcasted_iota(I32, (rg, tk), 1)
    tg = qs + lax.broadcasted_iota(I32, (rg, 1), 0) % tq
    n_tiles = qs // tk + 1

    def scores(j, masked):
        ks0 = pl.multiple_of(j * tk, tk)
        mt = _dot(selb, e_ref[j])
        mg = jnp.concatenate([mt] * (rg // tq), axis=0)
        k_tile = ks_ref[0, pl.ds(ks0, tk), :]
        for g in range(NSA_GROUPS):
            s = _dot_nt(q4[g * rg:(g + 1) * rg], k_tile) * (scale * LOG2E)
            s = jnp.where(mg > 0.5, s, NEG)
            if masked:
                s = jnp.where(col0 + ks0 <= tg, s, NEG)
            _score_store(g, j, s, s_scr, mx_scr)

    _causal_tiles(scores, n_tiles - 1)
    _row_max_finish(mx_scr)

    def probs(j, carry):
        ks0 = pl.multiple_of(j * tk, tk)
        v_tile = vs_ref[0, pl.ds(ks0, tk), :]
        for g in range(NSA_GROUPS):
            _prob_accumulate(g, j, v_tile, s_scr, mx_scr, l_scr, acc_scr)
        return carry

    _paired_tiles(n_tiles, lambda j: probs(j, 0))
    o_slc = jnp.concatenate([_softmax_out(g, l_scr, acc_scr) for g in range(NSA_GROUPS)], axis=0)

    gates = small_ref[0]
    for h in range(HEADS):
        rows = slice(h * tq, (h + 1) * tq)
        g0 = gates[:, SMALL_G + 3 * h:SMALL_G + 3 * h + 1]
        g1 = gates[:, SMALL_G + 3 * h + 1:SMALL_G + 3 * h + 2]
        g2 = gates[:, SMALL_G + 3 * h + 2:SMALL_G + 3 * h + 3]
        o = g0 * cmp_scr[rows, :] + g1 * o_slc[rows] + g2 * win_scr[rows, :]
        o_ref[0, :, h * HEAD_W:(h + 1) * HEAD_W] = o.astype(BF16)


def _nsa(p3, kc, vc, small3, ov, emat, *, tq, tk):
    b, s, _ = p3.shape
    ng = kc.shape[1]
    col = lambda off: (lambda bi, i: (bi, 0, off // LANES))
    return pl.pallas_call(
        functools.partial(_nsa_kernel, tq=tq, tk=tk, seq=s),
        out_shape=jax.ShapeDtypeStruct((b, s, BR_WIDTH), BF16),
        grid=(b, s // tq),
        in_specs=[pl.BlockSpec((1, tq, 512), lambda bi, i: (bi, i, P_BQ // 512)),
                  pl.BlockSpec((1, ng, HEAD_W), lambda bi, i: (bi, 0, 0)),
                  pl.BlockSpec((1, ng, HEAD_W), lambda bi, i: (bi, 0, 0)),
                  pl.BlockSpec((1, s, LANES), col(P_KS)),
                  pl.BlockSpec((1, s, LANES), col(P_VS)),
                  pl.BlockSpec((1, s, LANES), col(P_KW)),
                  pl.BlockSpec((1, s, LANES), col(P_VW)),
                  pl.BlockSpec((1, tq, LANES), lambda bi, i: (bi, i, 0)),
                  pl.BlockSpec(ov.shape, lambda bi, i: (0, 0)),
                  pl.BlockSpec(emat.shape, lambda bi, i: (0, 0, 0))],
        out_specs=pl.BlockSpec((1, tq, BR_WIDTH), lambda bi, i: (bi, i, 0)),
        scratch_shapes=(_softmax_scratch(NSA_GROUPS, s // tk, HEADS * tq // NSA_GROUPS, tk)
                        + [pltpu.VMEM((HEADS * tq, HEAD_W), F32), pltpu.VMEM((HEADS * tq, HEAD_W), F32)]),
        compiler_params=_cparams(2),
        name="nsa_attn",
    )(p3, kc, vc, p3, p3, p3, p3, small3, ov, emat)


def _mla_attn_kernel(qn_ref, qr_ref, kn_ref, kr_ref, v_ref, o_ref, s_scr, mx_scr, l_scr, acc_scr,
                     *, tq, tk):
    qs = pl.program_id(1) * tq
    n_full = qs // tk
    scale = (MLA_NOPE + MLA_ROPE) ** -0.5 * LOG2E
    lane = lax.broadcasted_iota(I32, (tq, LANES), 1)
    row_t = qs + lax.broadcasted_iota(I32, (tq, 1), 0)
    col0 = lax.broadcasted_iota(I32, (tq, tk), 1)
    _softmax_init(mx_scr, l_scr, acc_scr)

    def scores(j, masked):
        ks = pl.multiple_of(j * tk, tk)
        kr_tile = kr_ref[0, pl.ds(ks, tk), :]
        for h in range(HEADS):
            hs = slice(h * HEAD_W, (h + 1) * HEAD_W)
            pair = qr_ref[0, :, (h // 2) * LANES:(h // 2 + 1) * LANES]
            keep = (lane < MLA_ROPE) if h % 2 == 0 else (lane >= MLA_ROPE)
            qr = jnp.where(keep, pair, jnp.zeros_like(pair))
            s = _dot_nt(jnp.concatenate([qn_ref[0, :, hs], qr], axis=1),
                        jnp.concatenate([kn_ref[0, pl.ds(ks, tk), hs], kr_tile], axis=1)) * scale
            if masked:
                s = jnp.where(col0 + ks <= row_t, s, NEG)
            _score_store(h, j, s, s_scr, mx_scr)

    _causal_tiles(scores, n_full)
    _row_max_finish(mx_scr)

    def probs(j, carry):
        ks = pl.multiple_of(j * tk, tk)
        for h in range(HEADS):
            v_tile = v_ref[0, pl.ds(ks, tk), h * HEAD_W:(h + 1) * HEAD_W]
            _prob_accumulate(h, j, v_tile, s_scr, mx_scr, l_scr, acc_scr)
        return carry

    _paired_tiles(n_full + 1, lambda j: probs(j, 0))
    for h in range(HEADS):
        o_ref[0, :, h * HEAD_W:(h + 1) * HEAD_W] = _softmax_out(h, l_scr, acc_scr).astype(BF16)


def _mla_attn(q3, kv3, p3, *, tq, tk):
    b, s, _ = q3.shape
    return pl.pallas_call(
        functools.partial(_mla_attn_kernel, tq=tq, tk=tk),
        out_shape=jax.ShapeDtypeStruct((b, s, BR_WIDTH), BF16),
        grid=(b, s // tq),
        in_specs=[pl.BlockSpec((1, tq, 512), lambda bi, i: (bi, i, 0)),
                  pl.BlockSpec((1, tq, 256), lambda bi, i: (bi, i, 2)),
                  pl.BlockSpec((1, s, 512), lambda bi, i: (bi, 0, 0)),
                  pl.BlockSpec((1, s, LANES), lambda bi, i: (bi, 0, P_KR // LANES)),
                  pl.BlockSpec((1, s, 512), lambda bi, i: (bi, 0, 1))],
        out_specs=pl.BlockSpec((1, tq, BR_WIDTH), lambda bi, i: (bi, i, 0)),
        scratch_shapes=_softmax_scratch(HEADS, s // tk, tq, tk),
        compiler_params=_cparams(2),
        name="mla_attn",
    )(q3, q3, kv3, p3, kv3)


def _sortable_key(x):
    bits = pltpu.bitcast(x + 0.0, I32)
    return bits ^ (lax.shift_right_arithmetic(bits, 31) & 0x7FFFFFFF)


def _fold_rows(x, op):
    n = x.shape[0] // SUBLANES
    return op(x.reshape(n, SUBLANES, x.shape[1]), axis=0)


def _count16(half_scr, n_tiles, pred, tq):
    def count_tile(j, cnt):
        hit = pred(half_scr[j]).astype(I16)
        parts = [hit[r:r + PACKED_ROWS] for r in range(0, hit.shape[0], PACKED_ROWS)]
        while len(parts) > 1:
            parts = [a + b for a, b in zip(parts[0::2], parts[1::2])]
        return cnt + parts[0]

    cnt = lax.fori_loop(0, n_tiles, count_tile, jnp.zeros((PACKED_ROWS, tq), I16))
    return jnp.sum(cnt.astype(I32), axis=0, keepdims=True)


def _bisect16(half_scr, n_tiles, need, tq):
    def bit_body(i, carry):
        th, tot = carry
        cand = th + lax.shift_left(jnp.int32(1), 15 - i)
        c16 = cand.astype(I16)
        total = _count16(half_scr, n_tiles, lambda x: x >= c16, tq)
        ok = total >= need
        return jnp.where(ok, cand, th), jnp.where(ok, total, tot)

    return lax.fori_loop(0, 16, bit_body, (jnp.full((1, tq), HALF_MIN, I32), jnp.full((1, tq), -1, I32)))


def _dsa_kernel(qt_ref, iqt_ref, iwt_ref, k_ref, ik_ref, vt_ref, tri_ref, o_ref,
                key_scr, half_scr, s_scr, mx_scr, l_scr, acc_scr, *, tq, tk, top):
    qs = pl.program_id(1) * tq
    n_tiles = (qs + tq - 1) // tk + 1
    scale = DSA_DIM ** -0.5 * LOG2E
    t_lane = qs + lax.broadcasted_iota(I32, (tk, tq), 1)
    krow0 = lax.broadcasted_iota(I32, (tk, tq), 0)
    half_rows = lax.broadcasted_iota(I32, (LANES, tq), 0) < IDX_DIM
    vt_per_tile = tk // VT_TILE

    def score_tile(j, masked):
        ks = pl.multiple_of(j * tk, tk)
        ikt = ik_ref[0, pl.ds(ks, tk), :]
        acc = jnp.zeros((tk, tq), F32)
        for h in range(IDX_HEADS):
            pair = iqt_ref[0, (h // 2) * LANES:(h // 2 + 1) * LANES, :]
            keep = half_rows if h % 2 == 0 else jnp.logical_not(half_rows)
            iq_h = jnp.where(keep, pair, jnp.zeros_like(pair))
            w_h = iwt_ref[0, SMALL_IW + h:SMALL_IW + h + 1, :]
            acc = acc + w_h * jnp.maximum(_dot(ikt, iq_h), 0.0)
        key = _sortable_key(acc)
        if masked:
            key = jnp.where(krow0 + ks <= t_lane, key, INT_MIN)
        key_scr[j] = key
        half_scr[j] = lax.shift_right_arithmetic(key, 16).astype(I16)

    _causal_tiles(score_tile, n_tiles - 1)

    th_hi, n_ge_hi = _bisect16(half_scr, n_tiles, top, tq)
    hi16 = th_hi.astype(I16)
    n_gt_hi = _count16(half_scr, n_tiles, lambda x: x > hi16, tq)
    need_lo = top - n_gt_hi

    def low_tile(j, carry):
        key = key_scr[j]
        lo = (key & 0xFFFF) + HALF_MIN
        same_hi = lax.shift_right_arithmetic(key, 16) == th_hi
        half_scr[j] = jnp.where(same_hi, lo, HALF_MIN).astype(I16)
        return carry

    lax.fori_loop(0, n_tiles, low_tile, 0)
    th_lo, n_ge_lo = _bisect16(half_scr, n_tiles, need_lo, tq)
    theta = lax.shift_left(th_hi, 16) + (th_lo - HALF_MIN)
    n_ge_lo = jnp.where(n_ge_lo >= 0, n_ge_lo, n_ge_hi - n_gt_hi)
    tied = (n_ge_lo > need_lo) & (n_ge_hi >= 0)

    @pl.when(jnp.max(tied.astype(I32)) > 0)
    def _():
        lo16 = th_lo.astype(I16)
        n_gt = n_gt_hi + _count16(half_scr, n_tiles, lambda x: x > lo16, tq)
        keep_eq = jnp.where(tied, top - n_gt, tk * key_scr.shape[0]).astype(F32)

        def demote(j, seen):
            k = key_scr[j]
            eq = k == theta
            eqf = jnp.where(eq, 1.0, 0.0)
            before = _dot(tri_ref[...], eqf.astype(BF16)) + seen
            key_scr[j] = jnp.where(jnp.where(eq, before, -1.0) >= keep_eq, k - 1, k)
            return seen + jnp.sum(_fold_rows(eqf, jnp.sum), axis=0, keepdims=True)

        lax.fori_loop(0, n_tiles, demote, jnp.zeros((1, tq), F32))

    theta = jnp.maximum(theta, INT_MIN + 1)

    mx_scr[...] = jnp.full(mx_scr.shape, NEG, F32)
    l_scr[...] = jnp.zeros(l_scr.shape, F32)
    acc_scr[...] = jnp.zeros(acc_scr.shape, F32)

    def scores(j, carry):
        ks = pl.multiple_of(j * tk, tk)
        sel = key_scr[j] >= theta
        for h in range(HEADS):
            hs = slice(h * HEAD_W, (h + 1) * HEAD_W)
            s = _dot(k_ref[0, pl.ds(ks, tk), hs], qt_ref[0, hs, :]) * scale
            s = jnp.where(sel, s, NEG)
            s_scr[h, j] = s
            mx_scr[h] = jnp.maximum(mx_scr[h], _fold_rows(s, jnp.max))
        return carry

    _paired_tiles(n_tiles, lambda j: scores(j, 0))
    for h in range(HEADS):
        m = jnp.max(mx_scr[h], axis=0, keepdims=True)
        mx_scr[h] = jnp.broadcast_to(m, (SUBLANES, tq))

    def probs(j, carry):
        for h in range(HEADS):
            p = jnp.exp2(s_scr[h, j] - mx_scr[h][0:1])
            l_scr[h] += _fold_rows(p, jnp.sum)
            pb = p.astype(BF16)
            for c in range(vt_per_tile):
                vt = vt_ref[0, j * vt_per_tile + c, h * HEAD_W:(h + 1) * HEAD_W, :]
                acc_scr[h] += _dot(vt, pb[c * VT_TILE:(c + 1) * VT_TILE])
        return carry

    _paired_tiles(n_tiles, lambda j: probs(j, 0))
    for h in range(HEADS):
        ot = acc_scr[h] / jnp.sum(l_scr[h], axis=0, keepdims=True)
        o_ref[0, :, h * HEAD_W:(h + 1) * HEAD_W] = ot.T.astype(BF16)


def _dsa(p3, t3, vt4, smallt, *, tq, tk, top):
    b, s, _ = p3.shape
    n_vt = vt4.shape[1]
    tri = jnp.asarray(np.tril(np.ones((tk, tk), np.float32), -1), BF16)
    return pl.pallas_call(
        functools.partial(_dsa_kernel, tq=tq, tk=tk, top=top),
        out_shape=jax.ShapeDtypeStruct((b, s, BR_WIDTH), BF16),
        grid=(b, s // tq),
        in_specs=[pl.BlockSpec((1, 512, tq), lambda bi, i: (bi, T_DQ // 512, i)),
                  pl.BlockSpec((1, 512, tq), lambda bi, i: (bi, T_IQ // 512, i)),
                  pl.BlockSpec((1, LANES, tq), lambda bi, i: (bi, 0, i)),
                  pl.BlockSpec((1, s, 512), lambda bi, i: (bi, 0, P_DK // 512)),
                  pl.BlockSpec((1, s, LANES), lambda bi, i: (bi, 0, P_IK // LANES)),
                  pl.BlockSpec((1, n_vt, BR_WIDTH, VT_TILE), lambda bi, i: (bi, 0, 0, 0)),
                  pl.BlockSpec((tk, tk), lambda bi, i: (0, 0))],
        out_specs=pl.BlockSpec((1, tq, BR_WIDTH), lambda bi, i: (bi, i, 0)),
        scratch_shapes=[pltpu.VMEM((s // tk, tk, tq), I32),
                        pltpu.VMEM((s // tk, tk, tq), I16),
                        pltpu.VMEM((HEADS, s // tk, tk, tq), F32),
                        pltpu.VMEM((HEADS, SUBLANES, tq), F32),
                        pltpu.VMEM((HEADS, SUBLANES, tq), F32),
                        pltpu.VMEM((HEADS, HEAD_W, tq), F32)],
        compiler_params=_cparams(2),
        name="dsa_attn",
    )(t3, t3, smallt, p3, p3, vt4, tri)


def _merge_kernel(x_ref, g_ref, oa_ref, ob_ref, oc_ref, od_ref, wg_ref, wb_ref, wo_ref, o_ref):
    x = x_ref[...]
    d = x.shape[1]
    ms = jnp.mean(x * x, axis=-1, keepdims=True)
    h = (x * lax.rsqrt(ms + EPS) * g_ref[...]).astype(BF16)
    acc = jnp.zeros(x.shape, F32)
    for n, br_ref in enumerate((oa_ref, ob_ref, oc_ref, od_ref)):
        gate = jax.nn.sigmoid(_dot(h, wg_ref[:, n * d:(n + 1) * d]))
        acc = acc + gate * _dot(br_ref[...], wb_ref[n])
    o_ref[...] = x + _dot(acc.astype(BF16), wo_ref[...])


def _merge(x2, g, oa, ob, oc, od, wg, wb, wo, *, tm):
    m, d = x2.shape
    row = lambda w: pl.BlockSpec((tm, w), lambda i: (i, 0))
    return pl.pallas_call(
        _merge_kernel,
        out_shape=jax.ShapeDtypeStruct((m, d), F32),
        grid=(m // tm,),
        in_specs=[row(d), pl.BlockSpec((1, d), lambda i: (0, 0)),
                  row(BR_WIDTH), row(BR_WIDTH), row(BR_WIDTH), row(BR_WIDTH),
                  pl.BlockSpec(wg.shape, lambda i: (0, 0)),
                  pl.BlockSpec(wb.shape, lambda i: (0, 0, 0)),
                  pl.BlockSpec(wo.shape, lambda i: (0, 0))],
        out_specs=row(d),
        compiler_params=_cparams(1),
        name="merge",
    )(x2, g, oa, ob, oc, od, wg, wb, wo)


def _ffn_kernel(x_ref, g_ref, wg_ref, wu_ref, wd_ref, gf_ref, o_ref, h_scr, acc_scr, *, final):
    j = pl.program_id(1)

    @pl.when(j == 0)
    def _():
        x = x_ref[...]
        ms = jnp.mean(x * x, axis=-1, keepdims=True)
        h_scr[...] = (x * lax.rsqrt(ms + EPS) * g_ref[...]).astype(BF16)
        acc_scr[...] = jnp.zeros(acc_scr.shape, F32)

    h = h_scr[...]
    a = jax.nn.silu(_dot(h, wg_ref[...])) * _dot(h, wu_ref[...])
    acc_scr[...] += _dot(a.astype(BF16), wd_ref[...])

    @pl.when(j == pl.num_programs(1) - 1)
    def _():
        y = x_ref[...] + acc_scr[...]
        if final:
            ms = jnp.mean(y * y, axis=-1, keepdims=True)
            y = y * lax.rsqrt(ms + EPS) * gf_ref[...]
        o_ref[...] = y


def _ffn(x2, g, wg, wu, wd, gf, *, final, tm, tf):
    m, d = x2.shape
    dff = wg.shape[1]
    return pl.pallas_call(
        functools.partial(_ffn_kernel, final=final),
        out_shape=jax.ShapeDtypeStruct((m, d), F32),
        grid=(m // tm, dff // tf),
        in_specs=[pl.BlockSpec((tm, d), lambda i, j: (i, 0)),
                  pl.BlockSpec((1, d), lambda i, j: (0, 0)),
                  pl.BlockSpec((d, tf), lambda i, j: (0, j)),
                  pl.BlockSpec((d, tf), lambda i, j: (0, j)),
                  pl.BlockSpec((tf, d), lambda i, j: (j, 0)),
                  pl.BlockSpec((1, d), lambda i, j: (0, 0))],
        out_specs=pl.BlockSpec((tm, d), lambda i, j: (i, 0)),
        scratch_shapes=[pltpu.VMEM((tm, d), BF16), pltpu.VMEM((tm, d), F32)],
        compiler_params=_cparams(2),
        name="ffn",
    )(x2, g, wg, wu, wd, gf)


def _tiles(seq, m, dff):
    pick = lambda n, cands: next(c for c in cands if n % c == 0)
    tk = pick(seq, (512, 256))
    return dict(
        merge_tm=pick(m, (256, 128)),
        prep_ts=VT_TILE,
        diff_tq=128, mla_tq=pick(seq, (256, 128)), nsa_tq=256, dsa_tq=256, tk=tk,
        row_tm=pick(m, (512, 256, 128)),
        ffn_tf=pick(dff, (1408, 704, 256, 128)),
    )


def kernel(x, norm1_g, w_in, diff_lq1, diff_lk1, diff_lq2, diff_lk2, diff_subln_g, nsa_pe_k, nsa_w1_k, nsa_w2_k, nsa_pe_v, nsa_w1_v, nsa_w2_v, mla_q_norm_g, mla_w_uq, mla_kv_norm_g, mla_w_ukv, idx_k_norm_g, w_branch, w_out, norm2_g, w_gate_up, w_down, final_norm_g):
    b, seq, d = x.shape
    depth = w_in.shape[0]
    m = b * seq
    dff = w_down.shape[1]
    t = _tiles(seq, m, dff)
    tk = t["tk"]
    assert seq % SEL_LEN == 0 and seq >= WIN + t["nsa_tq"] and seq // SEL_LEN <= LANES
    assert seq % t["dsa_tq"] == 0 and tk % VT_TILE == 0 and tk >= min(IDX_TOPK, seq // 4)

    col_idx, gate_off, d_in = _in_proj_columns()
    assert w_in.shape[2] == d_in
    tab = jnp.concatenate([_rope_table(seq, rot, per) for rot, per in ROPE_KINDS], axis=1)

    ng = seq // CMP_STRIDE
    ns = seq // SEL_LEN
    c_start = np.arange(ng)[:, None] * CMP_STRIDE
    s_start = np.arange(LANES)[None, :] * SEL_LEN
    ov = ((c_start < s_start + SEL_LEN) & (c_start + CMP_LEN - 1 >= s_start)
          & (np.arange(LANES)[None, :] < ns))
    ov = jnp.asarray(ov, BF16)
    emat = np.arange(LANES)[:, None] == (np.arange(seq)[None, :] // SEL_LEN)
    emat = jnp.asarray(emat.reshape(LANES, seq // tk, tk).transpose(1, 0, 2), BF16)

    qd = MLA_NOPE + MLA_ROPE
    uq_idx = np.concatenate([np.concatenate([np.arange(h * qd, h * qd + MLA_NOPE) for h in range(HEADS)]),
                             np.concatenate([np.arange(h * qd + MLA_NOPE, (h + 1) * qd) for h in range(HEADS)])])
    kvd = MLA_NOPE + HEAD_W
    ukv_idx = np.concatenate([np.concatenate([np.arange(h * kvd, h * kvd + MLA_NOPE) for h in range(HEADS)]),
                              np.concatenate([np.arange(h * kvd + MLA_NOPE, (h + 1) * kvd) for h in range(HEADS)])])

    x2 = x.reshape(m, d)
    half_w1 = CMP_STRIDE * NSA_DK
    for l in range(depth):
        lam_init = 0.8 - 0.6 * math.exp(-0.3 * l)
        w_in_l = w_in[l].astype(BF16)
        w_mix = _take_cols(w_in_l, col_idx)
        w_gate = w_in_l[:, gate_off:]

        gq = jnp.pad(mla_q_norm_g[l], (0, 512 - MLA_Q_LORA))[None]
        gkv = mla_kv_norm_g[l][None]
        gik = jnp.concatenate([idx_k_norm_g[l], idx_k_norm_g[l]])[None]
        wq = jnp.pad(_take_cols(mla_w_uq[l].astype(BF16), uq_idx), ((0, 512 - MLA_Q_LORA), (0, 0)))
        wkv = _take_cols(mla_w_ukv[l].astype(BF16), ukv_idx)
        p2, t3, vt4, kc_tok, vc_tok, small, smallt, q_c, kv_c = _proj_prep(
            x2, norm1_g[l][None], w_mix, tab, gq, gkv, gik, wq, wkv, batch=b, seq=seq, ts=t["prep_ts"])
        p3 = p2.reshape(b, seq, P_WIDTH)
        small3 = small.reshape(b, seq, LANES)

        lv = jnp.stack([diff_lq1[l], diff_lk1[l], diff_lq2[l], diff_lk2[l]])
        o_a = _diff_attn(p3, lv, diff_subln_g[l][None], lam_init=lam_init, tq=t["diff_tq"], tk=tk)

        w1k, w1v = nsa_w1_k[l].astype(BF16), nsa_w1_v[l].astype(BF16)
        w1k_cat = jnp.concatenate([w1k[:half_w1], w1k[half_w1:]], axis=1)
        w1v_cat = jnp.concatenate([w1v[:half_w1], w1v[half_w1:]], axis=1)
        pek = jnp.broadcast_to(nsa_pe_k[l].reshape(1, -1), (8, CMP_LEN * NSA_DK)).astype(BF16)
        pev = jnp.broadcast_to(nsa_pe_v[l].reshape(1, -1), (8, CMP_LEN * NSA_DK)).astype(BF16)
        kc, vc = _nsa_compress(kc_tok.reshape(b, ng, half_w1), vc_tok.reshape(b, ng, half_w1),
                               w1k_cat, w1v_cat, pek, pev, w1k, w1v,
                               nsa_w2_k[l].astype(BF16), nsa_w2_v[l].astype(BF16))
        o_b = _nsa(p3, kc, vc, small3, ov, emat, tq=t["nsa_tq"], tk=tk)

        o_c = _mla_attn(q_c.reshape(b, seq, -1), kv_c.reshape(b, seq, -1), p3, tq=t["mla_tq"], tk=tk)

        o_d = _dsa(p3, t3, vt4, smallt, tq=t["dsa_tq"], tk=tk, top=min(IDX_TOPK, seq // 4))

        x2 = _merge(x2, norm1_g[l][None],
                    o_a.reshape(m, -1), o_b.reshape(m, -1), o_c.reshape(m, -1), o_d.reshape(m, -1),
                    w_gate, w_branch[l].astype(BF16), w_out[l].astype(BF16), tm=t["merge_tm"])
        wgu = w_gate_up[l].astype(BF16)
        x2 = _ffn(x2, norm2_g[l][None], wgu[:, :dff], wgu[:, dff:], w_down[l].astype(BF16),
                  final_norm_g[None], final=(l == depth - 1), tm=t["row_tm"], tf=t["ffn_tf"])
    return x2.reshape(b, seq, d)
```

```python
import functools
import math

import numpy as np
import jax
import jax.numpy as jnp
from jax import lax
from jax.experimental import pallas as pl
from jax.experimental.pallas import tpu as pltpu

F32 = jnp.float32
BF16 = jnp.bfloat16
I32 = jnp.int32
I16 = jnp.int16

LANES = 128
SUBLANES = 8
PACKED_ROWS = 16
HALF_MIN = -32768
VMEM_LIMIT = 56 * 1024 * 1024

ROPE_THETA = 500000.0
NEG = -1e30
LOG2E = math.log2(math.e)
FORCE_SCORE = 1e9
PAD_SCORE = -3e38
EPS = 1e-6
INT_MIN = -2147483648

HEADS = 4
HEAD_W = 128
BR_WIDTH = HEADS * HEAD_W
DA_DIM = 64
NSA_DK = 128
CMP_LEN = 32
CMP_STRIDE = 16
SEL_LEN = 64
SEL_N = 16
WIN = 512
MLA_Q_LORA = 384
MLA_KV_LORA = 256
MLA_NOPE = 128
MLA_ROPE = 64
DSA_DIM = 128
IDX_HEADS = 8
IDX_DIM = 64
IDX_TOPK = 256

Z_AQ, Z_AK, Z_AV, Z_BQ, Z_DQ, Z_DK, Z_DV, Z_IQ = (i * 512 for i in range(8))
Z_CQ = 4096
Z_CKV = 4608
Z_KC, Z_KS, Z_KW, Z_VC, Z_VS, Z_VW, Z_KR, Z_IK, Z_SMALL = (4864 + i * 128 for i in range(9))
Z_WIDTH = 6144
P_AQ, P_AK, P_AV, P_BQ, P_DK = (i * 512 for i in range(5))
P_KS, P_KW, P_VS, P_VW, P_KR, P_IK = (2560 + i * 128 for i in range(6))
P_WIDTH = 3328
T_DQ, T_IQ = 0, 512
T_ROWS = 1024
VT_TILE = 256
SMALL_G = 0
SMALL_IW = 12

ROPE_KINDS = ((16, 64), (32, 128), (64, 64))
TAB_W = 3 * LANES


def _cparams(n_axes):
    return pltpu.CompilerParams(dimension_semantics=("arbitrary",) * n_axes,
                                vmem_limit_bytes=VMEM_LIMIT)


def _dot(a, b):
    return jnp.dot(a, b, preferred_element_type=F32)


def _dot_nt(a, b):
    return lax.dot_general(a, b, (((1,), (1,)), ((), ())), preferred_element_type=F32)


def _in_proj_columns():
    names = (("a_q", 512), ("a_k", 512), ("a_v", 512), ("b_q", 512),
             ("b_kc", 128), ("b_vc", 128), ("b_ks", 128), ("b_vs", 128),
             ("b_kw", 128), ("b_vw", 128), ("b_g", 12),
             ("c_q", 384), ("c_kv", 256), ("c_kr", 64),
             ("d_q", 512), ("d_k", 512), ("d_v", 512),
             ("d_iq", 512), ("d_ik", 64), ("d_iw", 8), ("gate", 4096))
    off, o = {}, 0
    for nm, n in names:
        off[nm] = (o, n)
        o += n
    idx = np.full((Z_WIDTH,), -1, np.int64)

    def put(dst, nm):
        s, n = off[nm]
        idx[dst:dst + n] = np.arange(s, s + n)

    put(Z_AQ, "a_q"); put(Z_AK, "a_k"); put(Z_AV, "a_v"); put(Z_BQ, "b_q")
    put(Z_DQ, "d_q"); put(Z_DK, "d_k"); put(Z_DV, "d_v"); put(Z_IQ, "d_iq")
    put(Z_CQ, "c_q"); put(Z_CKV, "c_kv")
    put(Z_KC, "b_kc"); put(Z_KS, "b_ks"); put(Z_KW, "b_kw")
    put(Z_VC, "b_vc"); put(Z_VS, "b_vs"); put(Z_VW, "b_vw")
    put(Z_KR, "c_kr"); put(Z_KR + 64, "c_kr")
    put(Z_IK, "d_ik"); put(Z_IK + 64, "d_ik")
    put(Z_SMALL + SMALL_G, "b_g"); put(Z_SMALL + SMALL_IW, "d_iw")
    return idx, off["gate"][0], o


def _take_cols(w, idx):
    runs, i, n = [], 0, len(idx)
    while i < n:
        j = i + 1
        if idx[i] < 0:
            while j < n and idx[j] < 0:
                j += 1
            runs.append(jnp.zeros((w.shape[0], j - i), w.dtype))
        else:
            while j < n and idx[j] == idx[j - 1] + 1:
                j += 1
            runs.append(w[:, int(idx[i]):int(idx[i]) + (j - i)])
        i = j
    return jnp.concatenate(runs, axis=1)


def _rope_table(seq, rot, period):
    half = rot // 2
    inv = jnp.power(jnp.float32(ROPE_THETA), -jnp.arange(0, rot, 2, dtype=F32) / rot)
    ang = jnp.arange(seq, dtype=F32)[:, None] * inv[None, :]
    cos, sin = jnp.cos(ang), jnp.sin(ang)
    lane = np.arange(LANES) % period
    in1 = lane < half
    in2 = (lane >= half) & (lane < 2 * half)
    fidx = np.where(in1, lane, np.where(in2, lane - half, 0))
    cosl, sinl = cos[:, fidx], sin[:, fidx]
    c = jnp.where(jnp.asarray(in1 | in2)[None], cosl, 1.0)
    s1 = jnp.where(jnp.asarray(in1)[None], -sinl, 0.0)
    s2 = jnp.where(jnp.asarray(in2)[None], sinl, 0.0)
    return jnp.concatenate([c, s1, s2], axis=1)


def _rope128(x, tab, half):
    return (x * tab[:, 0:LANES]
            + pltpu.roll(x, LANES - half, 1) * tab[:, LANES:2 * LANES]
            + pltpu.roll(x, half, 1) * tab[:, 2 * LANES:3 * LANES])


PROJ_TILE = 512


def _proj_prep_kernel(x_ref, g_ref, w_ref, tab_ref, gq_ref, gkv_ref, gik_ref, wq_ref, wkv_ref,
                      p_ref, t_ref, vt_ref, kc_ref, vc_ref, small_ref, smallt_ref, q_ref, kv_ref):
    x = x_ref[...]
    ms = jnp.mean(x * x, axis=-1, keepdims=True)
    h = (x * lax.rsqrt(ms + EPS) * g_ref[...]).astype(BF16)
    z_tiles = {}

    def z_cols(off, width):
        t = off // PROJ_TILE
        assert (off + width - 1) // PROJ_TILE == t
        if t not in z_tiles:
            z_tiles[t] = _dot(h, w_ref[:, t * PROJ_TILE:(t + 1) * PROJ_TILE])
        lo = off - t * PROJ_TILE
        return z_tiles[t][:, lo:lo + width]

    def zc(off, c=0):
        return z_cols(off + c * LANES, LANES)

    def tab(kind):
        return tab_ref[:, kind * TAB_W:(kind + 1) * TAB_W]

    def put(off, c, v):
        p_ref[:, off + c * LANES:off + (c + 1) * LANES] = v.astype(BF16)

    def rope(off, c, kind):
        return _rope128(zc(off, c), tab(kind), ROPE_KINDS[kind][0] // 2)

    for zoff, poff, kind in ((Z_AQ, P_AQ, 0), (Z_AK, P_AK, 0), (Z_BQ, P_BQ, 1), (Z_DK, P_DK, 1)):
        for c in range(4):
            put(poff, c, rope(zoff, c, kind))
    for c in range(4):
        put(P_AV, c, zc(Z_AV, c))
    put(P_VS, 0, zc(Z_VS)); put(P_VW, 0, zc(Z_VW))
    put(P_KS, 0, rope(Z_KS, 0, 1)); put(P_KW, 0, rope(Z_KW, 0, 1))
    put(P_KR, 0, rope(Z_KR, 0, 2))
    kc_ref[...] = rope(Z_KC, 0, 1).astype(BF16)
    vc_ref[...] = zc(Z_VC).astype(BF16)

    for zoff, toff, kind in ((Z_DQ, T_DQ, 1), (Z_IQ, T_IQ, 0)):
        for c in range(4):
            t_ref[0, toff + c * LANES:toff + (c + 1) * LANES, :] = rope(zoff, c, kind).T.astype(BF16)
    for c in range(4):
        vt_ref[0, 0, c * LANES:(c + 1) * LANES, :] = zc(Z_DV, c).T.astype(BF16)

    cq = z_cols(Z_CQ, 512)
    ms = jnp.sum(cq * cq, axis=-1, keepdims=True) * (1.0 / MLA_Q_LORA)
    q = _dot((cq * lax.rsqrt(ms + EPS) * gq_ref[...]).astype(BF16), wq_ref[...])
    nn = HEADS * MLA_NOPE
    q_ref[:, :nn] = q[:, :nn].astype(BF16)
    for c in range(nn // LANES, (nn + HEADS * MLA_ROPE) // LANES):
        tile = _rope128(q[:, c * LANES:(c + 1) * LANES], tab(2), MLA_ROPE // 2)
        q_ref[:, c * LANES:(c + 1) * LANES] = tile.astype(BF16)
    ckv = z_cols(Z_CKV, MLA_KV_LORA)
    ms = jnp.mean(ckv * ckv, axis=-1, keepdims=True)
    kv_ref[...] = _dot((ckv * lax.rsqrt(ms + EPS) * gkv_ref[...]).astype(BF16), wkv_ref[...]).astype(BF16)

    ik = zc(Z_IK)
    ms = jnp.mean(ik * ik, axis=-1, keepdims=True)
    ikn = ik * lax.rsqrt(ms + EPS) * gik_ref[...]
    put(P_IK, 0, _rope128(ikn, tab(0), ROPE_KINDS[0][0] // 2))

    sm = zc(Z_SMALL)
    lane = lax.broadcasted_iota(I32, sm.shape, 1)
    iw_scale = IDX_HEADS ** -0.5 * IDX_DIM ** -0.5
    small = jnp.where(lane < SMALL_IW, jax.nn.sigmoid(sm), sm * iw_scale)
    small_ref[...] = small
    smallt_ref[0] = small.T


def _proj_prep(x2, g, w, tab, gq, gkv, gik, wq, wkv, *, batch, seq, ts):
    m, d = x2.shape
    spb = seq // ts
    assert ts == VT_TILE and w.shape == (d, Z_WIDTH)
    row = lambda w: pl.BlockSpec((ts, w), lambda i: (i, 0))
    const = lambda a: pl.BlockSpec(a.shape, lambda i: (0,) * a.ndim)
    return pl.pallas_call(
        _proj_prep_kernel,
        out_shape=(jax.ShapeDtypeStruct((m, P_WIDTH), BF16),
                   jax.ShapeDtypeStruct((batch, T_ROWS, seq), BF16),
                   jax.ShapeDtypeStruct((batch, spb, BR_WIDTH, VT_TILE), BF16),
                   jax.ShapeDtypeStruct((m, LANES), BF16),
                   jax.ShapeDtypeStruct((m, LANES), BF16),
                   jax.ShapeDtypeStruct((m, LANES), F32),
                   jax.ShapeDtypeStruct((batch, LANES, seq), F32),
                   jax.ShapeDtypeStruct((m, wq.shape[1]), BF16),
                   jax.ShapeDtypeStruct((m, wkv.shape[1]), BF16)),
        grid=(m // ts,),
        in_specs=[row(d),
                  pl.BlockSpec((1, d), lambda i: (0, 0)),
                  pl.BlockSpec((d, Z_WIDTH), lambda i: (0, 0)),
                  pl.BlockSpec((ts, 3 * TAB_W), lambda i: (i % spb, 0)),
                  pl.BlockSpec((1, 512), lambda i: (0, 0)),
                  pl.BlockSpec((1, MLA_KV_LORA), lambda i: (0, 0)),
                  pl.BlockSpec((1, LANES), lambda i: (0, 0)),
                  const(wq), const(wkv)],
        out_specs=(row(P_WIDTH),
                   pl.BlockSpec((1, T_ROWS, ts), lambda i: (i // spb, 0, i % spb)),
                   pl.BlockSpec((1, 1, BR_WIDTH, VT_TILE), lambda i: (i // spb, i % spb, 0, 0)),
                   row(LANES), row(LANES), row(LANES),
                   pl.BlockSpec((1, LANES, ts), lambda i: (i // spb, 0, i % spb)),
                   row(wq.shape[1]), row(wkv.shape[1])),
        compiler_params=_cparams(1),
        name="proj_prep",
    )(x2, g, w, tab, gq, gkv, gik, wq, wkv)


def _softmax_init(mx_scr, l_scr, acc_scr):
    mx_scr[...] = jnp.full(mx_scr.shape, NEG, F32)
    l_scr[...] = jnp.zeros(l_scr.shape, F32)
    acc_scr[...] = jnp.zeros(acc_scr.shape, F32)


def _score_store(g, j, s, s_scr, mx_scr):
    s_scr[g, j] = s
    m = s[:, 0:LANES]
    for c in range(1, s.shape[1] // LANES):
        m = jnp.maximum(m, s[:, c * LANES:(c + 1) * LANES])
    mx_scr[g] = jnp.maximum(mx_scr[g], m)


def _row_max_finish(mx_scr):
    for g in range(mx_scr.shape[0]):
        m = jnp.max(mx_scr[g], axis=-1, keepdims=True)
        mx_scr[g] = jnp.broadcast_to(m, mx_scr.shape[1:])


def _prob_accumulate(g, j, v_tile, s_scr, mx_scr, l_scr, acc_scr):
    mb = mx_scr[g]
    s = s_scr[g, j]
    ps = [jnp.exp2(s[:, c * LANES:(c + 1) * LANES] - mb) for c in range(s.shape[1] // LANES)]
    tot = ps[0]
    for p in ps[1:]:
        tot = tot + p
    l_scr[g] += tot
    acc_scr[g] += _dot(jnp.concatenate(ps, axis=1).astype(BF16), v_tile)


def _softmax_out(g, l_scr, acc_scr):
    return acc_scr[g] / jnp.sum(l_scr[g], axis=-1, keepdims=True)


def _paired_tiles(n, step):
    def pair(jj, carry):
        step(2 * jj)
        step(2 * jj + 1)
        return carry

    lax.fori_loop(0, n // 2, pair, 0)

    @pl.when(n % 2 == 1)
    def _():
        step(n - 1)


def _causal_tiles(step, n_full):
    _paired_tiles(n_full, lambda j: step(j, False))
    step(n_full, True)


def _softmax_scratch(groups, n_tiles, rows, tk):
    return [pltpu.VMEM((groups, n_tiles, rows, tk), F32),
            pltpu.VMEM((groups, rows, LANES), F32),
            pltpu.VMEM((groups, rows, LANES), F32),
            pltpu.VMEM((groups, rows, HEAD_W), F32)]


def _diff_attn_kernel(q_ref, k_ref, v_ref, lv_ref, g_ref, o_ref, s_scr, mx_scr, l_scr, acc_scr,
                      *, tq, tk, lam_init):
    qs = pl.program_id(1) * tq
    n_full = qs // tk
    scale = DA_DIM ** -0.5 * LOG2E
    lv = lv_ref[...]
    lam = (jnp.exp(jnp.sum(lv[0:1] * lv[1:2], axis=-1, keepdims=True))
           - jnp.exp(jnp.sum(lv[2:3] * lv[3:4], axis=-1, keepdims=True)) + lam_init)
    lane = lax.broadcasted_iota(I32, (tq, HEAD_W), 1)
    row_t = qs + lax.broadcasted_iota(I32, (2 * tq, 1), 0) % tq
    col0 = lax.broadcasted_iota(I32, (2 * tq, tk), 1)
    _softmax_init(mx_scr, l_scr, acc_scr)

    def scores(j, masked):
        ks = pl.multiple_of(j * tk, tk)
        for h in range(HEADS):
            hs = slice(h * HEAD_W, (h + 1) * HEAD_W)
            qh = q_ref[0, :, hs]
            zero = jnp.zeros_like(qh)
            q2 = jnp.concatenate([jnp.where(lane < DA_DIM, qh, zero),
                                  jnp.where(lane >= DA_DIM, qh, zero)], axis=0)
            s = _dot_nt(q2, k_ref[0, pl.ds(ks, tk), hs]) * scale
            if masked:
                s = jnp.where(col0 + ks <= row_t, s, NEG)
            _score_store(h, j, s, s_scr, mx_scr)

    _causal_tiles(scores, n_full)
    _row_max_finish(mx_scr)

    def probs(j, carry):
        ks = pl.multiple_of(j * tk, tk)
        for h in range(HEADS):
            v_tile = v_ref[0, pl.ds(ks, tk), h * HEAD_W:(h + 1) * HEAD_W]
            _prob_accumulate(h, j, v_tile, s_scr, mx_scr, l_scr, acc_scr)
        return carry

    _paired_tiles(n_full + 1, lambda j: probs(j, 0))
    for h in range(HEADS):
        o2 = _softmax_out(h, l_scr, acc_scr)
        o = o2[:tq] - lam * o2[tq:]
        ms = jnp.mean(o * o, axis=-1, keepdims=True)
        o = o * lax.rsqrt(ms + EPS) * g_ref[...]
        o_ref[0, :, h * HEAD_W:(h + 1) * HEAD_W] = (o * (1.0 - lam_init)).astype(BF16)


def _diff_attn(p3, lv, g, *, lam_init, tq, tk):
    b, s, _ = p3.shape
    return pl.pallas_call(
        functools.partial(_diff_attn_kernel, tq=tq, tk=tk, lam_init=lam_init),
        out_shape=jax.ShapeDtypeStruct((b, s, BR_WIDTH), BF16),
        grid=(b, s // tq),
        in_specs=[pl.BlockSpec((1, tq, 512), lambda bi, i: (bi, i, P_AQ // 512)),
                  pl.BlockSpec((1, s, 512), lambda bi, i: (bi, 0, P_AK // 512)),
                  pl.BlockSpec((1, s, 512), lambda bi, i: (bi, 0, P_AV // 512)),
                  pl.BlockSpec((4, DA_DIM), lambda bi, i: (0, 0)),
                  pl.BlockSpec((1, HEAD_W), lambda bi, i: (0, 0))],
        out_specs=pl.BlockSpec((1, tq, BR_WIDTH), lambda bi, i: (bi, i, 0)),
        scratch_shapes=_softmax_scratch(HEADS, s // tk, 2 * tq, tk),
        compiler_params=_cparams(2),
        name="diff_attn",
    )(p3, p3, p3, lv, g)


def _nsa_compress_kernel(gk_ref, gv_ref, w1k_ref, w1v_ref, pek_ref, pev_ref,
                         w1kf_ref, w1vf_ref, w2k_ref, w2v_ref, kc_ref, vc_ref):
    def one(g_ref, w1cat_ref, pe_ref, w1f_ref, w2_ref, o_ref):
        y = _dot(g_ref[0], w1cat_ref[...])
        n = y.shape[0]
        nxt = pltpu.roll(y[:, HEAD_W:], n - 1, 0)
        c = _dot(pe_ref[...], w1f_ref[...])[0:1]
        hid = jax.nn.gelu(y[:, :HEAD_W] + nxt + c)
        o_ref[0] = _dot(hid.astype(BF16), w2_ref[...]).astype(BF16)

    one(gk_ref, w1k_ref, pek_ref, w1kf_ref, w2k_ref, kc_ref)
    one(gv_ref, w1v_ref, pev_ref, w1vf_ref, w2v_ref, vc_ref)


def _nsa_compress(gk, gv, w1k_cat, w1v_cat, pek, pev, w1k, w1v, w2k, w2v):
    b, ng, gw = gk.shape
    full = lambda shape: pl.BlockSpec(shape, lambda bi: (0,) * len(shape))
    return pl.pallas_call(
        _nsa_compress_kernel,
        out_shape=(jax.ShapeDtypeStruct((b, ng, HEAD_W), BF16),
                   jax.ShapeDtypeStruct((b, ng, HEAD_W), BF16)),
        grid=(b,),
        in_specs=[pl.BlockSpec((1, ng, gw), lambda bi: (bi, 0, 0)),
                  pl.BlockSpec((1, ng, gw), lambda bi: (bi, 0, 0)),
                  full(w1k_cat.shape), full(w1v_cat.shape), full(pek.shape), full(pev.shape),
                  full(w1k.shape), full(w1v.shape), full(w2k.shape), full(w2v.shape)],
        out_specs=(pl.BlockSpec((1, ng, HEAD_W), lambda bi: (bi, 0, 0)),
                   pl.BlockSpec((1, ng, HEAD_W), lambda bi: (bi, 0, 0))),
        compiler_params=_cparams(1),
        name="nsa_compress",
    )(gk, gv, w1k_cat, w1v_cat, pek, pev, w1k, w1v, w2k, w2v)


NSA_GROUPS = 2


def _nsa_kernel(q_ref, kc_ref, vc_ref, ks_ref, vs_ref, kw_ref, vw_ref, small_ref, ov_ref, e_ref,
                o_ref, s_scr, mx_scr, l_scr, acc_scr, cmp_scr, win_scr, *, tq, tk, seq):
    qs = pl.program_id(1) * tq
    scale = NSA_DK ** -0.5
    ns = seq // SEL_LEN
    n_sel = min(SEL_N, ns)
    r = HEADS * tq
    rg = r // NSA_GROUPS
    q4 = jnp.concatenate([q_ref[0, :, h * HEAD_W:(h + 1) * HEAD_W] for h in range(HEADS)], axis=0)
    t1 = qs + lax.broadcasted_iota(I32, (tq, 1), 0)
    t4 = qs + lax.broadcasted_iota(I32, (r, 1), 0) % tq

    wspan = WIN + tq
    start = pl.multiple_of(jnp.maximum(qs - WIN, 0), tq)
    sw = _dot_nt(q4, kw_ref[0, pl.ds(start, wspan), :]) * (scale * LOG2E)
    dist = (t4 - start) - lax.broadcasted_iota(I32, (r, wspan), 1)
    sw = jnp.where(pltpu.bitcast(dist, jnp.uint32) < jnp.uint32(WIN), sw, NEG)
    e = jnp.exp2(sw - jnp.max(sw, axis=-1, keepdims=True))
    win_scr[...] = (_dot(e.astype(BF16), vw_ref[0, pl.ds(start, wspan), :])
                    / jnp.sum(e, axis=-1, keepdims=True))

    kc = kc_ref[0]
    nc_pad = kc.shape[0]
    sc = _dot_nt(q4, kc) * (scale * LOG2E)
    c_end = lax.broadcasted_iota(I32, (r, nc_pad), 1) * CMP_STRIDE + (CMP_LEN - 1)
    cmask = c_end <= t4
    mx = jnp.max(jnp.where(cmask, sc, NEG), axis=-1, keepdims=True)
    e = jnp.where(cmask, jnp.exp2(sc - mx), 0.0)
    den = jnp.sum(e, axis=-1, keepdims=True)
    pc = e / jnp.where(den > 0.0, den, 1.0)
    cmp_scr[...] = _dot(pc.astype(BF16), vc_ref[0])

    psum = pc[0:tq] + pc[tq:2 * tq] + pc[2 * tq:3 * tq] + pc[3 * tq:4 * tq]
    ov = ov_ref[...]
    hi = psum.astype(BF16)
    r1 = psum - hi.astype(F32)
    mid = r1.astype(BF16)
    lo = (r1 - mid.astype(F32)).astype(BF16)
    imp = _dot(hi, ov) + _dot(mid, ov) + _dot(lo, ov)

    blk = lax.broadcasted_iota(I32, (tq, LANES), 1)
    cur = t1 // SEL_LEN
    forced = (blk == 0) | (blk == cur) | (blk == cur - 1)
    visible = blk * SEL_LEN <= t1
    score = jnp.where(visible, jnp.where(forced, FORCE_SCORE, imp), NEG)
    score = jnp.where(blk < ns, score, PAD_SCORE)
    ns_pad = -(-ns // SUBLANES) * SUBLANES
    score_t = score.T[:ns_pad]
    blk_t = lax.broadcasted_iota(I32, (ns_pad, tq), 0)
    rank = jnp.zeros((ns_pad, tq), I32)
    for jp in range(ns):
        row = score_t[jp:jp + 1, :]
        later = (blk_t > jp).astype(I32)
        rank = rank + jnp.where(row > score_t, 1, jnp.where(row == score_t, later, 0))
    sel_t = jnp.where(rank < n_sel, 1.0, 0.0)
    if ns_pad < LANES:
        sel_t = jnp.concatenate([sel_t, jnp.zeros((LANES - ns_pad, tq), F32)], axis=0)
    selb = sel_t.T.astype(BF16)

    _softmax_init(mx_scr, l_scr, acc_scr)
    col0 = lax.broadcasted_iota(I32, (rg, tk), 1)
    tg = qs + lax.broadcasted_iota(I32, (rg, 1), 0) % tq
    n_tiles = qs // tk + 1

    def scores(j, masked):
        ks0 = pl.multiple_of(j * tk, tk)
        mt = _dot(selb, e_ref[j])
        mg = jnp.concatenate([mt] * (rg // tq), axis=0)
        k_tile = ks_ref[0, pl.ds(ks0, tk), :]
        for g in range(NSA_GROUPS):
            s = _dot_nt(q4[g * rg:(g + 1) * rg], k_tile) * (scale * LOG2E)
            s = jnp.where(mg > 0.5, s, NEG)
            if masked:
                s = jnp.where(col0 + ks0 <= tg, s, NEG)
            _score_store(g, j, s, s_scr, mx_scr)

    _causal_tiles(scores, n_tiles - 1)
    _row_max_finish(mx_scr)

    def probs(j, carry):
        ks0 = pl.multiple_of(j * tk, tk)
        v_tile = vs_ref[0, pl.ds(ks0, tk), :]
        for g in range(NSA_GROUPS):
            _prob_accumulate(g, j, v_tile, s_scr, mx_scr, l_scr, acc_scr)
        return carry

    _paired_tiles(n_tiles, lambda j: probs(j, 0))
    o_slc = jnp.concatenate([_softmax_out(g, l_scr, acc_scr) for g in range(NSA_GROUPS)], axis=0)

    gates = small_ref[0]
    for h in range(HEADS):
        rows = slice(h * tq, (h + 1) * tq)
        g0 = gates[:, SMALL_G + 3 * h:SMALL_G + 3 * h + 1]
        g1 = gates[:, SMALL_G + 3 * h + 1:SMALL_G + 3 * h + 2]
        g2 = gates[:, SMALL_G + 3 * h + 2:SMALL_G + 3 * h + 3]
        o = g0 * cmp_scr[rows, :] + g1 * o_slc[rows] + g2 * win_scr[rows, :]
        o_ref[0, :, h * HEAD_W:(h + 1) * HEAD_W] = o.astype(BF16)


def _nsa(p3, kc, vc, small3, ov, emat, *, tq, tk):
    b, s, _ = p3.shape
    ng = kc.shape[1]
    col = lambda off: (lambda bi, i: (bi, 0, off // LANES))
    return pl.pallas_call(
        functools.partial(_nsa_kernel, tq=tq, tk=tk, seq=s),
        out_shape=jax.ShapeDtypeStruct((b, s, BR_WIDTH), BF16),
        grid=(b, s // tq),
        in_specs=[pl.BlockSpec((1, tq, 512), lambda bi, i: (bi, i, P_BQ // 512)),
                  pl.BlockSpec((1, ng, HEAD_W), lambda bi, i: (bi, 0, 0)),
                  pl.BlockSpec((1, ng, HEAD_W), lambda bi, i: (bi, 0, 0)),
                  pl.BlockSpec((1, s, LANES), col(P_KS)),
                  pl.BlockSpec((1, s, LANES), col(P_VS)),
                  pl.BlockSpec((1, s, LANES), col(P_KW)),
                  pl.BlockSpec((1, s, LANES), col(P_VW)),
                  pl.BlockSpec((1, tq, LANES), lambda bi, i: (bi, i, 0)),
                  pl.BlockSpec(ov.shape, lambda bi, i: (0, 0)),
                  pl.BlockSpec(emat.shape, lambda bi, i: (0, 0, 0))],
        out_specs=pl.BlockSpec((1, tq, BR_WIDTH), lambda bi, i: (bi, i, 0)),
        scratch_shapes=(_softmax_scratch(NSA_GROUPS, s // tk, HEADS * tq // NSA_GROUPS, tk)
                        + [pltpu.VMEM((HEADS * tq, HEAD_W), F32), pltpu.VMEM((HEADS * tq, HEAD_W), F32)]),
        compiler_params=_cparams(2),
        name="nsa_attn",
    )(p3, kc, vc, p3, p3, p3, p3, small3, ov, emat)


def _mla_attn_kernel(qn_ref, qr_ref, kn_ref, kr_ref, v_ref, o_ref, s_scr, mx_scr, l_scr, acc_scr,
                     *, tq, tk):
    qs = pl.program_id(1) * tq
    n_full = qs // tk
    scale = (MLA_NOPE + MLA_ROPE) ** -0.5 * LOG2E
    lane = lax.broadcasted_iota(I32, (tq, LANES), 1)
    row_t = qs + lax.broadcasted_iota(I32, (tq, 1), 0)
    col0 = lax.broadcasted_iota(I32, (tq, tk), 1)
    _softmax_init(mx_scr, l_scr, acc_scr)

    def scores(j, masked):
        ks = pl.multiple_of(j * tk, tk)
        kr_tile = kr_ref[0, pl.ds(ks, tk), :]
        for h in range(HEADS):
            hs = slice(h * HEAD_W, (h + 1) * HEAD_W)
            pair = qr_ref[0, :, (h // 2) * LANES:(h // 2 + 1) * LANES]
            keep = (lane < MLA_ROPE) if h % 2 == 0 else (lane >= MLA_ROPE)
            qr = jnp.where(keep, pair, jnp.zeros_like(pair))
            s = _dot_nt(jnp.concatenate([qn_ref[0, :, hs], qr], axis=1),
                        jnp.concatenate([kn_ref[0, pl.ds(ks, tk), hs], kr_tile], axis=1)) * scale
            if masked:
                s = jnp.where(col0 + ks <= row_t, s, NEG)
            _score_store(h, j, s, s_scr, mx_scr)

    _causal_tiles(scores, n_full)
    _row_max_finish(mx_scr)

    def probs(j, carry):
        ks = pl.multiple_of(j * tk, tk)
        for h in range(HEADS):
            v_tile = v_ref[0, pl.ds(ks, tk), h * HEAD_W:(h + 1) * HEAD_W]
            _prob_accumulate(h, j, v_tile, s_scr, mx_scr, l_scr, acc_scr)
        return carry

    _paired_tiles(n_full + 1, lambda j: probs(j, 0))
    for h in range(HEADS):
        o_ref[0, :, h * HEAD_W:(h + 1) * HEAD_W] = _softmax_out(h, l_scr, acc_scr).astype(BF16)


def _mla_attn(q3, kv3, p3, *, tq, tk):
    b, s, _ = q3.shape
    return pl.pallas_call(
        functools.partial(_mla_attn_kernel, tq=tq, tk=tk),
        out_shape=jax.ShapeDtypeStruct((b, s, BR_WIDTH), BF16),
        grid=(b, s // tq),
        in_specs=[pl.BlockSpec((1, tq, 512), lambda bi, i: (bi, i, 0)),
                  pl.BlockSpec((1, tq, 256), lambda bi, i: (bi, i, 2)),
                  pl.BlockSpec((1, s, 512), lambda bi, i: (bi, 0, 0)),
                  pl.BlockSpec((1, s, LANES), lambda bi, i: (bi, 0, P_KR // LANES)),
                  pl.BlockSpec((1, s, 512), lambda bi, i: (bi, 0, 1))],
        out_specs=pl.BlockSpec((1, tq, BR_WIDTH), lambda bi, i: (bi, i, 0)),
        scratch_shapes=_softmax_scratch(HEADS, s // tk, tq, tk),
        compiler_params=_cparams(2),
        name="mla_attn",
    )(q3, q3, kv3, p3, kv3)


def _sortable_key(x):
    bits = pltpu.bitcast(x + 0.0, I32)
    return bits ^ (lax.shift_right_arithmetic(bits, 31) & 0x7FFFFFFF)


def _fold_rows(x, op):
    n = x.shape[0] // SUBLANES
    return op(x.reshape(n, SUBLANES, x.shape[1]), axis=0)


def _count16(half_scr, n_tiles, pred, tq):
    def count_tile(j, cnt):
        hit = pred(half_scr[j]).astype(I16)
        parts = [hit[r:r + PACKED_ROWS] for r in range(0, hit.shape[0], PACKED_ROWS)]
        while len(parts) > 1:
            parts = [a + b for a, b in zip(parts[0::2], parts[1::2])]
        return cnt + parts[0]

    cnt = lax.fori_loop(0, n_tiles, count_tile, jnp.zeros((PACKED_ROWS, tq), I16))
    return jnp.sum(cnt.astype(I32), axis=0, keepdims=True)


def _bisect16(half_scr, n_tiles, need, tq):
    def bit_body(i, carry):
        th, tot = carry
        cand = th + lax.shift_left(jnp.int32(1), 15 - i)
        c16 = cand.astype(I16)
        total = _count16(half_scr, n_tiles, lambda x: x >= c16, tq)
        ok = total >= need
        return jnp.where(ok, cand, th), jnp.where(ok, total, tot)

    return lax.fori_loop(0, 16, bit_body, (jnp.full((1, tq), HALF_MIN, I32), jnp.full((1, tq), -1, I32)))


def _dsa_kernel(qt_ref, iqt_ref, iwt_ref, k_ref, ik_ref, vt_ref, tri_ref, o_ref,
                key_scr, half_scr, s_scr, mx_scr, l_scr, acc_scr, *, tq, tk, top):
    qs = pl.program_id(1) * tq
    n_tiles = (qs + tq - 1) // tk + 1
    scale = DSA_DIM ** -0.5 * LOG2E
    t_lane = qs + lax.broadcasted_iota(I32, (tk, tq), 1)
    krow0 = lax.broadcasted_iota(I32, (tk, tq), 0)
    half_rows = lax.broadcasted_iota(I32, (LANES, tq), 0) < IDX_DIM
    vt_per_tile = tk // VT_TILE

    def score_tile(j, masked):
        ks = pl.multiple_of(j * tk, tk)
        ikt = ik_ref[0, pl.ds(ks, tk), :]
        acc = jnp.zeros((tk, tq), F32)
        for h in range(IDX_HEADS):
            pair = iqt_ref[0, (h // 2) * LANES:(h // 2 + 1) * LANES, :]
            keep = half_rows if h % 2 == 0 else jnp.logical_not(half_rows)
            iq_h = jnp.where(keep, pair, jnp.zeros_like(pair))
            w_h = iwt_ref[0, SMALL_IW + h:SMALL_IW + h + 1, :]
            acc = acc + w_h * jnp.maximum(_dot(ikt, iq_h), 0.0)
        key = _sortable_key(acc)
        if masked:
            key = jnp.where(krow0 + ks <= t_lane, key, INT_MIN)
        key_scr[j] = key
        half_scr[j] = lax.shift_right_arithmetic(key, 16).astype(I16)

    _causal_tiles(score_tile, n_tiles - 1)

    th_hi, n_ge_hi = _bisect16(half_scr, n_tiles, top, tq)
    hi16 = th_hi.astype(I16)
    n_gt_hi = _count16(half_scr, n_tiles, lambda x: x > hi16, tq)
    need_lo = top - n_gt_hi

    def low_tile(j, carry):
        key = key_scr[j]
        lo = (key & 0xFFFF) + HALF_MIN
        same_hi = lax.shift_right_arithmetic(key, 16) == th_hi
        half_scr[j] = jnp.where(same_hi, lo, HALF_MIN).astype(I16)
        return carry

    lax.fori_loop(0, n_tiles, low_tile, 0)
    th_lo, n_ge_lo = _bisect16(half_scr, n_tiles, need_lo, tq)
    theta = lax.shift_left(th_hi, 16) + (th_lo - HALF_MIN)
    n_ge_lo = jnp.where(n_ge_lo >= 0, n_ge_lo, n_ge_hi - n_gt_hi)
    tied = (n_ge_lo > need_lo) & (n_ge_hi >= 0)

    @pl.when(jnp.max(tied.astype(I32)) > 0)
    def _():
        lo16 = th_lo.astype(I16)
        n_gt = n_gt_hi + _count16(half_scr, n_tiles, lambda x: x > lo16, tq)
        keep_eq = jnp.where(tied, top - n_gt, tk * key_scr.shape[0]).astype(F32)

        def demote(j, seen):
            k = key_scr[j]
            eq = k == theta
            eqf = jnp.where(eq, 1.0, 0.0)
            before = _dot(tri_ref[...], eqf.astype(BF16)) + seen
            key_scr[j] = jnp.where(jnp.where(eq, before, -1.0) >= keep_eq, k - 1, k)
            return seen + jnp.sum(_fold_rows(eqf, jnp.sum), axis=0, keepdims=True)

        lax.fori_loop(0, n_tiles, demote, jnp.zeros((1, tq), F32))

    theta = jnp.maximum(theta, INT_MIN + 1)

    mx_scr[...] = jnp.full(mx_scr.shape, NEG, F32)
    l_scr[...] = jnp.zeros(l_scr.shape, F32)
    acc_scr[...] = jnp.zeros(acc_scr.shape, F32)

    def scores(j, carry):
        ks = pl.multiple_of(j * tk, tk)
        sel = key_scr[j] >= theta
        for h in range(HEADS):
            hs = slice(h * HEAD_W, (h + 1) * HEAD_W)
            s = _dot(k_ref[0, pl.ds(ks, tk), hs], qt_ref[0, hs, :]) * scale
            s = jnp.where(sel, s, NEG)
            s_scr[h, j] = s
            mx_scr[h] = jnp.maximum(mx_scr[h], _fold_rows(s, jnp.max))
        return carry

    _paired_tiles(n_tiles, lambda j: scores(j, 0))
    for h in range(HEADS):
        m = jnp.max(mx_scr[h], axis=0, keepdims=True)
        mx_scr[h] = jnp.broadcast_to(m, (SUBLANES, tq))

    def probs(j, carry):
        for h in range(HEADS):
            p = jnp.exp2(s_scr[h, j] - mx_scr[h][0:1])
            l_scr[h] += _fold_rows(p, jnp.sum)
            pb = p.astype(BF16)
            for c in range(vt_per_tile):
                vt = vt_ref[0, j * vt_per_tile + c, h * HEAD_W:(h + 1) * HEAD_W, :]
                acc_scr[h] += _dot(vt, pb[c * VT_TILE:(c + 1) * VT_TILE])
        return carry

    _paired_tiles(n_tiles, lambda j: probs(j, 0))
    for h in range(HEADS):
        ot = acc_scr[h] / jnp.sum(l_scr[h], axis=0, keepdims=True)
        o_ref[0, :, h * HEAD_W:(h + 1) * HEAD_W] = ot.T.astype(BF16)


def _dsa(p3, t3, vt4, smallt, *, tq, tk, top):
    b, s, _ = p3.shape
    n_vt = vt4.shape[1]
    tri = jnp.asarray(np.tril(np.ones((tk, tk), np.float32), -1), BF16)
    return pl.pallas_call(
        functools.partial(_dsa_kernel, tq=tq, tk=tk, top=top),
        out_shape=jax.ShapeDtypeStruct((b, s, BR_WIDTH), BF16),
        grid=(b, s // tq),
        in_specs=[pl.BlockSpec((1, 512, tq), lambda bi, i: (bi, T_DQ // 512, i)),
                  pl.BlockSpec((1, 512, tq), lambda bi, i: (bi, T_IQ // 512, i)),
                  pl.BlockSpec((1, LANES, tq), lambda bi, i: (bi, 0, i)),
                  pl.BlockSpec((1, s, 512), lambda bi, i: (bi, 0, P_DK // 512)),
                  pl.BlockSpec((1, s, LANES), lambda bi, i: (bi, 0, P_IK // LANES)),
                  pl.BlockSpec((1, n_vt, BR_WIDTH, VT_TILE), lambda bi, i: (bi, 0, 0, 0)),
                  pl.BlockSpec((tk, tk), lambda bi, i: (0, 0))],
        out_specs=pl.BlockSpec((1, tq, BR_WIDTH), lambda bi, i: (bi, i, 0)),
        scratch_shapes=[pltpu.VMEM((s // tk, tk, tq), I32),
                        pltpu.VMEM((s // tk, tk, tq), I16),
                        pltpu.VMEM((HEADS, s // tk, tk, tq), F32),
                        pltpu.VMEM((HEADS, SUBLANES, tq), F32),
                        pltpu.VMEM((HEADS, SUBLANES, tq), F32),
                        pltpu.VMEM((HEADS, HEAD_W, tq), F32)],
        compiler_params=_cparams(2),
        name="dsa_attn",
    )(t3, t3, smallt, p3, p3, vt4, tri)


def _merge_kernel(x_ref, g_ref, oa_ref, ob_ref, oc_ref, od_ref, wg_ref, wb_ref, wo_ref, o_ref):
    x = x_ref[...]
    d = x.shape[1]
    ms = jnp.mean(x * x, axis=-1, keepdims=True)
    h = (x * lax.rsqrt(ms + EPS) * g_ref[...]).astype(BF16)
    acc = jnp.zeros(x.shape, F32)
    for n, br_ref in enumerate((oa_ref, ob_ref, oc_ref, od_ref)):
        gate = jax.nn.sigmoid(_dot(h, wg_ref[:, n * d:(n + 1) * d]))
        acc = acc + gate * _dot(br_ref[...], wb_ref[n])
    o_ref[...] = x + _dot(acc.astype(BF16), wo_ref[...])


def _merge(x2, g, oa, ob, oc, od, wg, wb, wo, *, tm):
    m, d = x2.shape
    row = lambda w: pl.BlockSpec((tm, w), lambda i: (i, 0))
    return pl.pallas_call(
        _merge_kernel,
        out_shape=jax.ShapeDtypeStruct((m, d), F32),
        grid=(m // tm,),
        in_specs=[row(d), pl.BlockSpec((1, d), lambda i: (0, 0)),
                  row(BR_WIDTH), row(BR_WIDTH), row(BR_WIDTH), row(BR_WIDTH),
                  pl.BlockSpec(wg.shape, lambda i: (0, 0)),
                  pl.BlockSpec(wb.shape, lambda i: (0, 0, 0)),
                  pl.BlockSpec(wo.shape, lambda i: (0, 0))],
        out_specs=row(d),
        compiler_params=_cparams(1),
        name="merge",
    )(x2, g, oa, ob, oc, od, wg, wb, wo)


def _ffn_kernel(x_ref, g_ref, wg_ref, wu_ref, wd_ref, gf_ref, o_ref, h_scr, acc_scr, *, final):
    j = pl.program_id(1)

    @pl.when(j == 0)
    def _():
        x = x_ref[...]
        ms = jnp.mean(x * x, axis=-1, keepdims=True)
        h_scr[...] = (x * lax.rsqrt(ms + EPS) * g_ref[...]).astype(BF16)
        acc_scr[...] = jnp.zeros(acc_scr.shape, F32)

    h = h_scr[...]
    a = jax.nn.silu(_dot(h, wg_ref[...])) * _dot(h, wu_ref[...])
    acc_scr[...] += _dot(a.astype(BF16), wd_ref[...])

    @pl.when(j == pl.num_programs(1) - 1)
    def _():
        y = x_ref[...] + acc_scr[...]
        if final:
            ms = jnp.mean(y * y, axis=-1, keepdims=True)
            y = y * lax.rsqrt(ms + EPS) * gf_ref[...]
        o_ref[...] = y


def _ffn(x2, g, wgu, wd, gf, *, final, tm, tf):
    m, d = x2.shape
    dff = wd.shape[0]
    n_f = dff // tf
    return pl.pallas_call(
        functools.partial(_ffn_kernel, final=final),
        out_shape=jax.ShapeDtypeStruct((m, d), F32),
        grid=(m // tm, n_f),
        in_specs=[pl.BlockSpec((tm, d), lambda i, j: (i, 0)),
                  pl.BlockSpec((1, d), lambda i, j: (0, 0)),
                  pl.BlockSpec((d, tf), lambda i, j: (0, j)),
                  pl.BlockSpec((d, tf), lambda i, j: (0, j + n_f)),
                  pl.BlockSpec((tf, d), lambda i, j: (j, 0)),
                  pl.BlockSpec((1, d), lambda i, j: (0, 0))],
        out_specs=pl.BlockSpec((tm, d), lambda i, j: (i, 0)),
        scratch_shapes=[pltpu.VMEM((tm, d), BF16), pltpu.VMEM((tm, d), F32)],
        compiler_params=_cparams(2),
        name="ffn",
    )(x2, g, wgu, wgu, wd, gf)


def _tiles(seq, m, dff):
    pick = lambda n, cands: next(c for c in cands if n % c == 0)
    tk = pick(seq, (512, 256))
    return dict(
        merge_tm=pick(m, (256, 128)),
        prep_ts=VT_TILE,
        diff_tq=128, mla_tq=pick(seq, (256, 128)), nsa_tq=256, dsa_tq=256, tk=tk,
        row_tm=pick(m, (512, 256, 128)),
        ffn_tf=pick(dff, (1408, 704, 256, 128)),
    )


def kernel(x, norm1_g, w_in, diff_lq1, diff_lk1, diff_lq2, diff_lk2, diff_subln_g, nsa_pe_k, nsa_w1_k, nsa_w2_k, nsa_pe_v, nsa_w1_v, nsa_w2_v, mla_q_norm_g, mla_w_uq, mla_kv_norm_g, mla_w_ukv, idx_k_norm_g, w_branch, w_out, norm2_g, w_gate_up, w_down, final_norm_g):
    b, seq, d = x.shape
    depth = w_in.shape[0]
    m = b * seq
    dff = w_down.shape[1]
    t = _tiles(seq, m, dff)
    tk = t["tk"]
    assert seq % SEL_LEN == 0 and seq >= WIN + t["nsa_tq"] and seq // SEL_LEN <= LANES
    assert seq % t["dsa_tq"] == 0 and tk % VT_TILE == 0 and tk >= min(IDX_TOPK, seq // 4)

    col_idx, gate_off, d_in = _in_proj_columns()
    assert w_in.shape[2] == d_in
    tab = jnp.concatenate([_rope_table(seq, rot, per) for rot, per in ROPE_KINDS], axis=1)

    ng = seq // CMP_STRIDE
    ns = seq // SEL_LEN
    c_start = np.arange(ng)[:, None] * CMP_STRIDE
    s_start = np.arange(LANES)[None, :] * SEL_LEN
    ov = ((c_start < s_start + SEL_LEN) & (c_start + CMP_LEN - 1 >= s_start)
          & (np.arange(LANES)[None, :] < ns))
    ov = jnp.asarray(ov, BF16)
    emat = np.arange(LANES)[:, None] == (np.arange(seq)[None, :] // SEL_LEN)
    emat = jnp.asarray(emat.reshape(LANES, seq // tk, tk).transpose(1, 0, 2), BF16)

    qd = MLA_NOPE + MLA_ROPE
    uq_idx = np.concatenate([np.concatenate([np.arange(h * qd, h * qd + MLA_NOPE) for h in range(HEADS)]),
                             np.concatenate([np.arange(h * qd + MLA_NOPE, (h + 1) * qd) for h in range(HEADS)])])
    kvd = MLA_NOPE + HEAD_W
    ukv_idx = np.concatenate([np.concatenate([np.arange(h * kvd, h * kvd + MLA_NOPE) for h in range(HEADS)]),
                              np.concatenate([np.arange(h * kvd + MLA_NOPE, (h + 1) * kvd) for h in range(HEADS)])])

    x2 = x.reshape(m, d)
    half_w1 = CMP_STRIDE * NSA_DK
    for l in range(depth):
        lam_init = 0.8 - 0.6 * math.exp(-0.3 * l)
        w_in_l = w_in[l].astype(BF16)
        w_mix = _take_cols(w_in_l, col_idx)
        w_gate = w_in_l[:, gate_off:]

        gq = jnp.pad(mla_q_norm_g[l], (0, 512 - MLA_Q_LORA))[None]
        gkv = mla_kv_norm_g[l][None]
        gik = jnp.concatenate([idx_k_norm_g[l], idx_k_norm_g[l]])[None]
        wq = jnp.pad(_take_cols(mla_w_uq[l].astype(BF16), uq_idx), ((0, 512 - MLA_Q_LORA), (0, 0)))
        wkv = _take_cols(mla_w_ukv[l].astype(BF16), ukv_idx)
        p2, t3, vt4, kc_tok, vc_tok, small, smallt, q_c, kv_c = _proj_prep(
            x2, norm1_g[l][None], w_mix, tab, gq, gkv, gik, wq, wkv, batch=b, seq=seq, ts=t["prep_ts"])
        p3 = p2.reshape(b, seq, P_WIDTH)
        small3 = small.reshape(b, seq, LANES)

        lv = jnp.stack([diff_lq1[l], diff_lk1[l], diff_lq2[l], diff_lk2[l]])
        o_a = _diff_attn(p3, lv, diff_subln_g[l][None], lam_init=lam_init, tq=t["diff_tq"], tk=tk)

        w1k, w1v = nsa_w1_k[l].astype(BF16), nsa_w1_v[l].astype(BF16)
        w1k_cat = jnp.concatenate([w1k[:half_w1], w1k[half_w1:]], axis=1)
        w1v_cat = jnp.concatenate([w1v[:half_w1], w1v[half_w1:]], axis=1)
        pek = jnp.broadcast_to(nsa_pe_k[l].reshape(1, -1), (8, CMP_LEN * NSA_DK)).astype(BF16)
        pev = jnp.broadcast_to(nsa_pe_v[l].reshape(1, -1), (8, CMP_LEN * NSA_DK)).astype(BF16)
        kc, vc = _nsa_compress(kc_tok.reshape(b, ng, half_w1), vc_tok.reshape(b, ng, half_w1),
                               w1k_cat, w1v_cat, pek, pev, w1k, w1v,
                               nsa_w2_k[l].astype(BF16), nsa_w2_v[l].astype(BF16))
        o_b = _nsa(p3, kc, vc, small3, ov, emat, tq=t["nsa_tq"], tk=tk)

        o_c = _mla_attn(q_c.reshape(b, seq, -1), kv_c.reshape(b, seq, -1), p3, tq=t["mla_tq"], tk=tk)

        o_d = _dsa(p3, t3, vt4, smallt, tq=t["dsa_tq"], tk=tk, top=min(IDX_TOPK, seq // 4))

        x2 = _merge(x2, norm1_g[l][None],
                    o_a.reshape(m, -1), o_b.reshape(m, -1), o_c.reshape(m, -1), o_d.reshape(m, -1),
                    w_gate, w_branch[l].astype(BF16), w_out[l].astype(BF16), tm=t["merge_tm"])
        x2 = _ffn(x2, norm2_g[l][None], w_gate_up[l].astype(BF16), w_down[l].astype(BF16),
                  final_norm_g[None], final=(l == depth - 1), tm=t["row_tm"], tf=t["ffn_tf"])
    return x2.reshape(b, seq, d)
```

```python
import functools
import math

import numpy as np
import jax
import jax.numpy as jnp
from jax import lax
from jax.experimental import pallas as pl
from jax.experimental.pallas import tpu as pltpu

F32 = jnp.float32
BF16 = jnp.bfloat16
I32 = jnp.int32
I16 = jnp.int16

LANES = 128
SUBLANES = 8
PACKED_ROWS = 16
HALF_MIN = -32768
VMEM_LIMIT = 56 * 1024 * 1024

ROPE_THETA = 500000.0
NEG = -1e30
LOG2E = math.log2(math.e)
FORCE_SCORE = 1e9
PAD_SCORE = -3e38
EPS = 1e-6
INT_MIN = -2147483648

HEADS = 4
HEAD_W = 128
BR_WIDTH = HEADS * HEAD_W
DA_DIM = 64
NSA_DK = 128
CMP_LEN = 32
CMP_STRIDE = 16
SEL_LEN = 64
SEL_N = 16
WIN = 512
MLA_Q_LORA = 384
MLA_KV_LORA = 256
MLA_NOPE = 128
MLA_ROPE = 64
DSA_DIM = 128
IDX_HEADS = 8
IDX_DIM = 64
IDX_TOPK = 256

Z_AQ, Z_AK, Z_AV, Z_BQ, Z_DQ, Z_DK, Z_DV, Z_IQ = (i * 512 for i in range(8))
Z_CQ = 4096
Z_CKV = 4608
Z_KC, Z_KS, Z_KW, Z_VC, Z_VS, Z_VW, Z_KR, Z_IK, Z_SMALL = (4864 + i * 128 for i in range(9))
Z_WIDTH = 6144
P_AQ, P_AK, P_AV, P_BQ, P_DK = (i * 512 for i in range(5))
P_KS, P_KW, P_VS, P_VW, P_KR, P_IK = (2560 + i * 128 for i in range(6))
P_WIDTH = 3328
T_DQ, T_IQ = 0, 512
T_ROWS = 1024
VT_TILE = 256
SMALL_G = 0
SMALL_IW = 12

ROPE_KINDS = ((16, 64), (32, 128), (64, 64))
TAB_W = 3 * LANES


def _cparams(n_axes):
    return pltpu.CompilerParams(dimension_semantics=("arbitrary",) * n_axes,
                                vmem_limit_bytes=VMEM_LIMIT)


def _dot(a, b):
    return jnp.dot(a, b, preferred_element_type=F32)


def _dot_nt(a, b):
    return lax.dot_general(a, b, (((1,), (1,)), ((), ())), preferred_element_type=F32)


def _in_proj_columns():
    names = (("a_q", 512), ("a_k", 512), ("a_v", 512), ("b_q", 512),
             ("b_kc", 128), ("b_vc", 128), ("b_ks", 128), ("b_vs", 128),
             ("b_kw", 128), ("b_vw", 128), ("b_g", 12),
             ("c_q", 384), ("c_kv", 256), ("c_kr", 64),
             ("d_q", 512), ("d_k", 512), ("d_v", 512),
             ("d_iq", 512), ("d_ik", 64), ("d_iw", 8), ("gate", 4096))
    off, o = {}, 0
    for nm, n in names:
        off[nm] = (o, n)
        o += n
    idx = np.full((Z_WIDTH,), -1, np.int64)

    def put(dst, nm):
        s, n = off[nm]
        idx[dst:dst + n] = np.arange(s, s + n)

    put(Z_AQ, "a_q"); put(Z_AK, "a_k"); put(Z_AV, "a_v"); put(Z_BQ, "b_q")
    put(Z_DQ, "d_q"); put(Z_DK, "d_k"); put(Z_DV, "d_v"); put(Z_IQ, "d_iq")
    put(Z_CQ, "c_q"); put(Z_CKV, "c_kv")
    put(Z_KC, "b_kc"); put(Z_KS, "b_ks"); put(Z_KW, "b_kw")
    put(Z_VC, "b_vc"); put(Z_VS, "b_vs"); put(Z_VW, "b_vw")
    put(Z_KR, "c_kr"); put(Z_KR + 64, "c_kr")
    put(Z_IK, "d_ik"); put(Z_IK + 64, "d_ik")
    put(Z_SMALL + SMALL_G, "b_g"); put(Z_SMALL + SMALL_IW, "d_iw")
    return idx, off["gate"][0], o


def _take_cols(w, idx):
    runs, i, n = [], 0, len(idx)
    while i < n:
        j = i + 1
        if idx[i] < 0:
            while j < n and idx[j] < 0:
                j += 1
            runs.append(jnp.zeros((w.shape[0], j - i), w.dtype))
        else:
            while j < n and idx[j] == idx[j - 1] + 1:
                j += 1
            runs.append(w[:, int(idx[i]):int(idx[i]) + (j - i)])
        i = j
    return jnp.concatenate(runs, axis=1)


def _rope_table(seq, rot, period):
    half = rot // 2
    inv = jnp.power(jnp.float32(ROPE_THETA), -jnp.arange(0, rot, 2, dtype=F32) / rot)
    ang = jnp.arange(seq, dtype=F32)[:, None] * inv[None, :]
    cos, sin = jnp.cos(ang), jnp.sin(ang)
    lane = np.arange(LANES) % period
    in1 = lane < half
    in2 = (lane >= half) & (lane < 2 * half)
    fidx = np.where(in1, lane, np.where(in2, lane - half, 0))
    cosl, sinl = cos[:, fidx], sin[:, fidx]
    c = jnp.where(jnp.asarray(in1 | in2)[None], cosl, 1.0)
    s1 = jnp.where(jnp.asarray(in1)[None], -sinl, 0.0)
    s2 = jnp.where(jnp.asarray(in2)[None], sinl, 0.0)
    return jnp.concatenate([c, s1, s2], axis=1)


def _rope128(x, tab, half):
    return (x * tab[:, 0:LANES]
            + pltpu.roll(x, LANES - half, 1) * tab[:, LANES:2 * LANES]
            + pltpu.roll(x, half, 1) * tab[:, 2 * LANES:3 * LANES])


PROJ_TILE = 512


def _proj_prep_kernel(x_ref, g_ref, w_ref, tab_ref, gq_ref, gkv_ref, gik_ref, wq_ref, wkv_ref,
                      p_ref, t_ref, vt_ref, kc_ref, vc_ref, small_ref, smallt_ref, q_ref, kv_ref):
    x = x_ref[...]
    ms = jnp.mean(x * x, axis=-1, keepdims=True)
    h = (x * lax.rsqrt(ms + EPS) * g_ref[...]).astype(BF16)
    z_tiles = {}

    def z_cols(off, width):
        t = off // PROJ_TILE
        assert (off + width - 1) // PROJ_TILE == t
        if t not in z_tiles:
            z_tiles[t] = _dot(h, w_ref[:, t * PROJ_TILE:(t + 1) * PROJ_TILE])
        lo = off - t * PROJ_TILE
        return z_tiles[t][:, lo:lo + width]

    def zc(off, c=0):
        return z_cols(off + c * LANES, LANES)

    def tab(kind):
        return tab_ref[:, kind * TAB_W:(kind + 1) * TAB_W]

    def put(off, c, v):
        p_ref[:, off + c * LANES:off + (c + 1) * LANES] = v.astype(BF16)

    def rope(off, c, kind):
        return _rope128(zc(off, c), tab(kind), ROPE_KINDS[kind][0] // 2)

    for zoff, poff, kind in ((Z_AQ, P_AQ, 0), (Z_AK, P_AK, 0), (Z_BQ, P_BQ, 1), (Z_DK, P_DK, 1)):
        for c in range(4):
            put(poff, c, rope(zoff, c, kind))
    for c in range(4):
        put(P_AV, c, zc(Z_AV, c))
    put(P_VS, 0, zc(Z_VS)); put(P_VW, 0, zc(Z_VW))
    put(P_KS, 0, rope(Z_KS, 0, 1)); put(P_KW, 0, rope(Z_KW, 0, 1))
    put(P_KR, 0, rope(Z_KR, 0, 2))
    kc_ref[...] = rope(Z_KC, 0, 1).astype(BF16)
    vc_ref[...] = zc(Z_VC).astype(BF16)

    for zoff, toff, kind in ((Z_DQ, T_DQ, 1), (Z_IQ, T_IQ, 0)):
        for c in range(4):
            t_ref[0, toff + c * LANES:toff + (c + 1) * LANES, :] = rope(zoff, c, kind).T.astype(BF16)
    for c in range(4):
        vt_ref[0, 0, c * LANES:(c + 1) * LANES, :] = zc(Z_DV, c).T.astype(BF16)

    cq = z_cols(Z_CQ, 512)
    ms = jnp.sum(cq * cq, axis=-1, keepdims=True) * (1.0 / MLA_Q_LORA)
    q = _dot((cq * lax.rsqrt(ms + EPS) * gq_ref[...]).astype(BF16), wq_ref[...])
    nn = HEADS * MLA_NOPE
    q_ref[:, :nn] = q[:, :nn].astype(BF16)
    for c in range(nn // LANES, (nn + HEADS * MLA_ROPE) // LANES):
        tile = _rope128(q[:, c * LANES:(c + 1) * LANES], tab(2), MLA_ROPE // 2)
        q_ref[:, c * LANES:(c + 1) * LANES] = tile.astype(BF16)
    ckv = z_cols(Z_CKV, MLA_KV_LORA)
    ms = jnp.mean(ckv * ckv, axis=-1, keepdims=True)
    kv_ref[...] = _dot((ckv * lax.rsqrt(ms + EPS) * gkv_ref[...]).astype(BF16), wkv_ref[...]).astype(BF16)

    ik = zc(Z_IK)
    ms = jnp.mean(ik * ik, axis=-1, keepdims=True)
    ikn = ik * lax.rsqrt(ms + EPS) * gik_ref[...]
    put(P_IK, 0, _rope128(ikn, tab(0), ROPE_KINDS[0][0] // 2))

    sm = zc(Z_SMALL)
    lane = lax.broadcasted_iota(I32, sm.shape, 1)
    iw_scale = IDX_HEADS ** -0.5 * IDX_DIM ** -0.5
    small = jnp.where(lane < SMALL_IW, jax.nn.sigmoid(sm), sm * iw_scale)
    small_ref[...] = small
    smallt_ref[0] = small.T


def _proj_prep(x2, g, w, tab, gq, gkv, gik, wq, wkv, *, batch, seq, ts):
    m, d = x2.shape
    spb = seq // ts
    assert ts == VT_TILE and w.shape == (d, Z_WIDTH)
    row = lambda w: pl.BlockSpec((ts, w), lambda i: (i, 0))
    const = lambda a: pl.BlockSpec(a.shape, lambda i: (0,) * a.ndim)
    return pl.pallas_call(
        _proj_prep_kernel,
        out_shape=(jax.ShapeDtypeStruct((m, P_WIDTH), BF16),
                   jax.ShapeDtypeStruct((batch, T_ROWS, seq), BF16),
                   jax.ShapeDtypeStruct((batch, spb, BR_WIDTH, VT_TILE), BF16),
                   jax.ShapeDtypeStruct((m, LANES), BF16),
                   jax.ShapeDtypeStruct((m, LANES), BF16),
                   jax.ShapeDtypeStruct((m, LANES), F32),
                   jax.ShapeDtypeStruct((batch, LANES, seq), F32),
                   jax.ShapeDtypeStruct((m, wq.shape[1]), BF16),
                   jax.ShapeDtypeStruct((m, wkv.shape[1]), BF16)),
        grid=(m // ts,),
        in_specs=[row(d),
                  pl.BlockSpec((1, d), lambda i: (0, 0)),
                  pl.BlockSpec((d, Z_WIDTH), lambda i: (0, 0)),
                  pl.BlockSpec((ts, 3 * TAB_W), lambda i: (i % spb, 0)),
                  pl.BlockSpec((1, 512), lambda i: (0, 0)),
                  pl.BlockSpec((1, MLA_KV_LORA), lambda i: (0, 0)),
                  pl.BlockSpec((1, LANES), lambda i: (0, 0)),
                  const(wq), const(wkv)],
        out_specs=(row(P_WIDTH),
                   pl.BlockSpec((1, T_ROWS, ts), lambda i: (i // spb, 0, i % spb)),
                   pl.BlockSpec((1, 1, BR_WIDTH, VT_TILE), lambda i: (i // spb, i % spb, 0, 0)),
                   row(LANES), row(LANES), row(LANES),
                   pl.BlockSpec((1, LANES, ts), lambda i: (i // spb, 0, i % spb)),
                   row(wq.shape[1]), row(wkv.shape[1])),
        compiler_params=_cparams(1),
        name="proj_prep",
    )(x2, g, w, tab, gq, gkv, gik, wq, wkv)


def _softmax_init(mx_scr, l_scr, acc_scr):
    mx_scr[...] = jnp.full(mx_scr.shape, NEG, F32)
    l_scr[...] = jnp.zeros(l_scr.shape, F32)
    acc_scr[...] = jnp.zeros(acc_scr.shape, F32)


def _score_store(g, j, s, s_scr, mx_scr):
    s_scr[g, j] = s
    m = s[:, 0:LANES]
    for c in range(1, s.shape[1] // LANES):
        m = jnp.maximum(m, s[:, c * LANES:(c + 1) * LANES])
    mx_scr[g] = jnp.maximum(mx_scr[g], m)


def _row_max_finish(mx_scr):
    for g in range(mx_scr.shape[0]):
        m = jnp.max(mx_scr[g], axis=-1, keepdims=True)
        mx_scr[g] = jnp.broadcast_to(m, mx_scr.shape[1:])


def _prob_accumulate(g, j, v_tile, s_scr, mx_scr, l_scr, acc_scr):
    mb = mx_scr[g]
    s = s_scr[g, j]
    ps = [jnp.exp2(s[:, c * LANES:(c + 1) * LANES] - mb) for c in range(s.shape[1] // LANES)]
    tot = ps[0]
    for p in ps[1:]:
        tot = tot + p
    l_scr[g] += tot
    acc_scr[g] += _dot(jnp.concatenate(ps, axis=1).astype(BF16), v_tile)


def _softmax_out(g, l_scr, acc_scr):
    return acc_scr[g] / jnp.sum(l_scr[g], axis=-1, keepdims=True)


def _paired_tiles(n, step):
    def pair(jj, carry):
        step(2 * jj)
        step(2 * jj + 1)
        return carry

    lax.fori_loop(0, n // 2, pair, 0)

    @pl.when(n % 2 == 1)
    def _():
        step(n - 1)


def _causal_tiles(step, n_full):
    _paired_tiles(n_full, lambda j: step(j, False))
    step(n_full, True)


def _softmax_scratch(groups, n_tiles, rows, tk):
    return [pltpu.VMEM((groups, n_tiles, rows, tk), F32),
            pltpu.VMEM((groups, rows, LANES), F32),
            pltpu.VMEM((groups, rows, LANES), F32),
            pltpu.VMEM((groups, rows, HEAD_W), F32)]


def _diff_attn_kernel(q_ref, k_ref, v_ref, lv_ref, g_ref, o_ref, s_scr, mx_scr, l_scr, acc_scr,
                      *, tq, tk, lam_init):
    qs = pl.program_id(1) * tq
    n_full = qs // tk
    scale = DA_DIM ** -0.5 * LOG2E
    lv = lv_ref[...]
    lam = (jnp.exp(jnp.sum(lv[0:1] * lv[1:2], axis=-1, keepdims=True))
           - jnp.exp(jnp.sum(lv[2:3] * lv[3:4], axis=-1, keepdims=True)) + lam_init)
    lane = lax.broadcasted_iota(I32, (tq, HEAD_W), 1)
    row_t = qs + lax.broadcasted_iota(I32, (2 * tq, 1), 0) % tq
    col0 = lax.broadcasted_iota(I32, (2 * tq, tk), 1)
    _softmax_init(mx_scr, l_scr, acc_scr)

    def scores(j, masked):
        ks = pl.multiple_of(j * tk, tk)
        for h in range(HEADS):
            hs = slice(h * HEAD_W, (h + 1) * HEAD_W)
            qh = q_ref[0, :, hs]
            zero = jnp.zeros_like(qh)
            q2 = jnp.concatenate([jnp.where(lane < DA_DIM, qh, zero),
                                  jnp.where(lane >= DA_DIM, qh, zero)], axis=0)
            s = _dot_nt(q2, k_ref[0, pl.ds(ks, tk), hs]) * scale
            if masked:
                s = jnp.where(col0 + ks <= row_t, s, NEG)
            _score_store(h, j, s, s_scr, mx_scr)

    _causal_tiles(scores, n_full)
    _row_max_finish(mx_scr)

    def probs(j, carry):
        ks = pl.multiple_of(j * tk, tk)
        for h in range(HEADS):
            v_tile = v_ref[0, pl.ds(ks, tk), h * HEAD_W:(h + 1) * HEAD_W]
            _prob_accumulate(h, j, v_tile, s_scr, mx_scr, l_scr, acc_scr)
        return carry

    _paired_tiles(n_full + 1, lambda j: probs(j, 0))
    for h in range(HEADS):
        o2 = _softmax_out(h, l_scr, acc_scr)
        o = o2[:tq] - lam * o2[tq:]
        ms = jnp.mean(o * o, axis=-1, keepdims=True)
        o = o * lax.rsqrt(ms + EPS) * g_ref[...]
        o_ref[0, :, h * HEAD_W:(h + 1) * HEAD_W] = (o * (1.0 - lam_init)).astype(BF16)


def _diff_attn(p3, lv, g, *, lam_init, tq, tk):
    b, s, _ = p3.shape
    return pl.pallas_call(
        functools.partial(_diff_attn_kernel, tq=tq, tk=tk, lam_init=lam_init),
        out_shape=jax.ShapeDtypeStruct((b, s, BR_WIDTH), BF16),
        grid=(b, s // tq),
        in_specs=[pl.BlockSpec((1, tq, 512), lambda bi, i: (bi, i, P_AQ // 512)),
                  pl.BlockSpec((1, s, 512), lambda bi, i: (bi, 0, P_AK // 512)),
                  pl.BlockSpec((1, s, 512), lambda bi, i: (bi, 0, P_AV // 512)),
                  pl.BlockSpec((4, DA_DIM), lambda bi, i: (0, 0)),
                  pl.BlockSpec((1, HEAD_W), lambda bi, i: (0, 0))],
        out_specs=pl.BlockSpec((1, tq, BR_WIDTH), lambda bi, i: (bi, i, 0)),
        scratch_shapes=_softmax_scratch(HEADS, s // tk, 2 * tq, tk),
        compiler_params=_cparams(2),
        name="diff_attn",
    )(p3, p3, p3, lv, g)


def _nsa_compress_kernel(gk_ref, gv_ref, w1k_ref, w1v_ref, pek_ref, pev_ref,
                         w1kf_ref, w1vf_ref, w2k_ref, w2v_ref, kc_ref, vc_ref):
    def one(g_ref, w1cat_ref, pe_ref, w1f_ref, w2_ref, o_ref):
        y = _dot(g_ref[0], w1cat_ref[...])
        n = y.shape[0]
        nxt = pltpu.roll(y[:, HEAD_W:], n - 1, 0)
        c = _dot(pe_ref[...], w1f_ref[...])[0:1]
        hid = jax.nn.gelu(y[:, :HEAD_W] + nxt + c)
        o_ref[0] = _dot(hid.astype(BF16), w2_ref[...]).astype(BF16)

    one(gk_ref, w1k_ref, pek_ref, w1kf_ref, w2k_ref, kc_ref)
    one(gv_ref, w1v_ref, pev_ref, w1vf_ref, w2v_ref, vc_ref)


def _nsa_compress(gk, gv, w1k_cat, w1v_cat, pek, pev, w1k, w1v, w2k, w2v):
    b, ng, gw = gk.shape
    full = lambda shape: pl.BlockSpec(shape, lambda bi: (0,) * len(shape))
    return pl.pallas_call(
        _nsa_compress_kernel,
        out_shape=(jax.ShapeDtypeStruct((b, ng, HEAD_W), BF16),
                   jax.ShapeDtypeStruct((b, ng, HEAD_W), BF16)),
        grid=(b,),
        in_specs=[pl.BlockSpec((1, ng, gw), lambda bi: (bi, 0, 0)),
                  pl.BlockSpec((1, ng, gw), lambda bi: (bi, 0, 0)),
                  full(w1k_cat.shape), full(w1v_cat.shape), full(pek.shape), full(pev.shape),
                  full(w1k.shape), full(w1v.shape), full(w2k.shape), full(w2v.shape)],
        out_specs=(pl.BlockSpec((1, ng, HEAD_W), lambda bi: (bi, 0, 0)),
                   pl.BlockSpec((1, ng, HEAD_W), lambda bi: (bi, 0, 0))),
        compiler_params=_cparams(1),
        name="nsa_compress",
    )(gk, gv, w1k_cat, w1v_cat, pek, pev, w1k, w1v, w2k, w2v)


NSA_GROUPS = 2


def _nsa_kernel(q_ref, kc_ref, vc_ref, ks_ref, vs_ref, kw_ref, vw_ref, small_ref, ov_ref, e_ref,
                o_ref, s_scr, mx_scr, l_scr, acc_scr, cmp_scr, win_scr, *, tq, tk, seq):
    qs = pl.program_id(1) * tq
    scale = NSA_DK ** -0.5
    ns = seq // SEL_LEN
    n_sel = min(SEL_N, ns)
    r = HEADS * tq
    rg = r // NSA_GROUPS
    q4 = jnp.concatenate([q_ref[0, :, h * HEAD_W:(h + 1) * HEAD_W] for h in range(HEADS)], axis=0)
    t1 = qs + lax.broadcasted_iota(I32, (tq, 1), 0)
    t4 = qs + lax.broadcasted_iota(I32, (r, 1), 0) % tq

    wspan = WIN + tq
    start = pl.multiple_of(jnp.maximum(qs - WIN, 0), tq)
    sw = _dot_nt(q4, kw_ref[0, pl.ds(start, wspan), :]) * (scale * LOG2E)
    dist = (t4 - start) - lax.broadcasted_iota(I32, (r, wspan), 1)
    sw = jnp.where(pltpu.bitcast(dist, jnp.uint32) < jnp.uint32(WIN), sw, NEG)
    e = jnp.exp2(sw - jnp.max(sw, axis=-1, keepdims=True))
    win_scr[...] = (_dot(e.astype(BF16), vw_ref[0, pl.ds(start, wspan), :])
                    / jnp.sum(e, axis=-1, keepdims=True))

    kc = kc_ref[0]
    nc_pad = kc.shape[0]
    sc = _dot_nt(q4, kc) * (scale * LOG2E)
    c_end = lax.broadcasted_iota(I32, (r, nc_pad), 1) * CMP_STRIDE + (CMP_LEN - 1)
    cmask = c_end <= t4
    mx = jnp.max(jnp.where(cmask, sc, NEG), axis=-1, keepdims=True)
    e = jnp.where(cmask, jnp.exp2(sc - mx), 0.0)
    den = jnp.sum(e, axis=-1, keepdims=True)
    pc = e / jnp.where(den > 0.0, den, 1.0)
    cmp_scr[...] = _dot(pc.astype(BF16), vc_ref[0])

    psum = pc[0:tq] + pc[tq:2 * tq] + pc[2 * tq:3 * tq] + pc[3 * tq:4 * tq]
    ov = ov_ref[...]
    hi = psum.astype(BF16)
    r1 = psum - hi.astype(F32)
    mid = r1.astype(BF16)
    lo = (r1 - mid.astype(F32)).astype(BF16)
    imp = _dot(hi, ov) + _dot(mid, ov) + _dot(lo, ov)

    blk = lax.broadcasted_iota(I32, (tq, LANES), 1)
    cur = t1 // SEL_LEN
    forced = (blk == 0) | (blk == cur) | (blk == cur - 1)
    visible = blk * SEL_LEN <= t1
    score = jnp.where(visible, jnp.where(forced, FORCE_SCORE, imp), NEG)
    score = jnp.where(blk < ns, score, PAD_SCORE)
    ns_pad = -(-ns // SUBLANES) * SUBLANES
    score_t = score.T[:ns_pad]
    n_grp = ns_pad // SUBLANES
    groups = [score_t[g * SUBLANES:(g + 1) * SUBLANES] for g in range(n_grp)]
    sub = lax.broadcasted_iota(I32, (SUBLANES, tq), 0)
    ranks = [jnp.zeros((SUBLANES, tq), I32) for _ in range(n_grp)]
    for jp in range(ns):
        row = score_t[jp:jp + 1, :]
        for g in range(n_grp):
            if g * SUBLANES > jp:
                beats = (row >= groups[g]).astype(I32)
            elif g * SUBLANES + SUBLANES - 1 <= jp:
                beats = (row > groups[g]).astype(I32)
            else:
                beats = jnp.where(sub > (jp - g * SUBLANES), (row >= groups[g]).astype(I32),
                                  (row > groups[g]).astype(I32))
            ranks[g] = ranks[g] + beats
    rank = jnp.concatenate(ranks, axis=0)
    sel_t = jnp.where(rank < n_sel, 1.0, 0.0)
    if ns_pad < LANES:
        sel_t = jnp.concatenate([sel_t, jnp.zeros((LANES - ns_pad, tq), F32)], axis=0)
    selb = sel_t.T.astype(BF16)

    _softmax_init(mx_scr, l_scr, acc_scr)
    col0 = lax.broadcasted_iota(I32, (rg, tk), 1)
    tg = qs + lax.broadcasted_iota(I32, (rg, 1), 0) % tq
    n_tiles = qs // tk + 1

    def scores(j, masked):
        ks0 = pl.multiple_of(j * tk, tk)
        mt = _dot(selb, e_ref[j])
        mg = jnp.concatenate([mt] * (rg // tq), axis=0)
        k_tile = ks_ref[0, pl.ds(ks0, tk), :]
        for g in range(NSA_GROUPS):
            s = _dot_nt(q4[g * rg:(g + 1) * rg], k_tile) * (scale * LOG2E)
            s = jnp.where(mg > 0.5, s, NEG)
            if masked:
                s = jnp.where(col0 + ks0 <= tg, s, NEG)
            _score_store(g, j, s, s_scr, mx_scr)

    _causal_tiles(scores, n_tiles - 1)
    _row_max_finish(mx_scr)

    def probs(j, carry):
        ks0 = pl.multiple_of(j * tk, tk)
        v_tile = vs_ref[0, pl.ds(ks0, tk), :]
        for g in range(NSA_GROUPS):
            _prob_accumulate(g, j, v_tile, s_scr, mx_scr, l_scr, acc_scr)
        return carry

    _paired_tiles(n_tiles, lambda j: probs(j, 0))
    o_slc = jnp.concatenate([_softmax_out(g, l_scr, acc_scr) for g in range(NSA_GROUPS)], axis=0)

    gates = small_ref[0]
    for h in range(HEADS):
        rows = slice(h * tq, (h + 1) * tq)
        g0 = gates[:, SMALL_G + 3 * h:SMALL_G + 3 * h + 1]
        g1 = gates[:, SMALL_G + 3 * h + 1:SMALL_G + 3 * h + 2]
        g2 = gates[:, SMALL_G + 3 * h + 2:SMALL_G + 3 * h + 3]
        o = g0 * cmp_scr[rows, :] + g1 * o_slc[rows] + g2 * win_scr[rows, :]
        o_ref[0, :, h * HEAD_W:(h + 1) * HEAD_W] = o.astype(BF16)


def _nsa(p3, kc, vc, small3, ov, emat, *, tq, tk):
    b, s, _ = p3.shape
    ng = kc.shape[1]
    col = lambda off: (lambda bi, i: (bi, 0, off // LANES))
    return pl.pallas_call(
        functools.partial(_nsa_kernel, tq=tq, tk=tk, seq=s),
        out_shape=jax.ShapeDtypeStruct((b, s, BR_WIDTH), BF16),
        grid=(b, s // tq),
        in_specs=[pl.BlockSpec((1, tq, 512), lambda bi, i: (bi, i, P_BQ // 512)),
                  pl.BlockSpec((1, ng, HEAD_W), lambda bi, i: (bi, 0, 0)),
                  pl.BlockSpec((1, ng, HEAD_W), lambda bi, i: (bi, 0, 0)),
                  pl.BlockSpec((1, s, LANES), col(P_KS)),
                  pl.BlockSpec((1, s, LANES), col(P_VS)),
                  pl.BlockSpec((1, s, LANES), col(P_KW)),
                  pl.BlockSpec((1, s, LANES), col(P_VW)),
                  pl.BlockSpec((1, tq, LANES), lambda bi, i: (bi, i, 0)),
                  pl.BlockSpec(ov.shape, lambda bi, i: (0, 0)),
                  pl.BlockSpec(emat.shape, lambda bi, i: (0, 0, 0))],
        out_specs=pl.BlockSpec((1, tq, BR_WIDTH), lambda bi, i: (bi, i, 0)),
        scratch_shapes=(_softmax_scratch(NSA_GROUPS, s // tk, HEADS * tq // NSA_GROUPS, tk)
                        + [pltpu.VMEM((HEADS * tq, HEAD_W), F32), pltpu.VMEM((HEADS * tq, HEAD_W), F32)]),
        compiler_params=_cparams(2),
        name="nsa_attn",
    )(p3, kc, vc, p3, p3, p3, p3, small3, ov, emat)


def _mla_attn_kernel(qn_ref, qr_ref, kn_ref, kr_ref, v_ref, o_ref, s_scr, mx_scr, l_scr, acc_scr,
                     *, tq, tk):
    qs = pl.program_id(1) * tq
    n_full = qs // tk
    scale = (MLA_NOPE + MLA_ROPE) ** -0.5 * LOG2E
    lane = lax.broadcasted_iota(I32, (tq, LANES), 1)
    row_t = qs + lax.broadcasted_iota(I32, (tq, 1), 0)
    col0 = lax.broadcasted_iota(I32, (tq, tk), 1)
    _softmax_init(mx_scr, l_scr, acc_scr)

    def scores(j, masked):
        ks = pl.multiple_of(j * tk, tk)
        kr_tile = kr_ref[0, pl.ds(ks, tk), :]
        for h in range(HEADS):
            hs = slice(h * HEAD_W, (h + 1) * HEAD_W)
            pair = qr_ref[0, :, (h // 2) * LANES:(h // 2 + 1) * LANES]
            keep = (lane < MLA_ROPE) if h % 2 == 0 else (lane >= MLA_ROPE)
            qr = jnp.where(keep, pair, jnp.zeros_like(pair))
            s = _dot_nt(jnp.concatenate([qn_ref[0, :, hs], qr], axis=1),
                        jnp.concatenate([kn_ref[0, pl.ds(ks, tk), hs], kr_tile], axis=1)) * scale
            if masked:
                s = jnp.where(col0 + ks <= row_t, s, NEG)
            _score_store(h, j, s, s_scr, mx_scr)

    _causal_tiles(scores, n_full)
    _row_max_finish(mx_scr)

    def probs(j, carry):
        ks = pl.multiple_of(j * tk, tk)
        for h in range(HEADS):
            v_tile = v_ref[0, pl.ds(ks, tk), h * HEAD_W:(h + 1) * HEAD_W]
            _prob_accumulate(h, j, v_tile, s_scr, mx_scr, l_scr, acc_scr)
        return carry

    _paired_tiles(n_full + 1, lambda j: probs(j, 0))
    for h in range(HEADS):
        o_ref[0, :, h * HEAD_W:(h + 1) * HEAD_W] = _softmax_out(h, l_scr, acc_scr).astype(BF16)


def _mla_attn(q3, kv3, p3, *, tq, tk):
    b, s, _ = q3.shape
    return pl.pallas_call(
        functools.partial(_mla_attn_kernel, tq=tq, tk=tk),
        out_shape=jax.ShapeDtypeStruct((b, s, BR_WIDTH), BF16),
        grid=(b, s // tq),
        in_specs=[pl.BlockSpec((1, tq, 512), lambda bi, i: (bi, i, 0)),
                  pl.BlockSpec((1, tq, 256), lambda bi, i: (bi, i, 2)),
                  pl.BlockSpec((1, s, 512), lambda bi, i: (bi, 0, 0)),
                  pl.BlockSpec((1, s, LANES), lambda bi, i: (bi, 0, P_KR // LANES)),
                  pl.BlockSpec((1, s, 512), lambda bi, i: (bi, 0, 1))],
        out_specs=pl.BlockSpec((1, tq, BR_WIDTH), lambda bi, i: (bi, i, 0)),
        scratch_shapes=_softmax_scratch(HEADS, s // tk, tq, tk),
        compiler_params=_cparams(2),
        name="mla_attn",
    )(q3, q3, kv3, p3, kv3)


def _sortable_key(x):
    bits = pltpu.bitcast(x + 0.0, I32)
    return bits ^ (lax.shift_right_arithmetic(bits, 31) & 0x7FFFFFFF)


def _fold_rows(x, op):
    n = x.shape[0] // SUBLANES
    return op(x.reshape(n, SUBLANES, x.shape[1]), axis=0)


def _count16(half_scr, n_tiles, pred, tq):
    def count_tile(j, cnt):
        hit = pred(half_scr[j]).astype(I16)
        parts = [hit[r:r + PACKED_ROWS] for r in range(0, hit.shape[0], PACKED_ROWS)]
        while len(parts) > 1:
            parts = [a + b for a, b in zip(parts[0::2], parts[1::2])]
        return cnt + parts[0]

    cnt = lax.fori_loop(0, n_tiles, count_tile, jnp.zeros((PACKED_ROWS, tq), I16))
    return jnp.sum(cnt.astype(I32), axis=0, keepdims=True)


def _bisect16(half_scr, n_tiles, need, tq):
    def bit_body(i, carry):
        th, tot = carry
        cand = th + lax.shift_left(jnp.int32(1), 15 - i)
        c16 = cand.astype(I16)
        total = _count16(half_scr, n_tiles, lambda x: x >= c16, tq)
        ok = total >= need
        return jnp.where(ok, cand, th), jnp.where(ok, total, tot)

    return lax.fori_loop(0, 16, bit_body, (jnp.full((1, tq), HALF_MIN, I32), jnp.full((1, tq), -1, I32)))


def _dsa_kernel(qt_ref, iqt_ref, iwt_ref, k_ref, ik_ref, vt_ref, tri_ref, o_ref,
                key_scr, half_scr, s_scr, mx_scr, l_scr, acc_scr, *, tq, tk, top):
    qs = pl.program_id(1) * tq
    n_tiles = (qs + tq - 1) // tk + 1
    scale = DSA_DIM ** -0.5 * LOG2E
    t_lane = qs + lax.broadcasted_iota(I32, (tk, tq), 1)
    krow0 = lax.broadcasted_iota(I32, (tk, tq), 0)
    half_rows = lax.broadcasted_iota(I32, (LANES, tq), 0) < IDX_DIM
    vt_per_tile = tk // VT_TILE

    def score_tile(j, masked):
        ks = pl.multiple_of(j * tk, tk)
        ikt = ik_ref[0, pl.ds(ks, tk), :]
        acc = jnp.zeros((tk, tq), F32)
        for h in range(IDX_HEADS):
            pair = iqt_ref[0, (h // 2) * LANES:(h // 2 + 1) * LANES, :]
            keep = half_rows if h % 2 == 0 else jnp.logical_not(half_rows)
            iq_h = jnp.where(keep, pair, jnp.zeros_like(pair))
            w_h = iwt_ref[0, SMALL_IW + h:SMALL_IW + h + 1, :]
            acc = acc + w_h * jnp.maximum(_dot(ikt, iq_h), 0.0)
        key = _sortable_key(acc)
        if masked:
            key = jnp.where(krow0 + ks <= t_lane, key, INT_MIN)
        key_scr[j] = key
        half_scr[j] = lax.shift_right_arithmetic(key, 16).astype(I16)

    _causal_tiles(score_tile, n_tiles - 1)

    th_hi, n_ge_hi = _bisect16(half_scr, n_tiles, top, tq)
    hi16 = th_hi.astype(I16)
    n_gt_hi = _count16(half_scr, n_tiles, lambda x: x > hi16, tq)
    need_lo = top - n_gt_hi

    def low_tile(j, carry):
        key = key_scr[j]
        lo = (key & 0xFFFF) + HALF_MIN
        same_hi = lax.shift_right_arithmetic(key, 16) == th_hi
        half_scr[j] = jnp.where(same_hi, lo, HALF_MIN).astype(I16)
        return carry

    lax.fori_loop(0, n_tiles, low_tile, 0)
    th_lo, n_ge_lo = _bisect16(half_scr, n_tiles, need_lo, tq)
    theta = lax.shift_left(th_hi, 16) + (th_lo - HALF_MIN)
    n_ge_lo = jnp.where(n_ge_lo >= 0, n_ge_lo, n_ge_hi - n_gt_hi)
    tied = (n_ge_lo > need_lo) & (n_ge_hi >= 0)

    @pl.when(jnp.max(tied.astype(I32)) > 0)
    def _():
        lo16 = th_lo.astype(I16)
        n_gt = n_gt_hi + _count16(half_scr, n_tiles, lambda x: x > lo16, tq)
        keep_eq = jnp.where(tied, top - n_gt, tk * key_scr.shape[0]).astype(F32)

        def demote(j, seen):
            k = key_scr[j]
            eq = k == theta
            eqf = jnp.where(eq, 1.0, 0.0)
            before = _dot(tri_ref[...], eqf.astype(BF16)) + seen
            key_scr[j] = jnp.where(jnp.where(eq, before, -1.0) >= keep_eq, k - 1, k)
            return seen + jnp.sum(_fold_rows(eqf, jnp.sum), axis=0, keepdims=True)

        lax.fori_loop(0, n_tiles, demote, jnp.zeros((1, tq), F32))

    theta = jnp.maximum(theta, INT_MIN + 1)

    mx_scr[...] = jnp.full(mx_scr.shape, NEG, F32)
    l_scr[...] = jnp.zeros(l_scr.shape, F32)
    acc_scr[...] = jnp.zeros(acc_scr.shape, F32)

    def scores(j, carry):
        ks = pl.multiple_of(j * tk, tk)
        sel = key_scr[j] >= theta
        for h in range(HEADS):
            hs = slice(h * HEAD_W, (h + 1) * HEAD_W)
            s = _dot(k_ref[0, pl.ds(ks, tk), hs], qt_ref[0, hs, :]) * scale
            s = jnp.where(sel, s, NEG)
            s_scr[h, j] = s
            mx_scr[h] = jnp.maximum(mx_scr[h], _fold_rows(s, jnp.max))
        return carry

    _paired_tiles(n_tiles, lambda j: scores(j, 0))
    for h in range(HEADS):
        m = jnp.max(mx_scr[h], axis=0, keepdims=True)
        mx_scr[h] = jnp.broadcast_to(m, (SUBLANES, tq))

    def probs(j, carry):
        for h in range(HEADS):
            p = jnp.exp2(s_scr[h, j] - mx_scr[h][0:1])
            l_scr[h] += _fold_rows(p, jnp.sum)
            pb = p.astype(BF16)
            for c in range(vt_per_tile):
                vt = vt_ref[0, j * vt_per_tile + c, h * HEAD_W:(h + 1) * HEAD_W, :]
                acc_scr[h] += _dot(vt, pb[c * VT_TILE:(c + 1) * VT_TILE])
        return carry

    _paired_tiles(n_tiles, lambda j: probs(j, 0))
    for h in range(HEADS):
        ot = acc_scr[h] / jnp.sum(l_scr[h], axis=0, keepdims=True)
        o_ref[0, :, h * HEAD_W:(h + 1) * HEAD_W] = ot.T.astype(BF16)


def _dsa(p3, t3, vt4, smallt, *, tq, tk, top):
    b, s, _ = p3.shape
    n_vt = vt4.shape[1]
    tri = jnp.asarray(np.tril(np.ones((tk, tk), np.float32), -1), BF16)
    return pl.pallas_call(
        functools.partial(_dsa_kernel, tq=tq, tk=tk, top=top),
        out_shape=jax.ShapeDtypeStruct((b, s, BR_WIDTH), BF16),
        grid=(b, s // tq),
        in_specs=[pl.BlockSpec((1, 512, tq), lambda bi, i: (bi, T_DQ // 512, i)),
                  pl.BlockSpec((1, 512, tq), lambda bi, i: (bi, T_IQ // 512, i)),
                  pl.BlockSpec((1, LANES, tq), lambda bi, i: (bi, 0, i)),
                  pl.BlockSpec((1, s, 512), lambda bi, i: (bi, 0, P_DK // 512)),
                  pl.BlockSpec((1, s, LANES), lambda bi, i: (bi, 0, P_IK // LANES)),
                  pl.BlockSpec((1, n_vt, BR_WIDTH, VT_TILE), lambda bi, i: (bi, 0, 0, 0)),
                  pl.BlockSpec((tk, tk), lambda bi, i: (0, 0))],
        out_specs=pl.BlockSpec((1, tq, BR_WIDTH), lambda bi, i: (bi, i, 0)),
        scratch_shapes=[pltpu.VMEM((s // tk, tk, tq), I32),
                        pltpu.VMEM((s // tk, tk, tq), I16),
                        pltpu.VMEM((HEADS, s // tk, tk, tq), F32),
                        pltpu.VMEM((HEADS, SUBLANES, tq), F32),
                        pltpu.VMEM((HEADS, SUBLANES, tq), F32),
                        pltpu.VMEM((HEADS, HEAD_W, tq), F32)],
        compiler_params=_cparams(2),
        name="dsa_attn",
    )(t3, t3, smallt, p3, p3, vt4, tri)


def _merge_kernel(x_ref, g_ref, oa_ref, ob_ref, oc_ref, od_ref, wg_ref, wb_ref, wo_ref, o_ref):
    x = x_ref[...]
    d = x.shape[1]
    ms = jnp.mean(x * x, axis=-1, keepdims=True)
    h = (x * lax.rsqrt(ms + EPS) * g_ref[...]).astype(BF16)
    acc = jnp.zeros(x.shape, F32)
    for n, br_ref in enumerate((oa_ref, ob_ref, oc_ref, od_ref)):
        gate = jax.nn.sigmoid(_dot(h, wg_ref[:, n * d:(n + 1) * d]))
        acc = acc + gate * _dot(br_ref[...], wb_ref[n])
    o_ref[...] = x + _dot(acc.astype(BF16), wo_ref[...])


def _merge(x2, g, oa, ob, oc, od, wg, wb, wo, *, tm):
    m, d = x2.shape
    row = lambda w: pl.BlockSpec((tm, w), lambda i: (i, 0))
    return pl.pallas_call(
        _merge_kernel,
        out_shape=jax.ShapeDtypeStruct((m, d), F32),
        grid=(m // tm,),
        in_specs=[row(d), pl.BlockSpec((1, d), lambda i: (0, 0)),
                  row(BR_WIDTH), row(BR_WIDTH), row(BR_WIDTH), row(BR_WIDTH),
                  pl.BlockSpec(wg.shape, lambda i: (0, 0)),
                  pl.BlockSpec(wb.shape, lambda i: (0, 0, 0)),
                  pl.BlockSpec(wo.shape, lambda i: (0, 0))],
        out_specs=row(d),
        compiler_params=_cparams(1),
        name="merge",
    )(x2, g, oa, ob, oc, od, wg, wb, wo)


def _ffn_kernel(x_ref, g_ref, wg_ref, wu_ref, wd_ref, gf_ref, o_ref, h_scr, acc_scr, *, final):
    j = pl.program_id(1)

    @pl.when(j == 0)
    def _():
        x = x_ref[...]
        ms = jnp.mean(x * x, axis=-1, keepdims=True)
        h_scr[...] = (x * lax.rsqrt(ms + EPS) * g_ref[...]).astype(BF16)
        acc_scr[...] = jnp.zeros(acc_scr.shape, F32)

    h = h_scr[...]
    a = jax.nn.silu(_dot(h, wg_ref[...])) * _dot(h, wu_ref[...])
    acc_scr[...] += _dot(a.astype(BF16), wd_ref[...])

    @pl.when(j == pl.num_programs(1) - 1)
    def _():
        y = x_ref[...] + acc_scr[...]
        if final:
            ms = jnp.mean(y * y, axis=-1, keepdims=True)
            y = y * lax.rsqrt(ms + EPS) * gf_ref[...]
        o_ref[...] = y


def _ffn(x2, g, wgu, wd, gf, *, final, tm, tf):
    m, d = x2.shape
    dff = wd.shape[0]
    n_f = dff // tf
    return pl.pallas_call(
        functools.partial(_ffn_kernel, final=final),
        out_shape=jax.ShapeDtypeStruct((m, d), F32),
        grid=(m // tm, n_f),
        in_specs=[pl.BlockSpec((tm, d), lambda i, j: (i, 0)),
                  pl.BlockSpec((1, d), lambda i, j: (0, 0)),
                  pl.BlockSpec((d, tf), lambda i, j: (0, j)),
                  pl.BlockSpec((d, tf), lambda i, j: (0, j + n_f)),
                  pl.BlockSpec((tf, d), lambda i, j: (j, 0)),
                  pl.BlockSpec((1, d), lambda i, j: (0, 0))],
        out_specs=pl.BlockSpec((tm, d), lambda i, j: (i, 0)),
        scratch_shapes=[pltpu.VMEM((tm, d), BF16), pltpu.VMEM((tm, d), F32)],
        compiler_params=_cparams(2),
        name="ffn",
    )(x2, g, wgu, wgu, wd, gf)


def _tiles(seq, m, dff):
    pick = lambda n, cands: next(c for c in cands if n % c == 0)
    tk = pick(seq, (512, 256))
    return dict(
        merge_tm=pick(m, (256, 128)),
        prep_ts=VT_TILE,
        diff_tq=128, mla_tq=pick(seq, (256, 128)), nsa_tq=256, dsa_tq=256, tk=tk,
        row_tm=pick(m, (512, 256, 128)),
        ffn_tf=pick(dff, (1408, 704, 256, 128)),
    )


def kernel(x, norm1_g, w_in, diff_lq1, diff_lk1, diff_lq2, diff_lk2, diff_subln_g, nsa_pe_k, nsa_w1_k, nsa_w2_k, nsa_pe_v, nsa_w1_v, nsa_w2_v, mla_q_norm_g, mla_w_uq, mla_kv_norm_g, mla_w_ukv, idx_k_norm_g, w_branch, w_out, norm2_g, w_gate_up, w_down, final_norm_g):
    b, seq, d = x.shape
    depth = w_in.shape[0]
    m = b * seq
    dff = w_down.shape[1]
    t = _tiles(seq, m, dff)
    tk = t["tk"]
    assert seq % SEL_LEN == 0 and seq >= WIN + t["nsa_tq"] and seq // SEL_LEN <= LANES
    assert seq % t["dsa_tq"] == 0 and tk % VT_TILE == 0 and tk >= min(IDX_TOPK, seq // 4)

    col_idx, gate_off, d_in = _in_proj_columns()
    assert w_in.shape[2] == d_in
    tab = jnp.concatenate([_rope_table(seq, rot, per) for rot, per in ROPE_KINDS], axis=1)

    ng = seq // CMP_STRIDE
    ns = seq // SEL_LEN
    c_start = np.arange(ng)[:, None] * CMP_STRIDE
    s_start = np.arange(LANES)[None, :] * SEL_LEN
    ov = ((c_start < s_start + SEL_LEN) & (c_start + CMP_LEN - 1 >= s_start)
          & (np.arange(LANES)[None, :] < ns))
    ov = jnp.asarray(ov, BF16)
    emat = np.arange(LANES)[:, None] == (np.arange(seq)[None, :] // SEL_LEN)
    emat = jnp.asarray(emat.reshape(LANES, seq // tk, tk).transpose(1, 0, 2), BF16)

    qd = MLA_NOPE + MLA_ROPE
    uq_idx = np.concatenate([np.concatenate([np.arange(h * qd, h * qd + MLA_NOPE) for h in range(HEADS)]),
                             np.concatenate([np.arange(h * qd + MLA_NOPE, (h + 1) * qd) for h in range(HEADS)])])
    kvd = MLA_NOPE + HEAD_W
    ukv_idx = np.concatenate([np.concatenate([np.arange(h * kvd, h * kvd + MLA_NOPE) for h in range(HEADS)]),
                              np.concatenate([np.arange(h * kvd + MLA_NOPE, (h + 1) * kvd) for h in range(HEADS)])])

    x2 = x.reshape(m, d)
    half_w1 = CMP_STRIDE * NSA_DK
    for l in range(depth):
        lam_init = 0.8 - 0.6 * math.exp(-0.3 * l)
        w_in_l = w_in[l].astype(BF16)
        w_mix = _take_cols(w_in_l, col_idx)
        w_gate = w_in_l[:, gate_off:]

        gq = jnp.pad(mla_q_norm_g[l], (0, 512 - MLA_Q_LORA))[None]
        gkv = mla_kv_norm_g[l][None]
        gik = jnp.concatenate([idx_k_norm_g[l], idx_k_norm_g[l]])[None]
        wq = jnp.pad(_take_cols(mla_w_uq[l].astype(BF16), uq_idx), ((0, 512 - MLA_Q_LORA), (0, 0)))
        wkv = _take_cols(mla_w_ukv[l].astype(BF16), ukv_idx)
        p2, t3, vt4, kc_tok, vc_tok, small, smallt, q_c, kv_c = _proj_prep(
            x2, norm1_g[l][None], w_mix, tab, gq, gkv, gik, wq, wkv, batch=b, seq=seq, ts=t["prep_ts"])
        p3 = p2.reshape(b, seq, P_WIDTH)
        small3 = small.reshape(b, seq, LANES)

        lv = jnp.stack([diff_lq1[l], diff_lk1[l], diff_lq2[l], diff_lk2[l]])
        o_a = _diff_attn(p3, lv, diff_subln_g[l][None], lam_init=lam_init, tq=t["diff_tq"], tk=tk)

        w1k, w1v = nsa_w1_k[l].astype(BF16), nsa_w1_v[l].astype(BF16)
        w1k_cat = jnp.concatenate([w1k[:half_w1], w1k[half_w1:]], axis=1)
        w1v_cat = jnp.concatenate([w1v[:half_w1], w1v[half_w1:]], axis=1)
        pek = jnp.broadcast_to(nsa_pe_k[l].reshape(1, -1), (8, CMP_LEN * NSA_DK)).astype(BF16)
        pev = jnp.broadcast_to(nsa_pe_v[l].reshape(1, -1), (8, CMP_LEN * NSA_DK)).astype(BF16)
        kc, vc = _nsa_compress(kc_tok.reshape(b, ng, half_w1), vc_tok.reshape(b, ng, half_w1),
                               w1k_cat, w1v_cat, pek, pev, w1k, w1v,
                               nsa_w2_k[l].astype(BF16), nsa_w2_v[l].astype(BF16))
        o_b = _nsa(p3, kc, vc, small3, ov, emat, tq=t["nsa_tq"], tk=tk)

        o_c = _mla_attn(q_c.reshape(b, seq, -1), kv_c.reshape(b, seq, -1), p3, tq=t["mla_tq"], tk=tk)

        o_d = _dsa(p3, t3, vt4, smallt, tq=t["dsa_tq"], tk=tk, top=min(IDX_TOPK, seq // 4))

        x2 = _merge(x2, norm1_g[l][None],
                    o_a.reshape(m, -1), o_b.reshape(m, -1), o_c.reshape(m, -1), o_d.reshape(m, -1),
                    w_gate, w_branch[l].astype(BF16), w_out[l].astype(BF16), tm=t["merge_tm"])
        x2 = _ffn(x2, norm2_g[l][None], w_gate_up[l].astype(BF16), w_down[l].astype(BF16),
                  final_norm_g[None], final=(l == depth - 1), tm=t["row_tm"], tf=t["ffn_tf"])
    return x2.reshape(b, seq, d)
```

```python
import functools
import math

import numpy as np
import jax
import jax.numpy as jnp
from jax import lax
from jax.experimental import pallas as pl
from jax.experimental.pallas import tpu as pltpu

F32 = jnp.float32
BF16 = jnp.bfloat16
I32 = jnp.int32
I16 = jnp.int16

LANES = 128
SUBLANES = 8
PACKED_ROWS = 16
HALF_MIN = -32768
VMEM_LIMIT = 56 * 1024 * 1024

ROPE_THETA = 500000.0
NEG = -1e30
LOG2E = math.log2(math.e)
FORCE_SCORE = 1e9
PAD_SCORE = -3e38
EPS = 1e-6
INT_MIN = -2147483648

HEADS = 4
HEAD_W = 128
BR_WIDTH = HEADS * HEAD_W
DA_DIM = 64
NSA_DK = 128
CMP_LEN = 32
CMP_STRIDE = 16
SEL_LEN = 64
SEL_N = 16
WIN = 512
MLA_Q_LORA = 384
MLA_KV_LORA = 256
MLA_NOPE = 128
MLA_ROPE = 64
DSA_DIM = 128
IDX_HEADS = 8
IDX_DIM = 64
IDX_TOPK = 256

Z_AQ, Z_AK, Z_AV, Z_BQ, Z_DQ, Z_DK, Z_DV, Z_IQ = (i * 512 for i in range(8))
Z_CQ = 4096
Z_CKV = 4608
Z_KC, Z_KS, Z_KW, Z_VC, Z_VS, Z_VW, Z_KR, Z_IK, Z_SMALL = (4864 + i * 128 for i in range(9))
Z_WIDTH = 6144
P_AQ, P_AK, P_AV, P_BQ, P_DK = (i * 512 for i in range(5))
P_KS, P_KW, P_VS, P_VW, P_KR, P_IK = (2560 + i * 128 for i in range(6))
P_WIDTH = 3328
T_DQ, T_IQ = 0, 512
T_ROWS = 1024
VT_TILE = 256
SMALL_G = 0
SMALL_IW = 12

ROPE_KINDS = ((16, 64), (32, 128), (64, 64))
TAB_W = 3 * LANES


def _cparams(n_axes):
    return pltpu.CompilerParams(dimension_semantics=("arbitrary",) * n_axes,
                                vmem_limit_bytes=VMEM_LIMIT)


def _dot(a, b):
    return jnp.dot(a, b, preferred_element_type=F32)


def _dot_nt(a, b):
    return lax.dot_general(a, b, (((1,), (1,)), ((), ())), preferred_element_type=F32)


def _in_proj_columns():
    names = (("a_q", 512), ("a_k", 512), ("a_v", 512), ("b_q", 512),
             ("b_kc", 128), ("b_vc", 128), ("b_ks", 128), ("b_vs", 128),
             ("b_kw", 128), ("b_vw", 128), ("b_g", 12),
             ("c_q", 384), ("c_kv", 256), ("c_kr", 64),
             ("d_q", 512), ("d_k", 512), ("d_v", 512),
             ("d_iq", 512), ("d_ik", 64), ("d_iw", 8), ("gate", 4096))
    off, o = {}, 0
    for nm, n in names:
        off[nm] = (o, n)
        o += n
    idx = np.full((Z_WIDTH,), -1, np.int64)

    def put(dst, nm):
        s, n = off[nm]
        idx[dst:dst + n] = np.arange(s, s + n)

    put(Z_AQ, "a_q"); put(Z_AK, "a_k"); put(Z_AV, "a_v"); put(Z_BQ, "b_q")
    put(Z_DQ, "d_q"); put(Z_DK, "d_k"); put(Z_DV, "d_v"); put(Z_IQ, "d_iq")
    put(Z_CQ, "c_q"); put(Z_CKV, "c_kv")
    put(Z_KC, "b_kc"); put(Z_KS, "b_ks"); put(Z_KW, "b_kw")
    put(Z_VC, "b_vc"); put(Z_VS, "b_vs"); put(Z_VW, "b_vw")
    put(Z_KR, "c_kr"); put(Z_KR + 64, "c_kr")
    put(Z_IK, "d_ik"); put(Z_IK + 64, "d_ik")
    put(Z_SMALL + SMALL_G, "b_g"); put(Z_SMALL + SMALL_IW, "d_iw")
    return idx, off["gate"][0], o


def _take_cols(w, idx):
    runs, i, n = [], 0, len(idx)
    while i < n:
        j = i + 1
        if idx[i] < 0:
            while j < n and idx[j] < 0:
                j += 1
            runs.append(jnp.zeros(w.shape[:-1] + (j - i,), w.dtype))
        else:
            while j < n and idx[j] == idx[j - 1] + 1:
                j += 1
            runs.append(w[..., int(idx[i]):int(idx[i]) + (j - i)])
        i = j
    return jnp.concatenate(runs, axis=-1)


def _rope_table(seq, rot, period):
    half = rot // 2
    inv = jnp.power(jnp.float32(ROPE_THETA), -jnp.arange(0, rot, 2, dtype=F32) / rot)
    ang = jnp.arange(seq, dtype=F32)[:, None] * inv[None, :]
    cos, sin = jnp.cos(ang), jnp.sin(ang)
    lane = np.arange(LANES) % period
    in1 = lane < half
    in2 = (lane >= half) & (lane < 2 * half)
    fidx = np.where(in1, lane, np.where(in2, lane - half, 0))
    cosl, sinl = cos[:, fidx], sin[:, fidx]
    c = jnp.where(jnp.asarray(in1 | in2)[None], cosl, 1.0)
    s1 = jnp.where(jnp.asarray(in1)[None], -sinl, 0.0)
    s2 = jnp.where(jnp.asarray(in2)[None], sinl, 0.0)
    return jnp.concatenate([c, s1, s2], axis=1)


def _rope128(x, tab, half):
    return (x * tab[:, 0:LANES]
            + pltpu.roll(x, LANES - half, 1) * tab[:, LANES:2 * LANES]
            + pltpu.roll(x, half, 1) * tab[:, 2 * LANES:3 * LANES])


PROJ_TILE = 512


def _proj_prep_kernel(x_ref, g_ref, w_ref, tab_ref, gq_ref, gkv_ref, gik_ref, wq_ref, wkv_ref,
                      p_ref, t_ref, vt_ref, kc_ref, vc_ref, small_ref, smallt_ref, q_ref, kv_ref):
    x = x_ref[...]
    ms = jnp.mean(x * x, axis=-1, keepdims=True)
    h = (x * lax.rsqrt(ms + EPS) * g_ref[...]).astype(BF16)
    z_tiles = {}

    def z_cols(off, width):
        t = off // PROJ_TILE
        assert (off + width - 1) // PROJ_TILE == t
        if t not in z_tiles:
            z_tiles[t] = _dot(h, w_ref[:, t * PROJ_TILE:(t + 1) * PROJ_TILE])
        lo = off - t * PROJ_TILE
        return z_tiles[t][:, lo:lo + width]

    def zc(off, c=0):
        return z_cols(off + c * LANES, LANES)

    def tab(kind):
        return tab_ref[:, kind * TAB_W:(kind + 1) * TAB_W]

    def put(off, c, v):
        p_ref[:, off + c * LANES:off + (c + 1) * LANES] = v.astype(BF16)

    def rope(off, c, kind):
        return _rope128(zc(off, c), tab(kind), ROPE_KINDS[kind][0] // 2)

    for zoff, poff, kind in ((Z_AQ, P_AQ, 0), (Z_AK, P_AK, 0), (Z_BQ, P_BQ, 1), (Z_DK, P_DK, 1)):
        for c in range(4):
            put(poff, c, rope(zoff, c, kind))
    for c in range(4):
        put(P_AV, c, zc(Z_AV, c))
    put(P_VS, 0, zc(Z_VS)); put(P_VW, 0, zc(Z_VW))
    put(P_KS, 0, rope(Z_KS, 0, 1)); put(P_KW, 0, rope(Z_KW, 0, 1))
    put(P_KR, 0, rope(Z_KR, 0, 2))
    kc_ref[...] = rope(Z_KC, 0, 1).astype(BF16)
    vc_ref[...] = zc(Z_VC).astype(BF16)

    for zoff, toff, kind in ((Z_DQ, T_DQ, 1), (Z_IQ, T_IQ, 0)):
        for c in range(4):
            t_ref[0, toff + c * LANES:toff + (c + 1) * LANES, :] = rope(zoff, c, kind).T.astype(BF16)
    for c in range(4):
        vt_ref[0, 0, c * LANES:(c + 1) * LANES, :] = zc(Z_DV, c).T.astype(BF16)

    cq = z_cols(Z_CQ, 512)
    ms = jnp.sum(cq * cq, axis=-1, keepdims=True) * (1.0 / MLA_Q_LORA)
    q = _dot((cq * lax.rsqrt(ms + EPS) * gq_ref[...]).astype(BF16), wq_ref[...])
    nn = HEADS * MLA_NOPE
    q_ref[:, :nn] = q[:, :nn].astype(BF16)
    for c in range(nn // LANES, (nn + HEADS * MLA_ROPE) // LANES):
        tile = _rope128(q[:, c * LANES:(c + 1) * LANES], tab(2), MLA_ROPE // 2)
        q_ref[:, c * LANES:(c + 1) * LANES] = tile.astype(BF16)
    ckv = z_cols(Z_CKV, MLA_KV_LORA)
    ms = jnp.mean(ckv * ckv, axis=-1, keepdims=True)
    kv_ref[...] = _dot((ckv * lax.rsqrt(ms + EPS) * gkv_ref[...]).astype(BF16), wkv_ref[...]).astype(BF16)

    ik = zc(Z_IK)
    ms = jnp.mean(ik * ik, axis=-1, keepdims=True)
    ikn = ik * lax.rsqrt(ms + EPS) * gik_ref[...]
    put(P_IK, 0, _rope128(ikn, tab(0), ROPE_KINDS[0][0] // 2))

    sm = zc(Z_SMALL)
    lane = lax.broadcasted_iota(I32, sm.shape, 1)
    iw_scale = IDX_HEADS ** -0.5 * IDX_DIM ** -0.5
    small = jnp.where(lane < SMALL_IW, jax.nn.sigmoid(sm), sm * iw_scale)
    small_ref[...] = small
    smallt_ref[0] = small.T


def _proj_prep(x2, g, w, tab, gq, gkv, gik, wq, wkv, *, layer, batch, seq, ts):
    m, d = x2.shape
    spb = seq // ts
    assert ts == VT_TILE and w.shape[1:] == (d, Z_WIDTH)
    row = lambda w: pl.BlockSpec((ts, w), lambda i: (i, 0))
    const = lambda a: pl.BlockSpec(a.shape, lambda i: (0,) * a.ndim)
    return pl.pallas_call(
        _proj_prep_kernel,
        out_shape=(jax.ShapeDtypeStruct((m, P_WIDTH), BF16),
                   jax.ShapeDtypeStruct((batch, T_ROWS, seq), BF16),
                   jax.ShapeDtypeStruct((batch, spb, BR_WIDTH, VT_TILE), BF16),
                   jax.ShapeDtypeStruct((m, LANES), BF16),
                   jax.ShapeDtypeStruct((m, LANES), BF16),
                   jax.ShapeDtypeStruct((m, LANES), F32),
                   jax.ShapeDtypeStruct((batch, LANES, seq), F32),
                   jax.ShapeDtypeStruct((m, wq.shape[1]), BF16),
                   jax.ShapeDtypeStruct((m, wkv.shape[1]), BF16)),
        grid=(m // ts,),
        in_specs=[row(d),
                  pl.BlockSpec((1, d), lambda i: (0, 0)),
                  pl.BlockSpec((None, d, Z_WIDTH), lambda i: (layer, 0, 0)),
                  pl.BlockSpec((ts, 3 * TAB_W), lambda i: (i % spb, 0)),
                  pl.BlockSpec((1, 512), lambda i: (0, 0)),
                  pl.BlockSpec((1, MLA_KV_LORA), lambda i: (0, 0)),
                  pl.BlockSpec((1, LANES), lambda i: (0, 0)),
                  const(wq), const(wkv)],
        out_specs=(row(P_WIDTH),
                   pl.BlockSpec((1, T_ROWS, ts), lambda i: (i // spb, 0, i % spb)),
                   pl.BlockSpec((1, 1, BR_WIDTH, VT_TILE), lambda i: (i // spb, i % spb, 0, 0)),
                   row(LANES), row(LANES), row(LANES),
                   pl.BlockSpec((1, LANES, ts), lambda i: (i // spb, 0, i % spb)),
                   row(wq.shape[1]), row(wkv.shape[1])),
        compiler_params=_cparams(1),
        name="proj_prep",
    )(x2, g, w, tab, gq, gkv, gik, wq, wkv)


def _softmax_init(mx_scr, l_scr, acc_scr):
    mx_scr[...] = jnp.full(mx_scr.shape, NEG, F32)
    l_scr[...] = jnp.zeros(l_scr.shape, F32)
    acc_scr[...] = jnp.zeros(acc_scr.shape, F32)


def _score_store(g, j, s, s_scr, mx_scr):
    s_scr[g, j] = s
    m = s[:, 0:LANES]
    for c in range(1, s.shape[1] // LANES):
        m = jnp.maximum(m, s[:, c * LANES:(c + 1) * LANES])
    mx_scr[g] = jnp.maximum(mx_scr[g], m)


def _row_max_finish(mx_scr):
    for g in range(mx_scr.shape[0]):
        m = jnp.max(mx_scr[g], axis=-1, keepdims=True)
        mx_scr[g] = jnp.broadcast_to(m, mx_scr.shape[1:])


def _prob_accumulate(g, j, v_tile, s_scr, mx_scr, l_scr, acc_scr):
    mb = mx_scr[g]
    s = s_scr[g, j]
    ps = [jnp.exp2(s[:, c * LANES:(c + 1) * LANES] - mb) for c in range(s.shape[1] // LANES)]
    tot = ps[0]
    for p in ps[1:]:
        tot = tot + p
    l_scr[g] += tot
    acc_scr[g] += _dot(jnp.concatenate(ps, axis=1).astype(BF16), v_tile)


def _softmax_out(g, l_scr, acc_scr):
    return acc_scr[g] / jnp.sum(l_scr[g], axis=-1, keepdims=True)


def _paired_tiles(n, step):
    def pair(jj, carry):
        step(2 * jj)
        step(2 * jj + 1)
        return carry

    lax.fori_loop(0, n // 2, pair, 0)

    @pl.when(n % 2 == 1)
    def _():
        step(n - 1)


def _causal_tiles(step, n_full):
    _paired_tiles(n_full, lambda j: step(j, False))
    step(n_full, True)


def _softmax_scratch(groups, n_tiles, rows, tk):
    return [pltpu.VMEM((groups, n_tiles, rows, tk), F32),
            pltpu.VMEM((groups, rows, LANES), F32),
            pltpu.VMEM((groups, rows, LANES), F32),
            pltpu.VMEM((groups, rows, HEAD_W), F32)]


def _diff_attn_kernel(q_ref, k_ref, v_ref, lv_ref, g_ref, o_ref, s_scr, mx_scr, l_scr, acc_scr,
                      *, tq, tk, lam_init):
    qs = pl.program_id(1) * tq
    n_full = qs // tk
    scale = DA_DIM ** -0.5 * LOG2E
    lv = lv_ref[...]
    lam = (jnp.exp(jnp.sum(lv[0:1] * lv[1:2], axis=-1, keepdims=True))
           - jnp.exp(jnp.sum(lv[2:3] * lv[3:4], axis=-1, keepdims=True)) + lam_init)
    lane = lax.broadcasted_iota(I32, (tq, HEAD_W), 1)
    row_t = qs + lax.broadcasted_iota(I32, (2 * tq, 1), 0) % tq
    col0 = lax.broadcasted_iota(I32, (2 * tq, tk), 1)
    _softmax_init(mx_scr, l_scr, acc_scr)

    def scores(j, masked):
        ks = pl.multiple_of(j * tk, tk)
        for h in range(HEADS):
            hs = slice(h * HEAD_W, (h + 1) * HEAD_W)
            qh = q_ref[0, :, hs]
            zero = jnp.zeros_like(qh)
            q2 = jnp.concatenate([jnp.where(lane < DA_DIM, qh, zero),
                                  jnp.where(lane >= DA_DIM, qh, zero)], axis=0)
            s = _dot_nt(q2, k_ref[0, pl.ds(ks, tk), hs]) * scale
            if masked:
                s = jnp.where(col0 + ks <= row_t, s, NEG)
            _score_store(h, j, s, s_scr, mx_scr)

    _causal_tiles(scores, n_full)
    _row_max_finish(mx_scr)

    def probs(j, carry):
        ks = pl.multiple_of(j * tk, tk)
        for h in range(HEADS):
            v_tile = v_ref[0, pl.ds(ks, tk), h * HEAD_W:(h + 1) * HEAD_W]
            _prob_accumulate(h, j, v_tile, s_scr, mx_scr, l_scr, acc_scr)
        return carry

    _paired_tiles(n_full + 1, lambda j: probs(j, 0))
    for h in range(HEADS):
        o2 = _softmax_out(h, l_scr, acc_scr)
        o = o2[:tq] - lam * o2[tq:]
        ms = jnp.mean(o * o, axis=-1, keepdims=True)
        o = o * lax.rsqrt(ms + EPS) * g_ref[...]
        o_ref[0, :, h * HEAD_W:(h + 1) * HEAD_W] = (o * (1.0 - lam_init)).astype(BF16)


def _diff_attn(p3, lv, g, *, lam_init, tq, tk):
    b, s, _ = p3.shape
    return pl.pallas_call(
        functools.partial(_diff_attn_kernel, tq=tq, tk=tk, lam_init=lam_init),
        out_shape=jax.ShapeDtypeStruct((b, s, BR_WIDTH), BF16),
        grid=(b, s // tq),
        in_specs=[pl.BlockSpec((1, tq, 512), lambda bi, i: (bi, i, P_AQ // 512)),
                  pl.BlockSpec((1, s, 512), lambda bi, i: (bi, 0, P_AK // 512)),
                  pl.BlockSpec((1, s, 512), lambda bi, i: (bi, 0, P_AV // 512)),
                  pl.BlockSpec((4, DA_DIM), lambda bi, i: (0, 0)),
                  pl.BlockSpec((1, HEAD_W), lambda bi, i: (0, 0))],
        out_specs=pl.BlockSpec((1, tq, BR_WIDTH), lambda bi, i: (bi, i, 0)),
        scratch_shapes=_softmax_scratch(HEADS, s // tk, 2 * tq, tk),
        compiler_params=_cparams(2),
        name="diff_attn",
    )(p3, p3, p3, lv, g)


def _nsa_compress_kernel(gk_ref, gv_ref, w1k_ref, w1v_ref, pek_ref, pev_ref,
                         w1kf_ref, w1vf_ref, w2k_ref, w2v_ref, kc_ref, vc_ref):
    def one(g_ref, w1cat_ref, pe_ref, w1f_ref, w2_ref, o_ref):
        y = _dot(g_ref[0], w1cat_ref[...])
        n = y.shape[0]
        nxt = pltpu.roll(y[:, HEAD_W:], n - 1, 0)
        c = _dot(pe_ref[...], w1f_ref[...])[0:1]
        hid = jax.nn.gelu(y[:, :HEAD_W] + nxt + c)
        o_ref[0] = _dot(hid.astype(BF16), w2_ref[...]).astype(BF16)

    one(gk_ref, w1k_ref, pek_ref, w1kf_ref, w2k_ref, kc_ref)
    one(gv_ref, w1v_ref, pev_ref, w1vf_ref, w2v_ref, vc_ref)


def _nsa_compress(gk, gv, w1k_cat, w1v_cat, pek, pev, w1k, w1v, w2k, w2v):
    b, ng, gw = gk.shape
    full = lambda shape: pl.BlockSpec(shape, lambda bi: (0,) * len(shape))
    return pl.pallas_call(
        _nsa_compress_kernel,
        out_shape=(jax.ShapeDtypeStruct((b, ng, HEAD_W), BF16),
                   jax.ShapeDtypeStruct((b, ng, HEAD_W), BF16)),
        grid=(b,),
        in_specs=[pl.BlockSpec((1, ng, gw), lambda bi: (bi, 0, 0)),
                  pl.BlockSpec((1, ng, gw), lambda bi: (bi, 0, 0)),
                  full(w1k_cat.shape), full(w1v_cat.shape), full(pek.shape), full(pev.shape),
                  full(w1k.shape), full(w1v.shape), full(w2k.shape), full(w2v.shape)],
        out_specs=(pl.BlockSpec((1, ng, HEAD_W), lambda bi: (bi, 0, 0)),
                   pl.BlockSpec((1, ng, HEAD_W), lambda bi: (bi, 0, 0))),
        compiler_params=_cparams(1),
        name="nsa_compress",
    )(gk, gv, w1k_cat, w1v_cat, pek, pev, w1k, w1v, w2k, w2v)


NSA_GROUPS = 2


def _nsa_kernel(q_ref, kc_ref, vc_ref, ks_ref, vs_ref, kw_ref, vw_ref, small_ref, ov_ref, e_ref,
                o_ref, s_scr, mx_scr, l_scr, acc_scr, cmp_scr, win_scr, *, tq, tk, seq):
    qs = pl.program_id(1) * tq
    scale = NSA_DK ** -0.5
    ns = seq // SEL_LEN
    n_sel = min(SEL_N, ns)
    r = HEADS * tq
    rg = r // NSA_GROUPS
    q4 = jnp.concatenate([q_ref[0, :, h * HEAD_W:(h + 1) * HEAD_W] for h in range(HEADS)], axis=0)
    t1 = qs + lax.broadcasted_iota(I32, (tq, 1), 0)
    t4 = qs + lax.broadcasted_iota(I32, (r, 1), 0) % tq

    wspan = WIN + tq
    start = pl.multiple_of(jnp.maximum(qs - WIN, 0), tq)
    sw = _dot_nt(q4, kw_ref[0, pl.ds(start, wspan), :]) * (scale * LOG2E)
    dist = (t4 - start) - lax.broadcasted_iota(I32, (r, wspan), 1)
    sw = jnp.where(pltpu.bitcast(dist, jnp.uint32) < jnp.uint32(WIN), sw, NEG)
    e = jnp.exp2(sw - jnp.max(sw, axis=-1, keepdims=True))
    win_scr[...] = (_dot(e.astype(BF16), vw_ref[0, pl.ds(start, wspan), :])
                    / jnp.sum(e, axis=-1, keepdims=True))

    kc = kc_ref[0]
    nc_pad = kc.shape[0]
    sc = _dot_nt(q4, kc) * (scale * LOG2E)
    c_end = lax.broadcasted_iota(I32, (r, nc_pad), 1) * CMP_STRIDE + (CMP_LEN - 1)
    cmask = c_end <= t4
    mx = jnp.max(jnp.where(cmask, sc, NEG), axis=-1, keepdims=True)
    e = jnp.where(cmask, jnp.exp2(sc - mx), 0.0)
    den = jnp.sum(e, axis=-1, keepdims=True)
    pc = e / jnp.where(den > 0.0, den, 1.0)
    cmp_scr[...] = _dot(pc.astype(BF16), vc_ref[0])

    psum = pc[0:tq] + pc[tq:2 * tq] + pc[2 * tq:3 * tq] + pc[3 * tq:4 * tq]
    ov = ov_ref[...]
    hi = psum.astype(BF16)
    r1 = psum - hi.astype(F32)
    mid = r1.astype(BF16)
    lo = (r1 - mid.astype(F32)).astype(BF16)
    imp = _dot(hi, ov) + _dot(mid, ov) + _dot(lo, ov)

    blk = lax.broadcasted_iota(I32, (tq, LANES), 1)
    cur = t1 // SEL_LEN
    forced = (blk == 0) | (blk == cur) | (blk == cur - 1)
    visible = blk * SEL_LEN <= t1
    score = jnp.where(visible, jnp.where(forced, FORCE_SCORE, imp), NEG)
    score = jnp.where(blk < ns, score, PAD_SCORE)
    ns_pad = -(-ns // SUBLANES) * SUBLANES
    score_t = score.T[:ns_pad]
    n_grp = ns_pad // SUBLANES
    groups = [score_t[g * SUBLANES:(g + 1) * SUBLANES] for g in range(n_grp)]
    sub = lax.broadcasted_iota(I32, (SUBLANES, tq), 0)
    ranks = [jnp.zeros((SUBLANES, tq), I32) for _ in range(n_grp)]
    for jp in range(ns):
        row = score_t[jp:jp + 1, :]
        for g in range(n_grp):
            if g * SUBLANES > jp:
                beats = (row >= groups[g]).astype(I32)
            elif g * SUBLANES + SUBLANES - 1 <= jp:
                beats = (row > groups[g]).astype(I32)
            else:
                beats = jnp.where(sub > (jp - g * SUBLANES), (row >= groups[g]).astype(I32),
                                  (row > groups[g]).astype(I32))
            ranks[g] = ranks[g] + beats
    rank = jnp.concatenate(ranks, axis=0)
    sel_t = jnp.where(rank < n_sel, 1.0, 0.0)
    if ns_pad < LANES:
        sel_t = jnp.concatenate([sel_t, jnp.zeros((LANES - ns_pad, tq), F32)], axis=0)
    selb = sel_t.T.astype(BF16)

    _softmax_init(mx_scr, l_scr, acc_scr)
    col0 = lax.broadcasted_iota(I32, (rg, tk), 1)
    tg = qs + lax.broadcasted_iota(I32, (rg, 1), 0) % tq
    n_tiles = qs // tk + 1

    def scores(j, masked):
        ks0 = pl.multiple_of(j * tk, tk)
        mt = _dot(selb, e_ref[j])
        mg = jnp.concatenate([mt] * (rg // tq), axis=0)
        k_tile = ks_ref[0, pl.ds(ks0, tk), :]
        for g in range(NSA_GROUPS):
            s = _dot_nt(q4[g * rg:(g + 1) * rg], k_tile) * (scale * LOG2E)
            s = jnp.where(mg > 0.5, s, NEG)
            if masked:
                s = jnp.where(col0 + ks0 <= tg, s, NEG)
            _score_store(g, j, s, s_scr, mx_scr)

    _causal_tiles(scores, n_tiles - 1)
    _row_max_finish(mx_scr)

    def probs(j, carry):
        ks0 = pl.multiple_of(j * tk, tk)
        v_tile = vs_ref[0, pl.ds(ks0, tk), :]
        for g in range(NSA_GROUPS):
            _prob_accumulate(g, j, v_tile, s_scr, mx_scr, l_scr, acc_scr)
        return carry

    _paired_tiles(n_tiles, lambda j: probs(j, 0))
    o_slc = jnp.concatenate([_softmax_out(g, l_scr, acc_scr) for g in range(NSA_GROUPS)], axis=0)

    gates = small_ref[0]
    for h in range(HEADS):
        rows = slice(h * tq, (h + 1) * tq)
        g0 = gates[:, SMALL_G + 3 * h:SMALL_G + 3 * h + 1]
        g1 = gates[:, SMALL_G + 3 * h + 1:SMALL_G + 3 * h + 2]
        g2 = gates[:, SMALL_G + 3 * h + 2:SMALL_G + 3 * h + 3]
        o = g0 * cmp_scr[rows, :] + g1 * o_slc[rows] + g2 * win_scr[rows, :]
        o_ref[0, :, h * HEAD_W:(h + 1) * HEAD_W] = o.astype(BF16)


def _nsa(p3, kc, vc, small3, ov, emat, *, tq, tk):
    b, s, _ = p3.shape
    ng = kc.shape[1]
    col = lambda off: (lambda bi, i: (bi, 0, off // LANES))
    return pl.pallas_call(
        functools.partial(_nsa_kernel, tq=tq, tk=tk, seq=s),
        out_shape=jax.ShapeDtypeStruct((b, s, BR_WIDTH), BF16),
        grid=(b, s // tq),
        in_specs=[pl.BlockSpec((1, tq, 512), lambda bi, i: (bi, i, P_BQ // 512)),
                  pl.BlockSpec((1, ng, HEAD_W), lambda bi, i: (bi, 0, 0)),
                  pl.BlockSpec((1, ng, HEAD_W), lambda bi, i: (bi, 0, 0)),
                  pl.BlockSpec((1, s, LANES), col(P_KS)),
                  pl.BlockSpec((1, s, LANES), col(P_VS)),
                  pl.BlockSpec((1, s, LANES), col(P_KW)),
                  pl.BlockSpec((1, s, LANES), col(P_VW)),
                  pl.BlockSpec((1, tq, LANES), lambda bi, i: (bi, i, 0)),
                  pl.BlockSpec(ov.shape, lambda bi, i: (0, 0)),
                  pl.BlockSpec(emat.shape, lambda bi, i: (0, 0, 0))],
        out_specs=pl.BlockSpec((1, tq, BR_WIDTH), lambda bi, i: (bi, i, 0)),
        scratch_shapes=(_softmax_scratch(NSA_GROUPS, s // tk, HEADS * tq // NSA_GROUPS, tk)
                        + [pltpu.VMEM((HEADS * tq, HEAD_W), F32), pltpu.VMEM((HEADS * tq, HEAD_W), F32)]),
        compiler_params=_cparams(2),
        name="nsa_attn",
    )(p3, kc, vc, p3, p3, p3, p3, small3, ov, emat)


def _mla_attn_kernel(qn_ref, qr_ref, kn_ref, kr_ref, v_ref, o_ref, s_scr, mx_scr, l_scr, acc_scr,
                     *, tq, tk):
    qs = pl.program_id(1) * tq
    n_full = qs // tk
    scale = (MLA_NOPE + MLA_ROPE) ** -0.5 * LOG2E
    lane = lax.broadcasted_iota(I32, (tq, LANES), 1)
    row_t = qs + lax.broadcasted_iota(I32, (tq, 1), 0)
    col0 = lax.broadcasted_iota(I32, (tq, tk), 1)
    _softmax_init(mx_scr, l_scr, acc_scr)

    def scores(j, masked):
        ks = pl.multiple_of(j * tk, tk)
        kr_tile = kr_ref[0, pl.ds(ks, tk), :]
        for h in range(HEADS):
            hs = slice(h * HEAD_W, (h + 1) * HEAD_W)
            pair = qr_ref[0, :, (h // 2) * LANES:(h // 2 + 1) * LANES]
            keep = (lane < MLA_ROPE) if h % 2 == 0 else (lane >= MLA_ROPE)
            qr = jnp.where(keep, pair, jnp.zeros_like(pair))
            s = _dot_nt(jnp.concatenate([qn_ref[0, :, hs], qr], axis=1),
                        jnp.concatenate([kn_ref[0, pl.ds(ks, tk), hs], kr_tile], axis=1)) * scale
            if masked:
                s = jnp.where(col0 + ks <= row_t, s, NEG)
            _score_store(h, j, s, s_scr, mx_scr)

    _causal_tiles(scores, n_full)
    _row_max_finish(mx_scr)

    def probs(j, carry):
        ks = pl.multiple_of(j * tk, tk)
        for h in range(HEADS):
            v_tile = v_ref[0, pl.ds(ks, tk), h * HEAD_W:(h + 1) * HEAD_W]
            _prob_accumulate(h, j, v_tile, s_scr, mx_scr, l_scr, acc_scr)
        return carry

    _paired_tiles(n_full + 1, lambda j: probs(j, 0))
    for h in range(HEADS):
        o_ref[0, :, h * HEAD_W:(h + 1) * HEAD_W] = _softmax_out(h, l_scr, acc_scr).astype(BF16)


def _mla_attn(q3, kv3, p3, *, tq, tk):
    b, s, _ = q3.shape
    return pl.pallas_call(
        functools.partial(_mla_attn_kernel, tq=tq, tk=tk),
        out_shape=jax.ShapeDtypeStruct((b, s, BR_WIDTH), BF16),
        grid=(b, s // tq),
        in_specs=[pl.BlockSpec((1, tq, 512), lambda bi, i: (bi, i, 0)),
                  pl.BlockSpec((1, tq, 256), lambda bi, i: (bi, i, 2)),
                  pl.BlockSpec((1, s, 512), lambda bi, i: (bi, 0, 0)),
                  pl.BlockSpec((1, s, LANES), lambda bi, i: (bi, 0, P_KR // LANES)),
                  pl.BlockSpec((1, s, 512), lambda bi, i: (bi, 0, 1))],
        out_specs=pl.BlockSpec((1, tq, BR_WIDTH), lambda bi, i: (bi, i, 0)),
        scratch_shapes=_softmax_scratch(HEADS, s // tk, tq, tk),
        compiler_params=_cparams(2),
        name="mla_attn",
    )(q3, q3, kv3, p3, kv3)


def _sortable_key(x):
    bits = pltpu.bitcast(x + 0.0, I32)
    return bits ^ (lax.shift_right_arithmetic(bits, 31) & 0x7FFFFFFF)


def _fold_rows(x, op):
    n = x.shape[0] // SUBLANES
    return op(x.reshape(n, SUBLANES, x.shape[1]), axis=0)


def _count16(half_scr, n_tiles, pred, tq):
    def count_tile(j, cnt):
        hit = pred(half_scr[j]).astype(I16)
        parts = [hit[r:r + PACKED_ROWS] for r in range(0, hit.shape[0], PACKED_ROWS)]
        while len(parts) > 1:
            parts = [a + b for a, b in zip(parts[0::2], parts[1::2])]
        return cnt + parts[0]

    cnt = lax.fori_loop(0, n_tiles, count_tile, jnp.zeros((PACKED_ROWS, tq), I16))
    return jnp.sum(cnt.astype(I32), axis=0, keepdims=True)


def _bisect16(half_scr, n_tiles, need, tq):
    def bit_body(i, carry):
        th, tot = carry
        cand = th + lax.shift_left(jnp.int32(1), 15 - i)
        c16 = cand.astype(I16)
        total = _count16(half_scr, n_tiles, lambda x: x >= c16, tq)
        ok = total >= need
        return jnp.where(ok, cand, th), jnp.where(ok, total, tot)

    return lax.fori_loop(0, 16, bit_body, (jnp.full((1, tq), HALF_MIN, I32), jnp.full((1, tq), -1, I32)))


def _dsa_kernel(qt_ref, iqt_ref, iwt_ref, k_ref, ik_ref, vt_ref, tri_ref, o_ref,
                key_scr, half_scr, s_scr, mx_scr, l_scr, acc_scr, *, tq, tk, top):
    qs = pl.program_id(1) * tq
    n_tiles = (qs + tq - 1) // tk + 1
    scale = DSA_DIM ** -0.5 * LOG2E
    t_lane = qs + lax.broadcasted_iota(I32, (tk, tq), 1)
    krow0 = lax.broadcasted_iota(I32, (tk, tq), 0)
    half_rows = lax.broadcasted_iota(I32, (LANES, tq), 0) < IDX_DIM
    vt_per_tile = tk // VT_TILE

    def score_tile(j, masked):
        ks = pl.multiple_of(j * tk, tk)
        ikt = ik_ref[0, pl.ds(ks, tk), :]
        acc = jnp.zeros((tk, tq), F32)
        for h in range(IDX_HEADS):
            pair = iqt_ref[0, (h // 2) * LANES:(h // 2 + 1) * LANES, :]
            keep = half_rows if h % 2 == 0 else jnp.logical_not(half_rows)
            iq_h = jnp.where(keep, pair, jnp.zeros_like(pair))
            w_h = iwt_ref[0, SMALL_IW + h:SMALL_IW + h + 1, :]
            acc = acc + w_h * jnp.maximum(_dot(ikt, iq_h), 0.0)
        key = _sortable_key(acc)
        if masked:
            key = jnp.where(krow0 + ks <= t_lane, key, INT_MIN)
        key_scr[j] = key
        half_scr[j] = lax.shift_right_arithmetic(key, 16).astype(I16)

    _causal_tiles(score_tile, n_tiles - 1)

    th_hi, n_ge_hi = _bisect16(half_scr, n_tiles, top, tq)
    hi16 = th_hi.astype(I16)
    n_gt_hi = _count16(half_scr, n_tiles, lambda x: x > hi16, tq)
    need_lo = top - n_gt_hi

    def low_tile(j, carry):
        key = key_scr[j]
        lo = (key & 0xFFFF) + HALF_MIN
        same_hi = lax.shift_right_arithmetic(key, 16) == th_hi
        half_scr[j] = jnp.where(same_hi, lo, HALF_MIN).astype(I16)
        return carry

    lax.fori_loop(0, n_tiles, low_tile, 0)
    th_lo, n_ge_lo = _bisect16(half_scr, n_tiles, need_lo, tq)
    theta = lax.shift_left(th_hi, 16) + (th_lo - HALF_MIN)
    n_ge_lo = jnp.where(n_ge_lo >= 0, n_ge_lo, n_ge_hi - n_gt_hi)
    tied = (n_ge_lo > need_lo) & (n_ge_hi >= 0)

    @pl.when(jnp.max(tied.astype(I32)) > 0)
    def _():
        lo16 = th_lo.astype(I16)
        n_gt = n_gt_hi + _count16(half_scr, n_tiles, lambda x: x > lo16, tq)
        keep_eq = jnp.where(tied, top - n_gt, tk * key_scr.shape[0]).astype(F32)

        def demote(j, seen):
            k = key_scr[j]
            eq = k == theta
            eqf = jnp.where(eq, 1.0, 0.0)
            before = _dot(tri_ref[...], eqf.astype(BF16)) + seen
            key_scr[j] = jnp.where(jnp.where(eq, before, -1.0) >= keep_eq, k - 1, k)
            return seen + jnp.sum(_fold_rows(eqf, jnp.sum), axis=0, keepdims=True)

        lax.fori_loop(0, n_tiles, demote, jnp.zeros((1, tq), F32))

    theta = jnp.maximum(theta, INT_MIN + 1)

    mx_scr[...] = jnp.full(mx_scr.shape, NEG, F32)
    l_scr[...] = jnp.zeros(l_scr.shape, F32)
    acc_scr[...] = jnp.zeros(acc_scr.shape, F32)

    def scores(j, carry):
        ks = pl.multiple_of(j * tk, tk)
        sel = key_scr[j] >= theta
        for h in range(HEADS):
            hs = slice(h * HEAD_W, (h + 1) * HEAD_W)
            s = _dot(k_ref[0, pl.ds(ks, tk), hs], qt_ref[0, hs, :]) * scale
            s = jnp.where(sel, s, NEG)
            s_scr[h, j] = s
            mx_scr[h] = jnp.maximum(mx_scr[h], _fold_rows(s, jnp.max))
        return carry

    _paired_tiles(n_tiles, lambda j: scores(j, 0))
    for h in range(HEADS):
        m = jnp.max(mx_scr[h], axis=0, keepdims=True)
        mx_scr[h] = jnp.broadcast_to(m, (SUBLANES, tq))

    def probs(j, carry):
        for h in range(HEADS):
            p = jnp.exp2(s_scr[h, j] - mx_scr[h][0:1])
            l_scr[h] += _fold_rows(p, jnp.sum)
            pb = p.astype(BF16)
            for c in range(vt_per_tile):
                vt = vt_ref[0, j * vt_per_tile + c, h * HEAD_W:(h + 1) * HEAD_W, :]
                acc_scr[h] += _dot(vt, pb[c * VT_TILE:(c + 1) * VT_TILE])
        return carry

    _paired_tiles(n_tiles, lambda j: probs(j, 0))
    for h in range(HEADS):
        ot = acc_scr[h] / jnp.sum(l_scr[h], axis=0, keepdims=True)
        o_ref[0, :, h * HEAD_W:(h + 1) * HEAD_W] = ot.T.astype(BF16)


def _dsa(p3, t3, vt4, smallt, *, tq, tk, top):
    b, s, _ = p3.shape
    n_vt = vt4.shape[1]
    tri = jnp.asarray(np.tril(np.ones((tk, tk), np.float32), -1), BF16)
    return pl.pallas_call(
        functools.partial(_dsa_kernel, tq=tq, tk=tk, top=top),
        out_shape=jax.ShapeDtypeStruct((b, s, BR_WIDTH), BF16),
        grid=(b, s // tq),
        in_specs=[pl.BlockSpec((1, 512, tq), lambda bi, i: (bi, T_DQ // 512, i)),
                  pl.BlockSpec((1, 512, tq), lambda bi, i: (bi, T_IQ // 512, i)),
                  pl.BlockSpec((1, LANES, tq), lambda bi, i: (bi, 0, i)),
                  pl.BlockSpec((1, s, 512), lambda bi, i: (bi, 0, P_DK // 512)),
                  pl.BlockSpec((1, s, LANES), lambda bi, i: (bi, 0, P_IK // LANES)),
                  pl.BlockSpec((1, n_vt, BR_WIDTH, VT_TILE), lambda bi, i: (bi, 0, 0, 0)),
                  pl.BlockSpec((tk, tk), lambda bi, i: (0, 0))],
        out_specs=pl.BlockSpec((1, tq, BR_WIDTH), lambda bi, i: (bi, i, 0)),
        scratch_shapes=[pltpu.VMEM((s // tk, tk, tq), I32),
                        pltpu.VMEM((s // tk, tk, tq), I16),
                        pltpu.VMEM((HEADS, s // tk, tk, tq), F32),
                        pltpu.VMEM((HEADS, SUBLANES, tq), F32),
                        pltpu.VMEM((HEADS, SUBLANES, tq), F32),
                        pltpu.VMEM((HEADS, HEAD_W, tq), F32)],
        compiler_params=_cparams(2),
        name="dsa_attn",
    )(t3, t3, smallt, p3, p3, vt4, tri)


def _merge_kernel(x_ref, g_ref, oa_ref, ob_ref, oc_ref, od_ref, wg_ref, wb_ref, wo_ref, o_ref):
    x = x_ref[...]
    d = x.shape[1]
    ms = jnp.mean(x * x, axis=-1, keepdims=True)
    h = (x * lax.rsqrt(ms + EPS) * g_ref[...]).astype(BF16)
    acc = jnp.zeros(x.shape, F32)
    for n, br_ref in enumerate((oa_ref, ob_ref, oc_ref, od_ref)):
        gate = jax.nn.sigmoid(_dot(h, wg_ref[:, n * d:(n + 1) * d]))
        acc = acc + gate * _dot(br_ref[...], wb_ref[n])
    o_ref[...] = x + _dot(acc.astype(BF16), wo_ref[...])


def _merge(x2, g, oa, ob, oc, od, wg, wb, wo, *, layer, tm):
    m, d = x2.shape
    row = lambda w: pl.BlockSpec((tm, w), lambda i: (i, 0))
    return pl.pallas_call(
        _merge_kernel,
        out_shape=jax.ShapeDtypeStruct((m, d), F32),
        grid=(m // tm,),
        in_specs=[row(d), pl.BlockSpec((1, d), lambda i: (0, 0)),
                  row(BR_WIDTH), row(BR_WIDTH), row(BR_WIDTH), row(BR_WIDTH),
                  pl.BlockSpec((None,) + wg.shape[1:], lambda i: (layer, 0, 0)),
                  pl.BlockSpec((None,) + wb.shape[1:], lambda i: (layer, 0, 0, 0)),
                  pl.BlockSpec((None,) + wo.shape[1:], lambda i: (layer, 0, 0))],
        out_specs=row(d),
        compiler_params=_cparams(1),
        name="merge",
    )(x2, g, oa, ob, oc, od, wg, wb, wo)


def _ffn_kernel(x_ref, g_ref, wg_ref, wu_ref, wd_ref, gf_ref, o_ref, h_scr, acc_scr, *, final):
    j = pl.program_id(1)

    @pl.when(j == 0)
    def _():
        x = x_ref[...]
        ms = jnp.mean(x * x, axis=-1, keepdims=True)
        h_scr[...] = (x * lax.rsqrt(ms + EPS) * g_ref[...]).astype(BF16)
        acc_scr[...] = jnp.zeros(acc_scr.shape, F32)

    h = h_scr[...]
    a = jax.nn.silu(_dot(h, wg_ref[...])) * _dot(h, wu_ref[...])
    acc_scr[...] += _dot(a.astype(BF16), wd_ref[...])

    @pl.when(j == pl.num_programs(1) - 1)
    def _():
        y = x_ref[...] + acc_scr[...]
        if final:
            ms = jnp.mean(y * y, axis=-1, keepdims=True)
            y = y * lax.rsqrt(ms + EPS) * gf_ref[...]
        o_ref[...] = y


def _ffn(x2, g, wgu, wd, gf, *, layer, final, tm, tf):
    m, d = x2.shape
    dff = wd.shape[1]
    n_f = dff // tf
    return pl.pallas_call(
        functools.partial(_ffn_kernel, final=final),
        out_shape=jax.ShapeDtypeStruct((m, d), F32),
        grid=(m // tm, n_f),
        in_specs=[pl.BlockSpec((tm, d), lambda i, j: (i, 0)),
                  pl.BlockSpec((1, d), lambda i, j: (0, 0)),
                  pl.BlockSpec((None, d, tf), lambda i, j: (layer, 0, j)),
                  pl.BlockSpec((None, d, tf), lambda i, j: (layer, 0, j + n_f)),
                  pl.BlockSpec((None, tf, d), lambda i, j: (layer, j, 0)),
                  pl.BlockSpec((1, d), lambda i, j: (0, 0))],
        out_specs=pl.BlockSpec((tm, d), lambda i, j: (i, 0)),
        scratch_shapes=[pltpu.VMEM((tm, d), BF16), pltpu.VMEM((tm, d), F32)],
        compiler_params=_cparams(2),
        name="ffn",
    )(x2, g, wgu, wgu, wd, gf)


def _tiles(seq, m, dff):
    pick = lambda n, cands: next(c for c in cands if n % c == 0)
    tk = pick(seq, (512, 256))
    return dict(
        merge_tm=pick(m, (256, 128)),
        prep_ts=VT_TILE,
        diff_tq=128, mla_tq=pick(seq, (256, 128)), nsa_tq=256, dsa_tq=256, tk=tk,
        row_tm=pick(m, (512, 256, 128)),
        ffn_tf=pick(dff, (1408, 704, 256, 128)),
    )


def kernel(x, norm1_g, w_in, diff_lq1, diff_lk1, diff_lq2, diff_lk2, diff_subln_g, nsa_pe_k, nsa_w1_k, nsa_w2_k, nsa_pe_v, nsa_w1_v, nsa_w2_v, mla_q_norm_g, mla_w_uq, mla_kv_norm_g, mla_w_ukv, idx_k_norm_g, w_branch, w_out, norm2_g, w_gate_up, w_down, final_norm_g):
    b, seq, d = x.shape
    depth = w_in.shape[0]
    m = b * seq
    dff = w_down.shape[1]
    t = _tiles(seq, m, dff)
    tk = t["tk"]
    assert seq % SEL_LEN == 0 and seq >= WIN + t["nsa_tq"] and seq // SEL_LEN <= LANES
    assert seq % t["dsa_tq"] == 0 and tk % VT_TILE == 0 and tk >= min(IDX_TOPK, seq // 4)

    col_idx, gate_off, d_in = _in_proj_columns()
    assert w_in.shape[2] == d_in
    tab = jnp.concatenate([_rope_table(seq, rot, per) for rot, per in ROPE_KINDS], axis=1)

    ng = seq // CMP_STRIDE
    ns = seq // SEL_LEN
    c_start = np.arange(ng)[:, None] * CMP_STRIDE
    s_start = np.arange(LANES)[None, :] * SEL_LEN
    ov = ((c_start < s_start + SEL_LEN) & (c_start + CMP_LEN - 1 >= s_start)
          & (np.arange(LANES)[None, :] < ns))
    ov = jnp.asarray(ov, BF16)
    emat = np.arange(LANES)[:, None] == (np.arange(seq)[None, :] // SEL_LEN)
    emat = jnp.asarray(emat.reshape(LANES, seq // tk, tk).transpose(1, 0, 2), BF16)

    qd = MLA_NOPE + MLA_ROPE
    uq_idx = np.concatenate([np.concatenate([np.arange(h * qd, h * qd + MLA_NOPE) for h in range(HEADS)]),
                             np.concatenate([np.arange(h * qd + MLA_NOPE, (h + 1) * qd) for h in range(HEADS)])])
    kvd = MLA_NOPE + HEAD_W
    ukv_idx = np.concatenate([np.concatenate([np.arange(h * kvd, h * kvd + MLA_NOPE) for h in range(HEADS)]),
                              np.concatenate([np.arange(h * kvd + MLA_NOPE, (h + 1) * kvd) for h in range(HEADS)])])

    w_in_bf = w_in.astype(BF16)
    w_mix = _take_cols(w_in_bf, col_idx)
    w_gate = w_in_bf[..., gate_off:]
    w_branch_bf, w_out_bf = w_branch.astype(BF16), w_out.astype(BF16)
    w_gate_up_bf, w_down_bf = w_gate_up.astype(BF16), w_down.astype(BF16)

    x2 = x.reshape(m, d)
    half_w1 = CMP_STRIDE * NSA_DK
    for l in range(depth):
        lam_init = 0.8 - 0.6 * math.exp(-0.3 * l)

        gq = jnp.pad(mla_q_norm_g[l], (0, 512 - MLA_Q_LORA))[None]
        gkv = mla_kv_norm_g[l][None]
        gik = jnp.concatenate([idx_k_norm_g[l], idx_k_norm_g[l]])[None]
        wq = jnp.pad(_take_cols(mla_w_uq[l].astype(BF16), uq_idx), ((0, 512 - MLA_Q_LORA), (0, 0)))
        wkv = _take_cols(mla_w_ukv[l].astype(BF16), ukv_idx)
        p2, t3, vt4, kc_tok, vc_tok, small, smallt, q_c, kv_c = _proj_prep(
            x2, norm1_g[l][None], w_mix, tab, gq, gkv, gik, wq, wkv,
            layer=l, batch=b, seq=seq, ts=t["prep_ts"])
        p3 = p2.reshape(b, seq, P_WIDTH)
        small3 = small.reshape(b, seq, LANES)

        lv = jnp.stack([diff_lq1[l], diff_lk1[l], diff_lq2[l], diff_lk2[l]])
        o_a = _diff_attn(p3, lv, diff_subln_g[l][None], lam_init=lam_init, tq=t["diff_tq"], tk=tk)

        w1k, w1v = nsa_w1_k[l].astype(BF16), nsa_w1_v[l].astype(BF16)
        w1k_cat = jnp.concatenate([w1k[:half_w1], w1k[half_w1:]], axis=1)
        w1v_cat = jnp.concatenate([w1v[:half_w1], w1v[half_w1:]], axis=1)
        pek = jnp.broadcast_to(nsa_pe_k[l].reshape(1, -1), (8, CMP_LEN * NSA_DK)).astype(BF16)
        pev = jnp.broadcast_to(nsa_pe_v[l].reshape(1, -1), (8, CMP_LEN * NSA_DK)).astype(BF16)
        kc, vc = _nsa_compress(kc_tok.reshape(b, ng, half_w1), vc_tok.reshape(b, ng, half_w1),
                               w1k_cat, w1v_cat, pek, pev, w1k, w1v,
                               nsa_w2_k[l].astype(BF16), nsa_w2_v[l].astype(BF16))
        o_b = _nsa(p3, kc, vc, small3, ov, emat, tq=t["nsa_tq"], tk=tk)

        o_c = _mla_attn(q_c.reshape(b, seq, -1), kv_c.reshape(b, seq, -1), p3, tq=t["mla_tq"], tk=tk)

        o_d = _dsa(p3, t3, vt4, smallt, tq=t["dsa_tq"], tk=tk, top=min(IDX_TOPK, seq // 4))

        x2 = _merge(x2, norm1_g[l][None],
                    o_a.reshape(m, -1), o_b.reshape(m, -1), o_c.reshape(m, -1), o_d.reshape(m, -1),
                    w_gate, w_branch_bf, w_out_bf, layer=l, tm=t["merge_tm"])
        x2 = _ffn(x2, norm2_g[l][None], w_gate_up_bf, w_down_bf, final_norm_g[None],
                  layer=l, final=(l == depth - 1), tm=t["row_tm"], tf=t["ffn_tf"])
    return x2.reshape(b, seq, d)
```

```python
import functools
import math

import numpy as np
import jax
import jax.numpy as jnp
from jax import lax
from jax.experimental import pallas as pl
from jax.experimental.pallas import tpu as pltpu

F32 = jnp.float32
BF16 = jnp.bfloat16
I32 = jnp.int32
I16 = jnp.int16

LANES = 128
SUBLANES = 8
PACKED_ROWS = 16
HALF_MIN = -32768
VMEM_LIMIT = 56 * 1024 * 1024

ROPE_THETA = 500000.0
NEG = -1e30
LOG2E = math.log2(math.e)
FORCE_SCORE = 1e9
PAD_SCORE = -3e38
EPS = 1e-6
INT_MIN = -2147483648

HEADS = 4
HEAD_W = 128
BR_WIDTH = HEADS * HEAD_W
DA_DIM = 64
NSA_DK = 128
CMP_LEN = 32
CMP_STRIDE = 16
SEL_LEN = 64
SEL_N = 16
WIN = 512
MLA_Q_LORA = 384
MLA_KV_LORA = 256
MLA_NOPE = 128
MLA_ROPE = 64
DSA_DIM = 128
IDX_HEADS = 8
IDX_DIM = 64
IDX_TOPK = 256

Z_AQ, Z_AK, Z_AV, Z_BQ, Z_DQ, Z_DK, Z_DV, Z_IQ = (i * 512 for i in range(8))
Z_CQ = 4096
Z_CKV = 4608
Z_KC, Z_KS, Z_KW, Z_VC, Z_VS, Z_VW, Z_KR, Z_IK, Z_SMALL = (4864 + i * 128 for i in range(9))
Z_WIDTH = 6144
P_AQ, P_AK, P_AV, P_BQ, P_DK = (i * 512 for i in range(5))
P_KS, P_KW, P_VS, P_VW, P_KR, P_IK = (2560 + i * 128 for i in range(6))
P_WIDTH = 3328
T_DQ, T_IQ = 0, 512
T_ROWS = 1024
VT_TILE = 256
SMALL_G = 0
SMALL_IW = 12

ROPE_KINDS = ((16, 64), (32, 128), (64, 64))
TAB_W = 3 * LANES


def _cparams(n_axes):
    return pltpu.CompilerParams(dimension_semantics=("arbitrary",) * n_axes,
                                vmem_limit_bytes=VMEM_LIMIT)


def _dot(a, b):
    return jnp.dot(a, b, preferred_element_type=F32)


def _dot_nt(a, b):
    return lax.dot_general(a, b, (((1,), (1,)), ((), ())), preferred_element_type=F32)


def _in_proj_columns():
    names = (("a_q", 512), ("a_k", 512), ("a_v", 512), ("b_q", 512),
             ("b_kc", 128), ("b_vc", 128), ("b_ks", 128), ("b_vs", 128),
             ("b_kw", 128), ("b_vw", 128), ("b_g", 12),
             ("c_q", 384), ("c_kv", 256), ("c_kr", 64),
             ("d_q", 512), ("d_k", 512), ("d_v", 512),
             ("d_iq", 512), ("d_ik", 64), ("d_iw", 8), ("gate", 4096))
    off, o = {}, 0
    for nm, n in names:
        off[nm] = (o, n)
        o += n
    idx = np.full((Z_WIDTH,), -1, np.int64)

    def put(dst, nm):
        s, n = off[nm]
        idx[dst:dst + n] = np.arange(s, s + n)

    put(Z_AQ, "a_q"); put(Z_AK, "a_k"); put(Z_AV, "a_v"); put(Z_BQ, "b_q")
    put(Z_DQ, "d_q"); put(Z_DK, "d_k"); put(Z_DV, "d_v"); put(Z_IQ, "d_iq")
    put(Z_CQ, "c_q"); put(Z_CKV, "c_kv")
    put(Z_KC, "b_kc"); put(Z_KS, "b_ks"); put(Z_KW, "b_kw")
    put(Z_VC, "b_vc"); put(Z_VS, "b_vs"); put(Z_VW, "b_vw")
    put(Z_KR, "c_kr"); put(Z_KR + 64, "c_kr")
    put(Z_IK, "d_ik"); put(Z_IK + 64, "d_ik")
    put(Z_SMALL + SMALL_G, "b_g"); put(Z_SMALL + SMALL_IW, "d_iw")
    return idx, off["gate"][0], o


def _take_cols(w, idx):
    runs, i, n = [], 0, len(idx)
    while i < n:
        j = i + 1
        if idx[i] < 0:
            while j < n and idx[j] < 0:
                j += 1
            runs.append(jnp.zeros(w.shape[:-1] + (j - i,), w.dtype))
        else:
            while j < n and idx[j] == idx[j - 1] + 1:
                j += 1
            runs.append(w[..., int(idx[i]):int(idx[i]) + (j - i)])
        i = j
    return jnp.concatenate(runs, axis=-1)


def _rope_table(seq, rot, period):
    half = rot // 2
    inv = jnp.power(jnp.float32(ROPE_THETA), -jnp.arange(0, rot, 2, dtype=F32) / rot)
    ang = jnp.arange(seq, dtype=F32)[:, None] * inv[None, :]
    cos, sin = jnp.cos(ang), jnp.sin(ang)
    lane = np.arange(LANES) % period
    in1 = lane < half
    in2 = (lane >= half) & (lane < 2 * half)
    fidx = np.where(in1, lane, np.where(in2, lane - half, 0))
    cosl, sinl = cos[:, fidx], sin[:, fidx]
    c = jnp.where(jnp.asarray(in1 | in2)[None], cosl, 1.0)
    s1 = jnp.where(jnp.asarray(in1)[None], -sinl, 0.0)
    s2 = jnp.where(jnp.asarray(in2)[None], sinl, 0.0)
    return jnp.concatenate([c, s1, s2], axis=1)


def _rope128(x, tab, half):
    return (x * tab[:, 0:LANES]
            + pltpu.roll(x, LANES - half, 1) * tab[:, LANES:2 * LANES]
            + pltpu.roll(x, half, 1) * tab[:, 2 * LANES:3 * LANES])


PROJ_TILE = 512


def _proj_prep_kernel(x_ref, g_ref, w_ref, tab_ref, gq_ref, gkv_ref, gik_ref, wq_ref, wkv_ref,
                      p_ref, t_ref, vt_ref, kc_ref, vc_ref, small_ref, smallt_ref, q_ref, kv_ref):
    x = x_ref[...]
    ms = jnp.mean(x * x, axis=-1, keepdims=True)
    h = (x * lax.rsqrt(ms + EPS) * g_ref[...]).astype(BF16)
    z_tiles = {}

    def z_cols(off, width):
        t = off // PROJ_TILE
        assert (off + width - 1) // PROJ_TILE == t
        if t not in z_tiles:
            z_tiles[t] = _dot(h, w_ref[:, t * PROJ_TILE:(t + 1) * PROJ_TILE])
        lo = off - t * PROJ_TILE
        return z_tiles[t][:, lo:lo + width]

    def zc(off, c=0):
        return z_cols(off + c * LANES, LANES)

    def tab(kind):
        return tab_ref[:, kind * TAB_W:(kind + 1) * TAB_W]

    def put(off, c, v):
        p_ref[:, off + c * LANES:off + (c + 1) * LANES] = v.astype(BF16)

    def rope(off, c, kind):
        return _rope128(zc(off, c), tab(kind), ROPE_KINDS[kind][0] // 2)

    for zoff, poff, kind in ((Z_AQ, P_AQ, 0), (Z_AK, P_AK, 0), (Z_BQ, P_BQ, 1), (Z_DK, P_DK, 1)):
        for c in range(4):
            put(poff, c, rope(zoff, c, kind))
    for c in range(4):
        put(P_AV, c, zc(Z_AV, c))
    put(P_VS, 0, zc(Z_VS)); put(P_VW, 0, zc(Z_VW))
    put(P_KS, 0, rope(Z_KS, 0, 1)); put(P_KW, 0, rope(Z_KW, 0, 1))
    put(P_KR, 0, rope(Z_KR, 0, 2))
    kc_ref[...] = rope(Z_KC, 0, 1).astype(BF16)
    vc_ref[...] = zc(Z_VC).astype(BF16)

    for zoff, toff, kind in ((Z_DQ, T_DQ, 1), (Z_IQ, T_IQ, 0)):
        for c in range(4):
            t_ref[0, toff + c * LANES:toff + (c + 1) * LANES, :] = rope(zoff, c, kind).T.astype(BF16)
    for c in range(4):
        vt_ref[0, 0, c * LANES:(c + 1) * LANES, :] = zc(Z_DV, c).T.astype(BF16)

    cq = z_cols(Z_CQ, 512)
    ms = jnp.sum(cq * cq, axis=-1, keepdims=True) * (1.0 / MLA_Q_LORA)
    q = _dot((cq * lax.rsqrt(ms + EPS) * gq_ref[...]).astype(BF16), wq_ref[...])
    nn = HEADS * MLA_NOPE
    q_ref[:, :nn] = q[:, :nn].astype(BF16)
    for c in range(nn // LANES, (nn + HEADS * MLA_ROPE) // LANES):
        tile = _rope128(q[:, c * LANES:(c + 1) * LANES], tab(2), MLA_ROPE // 2)
        q_ref[:, c * LANES:(c + 1) * LANES] = tile.astype(BF16)
    ckv = z_cols(Z_CKV, MLA_KV_LORA)
    ms = jnp.mean(ckv * ckv, axis=-1, keepdims=True)
    kv_ref[...] = _dot((ckv * lax.rsqrt(ms + EPS) * gkv_ref[...]).astype(BF16), wkv_ref[...]).astype(BF16)

    ik = zc(Z_IK)
    ms = jnp.mean(ik * ik, axis=-1, keepdims=True)
    ikn = ik * lax.rsqrt(ms + EPS) * gik_ref[...]
    put(P_IK, 0, _rope128(ikn, tab(0), ROPE_KINDS[0][0] // 2))

    sm = zc(Z_SMALL)
    lane = lax.broadcasted_iota(I32, sm.shape, 1)
    iw_scale = IDX_HEADS ** -0.5 * IDX_DIM ** -0.5
    small = jnp.where(lane < SMALL_IW, jax.nn.sigmoid(sm), sm * iw_scale)
    small_ref[...] = small
    smallt_ref[0] = small.T


def _proj_prep(x2, g, w, tab, gq, gkv, gik, wq, wkv, *, layer, batch, seq, ts):
    m, d = x2.shape
    spb = seq // ts
    assert ts == VT_TILE and w.shape[1:] == (d, Z_WIDTH)
    row = lambda w: pl.BlockSpec((ts, w), lambda i: (i, 0))
    const = lambda a: pl.BlockSpec(a.shape, lambda i: (0,) * a.ndim)
    return pl.pallas_call(
        _proj_prep_kernel,
        out_shape=(jax.ShapeDtypeStruct((m, P_WIDTH), BF16),
                   jax.ShapeDtypeStruct((batch, T_ROWS, seq), BF16),
                   jax.ShapeDtypeStruct((batch, spb, BR_WIDTH, VT_TILE), BF16),
                   jax.ShapeDtypeStruct((m, LANES), BF16),
                   jax.ShapeDtypeStruct((m, LANES), BF16),
                   jax.ShapeDtypeStruct((m, LANES), F32),
                   jax.ShapeDtypeStruct((batch, LANES, seq), F32),
                   jax.ShapeDtypeStruct((m, wq.shape[1]), BF16),
                   jax.ShapeDtypeStruct((m, wkv.shape[1]), BF16)),
        grid=(m // ts,),
        in_specs=[row(d),
                  pl.BlockSpec((1, d), lambda i: (0, 0)),
                  pl.BlockSpec((None, d, Z_WIDTH), lambda i: (layer, 0, 0)),
                  pl.BlockSpec((ts, 3 * TAB_W), lambda i: (i % spb, 0)),
                  pl.BlockSpec((1, 512), lambda i: (0, 0)),
                  pl.BlockSpec((1, MLA_KV_LORA), lambda i: (0, 0)),
                  pl.BlockSpec((1, LANES), lambda i: (0, 0)),
                  const(wq), const(wkv)],
        out_specs=(row(P_WIDTH),
                   pl.BlockSpec((1, T_ROWS, ts), lambda i: (i // spb, 0, i % spb)),
                   pl.BlockSpec((1, 1, BR_WIDTH, VT_TILE), lambda i: (i // spb, i % spb, 0, 0)),
                   row(LANES), row(LANES), row(LANES),
                   pl.BlockSpec((1, LANES, ts), lambda i: (i // spb, 0, i % spb)),
                   row(wq.shape[1]), row(wkv.shape[1])),
        compiler_params=_cparams(1),
        name="proj_prep",
    )(x2, g, w, tab, gq, gkv, gik, wq, wkv)


def _softmax_init(mx_scr, l_scr, acc_scr):
    mx_scr[...] = jnp.full(mx_scr.shape, NEG, F32)
    l_scr[...] = jnp.zeros(l_scr.shape, F32)
    acc_scr[...] = jnp.zeros(acc_scr.shape, F32)


def _score_store(g, j, s, s_scr, mx_scr):
    s_scr[g, j] = s
    m = s[:, 0:LANES]
    for c in range(1, s.shape[1] // LANES):
        m = jnp.maximum(m, s[:, c * LANES:(c + 1) * LANES])
    mx_scr[g] = jnp.maximum(mx_scr[g], m)


def _row_max_finish(mx_scr):
    for g in range(mx_scr.shape[0]):
        m = jnp.max(mx_scr[g], axis=-1, keepdims=True)
        mx_scr[g] = jnp.broadcast_to(m, mx_scr.shape[1:])


def _prob_accumulate(g, j, v_tile, s_scr, mx_scr, l_scr, acc_scr):
    mb = mx_scr[g]
    s = s_scr[g, j]
    ps = [jnp.exp2(s[:, c * LANES:(c + 1) * LANES] - mb) for c in range(s.shape[1] // LANES)]
    tot = ps[0]
    for p in ps[1:]:
        tot = tot + p
    l_scr[g] += tot
    acc_scr[g] += _dot(jnp.concatenate(ps, axis=1).astype(BF16), v_tile)


def _softmax_out(g, l_scr, acc_scr):
    return acc_scr[g] / jnp.sum(l_scr[g], axis=-1, keepdims=True)


def _paired_tiles(n, step):
    def pair(jj, carry):
        step(2 * jj)
        step(2 * jj + 1)
        return carry

    lax.fori_loop(0, n // 2, pair, 0)

    @pl.when(n % 2 == 1)
    def _():
        step(n - 1)


def _causal_tiles(step, n_full):
    _paired_tiles(n_full, lambda j: step(j, False))
    step(n_full, True)


def _softmax_scratch(groups, n_tiles, rows, tk):
    return [pltpu.VMEM((groups, n_tiles, rows, tk), F32),
            pltpu.VMEM((groups, rows, LANES), F32),
            pltpu.VMEM((groups, rows, LANES), F32),
            pltpu.VMEM((groups, rows, HEAD_W), F32)]


def _diff_attn_kernel(q_ref, k_ref, v_ref, lv_ref, g_ref, o_ref, s_scr, mx_scr, l_scr, acc_scr,
                      *, tq, tk, lam_init):
    qs = pl.program_id(1) * tq
    n_full = qs // tk
    scale = DA_DIM ** -0.5 * LOG2E
    lv = lv_ref[...]
    lam = (jnp.exp(jnp.sum(lv[0:1] * lv[1:2], axis=-1, keepdims=True))
           - jnp.exp(jnp.sum(lv[2:3] * lv[3:4], axis=-1, keepdims=True)) + lam_init)
    lane = lax.broadcasted_iota(I32, (tq, HEAD_W), 1)
    row_t = qs + lax.broadcasted_iota(I32, (2 * tq, 1), 0) % tq
    col0 = lax.broadcasted_iota(I32, (2 * tq, tk), 1)
    _softmax_init(mx_scr, l_scr, acc_scr)

    def scores(j, masked):
        ks = pl.multiple_of(j * tk, tk)
        for h in range(HEADS):
            hs = slice(h * HEAD_W, (h + 1) * HEAD_W)
            qh = q_ref[0, :, hs]
            zero = jnp.zeros_like(qh)
            q2 = jnp.concatenate([jnp.where(lane < DA_DIM, qh, zero),
                                  jnp.where(lane >= DA_DIM, qh, zero)], axis=0)
            s = _dot_nt(q2, k_ref[0, pl.ds(ks, tk), hs]) * scale
            if masked:
                s = jnp.where(col0 + ks <= row_t, s, NEG)
            _score_store(h, j, s, s_scr, mx_scr)

    _causal_tiles(scores, n_full)
    _row_max_finish(mx_scr)

    def probs(j, carry):
        ks = pl.multiple_of(j * tk, tk)
        for h in range(HEADS):
            v_tile = v_ref[0, pl.ds(ks, tk), h * HEAD_W:(h + 1) * HEAD_W]
            _prob_accumulate(h, j, v_tile, s_scr, mx_scr, l_scr, acc_scr)
        return carry

    _paired_tiles(n_full + 1, lambda j: probs(j, 0))
    for h in range(HEADS):
        o2 = _softmax_out(h, l_scr, acc_scr)
        o = o2[:tq] - lam * o2[tq:]
        ms = jnp.mean(o * o, axis=-1, keepdims=True)
        o = o * lax.rsqrt(ms + EPS) * g_ref[...]
        o_ref[0, :, h * HEAD_W:(h + 1) * HEAD_W] = (o * (1.0 - lam_init)).astype(BF16)


def _diff_attn(p3, lv, g, *, lam_init, tq, tk):
    b, s, _ = p3.shape
    return pl.pallas_call(
        functools.partial(_diff_attn_kernel, tq=tq, tk=tk, lam_init=lam_init),
        out_shape=jax.ShapeDtypeStruct((b, s, BR_WIDTH), BF16),
        grid=(b, s // tq),
        in_specs=[pl.BlockSpec((1, tq, 512), lambda bi, i: (bi, i, P_AQ // 512)),
                  pl.BlockSpec((1, s, 512), lambda bi, i: (bi, 0, P_AK // 512)),
                  pl.BlockSpec((1, s, 512), lambda bi, i: (bi, 0, P_AV // 512)),
                  pl.BlockSpec((4, DA_DIM), lambda bi, i: (0, 0)),
                  pl.BlockSpec((1, HEAD_W), lambda bi, i: (0, 0))],
        out_specs=pl.BlockSpec((1, tq, BR_WIDTH), lambda bi, i: (bi, i, 0)),
        scratch_shapes=_softmax_scratch(HEADS, s // tk, 2 * tq, tk),
        compiler_params=_cparams(2),
        name="diff_attn",
    )(p3, p3, p3, lv, g)


def _nsa_compress_kernel(gk_ref, gv_ref, w1k_ref, w1v_ref, pek_ref, pev_ref,
                         w1kf_ref, w1vf_ref, w2k_ref, w2v_ref, kc_ref, vc_ref):
    def one(g_ref, w1cat_ref, pe_ref, w1f_ref, w2_ref, o_ref):
        y = _dot(g_ref[0], w1cat_ref[...])
        n = y.shape[0]
        nxt = pltpu.roll(y[:, HEAD_W:], n - 1, 0)
        c = _dot(pe_ref[...], w1f_ref[...])[0:1]
        hid = jax.nn.gelu(y[:, :HEAD_W] + nxt + c)
        o_ref[0] = _dot(hid.astype(BF16), w2_ref[...]).astype(BF16)

    one(gk_ref, w1k_ref, pek_ref, w1kf_ref, w2k_ref, kc_ref)
    one(gv_ref, w1v_ref, pev_ref, w1vf_ref, w2v_ref, vc_ref)


def _nsa_compress(gk, gv, w1k_cat, w1v_cat, pek, pev, w1k, w1v, w2k, w2v):
    b, ng, gw = gk.shape
    full = lambda shape: pl.BlockSpec(shape, lambda bi: (0,) * len(shape))
    return pl.pallas_call(
        _nsa_compress_kernel,
        out_shape=(jax.ShapeDtypeStruct((b, ng, HEAD_W), BF16),
                   jax.ShapeDtypeStruct((b, ng, HEAD_W), BF16)),
        grid=(b,),
        in_specs=[pl.BlockSpec((1, ng, gw), lambda bi: (bi, 0, 0)),
                  pl.BlockSpec((1, ng, gw), lambda bi: (bi, 0, 0)),
                  full(w1k_cat.shape), full(w1v_cat.shape), full(pek.shape), full(pev.shape),
                  full(w1k.shape), full(w1v.shape), full(w2k.shape), full(w2v.shape)],
        out_specs=(pl.BlockSpec((1, ng, HEAD_W), lambda bi: (bi, 0, 0)),
                   pl.BlockSpec((1, ng, HEAD_W), lambda bi: (bi, 0, 0))),
        compiler_params=_cparams(1),
        name="nsa_compress",
    )(gk, gv, w1k_cat, w1v_cat, pek, pev, w1k, w1v, w2k, w2v)


NSA_GROUPS = 2


def _nsa_kernel(q_ref, kc_ref, vc_ref, ks_ref, vs_ref, kw_ref, vw_ref, small_ref, ov_ref, e_ref,
                o_ref, s_scr, mx_scr, l_scr, acc_scr, cmp_scr, win_scr, bias_scr, *, tq, tk, seq):
    qs = pl.program_id(1) * tq
    scale = NSA_DK ** -0.5
    ns = seq // SEL_LEN
    n_sel = min(SEL_N, ns)
    r = HEADS * tq
    rg = r // NSA_GROUPS
    q4 = jnp.concatenate([q_ref[0, :, h * HEAD_W:(h + 1) * HEAD_W] for h in range(HEADS)], axis=0)
    t1 = qs + lax.broadcasted_iota(I32, (tq, 1), 0)
    t4 = qs + lax.broadcasted_iota(I32, (r, 1), 0) % tq

    wspan = WIN + tq
    start = pl.multiple_of(jnp.maximum(qs - WIN, 0), tq)
    sw = _dot_nt(q4, kw_ref[0, pl.ds(start, wspan), :]) * (scale * LOG2E)
    dist = (t4 - start) - lax.broadcasted_iota(I32, (r, wspan), 1)
    sw = jnp.where(pltpu.bitcast(dist, jnp.uint32) < jnp.uint32(WIN), sw, NEG)
    e = jnp.exp2(sw - jnp.max(sw, axis=-1, keepdims=True))
    win_scr[...] = (_dot(e.astype(BF16), vw_ref[0, pl.ds(start, wspan), :])
                    / jnp.sum(e, axis=-1, keepdims=True))

    kc = kc_ref[0]
    nc_pad = kc.shape[0]
    sc = _dot_nt(q4, kc) * (scale * LOG2E)
    c_end = lax.broadcasted_iota(I32, (r, nc_pad), 1) * CMP_STRIDE + (CMP_LEN - 1)
    cmask = c_end <= t4
    mx = jnp.max(jnp.where(cmask, sc, NEG), axis=-1, keepdims=True)
    e = jnp.where(cmask, jnp.exp2(sc - mx), 0.0)
    den = jnp.sum(e, axis=-1, keepdims=True)
    pc = e / jnp.where(den > 0.0, den, 1.0)
    cmp_scr[...] = _dot(pc.astype(BF16), vc_ref[0])

    psum = pc[0:tq] + pc[tq:2 * tq] + pc[2 * tq:3 * tq] + pc[3 * tq:4 * tq]
    ov = ov_ref[...]
    hi = psum.astype(BF16)
    r1 = psum - hi.astype(F32)
    mid = r1.astype(BF16)
    lo = (r1 - mid.astype(F32)).astype(BF16)
    imp = _dot(hi, ov) + _dot(mid, ov) + _dot(lo, ov)

    blk = lax.broadcasted_iota(I32, (tq, LANES), 1)
    cur = t1 // SEL_LEN
    forced = (blk == 0) | (blk == cur) | (blk == cur - 1)
    visible = blk * SEL_LEN <= t1
    score = jnp.where(visible, jnp.where(forced, FORCE_SCORE, imp), NEG)
    score = jnp.where(blk < ns, score, PAD_SCORE)
    ns_pad = -(-ns // SUBLANES) * SUBLANES
    score_t = score.T[:ns_pad]
    n_grp = ns_pad // SUBLANES
    groups = [score_t[g * SUBLANES:(g + 1) * SUBLANES] for g in range(n_grp)]
    sub = lax.broadcasted_iota(I32, (SUBLANES, tq), 0)
    ranks = [jnp.zeros((SUBLANES, tq), I32) for _ in range(n_grp)]
    for jp in range(ns):
        row = score_t[jp:jp + 1, :]
        for g in range(n_grp):
            if g * SUBLANES > jp:
                beats = (row >= groups[g]).astype(I32)
            elif g * SUBLANES + SUBLANES - 1 <= jp:
                beats = (row > groups[g]).astype(I32)
            else:
                beats = jnp.where(sub > (jp - g * SUBLANES), (row >= groups[g]).astype(I32),
                                  (row > groups[g]).astype(I32))
            ranks[g] = ranks[g] + beats
    rank = jnp.concatenate(ranks, axis=0)
    sel_t = jnp.where(rank < n_sel, 1.0, 0.0)
    if ns_pad < LANES:
        sel_t = jnp.concatenate([sel_t, jnp.zeros((LANES - ns_pad, tq), F32)], axis=0)
    selb = sel_t.T.astype(BF16)

    _softmax_init(mx_scr, l_scr, acc_scr)
    col0 = lax.broadcasted_iota(I32, (rg, tk), 1)
    tg = qs + lax.broadcasted_iota(I32, (rg, 1), 0) % tq
    n_tiles = qs // tk + 1

    for j in range(bias_scr.shape[0]):
        bias_scr[j] = jnp.where(_dot(selb, e_ref[j]) > 0.5, 0.0, NEG)

    def scores(j, masked):
        ks0 = pl.multiple_of(j * tk, tk)
        bias = jnp.concatenate([bias_scr[j]] * (rg // tq), axis=0)
        k_tile = ks_ref[0, pl.ds(ks0, tk), :]
        for g in range(NSA_GROUPS):
            s = _dot_nt(q4[g * rg:(g + 1) * rg], k_tile) * (scale * LOG2E) + bias
            if masked:
                s = jnp.where(col0 + ks0 <= tg, s, NEG)
            _score_store(g, j, s, s_scr, mx_scr)

    _causal_tiles(scores, n_tiles - 1)
    _row_max_finish(mx_scr)

    def probs(j, carry):
        ks0 = pl.multiple_of(j * tk, tk)
        v_tile = vs_ref[0, pl.ds(ks0, tk), :]
        for g in range(NSA_GROUPS):
            _prob_accumulate(g, j, v_tile, s_scr, mx_scr, l_scr, acc_scr)
        return carry

    _paired_tiles(n_tiles, lambda j: probs(j, 0))
    o_slc = jnp.concatenate([_softmax_out(g, l_scr, acc_scr) for g in range(NSA_GROUPS)], axis=0)

    gates = small_ref[0]
    for h in range(HEADS):
        rows = slice(h * tq, (h + 1) * tq)
        g0 = gates[:, SMALL_G + 3 * h:SMALL_G + 3 * h + 1]
        g1 = gates[:, SMALL_G + 3 * h + 1:SMALL_G + 3 * h + 2]
        g2 = gates[:, SMALL_G + 3 * h + 2:SMALL_G + 3 * h + 3]
        o = g0 * cmp_scr[rows, :] + g1 * o_slc[rows] + g2 * win_scr[rows, :]
        o_ref[0, :, h * HEAD_W:(h + 1) * HEAD_W] = o.astype(BF16)


def _nsa(p3, kc, vc, small3, ov, emat, *, tq, tk):
    b, s, _ = p3.shape
    ng = kc.shape[1]
    col = lambda off: (lambda bi, i: (bi, 0, off // LANES))
    return pl.pallas_call(
        functools.partial(_nsa_kernel, tq=tq, tk=tk, seq=s),
        out_shape=jax.ShapeDtypeStruct((b, s, BR_WIDTH), BF16),
        grid=(b, s // tq),
        in_specs=[pl.BlockSpec((1, tq, 512), lambda bi, i: (bi, i, P_BQ // 512)),
                  pl.BlockSpec((1, ng, HEAD_W), lambda bi, i: (bi, 0, 0)),
                  pl.BlockSpec((1, ng, HEAD_W), lambda bi, i: (bi, 0, 0)),
                  pl.BlockSpec((1, s, LANES), col(P_KS)),
                  pl.BlockSpec((1, s, LANES), col(P_VS)),
                  pl.BlockSpec((1, s, LANES), col(P_KW)),
                  pl.BlockSpec((1, s, LANES), col(P_VW)),
                  pl.BlockSpec((1, tq, LANES), lambda bi, i: (bi, i, 0)),
                  pl.BlockSpec(ov.shape, lambda bi, i: (0, 0)),
                  pl.BlockSpec(emat.shape, lambda bi, i: (0, 0, 0))],
        out_specs=pl.BlockSpec((1, tq, BR_WIDTH), lambda bi, i: (bi, i, 0)),
        scratch_shapes=(_softmax_scratch(NSA_GROUPS, s // tk, HEADS * tq // NSA_GROUPS, tk)
                        + [pltpu.VMEM((HEADS * tq, HEAD_W), F32), pltpu.VMEM((HEADS * tq, HEAD_W), F32),
                           pltpu.VMEM((s // tk, tq, tk), F32)]),
        compiler_params=_cparams(2),
        name="nsa_attn",
    )(p3, kc, vc, p3, p3, p3, p3, small3, ov, emat)


def _mla_attn_kernel(qn_ref, qr_ref, kn_ref, kr_ref, v_ref, o_ref, s_scr, mx_scr, l_scr, acc_scr,
                     *, tq, tk):
    qs = pl.program_id(1) * tq
    n_full = qs // tk
    scale = (MLA_NOPE + MLA_ROPE) ** -0.5 * LOG2E
    lane = lax.broadcasted_iota(I32, (tq, LANES), 1)
    row_t = qs + lax.broadcasted_iota(I32, (tq, 1), 0)
    col0 = lax.broadcasted_iota(I32, (tq, tk), 1)
    _softmax_init(mx_scr, l_scr, acc_scr)

    def scores(j, masked):
        ks = pl.multiple_of(j * tk, tk)
        kr_tile = kr_ref[0, pl.ds(ks, tk), :]
        for h in range(HEADS):
            hs = slice(h * HEAD_W, (h + 1) * HEAD_W)
            pair = qr_ref[0, :, (h // 2) * LANES:(h // 2 + 1) * LANES]
            keep = (lane < MLA_ROPE) if h % 2 == 0 else (lane >= MLA_ROPE)
            qr = jnp.where(keep, pair, jnp.zeros_like(pair))
            s = _dot_nt(jnp.concatenate([qn_ref[0, :, hs], qr], axis=1),
                        jnp.concatenate([kn_ref[0, pl.ds(ks, tk), hs], kr_tile], axis=1)) * scale
            if masked:
                s = jnp.where(col0 + ks <= row_t, s, NEG)
            _score_store(h, j, s, s_scr, mx_scr)

    _causal_tiles(scores, n_full)
    _row_max_finish(mx_scr)

    def probs(j, carry):
        ks = pl.multiple_of(j * tk, tk)
        for h in range(HEADS):
            v_tile = v_ref[0, pl.ds(ks, tk), h * HEAD_W:(h + 1) * HEAD_W]
            _prob_accumulate(h, j, v_tile, s_scr, mx_scr, l_scr, acc_scr)
        return carry

    _paired_tiles(n_full + 1, lambda j: probs(j, 0))
    for h in range(HEADS):
        o_ref[0, :, h * HEAD_W:(h + 1) * HEAD_W] = _softmax_out(h, l_scr, acc_scr).astype(BF16)


def _mla_attn(q3, kv3, p3, *, tq, tk):
    b, s, _ = q3.shape
    return pl.pallas_call(
        functools.partial(_mla_attn_kernel, tq=tq, tk=tk),
        out_shape=jax.ShapeDtypeStruct((b, s, BR_WIDTH), BF16),
        grid=(b, s // tq),
        in_specs=[pl.BlockSpec((1, tq, 512), lambda bi, i: (bi, i, 0)),
                  pl.BlockSpec((1, tq, 256), lambda bi, i: (bi, i, 2)),
                  pl.BlockSpec((1, s, 512), lambda bi, i: (bi, 0, 0)),
                  pl.BlockSpec((1, s, LANES), lambda bi, i: (bi, 0, P_KR // LANES)),
                  pl.BlockSpec((1, s, 512), lambda bi, i: (bi, 0, 1))],
        out_specs=pl.BlockSpec((1, tq, BR_WIDTH), lambda bi, i: (bi, i, 0)),
        scratch_shapes=_softmax_scratch(HEADS, s // tk, tq, tk),
        compiler_params=_cparams(2),
        name="mla_attn",
    )(q3, q3, kv3, p3, kv3)


def _sortable_key(x):
    bits = pltpu.bitcast(x + 0.0, I32)
    return bits ^ (lax.shift_right_arithmetic(bits, 31) & 0x7FFFFFFF)


def _fold_rows(x, op):
    n = x.shape[0] // SUBLANES
    return op(x.reshape(n, SUBLANES, x.shape[1]), axis=0)


def _count16(half_scr, n_tiles, pred, tq):
    def count_tile(j, cnt):
        hit = pred(half_scr[j]).astype(I16)
        parts = [hit[r:r + PACKED_ROWS] for r in range(0, hit.shape[0], PACKED_ROWS)]
        while len(parts) > 1:
            parts = [a + b for a, b in zip(parts[0::2], parts[1::2])]
        return cnt + parts[0]

    cnt = lax.fori_loop(0, n_tiles, count_tile, jnp.zeros((PACKED_ROWS, tq), I16))
    return jnp.sum(cnt.astype(I32), axis=0, keepdims=True)


def _bisect16(half_scr, n_tiles, need, tq):
    def bit_body(i, carry):
        th, tot = carry
        cand = th + lax.shift_left(jnp.int32(1), 15 - i)
        c16 = cand.astype(I16)
        total = _count16(half_scr, n_tiles, lambda x: x >= c16, tq)
        ok = total >= need
        return jnp.where(ok, cand, th), jnp.where(ok, total, tot)

    return lax.fori_loop(0, 16, bit_body, (jnp.full((1, tq), HALF_MIN, I32), jnp.full((1, tq), -1, I32)))


def _dsa_kernel(qt_ref, iqt_ref, iwt_ref, k_ref, ik_ref, vt_ref, tri_ref, o_ref,
                key_scr, half_scr, s_scr, mx_scr, l_scr, acc_scr, *, tq, tk, top):
    qs = pl.program_id(1) * tq
    n_tiles = (qs + tq - 1) // tk + 1
    scale = DSA_DIM ** -0.5 * LOG2E
    t_lane = qs + lax.broadcasted_iota(I32, (tk, tq), 1)
    krow0 = lax.broadcasted_iota(I32, (tk, tq), 0)
    half_rows = lax.broadcasted_iota(I32, (LANES, tq), 0) < IDX_DIM
    vt_per_tile = tk // VT_TILE

    def score_tile(j, masked):
        ks = pl.multiple_of(j * tk, tk)
        ikt = ik_ref[0, pl.ds(ks, tk), :]
        acc = jnp.zeros((tk, tq), F32)
        for h in range(IDX_HEADS):
            pair = iqt_ref[0, (h // 2) * LANES:(h // 2 + 1) * LANES, :]
            keep = half_rows if h % 2 == 0 else jnp.logical_not(half_rows)
            iq_h = jnp.where(keep, pair, jnp.zeros_like(pair))
            w_h = iwt_ref[0, SMALL_IW + h:SMALL_IW + h + 1, :]
            acc = acc + w_h * jnp.maximum(_dot(ikt, iq_h), 0.0)
        key = _sortable_key(acc)
        if masked:
            key = jnp.where(krow0 + ks <= t_lane, key, INT_MIN)
        key_scr[j] = key
        half_scr[j] = lax.shift_right_arithmetic(key, 16).astype(I16)

    _causal_tiles(score_tile, n_tiles - 1)

    th_hi, n_ge_hi = _bisect16(half_scr, n_tiles, top, tq)
    hi16 = th_hi.astype(I16)
    n_gt_hi = _count16(half_scr, n_tiles, lambda x: x > hi16, tq)
    need_lo = top - n_gt_hi

    def low_tile(j, carry):
        key = key_scr[j]
        lo = (key & 0xFFFF) + HALF_MIN
        same_hi = lax.shift_right_arithmetic(key, 16) == th_hi
        half_scr[j] = jnp.where(same_hi, lo, HALF_MIN).astype(I16)
        return carry

    lax.fori_loop(0, n_tiles, low_tile, 0)
    th_lo, n_ge_lo = _bisect16(half_scr, n_tiles, need_lo, tq)
    theta = lax.shift_left(th_hi, 16) + (th_lo - HALF_MIN)
    n_ge_lo = jnp.where(n_ge_lo >= 0, n_ge_lo, n_ge_hi - n_gt_hi)
    tied = (n_ge_lo > need_lo) & (n_ge_hi >= 0)

    @pl.when(jnp.max(tied.astype(I32)) > 0)
    def _():
        lo16 = th_lo.astype(I16)
        n_gt = n_gt_hi + _count16(half_scr, n_tiles, lambda x: x > lo16, tq)
        keep_eq = jnp.where(tied, top - n_gt, tk * key_scr.shape[0]).astype(F32)

        def demote(j, seen):
            k = key_scr[j]
            eq = k == theta
            eqf = jnp.where(eq, 1.0, 0.0)
            before = _dot(tri_ref[...], eqf.astype(BF16)) + seen
            key_scr[j] = jnp.where(jnp.where(eq, before, -1.0) >= keep_eq, k - 1, k)
            return seen + jnp.sum(_fold_rows(eqf, jnp.sum), axis=0, keepdims=True)

        lax.fori_loop(0, n_tiles, demote, jnp.zeros((1, tq), F32))

    theta = jnp.maximum(theta, INT_MIN + 1)

    mx_scr[...] = jnp.full(mx_scr.shape, NEG, F32)
    l_scr[...] = jnp.zeros(l_scr.shape, F32)
    acc_scr[...] = jnp.zeros(acc_scr.shape, F32)

    def scores(j, carry):
        ks = pl.multiple_of(j * tk, tk)
        sel = key_scr[j] >= theta
        for h in range(HEADS):
            hs = slice(h * HEAD_W, (h + 1) * HEAD_W)
            s = _dot(k_ref[0, pl.ds(ks, tk), hs], qt_ref[0, hs, :]) * scale
            s = jnp.where(sel, s, NEG)
            s_scr[h, j] = s
            mx_scr[h] = jnp.maximum(mx_scr[h], _fold_rows(s, jnp.max))
        return carry

    _paired_tiles(n_tiles, lambda j: scores(j, 0))
    for h in range(HEADS):
        m = jnp.max(mx_scr[h], axis=0, keepdims=True)
        mx_scr[h] = jnp.broadcast_to(m, (SUBLANES, tq))

    def probs(j, carry):
        for h in range(HEADS):
            p = jnp.exp2(s_scr[h, j] - mx_scr[h][0:1])
            l_scr[h] += _fold_rows(p, jnp.sum)
            pb = p.astype(BF16)
            for c in range(vt_per_tile):
                vt = vt_ref[0, j * vt_per_tile + c, h * HEAD_W:(h + 1) * HEAD_W, :]
                acc_scr[h] += _dot(vt, pb[c * VT_TILE:(c + 1) * VT_TILE])
        return carry

    _paired_tiles(n_tiles, lambda j: probs(j, 0))
    for h in range(HEADS):
        ot = acc_scr[h] / jnp.sum(l_scr[h], axis=0, keepdims=True)
        o_ref[0, :, h * HEAD_W:(h + 1) * HEAD_W] = ot.T.astype(BF16)


def _dsa(p3, t3, vt4, smallt, *, tq, tk, top):
    b, s, _ = p3.shape
    n_vt = vt4.shape[1]
    tri = jnp.asarray(np.tril(np.ones((tk, tk), np.float32), -1), BF16)
    return pl.pallas_call(
        functools.partial(_dsa_kernel, tq=tq, tk=tk, top=top),
        out_shape=jax.ShapeDtypeStruct((b, s, BR_WIDTH), BF16),
        grid=(b, s // tq),
        in_specs=[pl.BlockSpec((1, 512, tq), lambda bi, i: (bi, T_DQ // 512, i)),
                  pl.BlockSpec((1, 512, tq), lambda bi, i: (bi, T_IQ // 512, i)),
                  pl.BlockSpec((1, LANES, tq), lambda bi, i: (bi, 0, i)),
                  pl.BlockSpec((1, s, 512), lambda bi, i: (bi, 0, P_DK // 512)),
                  pl.BlockSpec((1, s, LANES), lambda bi, i: (bi, 0, P_IK // LANES)),
                  pl.BlockSpec((1, n_vt, BR_WIDTH, VT_TILE), lambda bi, i: (bi, 0, 0, 0)),
                  pl.BlockSpec((tk, tk), lambda bi, i: (0, 0))],
        out_specs=pl.BlockSpec((1, tq, BR_WIDTH), lambda bi, i: (bi, i, 0)),
        scratch_shapes=[pltpu.VMEM((s // tk, tk, tq), I32),
                        pltpu.VMEM((s // tk, tk, tq), I16),
                        pltpu.VMEM((HEADS, s // tk, tk, tq), F32),
                        pltpu.VMEM((HEADS, SUBLANES, tq), F32),
                        pltpu.VMEM((HEADS, SUBLANES, tq), F32),
                        pltpu.VMEM((HEADS, HEAD_W, tq), F32)],
        compiler_params=_cparams(2),
        name="dsa_attn",
    )(t3, t3, smallt, p3, p3, vt4, tri)


def _merge_kernel(x_ref, g_ref, oa_ref, ob_ref, oc_ref, od_ref, wg_ref, wb_ref, wo_ref, o_ref):
    x = x_ref[...]
    d = x.shape[1]
    ms = jnp.mean(x * x, axis=-1, keepdims=True)
    h = (x * lax.rsqrt(ms + EPS) * g_ref[...]).astype(BF16)
    acc = jnp.zeros(x.shape, F32)
    for n, br_ref in enumerate((oa_ref, ob_ref, oc_ref, od_ref)):
        gate = jax.nn.sigmoid(_dot(h, wg_ref[:, n * d:(n + 1) * d]))
        acc = acc + gate * _dot(br_ref[...], wb_ref[n])
    o_ref[...] = x + _dot(acc.astype(BF16), wo_ref[...])


def _merge(x2, g, oa, ob, oc, od, wg, wb, wo, *, layer, tm):
    m, d = x2.shape
    row = lambda w: pl.BlockSpec((tm, w), lambda i: (i, 0))
    return pl.pallas_call(
        _merge_kernel,
        out_shape=jax.ShapeDtypeStruct((m, d), F32),
        grid=(m // tm,),
        in_specs=[row(d), pl.BlockSpec((1, d), lambda i: (0, 0)),
                  row(BR_WIDTH), row(BR_WIDTH), row(BR_WIDTH), row(BR_WIDTH),
                  pl.BlockSpec((None,) + wg.shape[1:], lambda i: (layer, 0, 0)),
                  pl.BlockSpec((None,) + wb.shape[1:], lambda i: (layer, 0, 0, 0)),
                  pl.BlockSpec((None,) + wo.shape[1:], lambda i: (layer, 0, 0))],
        out_specs=row(d),
        compiler_params=_cparams(1),
        name="merge",
    )(x2, g, oa, ob, oc, od, wg, wb, wo)


def _ffn_kernel(x_ref, g_ref, wg_ref, wu_ref, wd_ref, gf_ref, o_ref, h_scr, acc_scr, *, final):
    j = pl.program_id(1)

    @pl.when(j == 0)
    def _():
        x = x_ref[...]
        ms = jnp.mean(x * x, axis=-1, keepdims=True)
        h_scr[...] = (x * lax.rsqrt(ms + EPS) * g_ref[...]).astype(BF16)
        acc_scr[...] = jnp.zeros(acc_scr.shape, F32)

    h = h_scr[...]
    a = jax.nn.silu(_dot(h, wg_ref[...])) * _dot(h, wu_ref[...])
    acc_scr[...] += _dot(a.astype(BF16), wd_ref[...])

    @pl.when(j == pl.num_programs(1) - 1)
    def _():
        y = x_ref[...] + acc_scr[...]
        if final:
            ms = jnp.mean(y * y, axis=-1, keepdims=True)
            y = y * lax.rsqrt(ms + EPS) * gf_ref[...]
        o_ref[...] = y


def _ffn(x2, g, wgu, wd, gf, *, layer, final, tm, tf):
    m, d = x2.shape
    dff = wd.shape[1]
    n_f = dff // tf
    return pl.pallas_call(
        functools.partial(_ffn_kernel, final=final),
        out_shape=jax.ShapeDtypeStruct((m, d), F32),
        grid=(m // tm, n_f),
        in_specs=[pl.BlockSpec((tm, d), lambda i, j: (i, 0)),
                  pl.BlockSpec((1, d), lambda i, j: (0, 0)),
                  pl.BlockSpec((None, d, tf), lambda i, j: (layer, 0, j)),
                  pl.BlockSpec((None, d, tf), lambda i, j: (layer, 0, j + n_f)),
                  pl.BlockSpec((None, tf, d), lambda i, j: (layer, j, 0)),
                  pl.BlockSpec((1, d), lambda i, j: (0, 0))],
        out_specs=pl.BlockSpec((tm, d), lambda i, j: (i, 0)),
        scratch_shapes=[pltpu.VMEM((tm, d), BF16), pltpu.VMEM((tm, d), F32)],
        compiler_params=_cparams(2),
        name="ffn",
    )(x2, g, wgu, wgu, wd, gf)


def _tiles(seq, m, dff):
    pick = lambda n, cands: next(c for c in cands if n % c == 0)
    tk = pick(seq, (512, 256))
    return dict(
        merge_tm=pick(m, (256, 128)),
        prep_ts=VT_TILE,
        diff_tq=128, mla_tq=pick(seq, (256, 128)), nsa_tq=256, dsa_tq=256, tk=tk,
        row_tm=pick(m, (512, 256, 128)),
        ffn_tf=pick(dff, (1408, 704, 256, 128)),
    )


def kernel(x, norm1_g, w_in, diff_lq1, diff_lk1, diff_lq2, diff_lk2, diff_subln_g, nsa_pe_k, nsa_w1_k, nsa_w2_k, nsa_pe_v, nsa_w1_v, nsa_w2_v, mla_q_norm_g, mla_w_uq, mla_kv_norm_g, mla_w_ukv, idx_k_norm_g, w_branch, w_out, norm2_g, w_gate_up, w_down, final_norm_g):
    b, seq, d = x.shape
    depth = w_in.shape[0]
    m = b * seq
    dff = w_down.shape[1]
    t = _tiles(seq, m, dff)
    tk = t["tk"]
    assert seq % SEL_LEN == 0 and seq >= WIN + t["nsa_tq"] and seq // SEL_LEN <= LANES
    assert seq % t["dsa_tq"] == 0 and tk % VT_TILE == 0 and tk >= min(IDX_TOPK, seq // 4)

    col_idx, gate_off, d_in = _in_proj_columns()
    assert w_in.shape[2] == d_in
    tab = jnp.concatenate([_rope_table(seq, rot, per) for rot, per in ROPE_KINDS], axis=1)

    ng = seq // CMP_STRIDE
    ns = seq // SEL_LEN
    c_start = np.arange(ng)[:, None] * CMP_STRIDE
    s_start = np.arange(LANES)[None, :] * SEL_LEN
    ov = ((c_start < s_start + SEL_LEN) & (c_start + CMP_LEN - 1 >= s_start)
          & (np.arange(LANES)[None, :] < ns))
    ov = jnp.asarray(ov, BF16)
    emat = np.arange(LANES)[:, None] == (np.arange(seq)[None, :] // SEL_LEN)
    emat = jnp.asarray(emat.reshape(LANES, seq // tk, tk).transpose(1, 0, 2), BF16)

    qd = MLA_NOPE + MLA_ROPE
    uq_idx = np.concatenate([np.concatenate([np.arange(h * qd, h * qd + MLA_NOPE) for h in range(HEADS)]),
                             np.concatenate([np.arange(h * qd + MLA_NOPE, (h + 1) * qd) for h in range(HEADS)])])
    kvd = MLA_NOPE + HEAD_W
    ukv_idx = np.concatenate([np.concatenate([np.arange(h * kvd, h * kvd + MLA_NOPE) for h in range(HEADS)]),
                              np.concatenate([np.arange(h * kvd + MLA_NOPE, (h + 1) * kvd) for h in range(HEADS)])])

    w_mix = _take_cols(w_in, col_idx).astype(BF16)
    w_gate = w_in[..., gate_off:].astype(BF16)
    w_branch_bf, w_out_bf = w_branch.astype(BF16), w_out.astype(BF16)
    w_gate_up_bf, w_down_bf = w_gate_up.astype(BF16), w_down.astype(BF16)

    x2 = x.reshape(m, d)
    half_w1 = CMP_STRIDE * NSA_DK
    for l in range(depth):
        lam_init = 0.8 - 0.6 * math.exp(-0.3 * l)

        gq = jnp.pad(mla_q_norm_g[l], (0, 512 - MLA_Q_LORA))[None]
        gkv = mla_kv_norm_g[l][None]
        gik = jnp.concatenate([idx_k_norm_g[l], idx_k_norm_g[l]])[None]
        wq = jnp.pad(_take_cols(mla_w_uq[l].astype(BF16), uq_idx), ((0, 512 - MLA_Q_LORA), (0, 0)))
        wkv = _take_cols(mla_w_ukv[l].astype(BF16), ukv_idx)
        p2, t3, vt4, kc_tok, vc_tok, small, smallt, q_c, kv_c = _proj_prep(
            x2, norm1_g[l][None], w_mix, tab, gq, gkv, gik, wq, wkv,
            layer=l, batch=b, seq=seq, ts=t["prep_ts"])
        p3 = p2.reshape(b, seq, P_WIDTH)
        small3 = small.reshape(b, seq, LANES)

        lv = jnp.stack([diff_lq1[l], diff_lk1[l], diff_lq2[l], diff_lk2[l]])
        o_a = _diff_attn(p3, lv, diff_subln_g[l][None], lam_init=lam_init, tq=t["diff_tq"], tk=tk)

        w1k, w1v = nsa_w1_k[l].astype(BF16), nsa_w1_v[l].astype(BF16)
        w1k_cat = jnp.concatenate([w1k[:half_w1], w1k[half_w1:]], axis=1)
        w1v_cat = jnp.concatenate([w1v[:half_w1], w1v[half_w1:]], axis=1)
        pek = jnp.broadcast_to(nsa_pe_k[l].reshape(1, -1), (8, CMP_LEN * NSA_DK)).astype(BF16)
        pev = jnp.broadcast_to(nsa_pe_v[l].reshape(1, -1), (8, CMP_LEN * NSA_DK)).astype(BF16)
        kc, vc = _nsa_compress(kc_tok.reshape(b, ng, half_w1), vc_tok.reshape(b, ng, half_w1),
                               w1k_cat, w1v_cat, pek, pev, w1k, w1v,
                               nsa_w2_k[l].astype(BF16), nsa_w2_v[l].astype(BF16))
        o_b = _nsa(p3, kc, vc, small3, ov, emat, tq=t["nsa_tq"], tk=tk)

        o_c = _mla_attn(q_c.reshape(b, seq, -1), kv_c.reshape(b, seq, -1), p3, tq=t["mla_tq"], tk=tk)

        o_d = _dsa(p3, t3, vt4, smallt, tq=t["dsa_tq"], tk=tk, top=min(IDX_TOPK, seq // 4))

        x2 = _merge(x2, norm1_g[l][None],
                    o_a.reshape(m, -1), o_b.reshape(m, -1), o_c.reshape(m, -1), o_d.reshape(m, -1),
                    w_gate, w_branch_bf, w_out_bf, layer=l, tm=t["merge_tm"])
        x2 = _ffn(x2, norm2_g[l][None], w_gate_up_bf, w_down_bf, final_norm_g[None],
                  layer=l, final=(l == depth - 1), tm=t["row_tm"], tf=t["ffn_tf"])
    return x2.reshape(b, seq, d)
```
